```python
import jax
import jax.numpy as jnp
from jax import lax
import numpy as np

D_MODEL = 1024
BATCH = 8
SEQ = 2048
DEPTH = 4

GRID_W = 64
CTX_LEN = 256
HEAD_DIM = D_MODEL // 16
A_HEADS = 4
B_HEADS = 4
C_HEADS = 8
A_WIDTH = A_HEADS * HEAD_DIM
B_WIDTH = B_HEADS * HEAD_DIM
C_WIDTH = C_HEADS * HEAD_DIM
MIX_WIDTH = A_WIDTH + B_WIDTH + C_WIDTH
HGRN_CHUNK = 16
HGRN_F_FLOOR = 1e-20
RWKV_W_RANK = 64
RWKV_A_RANK = 64
RWKV_G_RANK = 128
RWKV_LN_EPS = 64e-5
NA_WIN_ROWS = 8
NA_WIN_COLS = 16
NA_COL_BLOCK = 16
NA_KEY_COLS = NA_COL_BLOCK + NA_WIN_COLS
ROPE_THETA = 10000.0
N_GROUPS = 4
EXPERTS_PER_GROUP = 8
TOP_K_IN_GROUP = 2
D_EXPERT = 512
EPS = 1e-6
NEG_INF = -1e30
A_PROJ = 5 * A_WIDTH
B_PROJ = 3 * B_WIDTH + 2 * RWKV_W_RANK + 2 * RWKV_A_RANK + RWKV_G_RANK
C_PROJ = 3 * C_WIDTH
P_TOTAL = A_PROJ + B_PROJ + C_PROJ
RWKV_SPLITS = (B_WIDTH, 2 * B_WIDTH, 3 * B_WIDTH, 3 * B_WIDTH + RWKV_W_RANK, 3 * B_WIDTH + 2 * RWKV_W_RANK,
               3 * B_WIDTH + 2 * RWKV_W_RANK + RWKV_A_RANK, 3 * B_WIDTH + 2 * RWKV_W_RANK + 2 * RWKV_A_RANK)

kernel_name = 'hybrid_diffusion_trunk'


def rms_norm(x, gain):
    xf = x.astype(jnp.float32)
    return xf * lax.rsqrt(jnp.mean(xf * xf, axis=-1, keepdims=True) + EPS) * gain.astype(jnp.float32)


def to_heads(t, n_heads):
    b, t_len, _ = t.shape
    return t.reshape(b, t_len, n_heads, HEAD_DIM).transpose(0, 2, 1, 3)


def chunk_gla(q, k, v, log_f, s0):
    b, h, t_len, _ = q.shape
    n = t_len // HGRN_CHUNK

    def blocks(u):
        return u.reshape(b, h, n, HGRN_CHUNK, u.shape[-1])

    q, k, v, log_f = blocks(q), blocks(k), blocks(v), blocks(log_f)
    cum = jnp.cumsum(log_f, axis=3)
    tri = jnp.tril(jnp.ones((HGRN_CHUNK, HGRN_CHUNK), dtype=bool))[:, :, None]
    diff = cum[:, :, :, :, None, :] - cum[:, :, :, None, :, :]
    decay = jnp.where(tri, jnp.exp(jnp.where(tri, diff, 0.0)), 0.0)
    scores = jnp.einsum('bhntd,bhnsd,bhntsd->bhnts', q, k, decay)
    o_intra = jnp.einsum('bhnts,bhnsv->bhntv', scores, v)
    q_in = q * jnp.exp(cum)
    k_out = k * jnp.exp(cum[:, :, :, -1:, :] - cum)
    kv = jnp.einsum('bhnsd,bhnsv->bhndv', k_out, v)
    chunk_decay = jnp.exp(cum[:, :, :, -1, :])

    def step(s, inp):
        q_c, dec_c, kv_c = inp
        o_c = jnp.einsum('bhtd,bhdv->bhtv', q_c, s)
        return s * dec_c[..., None] + kv_c, o_c

    s_fin, o_inter = lax.scan(step, s0, (jnp.moveaxis(q_in, 2, 0), jnp.moveaxis(chunk_decay, 2, 0),
                                         jnp.moveaxis(kv, 2, 0)))
    o = o_intra + jnp.moveaxis(o_inter, 0, 2)
    return o.reshape(b, h, t_len, v.shape[-1]), s_fin


def hgrn2_forget(f_pre, lb):
    f = lb + (1.0 - lb) * jax.nn.sigmoid(f_pre)
    log_f = jnp.log(jnp.maximum(f, HGRN_F_FLOOR))
    k = (1.0 - lb) * jax.nn.sigmoid(-f_pre)
    return to_heads(log_f, A_HEADS), to_heads(k, A_HEADS)


def hgrn2_mixer(p_ctx, p_lat, lb_fwd, lb_bwd, gn_gain):
    qc, ffc, fbc, ic, gc = jnp.split(p_ctx, 5, axis=-1)
    ql, ffl, fbl, il, gl = jnp.split(p_lat, 5, axis=-1)
    qc, ic, ql, il = (to_heads(t, A_HEADS) for t in (qc, ic, ql, il))
    s0 = jnp.zeros((p_lat.shape[0], A_HEADS, HEAD_DIM, HEAD_DIM), jnp.float32)

    def flip(t):
        return t[:, :, ::-1]

    lf, kf = hgrn2_forget(ffc, lb_fwd)
    oc_f, s_f = chunk_gla(qc, kf, ic, lf, s0)
    lf, kf = hgrn2_forget(ffl, lb_fwd)
    ol_f, _ = chunk_gla(ql, kf, il, lf, s_f)
    lbk, kb = hgrn2_forget(fbc, lb_bwd)
    oc_b, s_b = chunk_gla(flip(qc), flip(kb), flip(ic), flip(lbk), s0)
    lbk, kb = hgrn2_forget(fbl, lb_bwd)
    ol_b, _ = chunk_gla(flip(ql), flip(kb), flip(il), flip(lbk), s_b)

    def readout(o, g):
        o = rms_norm(o.transpose(0, 2, 1, 3), gn_gain)
        return o.reshape(g.shape) * jax.nn.silu(g)

    return readout(ol_f + flip(ol_b), gl), readout(oc_f + flip(oc_b), gc)


def centred_shift(u):
    pad = jnp.pad(u, ((0, 0), (1, 1), (0, 0)))
    return 0.5 * (pad[:, :-2] + pad[:, 2:])


def rwkv7_scan(r, w, k, v, kk, ak, s0):
    def step(s, inp):
        r_t, w_t, k_t, v_t, kk_t, ak_t = inp
        sa = jnp.einsum('bhvk,bhk->bhv', s, kk_t)
        s = s * w_t[:, :, None, :] - sa[..., None] * ak_t[:, :, None, :] + v_t[..., None] * k_t[:, :, None, :]
        return s, jnp.einsum('bhvk,bhk->bhv', s, r_t)

    xs = tuple(jnp.moveaxis(u, 1, 0) for u in (r, w, k, v, kk, ak))
    s_fin, ys = lax.scan(step, s0, xs)
    return jnp.moveaxis(ys, 0, 1), s_fin


def rwkv7_mixer(p_ctx, p_lat, mu, w0, w_up, a0, a_up, g_up, kk_scale, k_a, r_k, ln_gain, ln_bias):
    def hd(t):
        return t.reshape(t.shape[0], t.shape[1], B_HEADS, HEAD_DIM)

    def prep(p):
        p = p + (centred_shift(p) - p) * mu
        r, k, v, wdf, wdb, adf, adb, gd = jnp.split(p, list(RWKV_SPLITS), axis=-1)
        kk = hd(k * kk_scale)
        kk = kk / jnp.maximum(jnp.sqrt(jnp.sum(kk * kk, axis=-1, keepdims=True)), 1e-12)
        dirs = []
        for d, (wd, ad) in enumerate(((wdf, adf), (wdb, adb))):
            w = -jax.nn.softplus(-(w0[d] + jnp.tanh(wd) @ w_up[d])) - 0.5
            decay = jnp.exp(-jnp.exp(w))
            a = jax.nn.sigmoid(a0[d] + ad @ a_up[d])
            k_d = k * (1.0 + (a - 1.0) * k_a)
            dirs.append((hd(decay), hd(k_d), hd(a) * kk))
        g = jax.nn.sigmoid(gd) @ g_up
        return hd(r), hd(v), kk, dirs, g

    rc, vc, kkc, dc, gc = prep(p_ctx)
    rl, vl, kkl, dl, gl = prep(p_lat)
    s0 = jnp.zeros((p_lat.shape[0], B_HEADS, HEAD_DIM, HEAD_DIM), jnp.float32)

    def flip(t):
        return t[:, ::-1]

    (wcf, kcf, akcf), (wcb, kcb, akcb) = dc
    (wlf, klf, aklf), (wlb, klb, aklb) = dl
    yc_f, s_f = rwkv7_scan(rc, wcf, kcf, vc, kkc, akcf, s0)
    yl_f, _ = rwkv7_scan(rl, wlf, klf, vl, kkl, aklf, s_f)
    yc_b, s_b = rwkv7_scan(flip(rc), flip(wcb), flip(kcb), flip(vc), flip(kkc), flip(akcb), s0)
    yl_b, _ = rwkv7_scan(flip(rl), flip(wlb), flip(klb), flip(vl), flip(kkl), flip(aklb), s_b)

    def readout(y, r, v, k_f, k_b, g):
        mean = jnp.mean(y, axis=-1, keepdims=True)
        var = jnp.mean(jnp.square(y - mean), axis=-1, keepdims=True)
        y = ((y - mean) * lax.rsqrt(var + RWKV_LN_EPS)).reshape(g.shape) * ln_gain + ln_bias
        bonus = (jnp.sum(r * (k_f + k_b) * r_k, axis=-1, keepdims=True) * v).reshape(g.shape)
        return (y + bonus) * g

    out_lat = readout(yl_f + flip(yl_b), rl, vl, klf, klb, gl)
    out_ctx = readout(yc_f + flip(yc_b), rc, vc, kcf, kcb, gc)
    return out_lat, out_ctx


def axial_rope(t, pos_r, pos_c):
    quarter = HEAD_DIM // 4
    inv = ROPE_THETA ** (-jnp.arange(quarter, dtype=jnp.float32) / quarter)
    ang = jnp.concatenate([pos_r[:, None] * inv, pos_c[:, None] * inv], axis=-1)
    cos = jnp.cos(ang)[None, :, None, :]
    sin = jnp.sin(ang)[None, :, None, :]
    t1, t2 = jnp.split(t, 2, axis=-1)
    return jnp.concatenate([t1 * cos - t2 * sin, t2 * cos + t1 * sin], axis=-1)


def neighbourhood_attention(p_ctx, p_lat, q_gain, k_gain, rpb, pos_r, pos_c, need_ctx):
    bsz, t_lat, _ = p_lat.shape
    n_ctx = p_ctx.shape[1]
    rows = t_lat // GRID_W
    win_r = min(NA_WIN_ROWS, rows)
    scale = HEAD_DIM ** -0.5

    def qkv(p):
        q, k, v = jnp.split(p, 3, axis=-1)
        hd = lambda t: t.reshape(t.shape[0], t.shape[1], C_HEADS, HEAD_DIM)
        return rms_norm(hd(q), q_gain), rms_norm(hd(k), k_gain), hd(v)

    qc, kc, vc = qkv(p_ctx)
    ql, kl, vl = qkv(p_lat)
    ql, kl = axial_rope(ql, pos_r, pos_c), axial_rope(kl, pos_r, pos_c)
    kc_h, vc_h = kc.transpose(0, 2, 1, 3), vc.transpose(0, 2, 1, 3)

    def grid(t):
        return t.transpose(0, 2, 1, 3).reshape(bsz, C_HEADS, rows, GRID_W, HEAD_DIM)

    qg, kg, vg = grid(ql), grid(kl), grid(vl)
    n_cb = GRID_W // NA_COL_BLOCK
    q_col = jnp.arange(GRID_W).reshape(n_cb, NA_COL_BLOCK)
    k_start = jnp.clip(jnp.arange(n_cb) * NA_COL_BLOCK - NA_WIN_COLS // 2, 0, GRID_W - NA_KEY_COLS)
    k_col = k_start[:, None] + jnp.arange(NA_KEY_COLS)
    w_start = jnp.clip(q_col - NA_WIN_COLS // 2, 0, GRID_W - NA_WIN_COLS)
    kc3 = k_col[:, None, :]
    in_win = (kc3 >= w_start[..., None]) & (kc3 < w_start[..., None] + NA_WIN_COLS)
    col_idx = jnp.clip(kc3 - q_col[..., None] + NA_WIN_COLS - 1, 0, 2 * NA_WIN_COLS - 2)
    n_win = win_r * NA_KEY_COLS

    def one_row(r):
        r_start = jnp.clip(r - win_r // 2, 0, rows - win_r)
        k_blk = lax.dynamic_slice_in_dim(kg, r_start, win_r, axis=2)[:, :, :, k_col]
        v_blk = lax.dynamic_slice_in_dim(vg, r_start, win_r, axis=2)[:, :, :, k_col]
        q_row = lax.dynamic_index_in_dim(qg, r, axis=2, keepdims=False).reshape(
            bsz, C_HEADS, n_cb, NA_COL_BLOCK, HEAD_DIM)
        s_win = jnp.einsum('bhjqd,bhrjkd->bhjqrk', q_row, k_blk) * scale
        row_idx = r_start + jnp.arange(win_r) - r + NA_WIN_ROWS - 1
        bias = rpb[:, row_idx[None, None, :, None], col_idx[:, :, None, :]]
        s_win = jnp.where(in_win[:, :, None, :], s_win + bias, NEG_INF)
        s_ctx = jnp.einsum('bhjqd,bhld->bhjql', q_row, kc_h) * scale
        s_all = jnp.concatenate([s_win.reshape(s_win.shape[:4] + (n_win,)), s_ctx], axis=-1)
        prob = jax.nn.softmax(s_all, axis=-1)
        p_win = prob[..., :n_win].reshape(s_win.shape)
        p_ctx = prob[..., n_win:]
        o = (jnp.einsum('bhjqrk,bhrjkd->bhjqd', p_win, v_blk)
             + jnp.einsum('bhjql,bhld->bhjqd', p_ctx, vc_h))
        return o.reshape(bsz, C_HEADS, GRID_W, HEAD_DIM)

    o = lax.map(one_row, jnp.arange(rows))
    out_lat = o.transpose(1, 0, 3, 2, 4).reshape(bsz, t_lat, C_WIDTH)
    out_ctx = None
    if need_ctx:
        qc_h = qc.transpose(0, 2, 1, 3)
        pc = jax.nn.softmax(jnp.einsum('bhqd,bhkd->bhqk', qc_h, kc_h) * scale, axis=-1)
        out_ctx = jnp.einsum('bhqk,bhkd->bhqd', pc, vc_h).transpose(0, 2, 1, 3).reshape(bsz, n_ctx, C_WIDTH)
    return out_lat, out_ctx


def hierarchical_moe(h, w_rg, b_rg, w_re, b_re, w_gate, w_up, w_down):
    g_logits = h @ w_rg + b_rg
    g_prob = jax.nn.softmax(g_logits, axis=-1)
    g_sel = jnp.argmax(g_logits, axis=-1)
    g_w = jnp.take_along_axis(g_prob, g_sel[:, None], axis=-1)
    e_logits = (h @ w_re + b_re).reshape(-1, N_GROUPS, EXPERTS_PER_GROUP)
    e_logits = jnp.take_along_axis(e_logits, g_sel[:, None, None], axis=1)[:, 0]
    e_prob = jax.nn.softmax(e_logits, axis=-1)
    top_w, top_i = lax.top_k(e_prob, TOP_K_IN_GROUP)
    top_w = top_w / jnp.sum(top_w, axis=-1, keepdims=True) * g_w
    e_w = jnp.sum(jax.nn.one_hot(top_i, EXPERTS_PER_GROUP, dtype=jnp.float32) * top_w[..., None], axis=1)
    comb = jax.nn.one_hot(g_sel, N_GROUPS, dtype=jnp.float32)[:, :, None] * e_w[:, None, :]
    y = jnp.zeros(h.shape, jnp.float32)
    for gi in range(N_GROUPS):
        hid = jax.nn.silu(jnp.einsum('nd,edf->nef', h, w_gate[gi])) * jnp.einsum('nd,edf->nef', h, w_up[gi])
        y = y + jnp.einsum('nef,efd->nd', hid * comb[:, gi, :, None], w_down[gi])
    return y


def setup_inputs(seed: int = 0) -> dict:
    key = jax.random.key(seed)
    ks = iter(jax.random.split(key, 40))
    D, G, E, F = D_MODEL, N_GROUPS, EXPERTS_PER_GROUP, D_EXPERT

    def nrm(shape, s):
        return s * jax.random.normal(next(ks), shape, jnp.float32)

    return {
        'x': nrm((BATCH, SEQ, D), 1.0),
        'c': nrm((BATCH, D), 1.0),
        'ctx': nrm((BATCH, CTX_LEN, D), 1.0),
        'c_ctx': nrm((D,), 1.0),
        'norm1_gain': 1.0 + nrm((DEPTH, D), 0.02),
        'norm2_gain': 1.0 + nrm((DEPTH, D), 0.02),
        'w_ada': nrm((DEPTH, D, 6 * D), 0.5 * D ** -0.5),
        'b_ada': nrm((DEPTH, 6 * D), 0.02),
        'w_in': nrm((DEPTH, D, P_TOTAL), D ** -0.5),
        'w_out': nrm((DEPTH, MIX_WIDTH, D), MIX_WIDTH ** -0.5),
        'hgrn_lb_logits': nrm((2, DEPTH, A_WIDTH), 0.5),
        'hgrn_gn_gain': 1.0 + nrm((DEPTH, A_HEADS, HEAD_DIM), 0.02),
        'rwkv_mu': jax.random.uniform(next(ks), (DEPTH, B_PROJ), jnp.float32, 0.0, 1.0),
        'rwkv_w0': -1.5 + nrm((DEPTH, 2, B_WIDTH), 0.5),
        'rwkv_w_up': nrm((DEPTH, 2, RWKV_W_RANK, B_WIDTH), RWKV_W_RANK ** -0.5),
        'rwkv_a0': nrm((DEPTH, 2, B_WIDTH), 0.1),
        'rwkv_a_up': nrm((DEPTH, 2, RWKV_A_RANK, B_WIDTH), RWKV_A_RANK ** -0.5),
        'rwkv_g_up': nrm((DEPTH, RWKV_G_RANK, B_WIDTH), RWKV_G_RANK ** -0.5),
        'rwkv_kk_scale': 0.85 + nrm((DEPTH, B_WIDTH), 0.02),
        'rwkv_k_a': 1.0 + nrm((DEPTH, B_WIDTH), 0.02),
        'rwkv_r_k': nrm((DEPTH, B_HEADS, HEAD_DIM), 0.1),
        'rwkv_ln_gain': 1.0 + nrm((DEPTH, B_WIDTH), 0.02),
        'rwkv_ln_bias': nrm((DEPTH, B_WIDTH), 0.02),
        'na_q_gain': 1.0 + nrm((DEPTH, HEAD_DIM), 0.02),
        'na_k_gain': 1.0 + nrm((DEPTH, HEAD_DIM), 0.02),
        'na_rpb': nrm((DEPTH, C_HEADS, 2 * NA_WIN_ROWS - 1, 2 * NA_WIN_COLS - 1), 0.1),
        'w_router_group': nrm((DEPTH, D, G), D ** -0.5),
        'b_router_group': nrm((DEPTH, G), 0.01),
        'w_router_expert': nrm((DEPTH, D, G * E), D ** -0.5),
        'b_router_expert': nrm((DEPTH, G * E), 0.01),
        'w_exp_gate': nrm((DEPTH, G, E, D, F), D ** -0.5),
        'w_exp_up': nrm((DEPTH, G, E, D, F), D ** -0.5),
        'w_exp_down': nrm((DEPTH, G, E, F, D), F ** -0.5),
    }


def reference(x, c, ctx, c_ctx, norm1_gain, norm2_gain, w_ada, b_ada, w_in, w_out,
              hgrn_lb_logits, hgrn_gn_gain, rwkv_mu, rwkv_w0, rwkv_w_up, rwkv_a0, rwkv_a_up,
              rwkv_g_up, rwkv_kk_scale, rwkv_k_a, rwkv_r_k, rwkv_ln_gain, rwkv_ln_bias,
              na_q_gain, na_k_gain, na_rpb, w_router_group, b_router_group,
              w_router_expert, b_router_expert, w_exp_gate, w_exp_up, w_exp_down):
    f32 = jnp.float32
    bsz, t_lat, _ = x.shape
    n_ctx = ctx.shape[1]
    pos = jnp.arange(t_lat)
    pos_r = (pos // GRID_W).astype(f32)
    pos_c = (pos % GRID_W).astype(f32)
    lb_p = jax.nn.softmax(hgrn_lb_logits.astype(f32), axis=1)
    lower_bounds = jnp.cumsum(lb_p, axis=1) - lb_p[:, :1]
    silu_c = jax.nn.silu(c.astype(f32))
    silu_cc = jax.nn.silu(c_ctx.astype(f32))
    xl = x.astype(f32)
    xc = ctx.astype(f32)
    for layer in range(DEPTH):
        last = layer == DEPTH - 1
        mod_l = (silu_c @ w_ada[layer] + b_ada[layer])[:, None, :]
        mod_c = (silu_cc @ w_ada[layer] + b_ada[layer])[None, None, :]
        sh1l, sc1l, g1l, sh2l, sc2l, g2l = jnp.split(mod_l, 6, axis=-1)
        sh1c, sc1c, g1c, sh2c, sc2c, g2c = jnp.split(mod_c, 6, axis=-1)
        hl = rms_norm(xl, norm1_gain[layer]) * (1.0 + sc1l) + sh1l
        hc = rms_norm(xc, norm1_gain[layer]) * (1.0 + sc1c) + sh1c
        pl = hl @ w_in[layer]
        pc = hc @ w_in[layer]
        al, bl, cl = jnp.split(pl, [A_PROJ, A_PROJ + B_PROJ], axis=-1)
        ac, bc, cc = jnp.split(pc, [A_PROJ, A_PROJ + B_PROJ], axis=-1)
        a_lat, a_ctx = hgrn2_mixer(ac, al, lower_bounds[0, layer], lower_bounds[1, layer], hgrn_gn_gain[layer])
        b_lat, b_ctx = rwkv7_mixer(bc, bl, rwkv_mu[layer], rwkv_w0[layer], rwkv_w_up[layer], rwkv_a0[layer],
                                   rwkv_a_up[layer], rwkv_g_up[layer], rwkv_kk_scale[layer], rwkv_k_a[layer],
                                   rwkv_r_k[layer], rwkv_ln_gain[layer], rwkv_ln_bias[layer])
        c_lat, c_ctxo = neighbourhood_attention(cc, cl, na_q_gain[layer], na_k_gain[layer], na_rpb[layer],
                                                pos_r, pos_c, not last)
        xl = xl + g1l * (jnp.concatenate([a_lat, b_lat, c_lat], axis=-1) @ w_out[layer])
        h2l = (rms_norm(xl, norm2_gain[layer]) * (1.0 + sc2l) + sh2l).reshape(-1, D_MODEL)
        moe_w = (w_router_group[layer], b_router_group[layer], w_router_expert[layer], b_router_expert[layer],
                 w_exp_gate[layer], w_exp_up[layer], w_exp_down[layer])
        if last:
            xl = xl + g2l * hierarchical_moe(h2l, *moe_w).reshape(xl.shape)
        else:
            xc = xc + g1c * (jnp.concatenate([a_ctx, b_ctx, c_ctxo], axis=-1) @ w_out[layer])
            h2c = (rms_norm(xc, norm2_gain[layer]) * (1.0 + sc2c) + sh2c).reshape(-1, D_MODEL)
            y = hierarchical_moe(jnp.concatenate([h2c, h2l], axis=0), *moe_w)
            xc = xc + g2c * y[:bsz * n_ctx].reshape(xc.shape)
            xl = xl + g2l * y[bsz * n_ctx:].reshape(xl.shape)
    return xl.astype(x.dtype)
```

```python
import functools

import jax
import jax.numpy as jnp
import numpy as np
from jax import lax
from jax.experimental import pallas as pl
from jax.experimental.pallas import tpu as pltpu

F32 = jnp.float32
BF16 = jnp.bfloat16

D_MODEL = 1024
DEPTH = 4
GRID_W = 64
HEAD_DIM = 64
A_HEADS = 4
B_HEADS = 4
C_HEADS = 8
A_WIDTH = A_HEADS * HEAD_DIM
B_WIDTH = B_HEADS * HEAD_DIM
C_WIDTH = C_HEADS * HEAD_DIM
HGRN_CHUNK = 16
HGRN_F_FLOOR = 1e-20
RWKV_W_RANK = 64
RWKV_A_RANK = 64
RWKV_G_RANK = 128
RWKV_LN_EPS = 64e-5
NA_WIN_ROWS = 8
NA_WIN_COLS = 16
ROPE_THETA = 10000.0
N_GROUPS = 4
EXPERTS_PER_GROUP = 8
N_EXPERTS = N_GROUPS * EXPERTS_PER_GROUP
D_EXPERT = 512
EPS = 1e-6
NEG_INF = -1e30
A_PROJ = 5 * A_WIDTH
B_PROJ = 3 * B_WIDTH + 2 * RWKV_W_RANK + 2 * RWKV_A_RANK + RWKV_G_RANK
C_PROJ = 3 * C_WIDTH
P_TOTAL = A_PROJ + B_PROJ + C_PROJ
B_TAIL = B_PROJ - 3 * B_WIDTH

LANES = 128
TOKEN_TILE = 256
RWKV_CHUNK = 64
RWKV_BLOCK = 128
HGRN_BLOCK = 128
VMEM_LIMIT = 56 * 1024 * 1024

_HI = lax.Precision.HIGHEST


def _cparams(*sem):
    return pltpu.CompilerParams(dimension_semantics=sem, vmem_limit_bytes=VMEM_LIMIT)


def _mm(a, b):
    return jnp.dot(a, b, preferred_element_type=F32)


def _mm_hi(a, b):
    return jnp.dot(a, b, preferred_element_type=F32, precision=_HI)


def _nt(a, b):
    return lax.dot_general(a, b, (((1,), (1,)), ((), ())), preferred_element_type=F32)


def _nt_hi(a, b):
    return lax.dot_general(a, b, (((1,), (1,)), ((), ())), preferred_element_type=F32, precision=_HI)


def _tn(a, b):
    return lax.dot_general(a, b, (((0,), (0,)), ((), ())), preferred_element_type=F32)


def _tn_hi(a, b):
    return lax.dot_general(a, b, (((0,), (0,)), ((), ())), preferred_element_type=F32, precision=_HI)


def _split3(a):
    a1 = a.astype(BF16)
    r1 = a - a1.astype(F32)
    a2 = r1.astype(BF16)
    a3 = (r1 - a2.astype(F32)).astype(BF16)
    return a1, a2, a3


def _mm_x3(a, m):
    a1, a2, a3 = _split3(a)
    return _mm(a1, m) + _mm(a2, m) + _mm(a3, m)


def _mm_l3(m, a):
    a1, a2, a3 = _split3(a)
    return _mm(m, a1) + _mm(m, a2) + _mm(m, a3)


def _split2(a):
    a1 = a.astype(BF16)
    return a1, (a - a1.astype(F32)).astype(BF16)


def _x3(dot, a, b):
    a1, a2 = _split2(a)
    b1, b2 = _split2(b)
    return dot(a1, b1) + (dot(a1, b2) + dot(a2, b1))


def _sigmoid(x):
    return 1.0 / (1.0 + jnp.exp(-x))


def _iota(shape, dim):
    return lax.broadcasted_iota(jnp.int32, shape, dim)


def _head_seg(n):
    return jnp.where(_iota((n, n), 0) // HEAD_DIM == _iota((n, n), 1) // HEAD_DIM, 1.0, 0.0).astype(BF16)


def _ada_body(c_ref, w_ref, b_ref, o_ref):
    c = c_ref[...]
    o_ref[...] = _mm_hi(c * _sigmoid(c), w_ref[...]) + b_ref[...]


def _ada_mod(cc, w_ada, b_ada):
    rows = cc.shape[0]
    tn = 1536
    return pl.pallas_call(
        _ada_body,
        grid=(DEPTH, 6 * D_MODEL // tn),
        in_specs=[
            pl.BlockSpec((rows, D_MODEL), lambda l, j: (0, 0)),
            pl.BlockSpec((None, D_MODEL, tn), lambda l, j: (l, 0, j)),
            pl.BlockSpec((None, 1, tn), lambda l, j: (l, 0, j)),
        ],
        out_specs=pl.BlockSpec((None, rows, tn), lambda l, j: (l, 0, j)),
        out_shape=jax.ShapeDtypeStruct((DEPTH, rows, 6 * D_MODEL), F32),
        compiler_params=_cparams("arbitrary", "arbitrary"),
        name="ada_mod",
    )(cc, w_ada, b_ada.reshape(DEPTH, 1, 6 * D_MODEL))


def _mod_row(i, blocks_per_seq, ctx_blocks, ctx_row):
    return jnp.where(i % blocks_per_seq < ctx_blocks, ctx_row, i // blocks_per_seq)


def _inproj_body(x_ref, m_ref, g_ref, w_ref, pa_ref, pb_ref, pc_ref):
    x = x_ref[...]
    h = x * lax.rsqrt(jnp.mean(x * x, axis=-1, keepdims=True) + EPS) * g_ref[...]
    h = (h * (1.0 + m_ref[1:2, :]) + m_ref[0:1, :]).astype(BF16)
    pa_ref[...] = _mm(h, w_ref[:, :A_PROJ])
    pb_ref[...] = _mm(h, w_ref[:, A_PROJ:A_PROJ + B_PROJ])
    pc_ref[...] = _mm(h, w_ref[:, A_PROJ + B_PROJ:])


def _inproj(x, mod_l, gain, w_bf, *, bps, ctx_blocks, ctx_row):
    n = x.shape[0]
    tm = TOKEN_TILE
    row = functools.partial(_mod_row, blocks_per_seq=bps, ctx_blocks=ctx_blocks, ctx_row=ctx_row)
    return pl.pallas_call(
        _inproj_body,
        grid=(n // tm,),
        in_specs=[
            pl.BlockSpec((tm, D_MODEL), lambda i: (i, 0)),
            pl.BlockSpec((None, 6, D_MODEL), lambda i: (row(i), 0, 0)),
            pl.BlockSpec((1, D_MODEL), lambda i: (0, 0)),
            pl.BlockSpec((D_MODEL, P_TOTAL), lambda i: (0, 0)),
        ],
        out_specs=[
            pl.BlockSpec((tm, A_PROJ), lambda i: (i, 0)),
            pl.BlockSpec((tm, B_PROJ), lambda i: (i, 0)),
            pl.BlockSpec((tm, C_PROJ), lambda i: (i, 0)),
        ],
        out_shape=[
            jax.ShapeDtypeStruct((n, A_PROJ), F32),
            jax.ShapeDtypeStruct((n, B_PROJ), F32),
            jax.ShapeDtypeStruct((n, C_PROJ), F32),
        ],
        compiler_params=_cparams("arbitrary"),
        name="inproj",
    )(x, mod_l, gain.reshape(1, D_MODEL), w_bf)


def _hgrn_body(q_ref, ff_ref, fb_ref, i_ref, g_ref, lb_ref, gn_ref, o_ref,
               qin_s, kout_s, cum_s, acc_s, *, t_ctx):
    t_tot = q_ref.shape[0]
    c = HGRN_CHUNK
    rb = HGRN_BLOCK
    n_blk = t_tot // rb
    n_chunk = t_tot // c
    n_chunk_ctx = t_ctx // c
    row = _iota((rb, rb), 0)
    col = _iota((rb, rb), 1)
    same = (row // c) == (col // c)
    same_bf = jnp.where(same, 1.0, 0.0).astype(BF16)
    seg = _head_seg(LANES)
    pos = _iota((rb, LANES), 0) % c
    blockdiag = _iota((LANES, LANES), 0) // HEAD_DIM == _iota((LANES, LANES), 1) // HEAD_DIM

    acc_s[...] = jnp.zeros_like(acc_s)

    for d in range(2):
        rev = d == 1
        f_ref = fb_ref if rev else ff_ref
        lb = lb_ref[d:d + 1, :]
        tri_bf = jnp.where(same & ((col >= row) if rev else (col <= row)), 1.0, 0.0).astype(BF16)

        def block(b, carry, rev=rev, f_ref=f_ref, lb=lb, tri_bf=tri_bf, d=d):
            sl = pl.ds(pl.multiple_of(b * rb, rb), rb)
            fpre = f_ref[sl, :]
            q = q_ref[sl, :]
            v = i_ref[sl, :]
            f = lb + (1.0 - lb) * _sigmoid(fpre)
            logf = jnp.log(jnp.maximum(f, HGRN_F_FLOOR))
            k = (1.0 - lb) * _sigmoid(-fpre)
            cum = _mm_l3(tri_bf, logf)
            tot = _mm_l3(same_bf, logf)
            o = jnp.zeros((rb, LANES), F32)
            for j in range(c):
                shift = (rb - j) % rb if rev else j
                valid = (pos <= c - 1 - j) if rev else (pos >= j)
                if j == 0:
                    ks, cs, vs = k, cum, v
                else:
                    ks = pltpu.roll(k, shift, 0)
                    cs = pltpu.roll(cum, shift, 0)
                    vs = pltpu.roll(v, shift, 0)
                dec = jnp.exp(jnp.where(valid, cum - cs, 0.0))
                prod = jnp.where(valid, q * ks * dec, 0.0)
                o = o + _mm_x3(prod, seg) * vs
            acc_s[sl, :] += o
            qin_s[d, sl, :] = q * jnp.exp(cum)
            kout_s[d, sl, :] = k * jnp.exp(tot - cum)
            cum_s[d, sl, :] = tot
            return carry

        lax.fori_loop(0, n_blk, block, 0)

    def step(i, carry):
        s_f, s_b = carry
        n_f = i
        n_b = jnp.where(i < n_chunk_ctx, n_chunk_ctx - 1 - i, n_chunk - 1 - (i - n_chunk_ctx))
        outs = []
        for d, n, s in ((0, n_f, s_f), (1, n_b, s_b)):
            sl = pl.ds(pl.multiple_of(n * c, c), c)
            qin = qin_s[d, sl, :].astype(BF16)
            kout = kout_s[d, sl, :].astype(BF16)
            v = i_ref[sl, :].astype(BF16)
            dec = jnp.exp(cum_s[d, pl.ds(pl.multiple_of(n * c, c), 1), :])
            acc_s[sl, :] += _nt(qin, s.astype(BF16))
            kv = jnp.where(blockdiag, _tn(v, kout), 0.0)
            outs.append(s * dec + kv)
        return tuple(outs)

    zero = jnp.zeros((LANES, LANES), F32)
    lax.fori_loop(0, n_chunk, step, (zero, zero))

    def readout(b, carry):
        sl = pl.ds(pl.multiple_of(b * rb, rb), rb)
        o = acc_s[sl, :]
        ms = _mm_x3(o * o, seg) * (1.0 / HEAD_DIM)
        g = g_ref[sl, :]
        o_ref[sl, :] = o * lax.rsqrt(ms + EPS) * gn_ref[...] * (g * _sigmoid(g))
        return carry

    lax.fori_loop(0, n_blk, readout, 0)


def _hgrn(pa, lb, gn_gain, *, t_ctx):
    bsz, t_tot, _ = pa.shape
    n_hp = A_WIDTH // LANES

    def sec(s):
        return pl.BlockSpec((None, t_tot, LANES), lambda b, h, s=s: (b, 0, s * n_hp + h))

    return pl.pallas_call(
        functools.partial(_hgrn_body, t_ctx=t_ctx),
        grid=(bsz, n_hp),
        in_specs=[sec(0), sec(1), sec(2), sec(3), sec(4),
                  pl.BlockSpec((2, LANES), lambda b, h: (0, h)),
                  pl.BlockSpec((1, LANES), lambda b, h: (0, h))],
        out_specs=pl.BlockSpec((None, t_tot, LANES), lambda b, h: (b, 0, h)),
        out_shape=jax.ShapeDtypeStruct((bsz, t_tot, A_WIDTH), F32),
        scratch_shapes=[
            pltpu.VMEM((2, t_tot, LANES), F32),
            pltpu.VMEM((2, t_tot, LANES), F32),
            pltpu.VMEM((2, t_tot, LANES), F32),
            pltpu.VMEM((t_tot, LANES), F32),
        ],
        compiler_params=_cparams("arbitrary", "arbitrary"),
        name="hgrn",
    )(pa, pa, pa, pa, pa, lb, gn_gain.reshape(1, A_WIDTH))


def _rwkv_prep_body(x_ref, xp_ref, xn_ref, mu_ref, w0_ref, wup_ref, a0_ref, aup_ref, gup_ref,
                    kks_ref, ka_ref, rk_ref,
                    r_o, v_o, kk_o, kf_o, kb_o, bf_o, bb_o, lwf_o, lwb_o, g_o, bonus_o, *, t_ctx, t_tot):
    tm = x_ref.shape[0]
    r0 = pl.program_id(1) * tm
    prev_ok = jnp.logical_and(r0 != 0, r0 != t_ctx)
    next_ok = jnp.logical_and(r0 + tm != t_ctx, r0 + tm != t_tot)
    first = _iota((tm, 1), 0) == 0
    last = _iota((tm, 1), 0) == tm - 1

    def shifted(lo, hi):
        x = x_ref[:, lo:hi]
        p_row = jnp.where(prev_ok, xp_ref[7:8, lo:hi], 0.0)
        n_row = jnp.where(next_ok, xn_ref[0:1, lo:hi], 0.0)
        prev = jnp.where(first, p_row, pltpu.roll(x, 1, 0))
        nxt = jnp.where(last, n_row, pltpu.roll(x, tm - 1, 0))
        return x + (0.5 * (prev + nxt) - x) * mu_ref[:, lo:hi]

    bw = B_WIDTH
    r = shifted(0, bw)
    k = shifted(bw, 2 * bw)
    v = shifted(2 * bw, 3 * bw)
    tail = shifted(3 * bw, B_PROJ)
    seg = _head_seg(bw)

    kk = k * kks_ref[...]
    kk = kk / jnp.maximum(jnp.sqrt(_mm_x3(kk * kk, seg)), 1e-12)
    ksum = jnp.zeros_like(k)
    for d, (k_o, b_o, lw_o) in enumerate(((kf_o, bf_o, lwf_o), (kb_o, bb_o, lwb_o))):
        wd = tail[:, d * RWKV_W_RANK:(d + 1) * RWKV_W_RANK]
        ad = tail[:, 2 * RWKV_W_RANK + d * RWKV_A_RANK:2 * RWKV_W_RANK + (d + 1) * RWKV_A_RANK]
        u = -(w0_ref[d:d + 1, :] + _mm_hi(jnp.tanh(wd), wup_ref[d]))
        w = -(jnp.maximum(u, 0.0) + jnp.log(1.0 + jnp.exp(-jnp.abs(u)))) - 0.5
        a = _sigmoid(a0_ref[d:d + 1, :] + _mm_hi(ad, aup_ref[d]))
        k_d = k * (1.0 + (a - 1.0) * ka_ref[...])
        ksum = ksum + k_d
        k_o[...] = k_d
        b_o[...] = a * kk
        lw_o[...] = -jnp.exp(w)
    gd = tail[:, 2 * RWKV_W_RANK + 2 * RWKV_A_RANK:]
    r_o[...] = r
    v_o[...] = v
    kk_o[...] = kk
    g_o[...] = _mm_hi(_sigmoid(gd), gup_ref[...])
    bonus_o[...] = _mm_x3(r * ksum * rk_ref[...], seg) * v


def _rwkv_prep(pb, mu, w0, w_up, a0, a_up, g_up, kk_scale, k_a, r_k, *, t_ctx):
    bsz, t_tot, _ = pb.shape
    tm = TOKEN_TILE
    nb8 = t_tot // 8
    per8 = tm // 8
    bw = B_WIDTH
    full = lambda shape: pl.BlockSpec(shape, lambda b, i: (0,) * len(shape))
    out = jax.ShapeDtypeStruct((bsz, t_tot, bw), F32)
    return pl.pallas_call(
        functools.partial(_rwkv_prep_body, t_ctx=t_ctx, t_tot=t_tot),
        grid=(bsz, t_tot // tm),
        in_specs=[
            pl.BlockSpec((None, tm, B_PROJ), lambda b, i: (b, i, 0)),
            pl.BlockSpec((None, 8, B_PROJ), lambda b, i: (b, jnp.maximum(i * per8 - 1, 0), 0)),
            pl.BlockSpec((None, 8, B_PROJ), lambda b, i: (b, jnp.minimum((i + 1) * per8, nb8 - 1), 0)),
            full((1, B_PROJ)), full((2, bw)), full((2, RWKV_W_RANK, bw)), full((2, bw)),
            full((2, RWKV_A_RANK, bw)), full((RWKV_G_RANK, bw)), full((1, bw)), full((1, bw)), full((1, bw)),
        ],
        out_specs=[pl.BlockSpec((None, tm, bw), lambda b, i: (b, i, 0))] * 11,
        out_shape=[out] * 11,
        compiler_params=_cparams("arbitrary", "arbitrary"),
        name="rwkv_prep",
    )(pb, pb, pb, mu.reshape(1, B_PROJ), w0, w_up, a0, a_up, g_up,
      kk_scale.reshape(1, bw), k_a.reshape(1, bw), r_k.reshape(1, bw))


def _rwkv_scan_body(r_ref, v_ref, kk_ref, kf_ref, kb_ref, bf_ref, bb_ref, lwf_ref, lwb_ref, y_ref,
                    rq_s, y0_s, p_s, z_s, *, t_ctx):
    t_tot = r_ref.shape[0]
    c = RWKV_CHUNK
    rb = RWKV_BLOCK
    n_blk = t_tot // rb
    n_chunk = t_tot // c
    n_chunk_ctx = t_ctx // c
    cpb = rb // c
    n_double = c.bit_length() - 2
    row = _iota((rb, rb), 0)
    col = _iota((rb, rb), 1)
    same = (row // c) == (col // c)
    same_bf = jnp.where(same, 1.0, 0.0).astype(BF16)
    eye = jnp.where(row == col, 1.0, 0.0)
    lane = _iota((1, LANES), 1)
    head_masks = [jnp.where(lane // HEAD_DIM == h, 1.0, 0.0) for h in range(LANES // HEAD_DIM)]
    r128 = _iota((LANES, LANES), 0)
    c128 = _iota((LANES, LANES), 1)
    blockdiag = (r128 // HEAD_DIM) == (c128 // HEAD_DIM)
    diag128 = r128 == c128

    for d in range(2):
        rev = d == 1
        k_ref, b_ref, lw_ref = (kb_ref, bb_ref, lwb_ref) if rev else (kf_ref, bf_ref, lwf_ref)
        incl = same & ((col >= row) if rev else (col <= row))
        strict = same & ((col > row) if rev else (col < row))
        incl_bf = jnp.where(incl, 1.0, 0.0).astype(BF16)

        def block(bi, carry, d=d, k_ref=k_ref, b_ref=b_ref, lw_ref=lw_ref, incl=incl, strict=strict,
                  incl_bf=incl_bf):
            sl = pl.ds(pl.multiple_of(bi * rb, rb), rb)
            lw = lw_ref[sl, :]
            r = r_ref[sl, :]
            v = v_ref[sl, :]
            kk = kk_ref[sl, :]
            k = k_ref[sl, :]
            b = b_ref[sl, :]
            cl = _mm_l3(incl_bf, lw)
            tot = _mm_l3(same_bf, lw)
            w_inv = jnp.exp(-cl)
            w_end = jnp.exp(tot - cl)
            kk_d = kk * jnp.exp(cl - lw)
            r_d = r * jnp.exp(cl)
            k_i = k * w_inv
            b_i = b * w_inv
            kkt = jnp.zeros((rb, LANES), F32)
            u = jnp.zeros((rb, LANES), F32)
            rq = jnp.zeros((rb, LANES), F32)
            y0 = jnp.zeros((rb, LANES), F32)
            for mh in head_masks:
                kkm = kk_d * mh
                rm = r_d * mh
                a_kb = jnp.where(strict, _nt_hi(kkm, k_i), 0.0)
                a_bb = jnp.where(strict, _nt_hi(kkm, b_i), 0.0)
                a_rk = jnp.where(incl, _nt_hi(rm, k_i), 0.0)
                a_rb = jnp.where(incl, _nt_hi(rm, b_i), 0.0)
                m_pow = -a_bb
                t_inv = eye + m_pow
                for _ in range(n_double):
                    m_pow = _mm_hi(m_pow, m_pow)
                    t_inv = _mm_hi(t_inv, eye + m_pow)
                kkt_h = _mm_hi(t_inv, kkm)
                u_h = _mm_hi(t_inv, _mm_hi(a_kb, v))
                kkt = kkt + kkt_h
                u = u + u_h * mh
                rq = rq + rm - _mm_hi(a_rb, kkt_h)
                y0 = y0 + (_mm_hi(a_rk, v) - _mm_hi(a_rb, u_h)) * mh
            rq_s[d, sl, :] = rq
            y0_s[d, sl, :] = y0
            k_e = k * w_end
            b_e = b * w_end
            for n in range(cpb):
                rows = slice(n * c, (n + 1) * c)
                wc = jnp.exp(tot[n * c:n * c + 1, :])
                p = jnp.where(diag128, wc, 0.0) - jnp.where(blockdiag, _tn_hi(kkt[rows], b_e[rows]), 0.0)
                z = jnp.where(blockdiag, _tn_hi(v[rows], k_e[rows]) - _tn_hi(u[rows], b_e[rows]), 0.0)
                p_s[d, bi * cpb + n] = p
                z_s[d, bi * cpb + n] = z
            return carry

        lax.fori_loop(0, n_blk, block, 0)

    y_ref[...] = jnp.zeros_like(y_ref)

    def step(i, carry):
        s_f, s_b = carry
        n_b = jnp.where(i < n_chunk_ctx, n_chunk_ctx - 1 - i, n_chunk - 1 - (i - n_chunk_ctx))
        outs = []
        for d, n, s in ((0, i, s_f), (1, n_b, s_b)):
            sl = pl.ds(pl.multiple_of(n * c, c), c)
            y_ref[sl, :] += _x3(_nt, rq_s[d, sl, :], s) + y0_s[d, sl, :]
            outs.append(_x3(_mm, s, p_s[d, n]) + z_s[d, n])
        return tuple(outs)

    zero = jnp.zeros((LANES, LANES), F32)
    lax.fori_loop(0, n_chunk, step, (zero, zero))


def _rwkv_scan(r, v, kk, kf, kb, bf, bb, lwf, lwb, *, t_ctx):
    bsz, t_tot, bw = r.shape
    n_chunk = t_tot // RWKV_CHUNK
    spec = pl.BlockSpec((None, t_tot, LANES), lambda b, h: (b, 0, h))
    return pl.pallas_call(
        functools.partial(_rwkv_scan_body, t_ctx=t_ctx),
        grid=(bsz, bw // LANES),
        in_specs=[spec] * 9,
        out_specs=spec,
        out_shape=jax.ShapeDtypeStruct((bsz, t_tot, bw), F32),
        scratch_shapes=[
            pltpu.VMEM((2, t_tot, LANES), F32),
            pltpu.VMEM((2, t_tot, LANES), F32),
            pltpu.VMEM((2, n_chunk, LANES, LANES), F32),
            pltpu.VMEM((2, n_chunk, LANES, LANES), F32),
        ],
        compiler_params=_cparams("arbitrary", "arbitrary"),
        name="rwkv_scan",
    )(r, v, kk, kf, kb, bf, bb, lwf, lwb)


def _rwkv_readout_body(y_ref, g_ref, bonus_ref, lng_ref, lnb_ref, o_ref):
    seg = _head_seg(B_WIDTH)
    y = y_ref[...]
    mean = _mm_x3(y, seg) * (1.0 / HEAD_DIM)
    yc = y - mean
    var = _mm_x3(yc * yc, seg) * (1.0 / HEAD_DIM)
    yn = yc * lax.rsqrt(var + RWKV_LN_EPS) * lng_ref[...] + lnb_ref[...]
    o_ref[...] = (yn + bonus_ref[...]) * g_ref[...]


def _rwkv_readout(y, g, bonus, ln_gain, ln_bias):
    n, bw = y.shape
    tm = 512
    spec = pl.BlockSpec((tm, bw), lambda i: (i, 0))
    vec = pl.BlockSpec((1, bw), lambda i: (0, 0))
    return pl.pallas_call(
        _rwkv_readout_body,
        grid=(n // tm,),
        in_specs=[spec, spec, spec, vec, vec],
        out_specs=spec,
        out_shape=jax.ShapeDtypeStruct((n, bw), F32),
        compiler_params=_cparams("arbitrary"),
        name="rwkv_readout",
    )(y, g, bonus, ln_gain.reshape(1, bw), ln_bias.reshape(1, bw))


def _na_body(q_ref, k_ref, v_ref, qg_ref, kg_ref, cos_ref, sin_ref, bias_ref, o_ref,
             qs, ks, vs, *, t_ctx, need_ctx):
    t_tot = q_ref.shape[0]
    t_lat = t_tot - t_ctx
    rows = t_lat // GRID_W
    win_r = min(NA_WIN_ROWS, rows)
    scale = HEAD_DIM ** -0.5
    seg = _head_seg(LANES)
    lane = _iota((1, LANES), 1)
    head_masks = [jnp.where(lane // HEAD_DIM == h, 1.0, 0.0) for h in range(LANES // HEAD_DIM)]
    half = (lane % HEAD_DIM) < HEAD_DIM // 2
    blk = 256

    def norm_block(i, carry):
        sl = pl.ds(pl.multiple_of(i * blk, blk), blk)
        q = q_ref[sl, :]
        k = k_ref[sl, :]
        q = q * lax.rsqrt(_mm_x3(q * q, seg) * (1.0 / HEAD_DIM) + EPS) * qg_ref[...]
        k = k * lax.rsqrt(_mm_x3(k * k, seg) * (1.0 / HEAD_DIM) + EPS) * kg_ref[...]
        cos = cos_ref[sl, :]
        sin = sin_ref[sl, :]

        def rope(t):
            swapped = jnp.where(half, pltpu.roll(t, LANES - HEAD_DIM // 2, 1), pltpu.roll(t, HEAD_DIM // 2, 1))
            return t * cos + swapped * sin

        qs[sl, :] = (rope(q) * scale).astype(BF16)
        ks[sl, :] = rope(k).astype(BF16)
        vs[sl, :] = v_ref[sl, :].astype(BF16)
        return carry

    lax.fori_loop(0, t_tot // blk, norm_block, 0)

    kc = ks[0:t_ctx, :]
    vc = vs[0:t_ctx, :]

    if need_ctx:
        qc = qs[0:t_ctx, :]
        out = jnp.zeros((t_ctx, LANES), F32)
        for mh in head_masks:
            s = _nt(qc * mh.astype(BF16), kc)
            p = jnp.exp(s - jnp.max(s, axis=-1, keepdims=True))
            o = _mm(p.astype(BF16), vc) / jnp.sum(p, axis=-1, keepdims=True)
            out = out + o * mh
        o_ref[0:t_ctx, :] = out
    else:
        o_ref[0:t_ctx, :] = jnp.zeros((t_ctx, LANES), F32)

    def q_row(r, carry):
        r_start = jnp.clip(r - win_r // 2, 0, rows - win_r)
        q_sl = pl.ds(pl.multiple_of(t_ctx + r * GRID_W, GRID_W), GRID_W)
        k_sl = pl.ds(pl.multiple_of(t_ctx + r_start * GRID_W, GRID_W), win_r * GRID_W)
        q = qs[q_sl, :]
        kw = ks[k_sl, :]
        vw = vs[k_sl, :]
        out = jnp.zeros((GRID_W, LANES), F32)
        for h, mh in enumerate(head_masks):
            qm = q * mh.astype(BF16)
            s_win = _nt(qm, kw) + bias_ref[h, r - r_start]
            s_ctx = _nt(qm, kc)
            m = jnp.maximum(jnp.max(s_win, axis=-1, keepdims=True), jnp.max(s_ctx, axis=-1, keepdims=True))
            p_win = jnp.exp(s_win - m)
            p_ctx = jnp.exp(s_ctx - m)
            den = jnp.sum(p_win, axis=-1, keepdims=True) + jnp.sum(p_ctx, axis=-1, keepdims=True)
            o = (_mm(p_win.astype(BF16), vw) + _mm(p_ctx.astype(BF16), vc)) / den
            out = out + o * mh
        o_ref[q_sl, :] = out
        return carry

    lax.fori_loop(0, rows, q_row, 0)


def _na_tables(t_ctx, t_lat):
    quarter = HEAD_DIM // 4
    pos = np.arange(t_lat)
    inv = ROPE_THETA ** (-np.arange(quarter, dtype=np.float32) / quarter)
    pos_r = (pos // GRID_W).astype(np.float32)
    pos_c = (pos % GRID_W).astype(np.float32)
    return pos_r, pos_c, inv


def _na_bias_table(rpb, rows):
    win_r = min(NA_WIN_ROWS, rows)
    c = np.arange(GRID_W)
    w_start = np.clip(c - NA_WIN_COLS // 2, 0, GRID_W - NA_WIN_COLS)
    kc = np.arange(GRID_W)
    in_win = (kc[None, :] >= w_start[:, None]) & (kc[None, :] < w_start[:, None] + NA_WIN_COLS)
    col_idx = np.clip(kc[None, :] - c[:, None] + NA_WIN_COLS - 1, 0, 2 * NA_WIN_COLS - 2)
    off = np.arange(win_r)
    i = np.arange(win_r)
    row_idx = i[None, :] - off[:, None] + NA_WIN_ROWS - 1
    g = rpb[:, row_idx[:, None, :, None], col_idx[None, :, None, :]]
    g = jnp.where(in_win[None, None, :, None, :], g, NEG_INF)
    return g.reshape(rpb.shape[0], win_r, GRID_W, win_r * GRID_W)


def _na(pc, q_gain, k_gain, rpb, *, t_ctx, need_ctx):
    bsz, t_tot, _ = pc.shape
    t_lat = t_tot - t_ctx
    rows = t_lat // GRID_W
    win_r = min(NA_WIN_ROWS, rows)
    n_hp = C_WIDTH // LANES
    hpl = LANES // HEAD_DIM
    pos_r, pos_c, inv = _na_tables(t_ctx, t_lat)
    ang = np.concatenate([pos_r[:, None] * inv, pos_c[:, None] * inv], axis=-1)
    ang = np.concatenate([np.zeros((t_ctx, HEAD_DIM // 2), np.float32), ang], axis=0)
    cos = np.cos(ang)
    sin = np.sin(ang)
    cos_t = jnp.asarray(np.tile(np.concatenate([cos, cos], axis=-1), (1, hpl)), F32)
    sin_t = jnp.asarray(np.tile(np.concatenate([-sin, sin], axis=-1), (1, hpl)), F32)
    bias = _na_bias_table(rpb, rows).reshape(n_hp, hpl, win_r, GRID_W, win_r * GRID_W)

    def sec(s):
        return pl.BlockSpec((None, t_tot, LANES), lambda b, h, s=s: (b, 0, s * n_hp + h))

    gain = lambda g: jnp.tile(g.reshape(1, HEAD_DIM), (1, hpl))
    return pl.pallas_call(
        functools.partial(_na_body, t_ctx=t_ctx, need_ctx=need_ctx),
        grid=(bsz, n_hp),
        in_specs=[sec(0), sec(1), sec(2),
                  pl.BlockSpec((1, LANES), lambda b, h: (0, 0)),
                  pl.BlockSpec((1, LANES), lambda b, h: (0, 0)),
                  pl.BlockSpec((t_tot, LANES), lambda b, h: (0, 0)),
                  pl.BlockSpec((t_tot, LANES), lambda b, h: (0, 0)),
                  pl.BlockSpec((None, hpl, win_r, GRID_W, win_r * GRID_W), lambda b, h: (h, 0, 0, 0, 0))],
        out_specs=pl.BlockSpec((None, t_tot, LANES), lambda b, h: (b, 0, h)),
        out_shape=jax.ShapeDtypeStruct((bsz, t_tot, C_WIDTH), F32),
        scratch_shapes=[pltpu.VMEM((t_tot, LANES), BF16)] * 3,
        compiler_params=_cparams("arbitrary", "arbitrary"),
        name="na",
    )(pc, pc, pc, gain(q_gain), gain(k_gain), cos_t, sin_t, bias)


def _outproj_body(x_ref, a_ref, b_ref, c_ref, m_ref, g_ref, w_ref, wr_ref, br_ref, xo_ref, h_ref, comb_ref):
    mix = (_mm(a_ref[...].astype(BF16), w_ref[0:A_WIDTH, :])
           + _mm(b_ref[...].astype(BF16), w_ref[A_WIDTH:A_WIDTH + B_WIDTH, :])
           + _mm(c_ref[...].astype(BF16), w_ref[A_WIDTH + B_WIDTH:, :]))
    x = x_ref[...] + m_ref[2:3, :] * mix
    xo_ref[...] = x
    h = x * lax.rsqrt(jnp.mean(x * x, axis=-1, keepdims=True) + EPS) * g_ref[...]
    h = h * (1.0 + m_ref[4:5, :]) + m_ref[3:4, :]
    h_ref[...] = h.astype(BF16)
    logits = _mm_hi(h, wr_ref[...]) + br_ref[...]
    lane = _iota(logits.shape, 1).astype(F32)
    big = float(LANES)
    is_g = (lane >= N_EXPERTS) & (lane < N_EXPERTS + N_GROUPS)
    gl = jnp.where(is_g, logits, -jnp.inf)
    gmax = jnp.max(gl, axis=-1, keepdims=True)
    g_w = 1.0 / jnp.sum(jnp.exp(gl - gmax), axis=-1, keepdims=True)
    g_sel = jnp.min(jnp.where(gl == gmax, lane, big), axis=-1, keepdims=True) - N_EXPERTS
    lo = g_sel * EXPERTS_PER_GROUP
    el = jnp.where((lane >= lo) & (lane < lo + EXPERTS_PER_GROUP), logits, -jnp.inf)
    m1 = jnp.max(el, axis=-1, keepdims=True)
    i1 = jnp.min(jnp.where(el == m1, lane, big), axis=-1, keepdims=True)
    el2 = jnp.where(lane == i1, -jnp.inf, el)
    m2 = jnp.max(el2, axis=-1, keepdims=True)
    i2 = jnp.min(jnp.where(el2 == m2, lane, big), axis=-1, keepdims=True)
    e2 = jnp.exp(m2 - m1)
    w1 = g_w / (1.0 + e2)
    comb_ref[...] = jnp.where(lane == i1, w1, 0.0) + jnp.where(lane == i2, w1 * e2, 0.0)


def _outproj(x, a, b, c, mod_l, gain2, w_bf, w_router, b_router, *, bps, ctx_blocks, ctx_row):
    n = x.shape[0]
    tm = TOKEN_TILE
    row = functools.partial(_mod_row, blocks_per_seq=bps, ctx_blocks=ctx_blocks, ctx_row=ctx_row)
    tok = lambda w: pl.BlockSpec((tm, w), lambda i: (i, 0))
    full = lambda shape: pl.BlockSpec(shape, lambda i: (0,) * len(shape))
    return pl.pallas_call(
        _outproj_body,
        grid=(n // tm,),
        in_specs=[tok(D_MODEL), tok(A_WIDTH), tok(B_WIDTH), tok(C_WIDTH),
                  pl.BlockSpec((None, 6, D_MODEL), lambda i: (row(i), 0, 0)),
                  full((1, D_MODEL)), full((D_MODEL, D_MODEL)), full((D_MODEL, LANES)), full((1, LANES))],
        out_specs=[tok(D_MODEL), tok(D_MODEL), tok(LANES)],
        out_shape=[jax.ShapeDtypeStruct((n, D_MODEL), F32),
                   jax.ShapeDtypeStruct((n, D_MODEL), BF16),
                   jax.ShapeDtypeStruct((n, LANES), F32)],
        compiler_params=_cparams("arbitrary"),
        name="outproj",
    )(x, a, b, c, mod_l, gain2.reshape(1, D_MODEL), w_bf, w_router, b_router)


def _moe_body(h_ref, comb_ref, x_ref, *rest):
    per = len(rest) - 5
    m_refs = rest[:per]
    wg_ref, wu_ref, wd_ref, o_ref, acc_s = rest[per:]
    e = pl.program_id(1)

    @pl.when(e == 0)
    def _():
        acc_s[...] = jnp.zeros_like(acc_s)

    h = h_ref[...]
    gate = _mm(h, wg_ref[...])
    up = _mm(h, wu_ref[...])
    comb = comb_ref[...]
    ce = jnp.sum(jnp.where(_iota(comb.shape, 1) == e, comb, 0.0), axis=-1, keepdims=True)
    hid = gate * _sigmoid(gate) * up * ce
    acc_s[...] += _mm(hid.astype(BF16), wd_ref[...])

    @pl.when(e == N_EXPERTS - 1)
    def _():
        for j, m_ref in enumerate(m_refs):
            sl = slice(j * TOKEN_TILE, (j + 1) * TOKEN_TILE)
            o_ref[sl, :] = x_ref[sl, :] + m_ref[5:6, :] * acc_s[sl, :]


def _moe(h, comb, x, mod_l, wg, wu, wd, layer, *, bps, ctx_blocks, ctx_row):
    n = x.shape[0]
    tm = 512
    per = tm // TOKEN_TILE
    row = functools.partial(_mod_row, blocks_per_seq=bps, ctx_blocks=ctx_blocks, ctx_row=ctx_row)
    mod_specs = [pl.BlockSpec((None, 6, D_MODEL), lambda i, e, j=j: (row(i * per + j), 0, 0)) for j in range(per)]
    return pl.pallas_call(
        _moe_body,
        grid=(n // tm, N_EXPERTS),
        in_specs=[
            pl.BlockSpec((tm, D_MODEL), lambda i, e: (i, 0)),
            pl.BlockSpec((tm, LANES), lambda i, e: (i, 0)),
            pl.BlockSpec((tm, D_MODEL), lambda i, e: (i, 0)),
            *mod_specs,
            pl.BlockSpec((None, None, D_MODEL, D_EXPERT), lambda i, e: (layer, e, 0, 0)),
            pl.BlockSpec((None, None, D_MODEL, D_EXPERT), lambda i, e: (layer, e, 0, 0)),
            pl.BlockSpec((None, None, D_EXPERT, D_MODEL), lambda i, e: (layer, e, 0, 0)),
        ],
        out_specs=pl.BlockSpec((tm, D_MODEL), lambda i, e: (i, 0)),
        out_shape=jax.ShapeDtypeStruct((n, D_MODEL), F32),
        scratch_shapes=[pltpu.VMEM((tm, D_MODEL), F32)],
        compiler_params=_cparams("arbitrary", "arbitrary"),
        name="moe",
    )(h, comb, x, *([mod_l] * per), wg, wu, wd)


def kernel(x, c, ctx, c_ctx, norm1_gain, norm2_gain, w_ada, b_ada, w_in, w_out, hgrn_lb_logits, hgrn_gn_gain, rwkv_mu, rwkv_w0, rwkv_w_up, rwkv_a0, rwkv_a_up, rwkv_g_up, rwkv_kk_scale, rwkv_k_a, rwkv_r_k, rwkv_ln_gain, rwkv_ln_bias, na_q_gain, na_k_gain, na_rpb, w_router_group, b_router_group, w_router_expert, b_router_expert, w_exp_gate, w_exp_up, w_exp_down):
    bsz, t_lat, _ = x.shape
    t_ctx = ctx.shape[1]
    t_tot = t_ctx + t_lat
    n = bsz * t_tot
    assert t_ctx % TOKEN_TILE == 0 and t_lat % TOKEN_TILE == 0 and bsz < 16
    bps = t_tot // TOKEN_TILE
    tile_kw = dict(bps=bps, ctx_blocks=t_ctx // TOKEN_TILE, ctx_row=bsz)

    lb_p = jax.nn.softmax(hgrn_lb_logits.astype(F32), axis=1)
    lower_bounds = jnp.cumsum(lb_p, axis=1) - lb_p[:, :1]

    cc = jnp.zeros((16, D_MODEL), F32).at[:bsz].set(c.astype(F32)).at[bsz].set(c_ctx.astype(F32))
    mod = _ada_mod(cc, w_ada, b_ada).reshape(DEPTH, 16, 6, D_MODEL)

    w_in_bf = w_in.astype(BF16)
    w_out_bf = w_out.astype(BF16)
    wg_bf = w_exp_gate.astype(BF16).reshape(DEPTH, N_EXPERTS, D_MODEL, D_EXPERT)
    wu_bf = w_exp_up.astype(BF16).reshape(DEPTH, N_EXPERTS, D_MODEL, D_EXPERT)
    wd_bf = w_exp_down.astype(BF16).reshape(DEPTH, N_EXPERTS, D_EXPERT, D_MODEL)
    pad = LANES - N_EXPERTS - N_GROUPS
    w_router = jnp.concatenate([w_router_expert, w_router_group,
                                jnp.zeros((DEPTH, D_MODEL, pad), F32)], axis=-1)
    b_router = jnp.concatenate([b_router_expert, b_router_group, jnp.zeros((DEPTH, pad), F32)], axis=-1)

    xs = jnp.concatenate([ctx.astype(F32), x.astype(F32)], axis=1).reshape(n, D_MODEL)
    for layer in range(DEPTH):
        last = layer == DEPTH - 1
        pa, pb, pc = _inproj(xs, mod[layer], norm1_gain[layer], w_in_bf[layer], **tile_kw)
        a_mix = _hgrn(pa.reshape(bsz, t_tot, A_PROJ), lower_bounds[:, layer], hgrn_gn_gain[layer], t_ctx=t_ctx)
        prep = _rwkv_prep(pb.reshape(bsz, t_tot, B_PROJ), rwkv_mu[layer], rwkv_w0[layer], rwkv_w_up[layer],
                          rwkv_a0[layer], rwkv_a_up[layer], rwkv_g_up[layer], rwkv_kk_scale[layer],
                          rwkv_k_a[layer], rwkv_r_k[layer], t_ctx=t_ctx)
        y = _rwkv_scan(*prep[:9], t_ctx=t_ctx)
        b_mix = _rwkv_readout(y.reshape(n, B_WIDTH), prep[9].reshape(n, B_WIDTH), prep[10].reshape(n, B_WIDTH),
                              rwkv_ln_gain[layer], rwkv_ln_bias[layer])
        c_mix = _na(pc.reshape(bsz, t_tot, C_PROJ), na_q_gain[layer], na_k_gain[layer], na_rpb[layer],
                    t_ctx=t_ctx, need_ctx=not last)
        xs, h2, comb = _outproj(xs, a_mix.reshape(n, A_WIDTH), b_mix, c_mix.reshape(n, C_WIDTH), mod[layer],
                                norm2_gain[layer], w_out_bf[layer], w_router[layer],
                                b_router[layer].reshape(1, LANES), **tile_kw)
        xs = _moe(h2, comb, xs, mod[layer], wg_bf, wu_bf, wd_bf, layer, **tile_kw)
    return xs.reshape(bsz, t_tot, D_MODEL)[:, t_ctx:].astype(x.dtype)
```

```python
import functools

import jax
import jax.numpy as jnp
import numpy as np
from jax import lax
from jax.experimental import pallas as pl
from jax.experimental.pallas import tpu as pltpu

F32 = jnp.float32
BF16 = jnp.bfloat16

D_MODEL = 1024
DEPTH = 4
GRID_W = 64
HEAD_DIM = 64
A_HEADS = 4
B_HEADS = 4
C_HEADS = 8
A_WIDTH = A_HEADS * HEAD_DIM
B_WIDTH = B_HEADS * HEAD_DIM
C_WIDTH = C_HEADS * HEAD_DIM
HGRN_CHUNK = 16
HGRN_F_FLOOR = 1e-20
RWKV_W_RANK = 64
RWKV_A_RANK = 64
RWKV_G_RANK = 128
RWKV_LN_EPS = 64e-5
NA_WIN_ROWS = 8
NA_WIN_COLS = 16
ROPE_THETA = 10000.0
N_GROUPS = 4
EXPERTS_PER_GROUP = 8
N_EXPERTS = N_GROUPS * EXPERTS_PER_GROUP
D_EXPERT = 512
EPS = 1e-6
NEG_INF = -1e30
A_PROJ = 5 * A_WIDTH
B_PROJ = 3 * B_WIDTH + 2 * RWKV_W_RANK + 2 * RWKV_A_RANK + RWKV_G_RANK
C_PROJ = 3 * C_WIDTH
P_TOTAL = A_PROJ + B_PROJ + C_PROJ
B_TAIL = B_PROJ - 3 * B_WIDTH

LANES = 128
TOKEN_TILE = 256
RWKV_CHUNK = 64
RWKV_BLOCK = 128
HGRN_BLOCK = 128
VMEM_LIMIT = 56 * 1024 * 1024

_HI = lax.Precision.HIGHEST


def _cparams(*sem):
    return pltpu.CompilerParams(dimension_semantics=sem, vmem_limit_bytes=VMEM_LIMIT)


def _mm(a, b):
    return jnp.dot(a, b, preferred_element_type=F32)


def _mm_hi(a, b):
    return jnp.dot(a, b, preferred_element_type=F32, precision=_HI)


def _nt(a, b):
    return lax.dot_general(a, b, (((1,), (1,)), ((), ())), preferred_element_type=F32)


def _nt_hi(a, b):
    return lax.dot_general(a, b, (((1,), (1,)), ((), ())), preferred_element_type=F32, precision=_HI)


def _tn(a, b):
    return lax.dot_general(a, b, (((0,), (0,)), ((), ())), preferred_element_type=F32)


def _tn_hi(a, b):
    return lax.dot_general(a, b, (((0,), (0,)), ((), ())), preferred_element_type=F32, precision=_HI)


def _split3(a):
    a1 = a.astype(BF16)
    r1 = a - a1.astype(F32)
    a2 = r1.astype(BF16)
    a3 = (r1 - a2.astype(F32)).astype(BF16)
    return a1, a2, a3


def _mm_x3(a, m):
    a1, a2, a3 = _split3(a)
    return _mm(a1, m) + _mm(a2, m) + _mm(a3, m)


def _mm_l3(m, a):
    a1, a2, a3 = _split3(a)
    return _mm(m, a1) + _mm(m, a2) + _mm(m, a3)


def _split2(a):
    a1 = a.astype(BF16)
    return a1, (a - a1.astype(F32)).astype(BF16)


def _x3p(dot, a, b):
    return dot(a[0], b[0]) + (dot(a[0], b[1]) + dot(a[1], b[0]))


def _x3(dot, a, b):
    return _x3p(dot, _split2(a), _split2(b))


def _sigmoid(x):
    return 1.0 / (1.0 + jnp.exp(-x))


def _iota(shape, dim):
    return lax.broadcasted_iota(jnp.int32, shape, dim)


def _head_seg(n):
    return jnp.where(_iota((n, n), 0) // HEAD_DIM == _iota((n, n), 1) // HEAD_DIM, 1.0, 0.0).astype(BF16)


def _ada_body(c_ref, w_ref, b_ref, o_ref):
    c = c_ref[...]
    o_ref[...] = _mm_hi(c * _sigmoid(c), w_ref[...]) + b_ref[...]


def _ada_mod(cc, w_ada, b_ada):
    rows = cc.shape[0]
    tn = 1536
    return pl.pallas_call(
        _ada_body,
        grid=(DEPTH, 6 * D_MODEL // tn),
        in_specs=[
            pl.BlockSpec((rows, D_MODEL), lambda l, j: (0, 0)),
            pl.BlockSpec((None, D_MODEL, tn), lambda l, j: (l, 0, j)),
            pl.BlockSpec((None, 1, tn), lambda l, j: (l, 0, j)),
        ],
        out_specs=pl.BlockSpec((None, rows, tn), lambda l, j: (l, 0, j)),
        out_shape=jax.ShapeDtypeStruct((DEPTH, rows, 6 * D_MODEL), F32),
        compiler_params=_cparams("arbitrary", "arbitrary"),
        name="ada_mod",
    )(cc, w_ada, b_ada.reshape(DEPTH, 1, 6 * D_MODEL))


def _mod_row(i, blocks_per_seq, ctx_blocks, ctx_row):
    return jnp.where(i % blocks_per_seq < ctx_blocks, ctx_row, i // blocks_per_seq)


def _inproj_body(x_ref, m_ref, g_ref, w_ref, pa_ref, pb_ref, pc_ref):
    x = x_ref[...]
    h = x * lax.rsqrt(jnp.mean(x * x, axis=-1, keepdims=True) + EPS) * g_ref[...]
    h = (h * (1.0 + m_ref[1:2, :]) + m_ref[0:1, :]).astype(BF16)
    pa_ref[...] = _mm(h, w_ref[:, :A_PROJ])
    pb_ref[...] = _mm(h, w_ref[:, A_PROJ:A_PROJ + B_PROJ])
    pc_ref[...] = _mm(h, w_ref[:, A_PROJ + B_PROJ:])


def _inproj(x, mod_l, gain, w_bf, *, bps, ctx_blocks, ctx_row):
    n = x.shape[0]
    tm = TOKEN_TILE
    row = functools.partial(_mod_row, blocks_per_seq=bps, ctx_blocks=ctx_blocks, ctx_row=ctx_row)
    return pl.pallas_call(
        _inproj_body,
        grid=(n // tm,),
        in_specs=[
            pl.BlockSpec((tm, D_MODEL), lambda i: (i, 0)),
            pl.BlockSpec((None, 6, D_MODEL), lambda i: (row(i), 0, 0)),
            pl.BlockSpec((1, D_MODEL), lambda i: (0, 0)),
            pl.BlockSpec((D_MODEL, P_TOTAL), lambda i: (0, 0)),
        ],
        out_specs=[
            pl.BlockSpec((tm, A_PROJ), lambda i: (i, 0)),
            pl.BlockSpec((tm, B_PROJ), lambda i: (i, 0)),
            pl.BlockSpec((tm, C_PROJ), lambda i: (i, 0)),
        ],
        out_shape=[
            jax.ShapeDtypeStruct((n, A_PROJ), F32),
            jax.ShapeDtypeStruct((n, B_PROJ), F32),
            jax.ShapeDtypeStruct((n, C_PROJ), F32),
        ],
        compiler_params=_cparams("arbitrary"),
        name="inproj",
    )(x, mod_l, gain.reshape(1, D_MODEL), w_bf)


def _hgrn_body(q_ref, ff_ref, fb_ref, i_ref, g_ref, lb_ref, gn_ref, o_ref,
               qin_s, kout_s, cum_s, acc_s, *, t_ctx):
    t_tot = q_ref.shape[0]
    c = HGRN_CHUNK
    rb = HGRN_BLOCK
    n_blk = t_tot // rb
    n_chunk = t_tot // c
    n_chunk_ctx = t_ctx // c
    row = _iota((rb, rb), 0)
    col = _iota((rb, rb), 1)
    same = (row // c) == (col // c)
    same_bf = jnp.where(same, 1.0, 0.0).astype(BF16)
    seg = _head_seg(LANES)
    pos = _iota((rb, LANES), 0) % c
    blockdiag = _iota((LANES, LANES), 0) // HEAD_DIM == _iota((LANES, LANES), 1) // HEAD_DIM

    acc_s[...] = jnp.zeros_like(acc_s)

    for d in range(2):
        rev = d == 1
        f_ref = fb_ref if rev else ff_ref
        lb = lb_ref[d:d + 1, :]
        tri_bf = jnp.where(same & ((col >= row) if rev else (col <= row)), 1.0, 0.0).astype(BF16)

        def block(b, carry, rev=rev, f_ref=f_ref, lb=lb, tri_bf=tri_bf, d=d):
            sl = pl.ds(pl.multiple_of(b * rb, rb), rb)
            fpre = f_ref[sl, :]
            q = q_ref[sl, :]
            v = i_ref[sl, :]
            f = lb + (1.0 - lb) * _sigmoid(fpre)
            logf = jnp.log(jnp.maximum(f, HGRN_F_FLOOR))
            k = (1.0 - lb) * _sigmoid(-fpre)
            cum = _mm_l3(tri_bf, logf)
            tot = _mm_l3(same_bf, logf)
            o = jnp.zeros((rb, LANES), F32)
            for j in range(c):
                shift = (rb - j) % rb if rev else j
                valid = (pos <= c - 1 - j) if rev else (pos >= j)
                if j == 0:
                    ks, cs, vs = k, cum, v
                else:
                    ks = pltpu.roll(k, shift, 0)
                    cs = pltpu.roll(cum, shift, 0)
                    vs = pltpu.roll(v, shift, 0)
                dec = jnp.exp(jnp.where(valid, cum - cs, 0.0))
                prod = jnp.where(valid, q * ks * dec, 0.0)
                o = o + _mm_x3(prod, seg) * vs
            acc_s[sl, :] += o
            qin_s[d, sl, :] = q * jnp.exp(cum)
            kout_s[d, sl, :] = k * jnp.exp(tot - cum)
            cum_s[d, sl, :] = tot
            return carry

        lax.fori_loop(0, n_blk, block, 0)

    def step(i, carry):
        s_f, s_b = carry
        n_f = i
        n_b = jnp.where(i < n_chunk_ctx, n_chunk_ctx - 1 - i, n_chunk - 1 - (i - n_chunk_ctx))
        outs = []
        for d, n, s in ((0, n_f, s_f), (1, n_b, s_b)):
            sl = pl.ds(pl.multiple_of(n * c, c), c)
            qin = qin_s[d, sl, :].astype(BF16)
            kout = kout_s[d, sl, :].astype(BF16)
            v = i_ref[sl, :].astype(BF16)
            dec = jnp.exp(cum_s[d, pl.ds(pl.multiple_of(n * c, c), 1), :])
            acc_s[sl, :] += _nt(qin, s.astype(BF16))
            kv = jnp.where(blockdiag, _tn(v, kout), 0.0)
            outs.append(s * dec + kv)
        return tuple(outs)

    zero = jnp.zeros((LANES, LANES), F32)
    lax.fori_loop(0, n_chunk, step, (zero, zero))

    def readout(b, carry):
        sl = pl.ds(pl.multiple_of(b * rb, rb), rb)
        o = acc_s[sl, :]
        ms = _mm_x3(o * o, seg) * (1.0 / HEAD_DIM)
        g = g_ref[sl, :]
        o_ref[sl, :] = o * lax.rsqrt(ms + EPS) * gn_ref[...] * (g * _sigmoid(g))
        return carry

    lax.fori_loop(0, n_blk, readout, 0)


def _hgrn(pa, lb, gn_gain, *, t_ctx):
    bsz, t_tot, _ = pa.shape
    n_hp = A_WIDTH // LANES

    def sec(s):
        return pl.BlockSpec((None, t_tot, LANES), lambda b, h, s=s: (b, 0, s * n_hp + h))

    return pl.pallas_call(
        functools.partial(_hgrn_body, t_ctx=t_ctx),
        grid=(bsz, n_hp),
        in_specs=[sec(0), sec(1), sec(2), sec(3), sec(4),
                  pl.BlockSpec((2, LANES), lambda b, h: (0, h)),
                  pl.BlockSpec((1, LANES), lambda b, h: (0, h))],
        out_specs=pl.BlockSpec((None, t_tot, LANES), lambda b, h: (b, 0, h)),
        out_shape=jax.ShapeDtypeStruct((bsz, t_tot, A_WIDTH), F32),
        scratch_shapes=[
            pltpu.VMEM((2, t_tot, LANES), F32),
            pltpu.VMEM((2, t_tot, LANES), F32),
            pltpu.VMEM((2, t_tot, LANES), F32),
            pltpu.VMEM((t_tot, LANES), F32),
        ],
        compiler_params=_cparams("arbitrary", "arbitrary"),
        name="hgrn",
    )(pa, pa, pa, pa, pa, lb, gn_gain.reshape(1, A_WIDTH))


def _rwkv_prep_body(x_ref, xp_ref, xn_ref, mu_ref, w0_ref, wup_ref, a0_ref, aup_ref, gup_ref,
                    kks_ref, ka_ref, rk_ref,
                    r_o, v_o, kk_o, kf_o, kb_o, bf_o, bb_o, lwf_o, lwb_o, g_o, bonus_o, *, t_ctx, t_tot):
    tm = x_ref.shape[0]
    r0 = pl.program_id(1) * tm
    prev_ok = jnp.logical_and(r0 != 0, r0 != t_ctx)
    next_ok = jnp.logical_and(r0 + tm != t_ctx, r0 + tm != t_tot)
    first = _iota((tm, 1), 0) == 0
    last = _iota((tm, 1), 0) == tm - 1

    def shifted(lo, hi):
        x = x_ref[:, lo:hi]
        p_row = jnp.where(prev_ok, xp_ref[7:8, lo:hi], 0.0)
        n_row = jnp.where(next_ok, xn_ref[0:1, lo:hi], 0.0)
        prev = jnp.where(first, p_row, pltpu.roll(x, 1, 0))
        nxt = jnp.where(last, n_row, pltpu.roll(x, tm - 1, 0))
        return x + (0.5 * (prev + nxt) - x) * mu_ref[:, lo:hi]

    bw = B_WIDTH
    r = shifted(0, bw)
    k = shifted(bw, 2 * bw)
    v = shifted(2 * bw, 3 * bw)
    tail = shifted(3 * bw, B_PROJ)
    seg = _head_seg(bw)

    kk = k * kks_ref[...]
    kk = kk / jnp.maximum(jnp.sqrt(_mm_x3(kk * kk, seg)), 1e-12)
    ksum = jnp.zeros_like(k)
    for d, (k_o, b_o, lw_o) in enumerate(((kf_o, bf_o, lwf_o), (kb_o, bb_o, lwb_o))):
        wd = tail[:, d * RWKV_W_RANK:(d + 1) * RWKV_W_RANK]
        ad = tail[:, 2 * RWKV_W_RANK + d * RWKV_A_RANK:2 * RWKV_W_RANK + (d + 1) * RWKV_A_RANK]
        u = -(w0_ref[d:d + 1, :] + _mm_hi(jnp.tanh(wd), wup_ref[d]))
        w = -(jnp.maximum(u, 0.0) + jnp.log(1.0 + jnp.exp(-jnp.abs(u)))) - 0.5
        a = _sigmoid(a0_ref[d:d + 1, :] + _mm_hi(ad, aup_ref[d]))
        k_d = k * (1.0 + (a - 1.0) * ka_ref[...])
        ksum = ksum + k_d
        k_o[...] = k_d
        b_o[...] = a * kk
        lw_o[...] = -jnp.exp(w)
    gd = tail[:, 2 * RWKV_W_RANK + 2 * RWKV_A_RANK:]
    r_o[...] = r
    v_o[...] = v
    kk_o[...] = kk
    g_o[...] = _mm_hi(_sigmoid(gd), gup_ref[...])
    bonus_o[...] = _mm_x3(r * ksum * rk_ref[...], seg) * v


def _rwkv_prep(pb, mu, w0, w_up, a0, a_up, g_up, kk_scale, k_a, r_k, *, t_ctx):
    bsz, t_tot, _ = pb.shape
    tm = TOKEN_TILE
    nb8 = t_tot // 8
    per8 = tm // 8
    bw = B_WIDTH
    full = lambda shape: pl.BlockSpec(shape, lambda b, i: (0,) * len(shape))
    out = jax.ShapeDtypeStruct((bsz, t_tot, bw), F32)
    return pl.pallas_call(
        functools.partial(_rwkv_prep_body, t_ctx=t_ctx, t_tot=t_tot),
        grid=(bsz, t_tot // tm),
        in_specs=[
            pl.BlockSpec((None, tm, B_PROJ), lambda b, i: (b, i, 0)),
            pl.BlockSpec((None, 8, B_PROJ), lambda b, i: (b, jnp.maximum(i * per8 - 1, 0), 0)),
            pl.BlockSpec((None, 8, B_PROJ), lambda b, i: (b, jnp.minimum((i + 1) * per8, nb8 - 1), 0)),
            full((1, B_PROJ)), full((2, bw)), full((2, RWKV_W_RANK, bw)), full((2, bw)),
            full((2, RWKV_A_RANK, bw)), full((RWKV_G_RANK, bw)), full((1, bw)), full((1, bw)), full((1, bw)),
        ],
        out_specs=[pl.BlockSpec((None, tm, bw), lambda b, i: (b, i, 0))] * 11,
        out_shape=[out] * 11,
        compiler_params=_cparams("arbitrary", "arbitrary"),
        name="rwkv_prep",
    )(pb, pb, pb, mu.reshape(1, B_PROJ), w0, w_up, a0, a_up, g_up,
      kk_scale.reshape(1, bw), k_a.reshape(1, bw), r_k.reshape(1, bw))


def _rwkv_scan_body(r_ref, v_ref, kk_ref, kf_ref, kb_ref, bf_ref, bb_ref, lwf_ref, lwb_ref, y_ref,
                    rq_s, y0_s, p_s, z_s, *, t_ctx):
    t_tot = r_ref.shape[0]
    c = RWKV_CHUNK
    rb = RWKV_BLOCK
    n_blk = t_tot // rb
    n_chunk = t_tot // c
    n_chunk_ctx = t_ctx // c
    cpb = rb // c
    n_double = c.bit_length() - 2
    row = _iota((rb, rb), 0)
    col = _iota((rb, rb), 1)
    same = (row // c) == (col // c)
    same_bf = jnp.where(same, 1.0, 0.0).astype(BF16)
    eye = jnp.where(row == col, 1.0, 0.0)
    lane = _iota((1, LANES), 1)
    head_masks = [jnp.where(lane // HEAD_DIM == h, 1.0, 0.0) for h in range(LANES // HEAD_DIM)]
    r128 = _iota((LANES, LANES), 0)
    c128 = _iota((LANES, LANES), 1)
    blockdiag = (r128 // HEAD_DIM) == (c128 // HEAD_DIM)
    diag128 = r128 == c128

    def block(bi, carry):
        sl = pl.ds(pl.multiple_of(bi * rb, rb), rb)
        for d in range(2):
            rev = d == 1
            k_ref, b_ref, lw_ref = (kb_ref, bb_ref, lwb_ref) if rev else (kf_ref, bf_ref, lwf_ref)
            incl = same & ((col >= row) if rev else (col <= row))
            strict = same & ((col > row) if rev else (col < row))
            incl_bf = jnp.where(incl, 1.0, 0.0).astype(BF16)
            lw = lw_ref[sl, :]
            r = r_ref[sl, :]
            v = v_ref[sl, :]
            kk = kk_ref[sl, :]
            k = k_ref[sl, :]
            b = b_ref[sl, :]
            cl = _mm_l3(incl_bf, lw)
            tot = _mm_l3(same_bf, lw)
            w_inv = jnp.exp(-cl)
            w_end = jnp.exp(tot - cl)
            kk_d = kk * jnp.exp(cl - lw)
            r_d = r * jnp.exp(cl)
            k_i = k * w_inv
            b_i = b * w_inv
            kb = jnp.concatenate([k_i, b_i], axis=0).astype(BF16)
            v_bf = v.astype(BF16)
            kkt = jnp.zeros((rb, LANES), F32)
            u = jnp.zeros((rb, LANES), F32)
            rq = jnp.zeros((rb, LANES), F32)
            y0 = jnp.zeros((rb, LANES), F32)
            for mh in head_masks:
                kkm = kk_d * mh
                rm = r_d * mh
                aa = _nt(jnp.concatenate([kkm, rm], axis=0).astype(BF16), kb)
                a_kb = jnp.where(strict, aa[:rb, :rb], 0.0)
                a_bb = jnp.where(strict, aa[:rb, rb:], 0.0)
                a_rk = jnp.where(incl, aa[rb:, :rb], 0.0)
                a_rb = jnp.where(incl, aa[rb:, rb:], 0.0)
                m_pow = -a_bb
                t_inv = eye + m_pow
                m_bf = m_pow.astype(BF16)
                m_pow = _mm(m_bf, m_bf)
                for _ in range(n_double - 1):
                    both = _mm(jnp.concatenate([t_inv, m_pow], axis=0).astype(BF16), m_pow.astype(BF16))
                    t_inv = t_inv + both[:rb]
                    m_pow = both[rb:]
                t_inv = t_inv + _mm(t_inv.astype(BF16), m_pow.astype(BF16))
                av = _mm(jnp.concatenate([a_kb, a_rk], axis=0).astype(BF16), v_bf)
                ku = _mm(t_inv.astype(BF16), jnp.concatenate([kkm, av[:rb]], axis=1).astype(BF16))
                rb_ku = _mm(a_rb.astype(BF16), ku.astype(BF16))
                kkt = kkt + ku[:, :LANES]
                u = u + ku[:, LANES:] * mh
                rq = rq + rm - rb_ku[:, :LANES]
                y0 = y0 + (av[rb:] - rb_ku[:, LANES:]) * mh
            rq_s[d, sl, :] = rq
            y0_s[d, sl, :] = y0
            b_e = (b * w_end).astype(BF16)
            k_e = (k * w_end).astype(BF16)
            ktu = jnp.concatenate([kkt, u], axis=1).astype(BF16)
            for n in range(cpb):
                rows = slice(n * c, (n + 1) * c)
                wc = jnp.exp(tot[n * c:n * c + 1, :])
                kub = _tn(ktu[rows], b_e[rows])
                p = jnp.where(diag128, wc, 0.0) - jnp.where(blockdiag, kub[:LANES], 0.0)
                z = jnp.where(blockdiag, _tn(v_bf[rows], k_e[rows]) - kub[LANES:], 0.0)
                p_s[d, bi * cpb + n] = p
                z_s[d, bi * cpb + n] = z
        return carry

    lax.fori_loop(0, n_blk, block, 0)

    y_ref[...] = jnp.zeros_like(y_ref)

    def step(i, carry):
        s_f, s_b = carry
        n_b = jnp.where(i < n_chunk_ctx, n_chunk_ctx - 1 - i, n_chunk - 1 - (i - n_chunk_ctx))
        outs = []
        for d, n, s in ((0, i, s_f), (1, n_b, s_b)):
            sl = pl.ds(pl.multiple_of(n * c, c), c)
            y_ref[sl, :] += _x3(_nt, rq_s[d, sl, :], s) + y0_s[d, sl, :]
            outs.append(_x3(_mm, s, p_s[d, n]) + z_s[d, n])
        return tuple(outs)

    zero = jnp.zeros((LANES, LANES), F32)
    lax.fori_loop(0, n_chunk, step, (zero, zero))


def _rwkv_scan(r, v, kk, kf, kb, bf, bb, lwf, lwb, *, t_ctx):
    bsz, t_tot, bw = r.shape
    n_chunk = t_tot // RWKV_CHUNK
    spec = pl.BlockSpec((None, t_tot, LANES), lambda b, h: (b, 0, h))
    return pl.pallas_call(
        functools.partial(_rwkv_scan_body, t_ctx=t_ctx),
        grid=(bsz, bw // LANES),
        in_specs=[spec] * 9,
        out_specs=spec,
        out_shape=jax.ShapeDtypeStruct((bsz, t_tot, bw), F32),
        scratch_shapes=[
            pltpu.VMEM((2, t_tot, LANES), F32),
            pltpu.VMEM((2, t_tot, LANES), F32),
            pltpu.VMEM((2, n_chunk, LANES, LANES), F32),
            pltpu.VMEM((2, n_chunk, LANES, LANES), F32),
        ],
        compiler_params=_cparams("arbitrary", "arbitrary"),
        name="rwkv_scan",
    )(r, v, kk, kf, kb, bf, bb, lwf, lwb)


def _rwkv_readout_body(y_ref, g_ref, bonus_ref, lng_ref, lnb_ref, o_ref):
    seg = _head_seg(B_WIDTH)
    y = y_ref[...]
    mean = _mm_x3(y, seg) * (1.0 / HEAD_DIM)
    yc = y - mean
    var = _mm_x3(yc * yc, seg) * (1.0 / HEAD_DIM)
    yn = yc * lax.rsqrt(var + RWKV_LN_EPS) * lng_ref[...] + lnb_ref[...]
    o_ref[...] = (yn + bonus_ref[...]) * g_ref[...]


def _rwkv_readout(y, g, bonus, ln_gain, ln_bias):
    n, bw = y.shape
    tm = 512
    spec = pl.BlockSpec((tm, bw), lambda i: (i, 0))
    vec = pl.BlockSpec((1, bw), lambda i: (0, 0))
    return pl.pallas_call(
        _rwkv_readout_body,
        grid=(n // tm,),
        in_specs=[spec, spec, spec, vec, vec],
        out_specs=spec,
        out_shape=jax.ShapeDtypeStruct((n, bw), F32),
        compiler_params=_cparams("arbitrary"),
        name="rwkv_readout",
    )(y, g, bonus, ln_gain.reshape(1, bw), ln_bias.reshape(1, bw))


def _na_body(q_ref, k_ref, v_ref, qg_ref, kg_ref, cos_ref, sin_ref, bias_ref, o_ref,
             qs, ks, vs, *, t_ctx, need_ctx):
    t_tot = q_ref.shape[0]
    t_lat = t_tot - t_ctx
    rows = t_lat // GRID_W
    win_r = min(NA_WIN_ROWS, rows)
    scale = HEAD_DIM ** -0.5
    seg = _head_seg(LANES)
    lane = _iota((1, LANES), 1)
    head_masks = [jnp.where(lane // HEAD_DIM == h, 1.0, 0.0) for h in range(LANES // HEAD_DIM)]
    half = (lane % HEAD_DIM) < HEAD_DIM // 2
    blk = 256

    def norm_block(i, carry):
        sl = pl.ds(pl.multiple_of(i * blk, blk), blk)
        q = q_ref[sl, :]
        k = k_ref[sl, :]
        q = q * lax.rsqrt(_mm_x3(q * q, seg) * (1.0 / HEAD_DIM) + EPS) * qg_ref[...]
        k = k * lax.rsqrt(_mm_x3(k * k, seg) * (1.0 / HEAD_DIM) + EPS) * kg_ref[...]
        cos = cos_ref[sl, :]
        sin = sin_ref[sl, :]

        def rope(t):
            swapped = jnp.where(half, pltpu.roll(t, LANES - HEAD_DIM // 2, 1), pltpu.roll(t, HEAD_DIM // 2, 1))
            return t * cos + swapped * sin

        qs[sl, :] = (rope(q) * scale).astype(BF16)
        ks[sl, :] = rope(k).astype(BF16)
        vs[sl, :] = v_ref[sl, :].astype(BF16)
        return carry

    lax.fori_loop(0, t_tot // blk, norm_block, 0)

    kc = ks[0:t_ctx, :]
    vc = vs[0:t_ctx, :]

    if need_ctx:
        qc = qs[0:t_ctx, :]
        out = jnp.zeros((t_ctx, LANES), F32)
        for mh in head_masks:
            s = _nt(qc * mh.astype(BF16), kc)
            p = jnp.exp(s - jnp.max(s, axis=-1, keepdims=True))
            o = _mm(p.astype(BF16), vc) / jnp.sum(p, axis=-1, keepdims=True)
            out = out + o * mh
        o_ref[0:t_ctx, :] = out
    else:
        o_ref[0:t_ctx, :] = jnp.zeros((t_ctx, LANES), F32)

    def q_row(r, carry):
        r_start = jnp.clip(r - win_r // 2, 0, rows - win_r)
        q_sl = pl.ds(pl.multiple_of(t_ctx + r * GRID_W, GRID_W), GRID_W)
        k_sl = pl.ds(pl.multiple_of(t_ctx + r_start * GRID_W, GRID_W), win_r * GRID_W)
        q = qs[q_sl, :]
        kw = ks[k_sl, :]
        vw = vs[k_sl, :]
        out = jnp.zeros((GRID_W, LANES), F32)
        for h, mh in enumerate(head_masks):
            qm = q * mh.astype(BF16)
            s_win = _nt(qm, kw) + bias_ref[h, r - r_start]
            s_ctx = _nt(qm, kc)
            m = jnp.maximum(jnp.max(s_win, axis=-1, keepdims=True), jnp.max(s_ctx, axis=-1, keepdims=True))
            p_win = jnp.exp(s_win - m)
            p_ctx = jnp.exp(s_ctx - m)
            den = jnp.sum(p_win, axis=-1, keepdims=True) + jnp.sum(p_ctx, axis=-1, keepdims=True)
            o = (_mm(p_win.astype(BF16), vw) + _mm(p_ctx.astype(BF16), vc)) / den
            out = out + o * mh
        o_ref[q_sl, :] = out
        return carry

    lax.fori_loop(0, rows, q_row, 0)


def _na_tables(t_ctx, t_lat):
    quarter = HEAD_DIM // 4
    pos = np.arange(t_lat)
    inv = ROPE_THETA ** (-np.arange(quarter, dtype=np.float32) / quarter)
    pos_r = (pos // GRID_W).astype(np.float32)
    pos_c = (pos % GRID_W).astype(np.float32)
    return pos_r, pos_c, inv


def _na_bias_table(rpb, rows):
    win_r = min(NA_WIN_ROWS, rows)
    c = np.arange(GRID_W)
    w_start = np.clip(c - NA_WIN_COLS // 2, 0, GRID_W - NA_WIN_COLS)
    kc = np.arange(GRID_W)
    in_win = (kc[None, :] >= w_start[:, None]) & (kc[None, :] < w_start[:, None] + NA_WIN_COLS)
    col_idx = np.clip(kc[None, :] - c[:, None] + NA_WIN_COLS - 1, 0, 2 * NA_WIN_COLS - 2)
    n_col = 2 * NA_WIN_COLS - 1
    onehot = jnp.asarray(col_idx[None] == np.arange(n_col)[:, None, None], F32)
    band = jnp.einsum('...hrj,jck->...hrck', rpb.astype(F32), onehot, precision=_HI)
    band = jnp.where(in_win, band, NEG_INF)
    per_off = [band[..., NA_WIN_ROWS - 1 - off:NA_WIN_ROWS - 1 - off + win_r, :, :] for off in range(win_r)]
    g = jnp.stack(per_off, axis=-4)
    g = jnp.swapaxes(g, -3, -2)
    return g.reshape(g.shape[:-2] + (win_r * GRID_W,))


def _na(pc, q_gain, k_gain, bias, layer, *, t_ctx, need_ctx):
    bsz, t_tot, _ = pc.shape
    t_lat = t_tot - t_ctx
    rows = t_lat // GRID_W
    win_r = min(NA_WIN_ROWS, rows)
    n_hp = C_WIDTH // LANES
    hpl = LANES // HEAD_DIM
    pos_r, pos_c, inv = _na_tables(t_ctx, t_lat)
    ang = np.concatenate([pos_r[:, None] * inv, pos_c[:, None] * inv], axis=-1)
    ang = np.concatenate([np.zeros((t_ctx, HEAD_DIM // 2), np.float32), ang], axis=0)
    cos = np.cos(ang)
    sin = np.sin(ang)
    cos_t = jnp.asarray(np.tile(np.concatenate([cos, cos], axis=-1), (1, hpl)), F32)
    sin_t = jnp.asarray(np.tile(np.concatenate([-sin, sin], axis=-1), (1, hpl)), F32)
    bias = bias.reshape(bias.shape[0], n_hp, hpl, win_r, GRID_W, win_r * GRID_W)

    def sec(s):
        return pl.BlockSpec((None, t_tot, LANES), lambda b, h, s=s: (b, 0, s * n_hp + h))

    gain = lambda g: jnp.tile(g.reshape(1, HEAD_DIM), (1, hpl))
    return pl.pallas_call(
        functools.partial(_na_body, t_ctx=t_ctx, need_ctx=need_ctx),
        grid=(bsz, n_hp),
        in_specs=[sec(0), sec(1), sec(2),
                  pl.BlockSpec((1, LANES), lambda b, h: (0, 0)),
                  pl.BlockSpec((1, LANES), lambda b, h: (0, 0)),
                  pl.BlockSpec((t_tot, LANES), lambda b, h: (0, 0)),
                  pl.BlockSpec((t_tot, LANES), lambda b, h: (0, 0)),
                  pl.BlockSpec((None, None, hpl, win_r, GRID_W, win_r * GRID_W),
                               lambda b, h: (layer, h, 0, 0, 0, 0))],
        out_specs=pl.BlockSpec((None, t_tot, LANES), lambda b, h: (b, 0, h)),
        out_shape=jax.ShapeDtypeStruct((bsz, t_tot, C_WIDTH), F32),
        scratch_shapes=[pltpu.VMEM((t_tot, LANES), BF16)] * 3,
        compiler_params=_cparams("arbitrary", "arbitrary"),
        name="na",
    )(pc, pc, pc, gain(q_gain), gain(k_gain), cos_t, sin_t, bias)


def _outproj_body(x_ref, a_ref, b_ref, c_ref, m_ref, g_ref, w_ref, wr_ref, br_ref, xo_ref, h_ref, comb_ref):
    mix = (_mm(a_ref[...].astype(BF16), w_ref[0:A_WIDTH, :])
           + _mm(b_ref[...].astype(BF16), w_ref[A_WIDTH:A_WIDTH + B_WIDTH, :])
           + _mm(c_ref[...].astype(BF16), w_ref[A_WIDTH + B_WIDTH:, :]))
    x = x_ref[...] + m_ref[2:3, :] * mix
    xo_ref[...] = x
    h = x * lax.rsqrt(jnp.mean(x * x, axis=-1, keepdims=True) + EPS) * g_ref[...]
    h = h * (1.0 + m_ref[4:5, :]) + m_ref[3:4, :]
    h_ref[...] = h.astype(BF16)
    logits = _mm_hi(h, wr_ref[...]) + br_ref[...]
    lane = _iota(logits.shape, 1).astype(F32)
    big = float(LANES)
    is_g = (lane >= N_EXPERTS) & (lane < N_EXPERTS + N_GROUPS)
    gl = jnp.where(is_g, logits, -jnp.inf)
    gmax = jnp.max(gl, axis=-1, keepdims=True)
    g_w = 1.0 / jnp.sum(jnp.exp(gl - gmax), axis=-1, keepdims=True)
    g_sel = jnp.min(jnp.where(gl == gmax, lane, big), axis=-1, keepdims=True) - N_EXPERTS
    lo = g_sel * EXPERTS_PER_GROUP
    el = jnp.where((lane >= lo) & (lane < lo + EXPERTS_PER_GROUP), logits, -jnp.inf)
    m1 = jnp.max(el, axis=-1, keepdims=True)
    i1 = jnp.min(jnp.where(el == m1, lane, big), axis=-1, keepdims=True)
    el2 = jnp.where(lane == i1, -jnp.inf, el)
    m2 = jnp.max(el2, axis=-1, keepdims=True)
    i2 = jnp.min(jnp.where(el2 == m2, lane, big), axis=-1, keepdims=True)
    e2 = jnp.exp(m2 - m1)
    w1 = g_w / (1.0 + e2)
    comb_ref[...] = jnp.where(lane == i1, w1, 0.0) + jnp.where(lane == i2, w1 * e2, 0.0)


def _outproj(x, a, b, c, mod_l, gain2, w_bf, w_router, b_router, *, bps, ctx_blocks, ctx_row):
    n = x.shape[0]
    tm = TOKEN_TILE
    row = functools.partial(_mod_row, blocks_per_seq=bps, ctx_blocks=ctx_blocks, ctx_row=ctx_row)
    tok = lambda w: pl.BlockSpec((tm, w), lambda i: (i, 0))
    full = lambda shape: pl.BlockSpec(shape, lambda i: (0,) * len(shape))
    return pl.pallas_call(
        _outproj_body,
        grid=(n // tm,),
        in_specs=[tok(D_MODEL), tok(A_WIDTH), tok(B_WIDTH), tok(C_WIDTH),
                  pl.BlockSpec((None, 6, D_MODEL), lambda i: (row(i), 0, 0)),
                  full((1, D_MODEL)), full((D_MODEL, D_MODEL)), full((D_MODEL, LANES)), full((1, LANES))],
        out_specs=[tok(D_MODEL), tok(D_MODEL), tok(LANES)],
        out_shape=[jax.ShapeDtypeStruct((n, D_MODEL), F32),
                   jax.ShapeDtypeStruct((n, D_MODEL), BF16),
                   jax.ShapeDtypeStruct((n, LANES), F32)],
        compiler_params=_cparams("arbitrary"),
        name="outproj",
    )(x, a, b, c, mod_l, gain2.reshape(1, D_MODEL), w_bf, w_router, b_router)


def _moe_body(h_ref, comb_ref, x_ref, *rest):
    per = len(rest) - 5
    m_refs = rest[:per]
    wg_ref, wu_ref, wd_ref, o_ref, acc_s = rest[per:]
    e = pl.program_id(1)

    @pl.when(e == 0)
    def _():
        acc_s[...] = jnp.zeros_like(acc_s)

    h = h_ref[...]
    gate = _mm(h, wg_ref[...])
    up = _mm(h, wu_ref[...])
    comb = comb_ref[...]
    ce = jnp.sum(jnp.where(_iota(comb.shape, 1) == e, comb, 0.0), axis=-1, keepdims=True)
    hid = gate * _sigmoid(gate) * up * ce
    acc_s[...] += _mm(hid.astype(BF16), wd_ref[...])

    @pl.when(e == N_EXPERTS - 1)
    def _():
        for j, m_ref in enumerate(m_refs):
            sl = slice(j * TOKEN_TILE, (j + 1) * TOKEN_TILE)
            o_ref[sl, :] = x_ref[sl, :] + m_ref[5:6, :] * acc_s[sl, :]


def _moe(h, comb, x, mod_l, wg, wu, wd, layer, *, bps, ctx_blocks, ctx_row):
    n = x.shape[0]
    tm = 512
    per = tm // TOKEN_TILE
    row = functools.partial(_mod_row, blocks_per_seq=bps, ctx_blocks=ctx_blocks, ctx_row=ctx_row)
    mod_specs = [pl.BlockSpec((None, 6, D_MODEL), lambda i, e, j=j: (row(i * per + j), 0, 0)) for j in range(per)]
    return pl.pallas_call(
        _moe_body,
        grid=(n // tm, N_EXPERTS),
        in_specs=[
            pl.BlockSpec((tm, D_MODEL), lambda i, e: (i, 0)),
            pl.BlockSpec((tm, LANES), lambda i, e: (i, 0)),
            pl.BlockSpec((tm, D_MODEL), lambda i, e: (i, 0)),
            *mod_specs,
            pl.BlockSpec((None, None, D_MODEL, D_EXPERT), lambda i, e: (layer, e, 0, 0)),
            pl.BlockSpec((None, None, D_MODEL, D_EXPERT), lambda i, e: (layer, e, 0, 0)),
            pl.BlockSpec((None, None, D_EXPERT, D_MODEL), lambda i, e: (layer, e, 0, 0)),
        ],
        out_specs=pl.BlockSpec((tm, D_MODEL), lambda i, e: (i, 0)),
        out_shape=jax.ShapeDtypeStruct((n, D_MODEL), F32),
        scratch_shapes=[pltpu.VMEM((tm, D_MODEL), F32)],
        compiler_params=_cparams("arbitrary", "arbitrary"),
        name="moe",
    )(h, comb, x, *([mod_l] * per), wg, wu, wd)


def kernel(x, c, ctx, c_ctx, norm1_gain, norm2_gain, w_ada, b_ada, w_in, w_out, hgrn_lb_logits, hgrn_gn_gain, rwkv_mu, rwkv_w0, rwkv_w_up, rwkv_a0, rwkv_a_up, rwkv_g_up, rwkv_kk_scale, rwkv_k_a, rwkv_r_k, rwkv_ln_gain, rwkv_ln_bias, na_q_gain, na_k_gain, na_rpb, w_router_group, b_router_group, w_router_expert, b_router_expert, w_exp_gate, w_exp_up, w_exp_down):
    bsz, t_lat, _ = x.shape
    t_ctx = ctx.shape[1]
    t_tot = t_ctx + t_lat
    n = bsz * t_tot
    assert t_ctx % TOKEN_TILE == 0 and t_lat % TOKEN_TILE == 0 and bsz < 16
    bps = t_tot // TOKEN_TILE
    tile_kw = dict(bps=bps, ctx_blocks=t_ctx // TOKEN_TILE, ctx_row=bsz)

    lb_p = jax.nn.softmax(hgrn_lb_logits.astype(F32), axis=1)
    lower_bounds = jnp.cumsum(lb_p, axis=1) - lb_p[:, :1]

    cc = jnp.zeros((16, D_MODEL), F32).at[:bsz].set(c.astype(F32)).at[bsz].set(c_ctx.astype(F32))
    mod = _ada_mod(cc, w_ada, b_ada).reshape(DEPTH, 16, 6, D_MODEL)

    w_in_bf = w_in.astype(BF16)
    w_out_bf = w_out.astype(BF16)
    wg_bf = w_exp_gate.astype(BF16).reshape(DEPTH, N_EXPERTS, D_MODEL, D_EXPERT)
    wu_bf = w_exp_up.astype(BF16).reshape(DEPTH, N_EXPERTS, D_MODEL, D_EXPERT)
    wd_bf = w_exp_down.astype(BF16).reshape(DEPTH, N_EXPERTS, D_EXPERT, D_MODEL)
    pad = LANES - N_EXPERTS - N_GROUPS
    w_router = jnp.concatenate([w_router_expert, w_router_group,
                                jnp.zeros((DEPTH, D_MODEL, pad), F32)], axis=-1)
    b_router = jnp.concatenate([b_router_expert, b_router_group, jnp.zeros((DEPTH, pad), F32)], axis=-1)

    na_bias = _na_bias_table(na_rpb, t_lat // GRID_W)
    xs = jnp.concatenate([ctx.astype(F32), x.astype(F32)], axis=1).reshape(n, D_MODEL)
    for layer in range(DEPTH):
        last = layer == DEPTH - 1
        pa, pb, pc = _inproj(xs, mod[layer], norm1_gain[layer], w_in_bf[layer], **tile_kw)
        a_mix = _hgrn(pa.reshape(bsz, t_tot, A_PROJ), lower_bounds[:, layer], hgrn_gn_gain[layer], t_ctx=t_ctx)
        prep = _rwkv_prep(pb.reshape(bsz, t_tot, B_PROJ), rwkv_mu[layer], rwkv_w0[layer], rwkv_w_up[layer],
                          rwkv_a0[layer], rwkv_a_up[layer], rwkv_g_up[layer], rwkv_kk_scale[layer],
                          rwkv_k_a[layer], rwkv_r_k[layer], t_ctx=t_ctx)
        y = _rwkv_scan(*prep[:9], t_ctx=t_ctx)
        b_mix = _rwkv_readout(y.reshape(n, B_WIDTH), prep[9].reshape(n, B_WIDTH), prep[10].reshape(n, B_WIDTH),
                              rwkv_ln_gain[layer], rwkv_ln_bias[layer])
        c_mix = _na(pc.reshape(bsz, t_tot, C_PROJ), na_q_gain[layer], na_k_gain[layer], na_bias, layer,
                    t_ctx=t_ctx, need_ctx=not last)
        xs, h2, comb = _outproj(xs, a_mix.reshape(n, A_WIDTH), b_mix, c_mix.reshape(n, C_WIDTH), mod[layer],
                                norm2_gain[layer], w_out_bf[layer], w_router[layer],
                                b_router[layer].reshape(1, LANES), **tile_kw)
        xs = _moe(h2, comb, xs, mod[layer], wg_bf, wu_bf, wd_bf, layer, **tile_kw)
    return xs.reshape(bsz, t_tot, D_MODEL)[:, t_ctx:].astype(x.dtype)
```

```python
import functools

import jax
import jax.numpy as jnp
import numpy as np
from jax import lax
from jax.experimental import pallas as pl
from jax.experimental.pallas import tpu as pltpu

F32 = jnp.float32
BF16 = jnp.bfloat16

D_MODEL = 1024
DEPTH = 4
GRID_W = 64
HEAD_DIM = 64
A_HEADS = 4
B_HEADS = 4
C_HEADS = 8
A_WIDTH = A_HEADS * HEAD_DIM
B_WIDTH = B_HEADS * HEAD_DIM
C_WIDTH = C_HEADS * HEAD_DIM
HGRN_CHUNK = 16
HGRN_F_FLOOR = 1e-20
RWKV_W_RANK = 64
RWKV_A_RANK = 64
RWKV_G_RANK = 128
RWKV_LN_EPS = 64e-5
NA_WIN_ROWS = 8
NA_WIN_COLS = 16
ROPE_THETA = 10000.0
N_GROUPS = 4
EXPERTS_PER_GROUP = 8
N_EXPERTS = N_GROUPS * EXPERTS_PER_GROUP
TOP_K = 2
D_EXPERT = 512
EPS = 1e-6
NEG_INF = -1e30
A_PROJ = 5 * A_WIDTH
B_PROJ = 3 * B_WIDTH + 2 * RWKV_W_RANK + 2 * RWKV_A_RANK + RWKV_G_RANK
C_PROJ = 3 * C_WIDTH
P_TOTAL = A_PROJ + B_PROJ + C_PROJ
B_TAIL = B_PROJ - 3 * B_WIDTH

LANES = 128
TOKEN_TILE = 256
RWKV_CHUNK = 64
RWKV_BLOCK = 128
HGRN_BLOCK = 128
MOE_TILE = 192
VMEM_LIMIT = 56 * 1024 * 1024

_HI = lax.Precision.HIGHEST


def _cparams(*sem):
    return pltpu.CompilerParams(dimension_semantics=sem, vmem_limit_bytes=VMEM_LIMIT)


def _mm(a, b):
    return jnp.dot(a, b, preferred_element_type=F32)


def _mm_hi(a, b):
    return jnp.dot(a, b, preferred_element_type=F32, precision=_HI)


def _nt(a, b):
    return lax.dot_general(a, b, (((1,), (1,)), ((), ())), preferred_element_type=F32)


def _nt_hi(a, b):
    return lax.dot_general(a, b, (((1,), (1,)), ((), ())), preferred_element_type=F32, precision=_HI)


def _tn(a, b):
    return lax.dot_general(a, b, (((0,), (0,)), ((), ())), preferred_element_type=F32)


def _tn_hi(a, b):
    return lax.dot_general(a, b, (((0,), (0,)), ((), ())), preferred_element_type=F32, precision=_HI)


def _split3(a):
    a1 = a.astype(BF16)
    r1 = a - a1.astype(F32)
    a2 = r1.astype(BF16)
    a3 = (r1 - a2.astype(F32)).astype(BF16)
    return a1, a2, a3


def _mm_x3(a, m):
    a1, a2, a3 = _split3(a)
    return _mm(a1, m) + _mm(a2, m) + _mm(a3, m)


def _mm_l3(m, a):
    a1, a2, a3 = _split3(a)
    return _mm(m, a1) + _mm(m, a2) + _mm(m, a3)


def _split2(a):
    a1 = a.astype(BF16)
    return a1, (a - a1.astype(F32)).astype(BF16)


def _x3p(dot, a, b):
    return dot(a[0], b[0]) + (dot(a[0], b[1]) + dot(a[1], b[0]))


def _x3(dot, a, b):
    return _x3p(dot, _split2(a), _split2(b))


def _sigmoid(x):
    return 1.0 / (1.0 + jnp.exp(-x))


def _iota(shape, dim):
    return lax.broadcasted_iota(jnp.int32, shape, dim)


def _head_seg(n):
    return jnp.where(_iota((n, n), 0) // HEAD_DIM == _iota((n, n), 1) // HEAD_DIM, 1.0, 0.0).astype(BF16)


def _ada_body(c_ref, w_ref, b_ref, o_ref):
    c = c_ref[...]
    o_ref[...] = _mm_hi(c * _sigmoid(c), w_ref[...]) + b_ref[...]


def _ada_mod(cc, w_ada, b_ada):
    rows = cc.shape[0]
    tn = 1536
    return pl.pallas_call(
        _ada_body,
        grid=(DEPTH, 6 * D_MODEL // tn),
        in_specs=[
            pl.BlockSpec((rows, D_MODEL), lambda l, j: (0, 0)),
            pl.BlockSpec((None, D_MODEL, tn), lambda l, j: (l, 0, j)),
            pl.BlockSpec((None, 1, tn), lambda l, j: (l, 0, j)),
        ],
        out_specs=pl.BlockSpec((None, rows, tn), lambda l, j: (l, 0, j)),
        out_shape=jax.ShapeDtypeStruct((DEPTH, rows, 6 * D_MODEL), F32),
        compiler_params=_cparams("arbitrary", "arbitrary"),
        name="ada_mod",
    )(cc, w_ada, b_ada.reshape(DEPTH, 1, 6 * D_MODEL))


def _mod_row(i, blocks_per_seq, ctx_blocks, ctx_row):
    return jnp.where(i % blocks_per_seq < ctx_blocks, ctx_row, i // blocks_per_seq)


def _inproj_body(x_ref, m_ref, g_ref, w_ref, pa_ref, pb_ref, pc_ref):
    x = x_ref[...]
    h = x * lax.rsqrt(jnp.mean(x * x, axis=-1, keepdims=True) + EPS) * g_ref[...]
    h = (h * (1.0 + m_ref[1:2, :]) + m_ref[0:1, :]).astype(BF16)
    pa_ref[...] = _mm(h, w_ref[:, :A_PROJ])
    pb_ref[...] = _mm(h, w_ref[:, A_PROJ:A_PROJ + B_PROJ])
    pc_ref[...] = _mm(h, w_ref[:, A_PROJ + B_PROJ:])


def _inproj(x, mod_l, gain, w_bf, *, bps, ctx_blocks, ctx_row):
    n = x.shape[0]
    tm = TOKEN_TILE
    row = functools.partial(_mod_row, blocks_per_seq=bps, ctx_blocks=ctx_blocks, ctx_row=ctx_row)
    return pl.pallas_call(
        _inproj_body,
        grid=(n // tm,),
        in_specs=[
            pl.BlockSpec((tm, D_MODEL), lambda i: (i, 0)),
            pl.BlockSpec((None, 6, D_MODEL), lambda i: (row(i), 0, 0)),
            pl.BlockSpec((1, D_MODEL), lambda i: (0, 0)),
            pl.BlockSpec((D_MODEL, P_TOTAL), lambda i: (0, 0)),
        ],
        out_specs=[
            pl.BlockSpec((tm, A_PROJ), lambda i: (i, 0)),
            pl.BlockSpec((tm, B_PROJ), lambda i: (i, 0)),
            pl.BlockSpec((tm, C_PROJ), lambda i: (i, 0)),
        ],
        out_shape=[
            jax.ShapeDtypeStruct((n, A_PROJ), F32),
            jax.ShapeDtypeStruct((n, B_PROJ), F32),
            jax.ShapeDtypeStruct((n, C_PROJ), F32),
        ],
        compiler_params=_cparams("arbitrary"),
        name="inproj",
    )(x, mod_l, gain.reshape(1, D_MODEL), w_bf)


def _hgrn_body(q_ref, ff_ref, fb_ref, i_ref, g_ref, lb_ref, gn_ref, o_ref,
               qin_s, kout_s, cum_s, acc_s, *, t_ctx):
    t_tot = q_ref.shape[0]
    c = HGRN_CHUNK
    rb = HGRN_BLOCK
    n_blk = t_tot // rb
    n_chunk = t_tot // c
    n_chunk_ctx = t_ctx // c
    row = _iota((rb, rb), 0)
    col = _iota((rb, rb), 1)
    same = (row // c) == (col // c)
    same_bf = jnp.where(same, 1.0, 0.0).astype(BF16)
    seg = _head_seg(LANES)
    pos = _iota((rb, LANES), 0) % c
    blockdiag = _iota((LANES, LANES), 0) // HEAD_DIM == _iota((LANES, LANES), 1) // HEAD_DIM

    acc_s[...] = jnp.zeros_like(acc_s)

    for d in range(2):
        rev = d == 1
        f_ref = fb_ref if rev else ff_ref
        lb = lb_ref[d:d + 1, :]
        tri_bf = jnp.where(same & ((col >= row) if rev else (col <= row)), 1.0, 0.0).astype(BF16)

        def block(b, carry, rev=rev, f_ref=f_ref, lb=lb, tri_bf=tri_bf, d=d):
            sl = pl.ds(pl.multiple_of(b * rb, rb), rb)
            fpre = f_ref[sl, :]
            q = q_ref[sl, :]
            v = i_ref[sl, :]
            f = lb + (1.0 - lb) * _sigmoid(fpre)
            logf = jnp.log(jnp.maximum(f, HGRN_F_FLOOR))
            k = (1.0 - lb) * _sigmoid(-fpre)
            cum = _mm_l3(tri_bf, logf)
            tot = _mm_l3(same_bf, logf)
            o = jnp.zeros((rb, LANES), F32)
            for j in range(c):
                shift = (rb - j) % rb if rev else j
                valid = (pos <= c - 1 - j) if rev else (pos >= j)
                if j == 0:
                    ks, cs, vs = k, cum, v
                else:
                    ks = pltpu.roll(k, shift, 0)
                    cs = pltpu.roll(cum, shift, 0)
                    vs = pltpu.roll(v, shift, 0)
                dec = jnp.exp(jnp.where(valid, cum - cs, 0.0))
                prod = jnp.where(valid, q * ks * dec, 0.0)
                o = o + _mm_x3(prod, seg) * vs
            acc_s[sl, :] += o
            qin_s[d, sl, :] = q * jnp.exp(cum)
            kout_s[d, sl, :] = k * jnp.exp(tot - cum)
            cum_s[d, sl, :] = tot
            return carry

        lax.fori_loop(0, n_blk, block, 0)

    def step(i, carry):
        s_f, s_b = carry
        n_f = i
        n_b = jnp.where(i < n_chunk_ctx, n_chunk_ctx - 1 - i, n_chunk - 1 - (i - n_chunk_ctx))
        outs = []
        for d, n, s in ((0, n_f, s_f), (1, n_b, s_b)):
            sl = pl.ds(pl.multiple_of(n * c, c), c)
            qin = qin_s[d, sl, :].astype(BF16)
            kout = kout_s[d, sl, :].astype(BF16)
            v = i_ref[sl, :].astype(BF16)
            dec = jnp.exp(cum_s[d, pl.ds(pl.multiple_of(n * c, c), 1), :])
            acc_s[sl, :] += _nt(qin, s.astype(BF16))
            kv = jnp.where(blockdiag, _tn(v, kout), 0.0)
            outs.append(s * dec + kv)
        return tuple(outs)

    zero = jnp.zeros((LANES, LANES), F32)
    lax.fori_loop(0, n_chunk, step, (zero, zero))

    def readout(b, carry):
        sl = pl.ds(pl.multiple_of(b * rb, rb), rb)
        o = acc_s[sl, :]
        ms = _mm_x3(o * o, seg) * (1.0 / HEAD_DIM)
        g = g_ref[sl, :]
        o_ref[sl, :] = o * lax.rsqrt(ms + EPS) * gn_ref[...] * (g * _sigmoid(g))
        return carry

    lax.fori_loop(0, n_blk, readout, 0)


def _hgrn(pa, lb, gn_gain, *, t_ctx):
    bsz, t_tot, _ = pa.shape
    n_hp = A_WIDTH // LANES

    def sec(s):
        return pl.BlockSpec((None, t_tot, LANES), lambda b, h, s=s: (b, 0, s * n_hp + h))

    return pl.pallas_call(
        functools.partial(_hgrn_body, t_ctx=t_ctx),
        grid=(bsz, n_hp),
        in_specs=[sec(0), sec(1), sec(2), sec(3), sec(4),
                  pl.BlockSpec((2, LANES), lambda b, h: (0, h)),
                  pl.BlockSpec((1, LANES), lambda b, h: (0, h))],
        out_specs=pl.BlockSpec((None, t_tot, LANES), lambda b, h: (b, 0, h)),
        out_shape=jax.ShapeDtypeStruct((bsz, t_tot, A_WIDTH), F32),
        scratch_shapes=[
            pltpu.VMEM((2, t_tot, LANES), F32),
            pltpu.VMEM((2, t_tot, LANES), F32),
            pltpu.VMEM((2, t_tot, LANES), F32),
            pltpu.VMEM((t_tot, LANES), F32),
        ],
        compiler_params=_cparams("arbitrary", "arbitrary"),
        name="hgrn",
    )(pa, pa, pa, pa, pa, lb, gn_gain.reshape(1, A_WIDTH))


def _rwkv_prep_body(x_ref, xp_ref, xn_ref, mu_ref, w0_ref, wup_ref, a0_ref, aup_ref, gup_ref,
                    kks_ref, ka_ref, rk_ref,
                    r_o, v_o, kk_o, kf_o, kb_o, bf_o, bb_o, lwf_o, lwb_o, g_o, bonus_o, *, t_ctx, t_tot):
    tm = x_ref.shape[0]
    r0 = pl.program_id(1) * tm
    prev_ok = jnp.logical_and(r0 != 0, r0 != t_ctx)
    next_ok = jnp.logical_and(r0 + tm != t_ctx, r0 + tm != t_tot)
    first = _iota((tm, 1), 0) == 0
    last = _iota((tm, 1), 0) == tm - 1

    def shifted(lo, hi):
        x = x_ref[:, lo:hi]
        p_row = jnp.where(prev_ok, xp_ref[7:8, lo:hi], 0.0)
        n_row = jnp.where(next_ok, xn_ref[0:1, lo:hi], 0.0)
        prev = jnp.where(first, p_row, pltpu.roll(x, 1, 0))
        nxt = jnp.where(last, n_row, pltpu.roll(x, tm - 1, 0))
        return x + (0.5 * (prev + nxt) - x) * mu_ref[:, lo:hi]

    bw = B_WIDTH
    r = shifted(0, bw)
    k = shifted(bw, 2 * bw)
    v = shifted(2 * bw, 3 * bw)
    tail = shifted(3 * bw, B_PROJ)
    seg = _head_seg(bw)

    kk = k * kks_ref[...]
    kk = kk / jnp.maximum(jnp.sqrt(_mm_x3(kk * kk, seg)), 1e-12)
    ksum = jnp.zeros_like(k)
    for d, (k_o, b_o, lw_o) in enumerate(((kf_o, bf_o, lwf_o), (kb_o, bb_o, lwb_o))):
        wd = tail[:, d * RWKV_W_RANK:(d + 1) * RWKV_W_RANK]
        ad = tail[:, 2 * RWKV_W_RANK + d * RWKV_A_RANK:2 * RWKV_W_RANK + (d + 1) * RWKV_A_RANK]
        u = -(w0_ref[d:d + 1, :] + _mm_hi(jnp.tanh(wd), wup_ref[d]))
        w = -(jnp.maximum(u, 0.0) + jnp.log(1.0 + jnp.exp(-jnp.abs(u)))) - 0.5
        a = _sigmoid(a0_ref[d:d + 1, :] + _mm_hi(ad, aup_ref[d]))
        k_d = k * (1.0 + (a - 1.0) * ka_ref[...])
        ksum = ksum + k_d
        k_o[...] = k_d
        b_o[...] = a * kk
        lw_o[...] = -jnp.exp(w)
    gd = tail[:, 2 * RWKV_W_RANK + 2 * RWKV_A_RANK:]
    r_o[...] = r
    v_o[...] = v
    kk_o[...] = kk
    g_o[...] = _mm_hi(_sigmoid(gd), gup_ref[...])
    bonus_o[...] = _mm_x3(r * ksum * rk_ref[...], seg) * v


def _rwkv_prep(pb, mu, w0, w_up, a0, a_up, g_up, kk_scale, k_a, r_k, *, t_ctx):
    bsz, t_tot, _ = pb.shape
    tm = TOKEN_TILE
    nb8 = t_tot // 8
    per8 = tm // 8
    bw = B_WIDTH
    full = lambda shape: pl.BlockSpec(shape, lambda b, i: (0,) * len(shape))
    out = jax.ShapeDtypeStruct((bsz, t_tot, bw), F32)
    return pl.pallas_call(
        functools.partial(_rwkv_prep_body, t_ctx=t_ctx, t_tot=t_tot),
        grid=(bsz, t_tot // tm),
        in_specs=[
            pl.BlockSpec((None, tm, B_PROJ), lambda b, i: (b, i, 0)),
            pl.BlockSpec((None, 8, B_PROJ), lambda b, i: (b, jnp.maximum(i * per8 - 1, 0), 0)),
            pl.BlockSpec((None, 8, B_PROJ), lambda b, i: (b, jnp.minimum((i + 1) * per8, nb8 - 1), 0)),
            full((1, B_PROJ)), full((2, bw)), full((2, RWKV_W_RANK, bw)), full((2, bw)),
            full((2, RWKV_A_RANK, bw)), full((RWKV_G_RANK, bw)), full((1, bw)), full((1, bw)), full((1, bw)),
        ],
        out_specs=[pl.BlockSpec((None, tm, bw), lambda b, i: (b, i, 0))] * 11,
        out_shape=[out] * 11,
        compiler_params=_cparams("arbitrary", "arbitrary"),
        name="rwkv_prep",
    )(pb, pb, pb, mu.reshape(1, B_PROJ), w0, w_up, a0, a_up, g_up,
      kk_scale.reshape(1, bw), k_a.reshape(1, bw), r_k.reshape(1, bw))


def _rwkv_scan_body(r_ref, v_ref, kk_ref, kf_ref, kb_ref, bf_ref, bb_ref, lwf_ref, lwb_ref, y_ref,
                    rq_s, y0_s, p_s, z_s, *, t_ctx):
    t_tot = r_ref.shape[0]
    c = RWKV_CHUNK
    rb = RWKV_BLOCK
    n_blk = t_tot // rb
    n_chunk = t_tot // c
    n_chunk_ctx = t_ctx // c
    cpb = rb // c
    n_double = c.bit_length() - 2
    row = _iota((rb, rb), 0)
    col = _iota((rb, rb), 1)
    same = (row // c) == (col // c)
    same_bf = jnp.where(same, 1.0, 0.0).astype(BF16)
    eye = jnp.where(row == col, 1.0, 0.0)
    lane = _iota((1, LANES), 1)
    head_masks = [jnp.where(lane // HEAD_DIM == h, 1.0, 0.0) for h in range(LANES // HEAD_DIM)]
    r128 = _iota((LANES, LANES), 0)
    c128 = _iota((LANES, LANES), 1)
    blockdiag = (r128 // HEAD_DIM) == (c128 // HEAD_DIM)
    diag128 = r128 == c128

    def block(bi, carry):
        sl = pl.ds(pl.multiple_of(bi * rb, rb), rb)
        for d in range(2):
            rev = d == 1
            k_ref, b_ref, lw_ref = (kb_ref, bb_ref, lwb_ref) if rev else (kf_ref, bf_ref, lwf_ref)
            incl = same & ((col >= row) if rev else (col <= row))
            strict = same & ((col > row) if rev else (col < row))
            incl_bf = jnp.where(incl, 1.0, 0.0).astype(BF16)
            lw = lw_ref[sl, :]
            r = r_ref[sl, :]
            v = v_ref[sl, :]
            kk = kk_ref[sl, :]
            k = k_ref[sl, :]
            b = b_ref[sl, :]
            cl = _mm_l3(incl_bf, lw)
            tot = _mm_l3(same_bf, lw)
            w_inv = jnp.exp(-cl)
            w_end = jnp.exp(tot - cl)
            kk_d = kk * jnp.exp(cl - lw)
            r_d = r * jnp.exp(cl)
            k_i = k * w_inv
            b_i = b * w_inv
            kb = jnp.concatenate([k_i, b_i], axis=0).astype(BF16)
            v_bf = v.astype(BF16)
            kkt = jnp.zeros((rb, LANES), F32)
            u = jnp.zeros((rb, LANES), F32)
            rq = jnp.zeros((rb, LANES), F32)
            y0 = jnp.zeros((rb, LANES), F32)
            for mh in head_masks:
                kkm = kk_d * mh
                rm = r_d * mh
                aa = _nt(jnp.concatenate([kkm, rm], axis=0).astype(BF16), kb)
                a_kb = jnp.where(strict, aa[:rb, :rb], 0.0)
                a_bb = jnp.where(strict, aa[:rb, rb:], 0.0)
                a_rk = jnp.where(incl, aa[rb:, :rb], 0.0)
                a_rb = jnp.where(incl, aa[rb:, rb:], 0.0)
                m_pow = -a_bb
                t_inv = eye + m_pow
                m_bf = m_pow.astype(BF16)
                m_pow = _mm(m_bf, m_bf)
                for _ in range(n_double - 1):
                    both = _mm(jnp.concatenate([t_inv, m_pow], axis=0).astype(BF16), m_pow.astype(BF16))
                    t_inv = t_inv + both[:rb]
                    m_pow = both[rb:]
                t_inv = t_inv + _mm(t_inv.astype(BF16), m_pow.astype(BF16))
                av = _mm(jnp.concatenate([a_kb, a_rk], axis=0).astype(BF16), v_bf)
                ku = _mm(t_inv.astype(BF16), jnp.concatenate([kkm, av[:rb]], axis=1).astype(BF16))
                rb_ku = _mm(a_rb.astype(BF16), ku.astype(BF16))
                kkt = kkt + ku[:, :LANES]
                u = u + ku[:, LANES:] * mh
                rq = rq + rm - rb_ku[:, :LANES]
                y0 = y0 + (av[rb:] - rb_ku[:, LANES:]) * mh
            rq_s[d, sl, :] = rq
            y0_s[d, sl, :] = y0
            b_e = (b * w_end).astype(BF16)
            k_e = (k * w_end).astype(BF16)
            ktu = jnp.concatenate([kkt, u], axis=1).astype(BF16)
            for n in range(cpb):
                rows = slice(n * c, (n + 1) * c)
                wc = jnp.exp(tot[n * c:n * c + 1, :])
                kub = _tn(ktu[rows], b_e[rows])
                p = jnp.where(diag128, wc, 0.0) - jnp.where(blockdiag, kub[:LANES], 0.0)
                z = jnp.where(blockdiag, _tn(v_bf[rows], k_e[rows]) - kub[LANES:], 0.0)
                p_s[d, bi * cpb + n] = p
                z_s[d, bi * cpb + n] = z
        return carry

    lax.fori_loop(0, n_blk, block, 0)

    y_ref[...] = jnp.zeros_like(y_ref)

    def step(i, carry):
        s_f, s_b = carry
        n_b = jnp.where(i < n_chunk_ctx, n_chunk_ctx - 1 - i, n_chunk - 1 - (i - n_chunk_ctx))
        outs = []
        for d, n, s in ((0, i, s_f), (1, n_b, s_b)):
            sl = pl.ds(pl.multiple_of(n * c, c), c)
            y_ref[sl, :] += _x3(_nt, rq_s[d, sl, :], s) + y0_s[d, sl, :]
            outs.append(_x3(_mm, s, p_s[d, n]) + z_s[d, n])
        return tuple(outs)

    zero = jnp.zeros((LANES, LANES), F32)
    lax.fori_loop(0, n_chunk, step, (zero, zero))


def _rwkv_scan(r, v, kk, kf, kb, bf, bb, lwf, lwb, *, t_ctx):
    bsz, t_tot, bw = r.shape
    n_chunk = t_tot // RWKV_CHUNK
    spec = pl.BlockSpec((None, t_tot, LANES), lambda b, h: (b, 0, h))
    return pl.pallas_call(
        functools.partial(_rwkv_scan_body, t_ctx=t_ctx),
        grid=(bsz, bw // LANES),
        in_specs=[spec] * 9,
        out_specs=spec,
        out_shape=jax.ShapeDtypeStruct((bsz, t_tot, bw), F32),
        scratch_shapes=[
            pltpu.VMEM((2, t_tot, LANES), F32),
            pltpu.VMEM((2, t_tot, LANES), F32),
            pltpu.VMEM((2, n_chunk, LANES, LANES), F32),
            pltpu.VMEM((2, n_chunk, LANES, LANES), F32),
        ],
        compiler_params=_cparams("arbitrary", "arbitrary"),
        name="rwkv_scan",
    )(r, v, kk, kf, kb, bf, bb, lwf, lwb)


def _rwkv_readout_body(y_ref, g_ref, bonus_ref, lng_ref, lnb_ref, o_ref):
    seg = _head_seg(B_WIDTH)
    y = y_ref[...]
    mean = _mm_x3(y, seg) * (1.0 / HEAD_DIM)
    yc = y - mean
    var = _mm_x3(yc * yc, seg) * (1.0 / HEAD_DIM)
    yn = yc * lax.rsqrt(var + RWKV_LN_EPS) * lng_ref[...] + lnb_ref[...]
    o_ref[...] = (yn + bonus_ref[...]) * g_ref[...]


def _rwkv_readout(y, g, bonus, ln_gain, ln_bias):
    n, bw = y.shape
    tm = 512
    spec = pl.BlockSpec((tm, bw), lambda i: (i, 0))
    vec = pl.BlockSpec((1, bw), lambda i: (0, 0))
    return pl.pallas_call(
        _rwkv_readout_body,
        grid=(n // tm,),
        in_specs=[spec, spec, spec, vec, vec],
        out_specs=spec,
        out_shape=jax.ShapeDtypeStruct((n, bw), F32),
        compiler_params=_cparams("arbitrary"),
        name="rwkv_readout",
    )(y, g, bonus, ln_gain.reshape(1, bw), ln_bias.reshape(1, bw))


def _na_body(q_ref, k_ref, v_ref, qg_ref, kg_ref, cos_ref, sin_ref, bias_ref, o_ref,
             qs, ks, vs, *, t_ctx, need_ctx):
    t_tot = q_ref.shape[0]
    t_lat = t_tot - t_ctx
    rows = t_lat // GRID_W
    win_r = min(NA_WIN_ROWS, rows)
    scale = HEAD_DIM ** -0.5
    seg = _head_seg(LANES)
    lane = _iota((1, LANES), 1)
    head_masks = [jnp.where(lane // HEAD_DIM == h, 1.0, 0.0) for h in range(LANES // HEAD_DIM)]
    half = (lane % HEAD_DIM) < HEAD_DIM // 2
    blk = 256

    def norm_block(i, carry):
        sl = pl.ds(pl.multiple_of(i * blk, blk), blk)
        q = q_ref[sl, :]
        k = k_ref[sl, :]
        q = q * lax.rsqrt(_mm_x3(q * q, seg) * (1.0 / HEAD_DIM) + EPS) * qg_ref[...]
        k = k * lax.rsqrt(_mm_x3(k * k, seg) * (1.0 / HEAD_DIM) + EPS) * kg_ref[...]
        cos = cos_ref[sl, :]
        sin = sin_ref[sl, :]

        def rope(t):
            swapped = jnp.where(half, pltpu.roll(t, LANES - HEAD_DIM // 2, 1), pltpu.roll(t, HEAD_DIM // 2, 1))
            return t * cos + swapped * sin

        qs[sl, :] = (rope(q) * scale).astype(BF16)
        ks[sl, :] = rope(k).astype(BF16)
        vs[sl, :] = v_ref[sl, :].astype(BF16)
        return carry

    lax.fori_loop(0, t_tot // blk, norm_block, 0)

    kc = ks[0:t_ctx, :]
    vc = vs[0:t_ctx, :]

    if need_ctx:
        qc = qs[0:t_ctx, :]
        out = jnp.zeros((t_ctx, LANES), F32)
        for mh in head_masks:
            s = _nt(qc * mh.astype(BF16), kc)
            p = jnp.exp(s - jnp.max(s, axis=-1, keepdims=True))
            o = _mm(p.astype(BF16), vc) / jnp.sum(p, axis=-1, keepdims=True)
            out = out + o * mh
        o_ref[0:t_ctx, :] = out
    else:
        o_ref[0:t_ctx, :] = jnp.zeros((t_ctx, LANES), F32)

    def q_row(r, carry):
        r_start = jnp.clip(r - win_r // 2, 0, rows - win_r)
        q_sl = pl.ds(pl.multiple_of(t_ctx + r * GRID_W, GRID_W), GRID_W)
        k_sl = pl.ds(pl.multiple_of(t_ctx + r_start * GRID_W, GRID_W), win_r * GRID_W)
        q = qs[q_sl, :]
        kw = ks[k_sl, :]
        vw = vs[k_sl, :]
        out = jnp.zeros((GRID_W, LANES), F32)
        for h, mh in enumerate(head_masks):
            qm = q * mh.astype(BF16)
            s_win = _nt(qm, kw) + bias_ref[h, r - r_start]
            s_ctx = _nt(qm, kc)
            m = jnp.maximum(jnp.max(s_win, axis=-1, keepdims=True), jnp.max(s_ctx, axis=-1, keepdims=True))
            p_win = jnp.exp(s_win - m)
            p_ctx = jnp.exp(s_ctx - m)
            den = jnp.sum(p_win, axis=-1, keepdims=True) + jnp.sum(p_ctx, axis=-1, keepdims=True)
            o = (_mm(p_win.astype(BF16), vw) + _mm(p_ctx.astype(BF16), vc)) / den
            out = out + o * mh
        o_ref[q_sl, :] = out
        return carry

    lax.fori_loop(0, rows, q_row, 0)


def _na_tables(t_ctx, t_lat):
    quarter = HEAD_DIM // 4
    pos = np.arange(t_lat)
    inv = ROPE_THETA ** (-np.arange(quarter, dtype=np.float32) / quarter)
    pos_r = (pos // GRID_W).astype(np.float32)
    pos_c = (pos % GRID_W).astype(np.float32)
    return pos_r, pos_c, inv


def _na_bias_table(rpb, rows):
    win_r = min(NA_WIN_ROWS, rows)
    c = np.arange(GRID_W)
    w_start = np.clip(c - NA_WIN_COLS // 2, 0, GRID_W - NA_WIN_COLS)
    kc = np.arange(GRID_W)
    in_win = (kc[None, :] >= w_start[:, None]) & (kc[None, :] < w_start[:, None] + NA_WIN_COLS)
    col_idx = np.clip(kc[None, :] - c[:, None] + NA_WIN_COLS - 1, 0, 2 * NA_WIN_COLS - 2)
    n_col = 2 * NA_WIN_COLS - 1
    onehot = jnp.asarray(col_idx[None] == np.arange(n_col)[:, None, None], F32)
    band = jnp.einsum('...hrj,jck->...hrck', rpb.astype(F32), onehot, precision=_HI)
    band = jnp.where(in_win, band, NEG_INF)
    per_off = [band[..., NA_WIN_ROWS - 1 - off:NA_WIN_ROWS - 1 - off + win_r, :, :] for off in range(win_r)]
    g = jnp.stack(per_off, axis=-4)
    g = jnp.swapaxes(g, -3, -2)
    return g.reshape(g.shape[:-2] + (win_r * GRID_W,))


def _na(pc, q_gain, k_gain, bias, layer, *, t_ctx, need_ctx):
    bsz, t_tot, _ = pc.shape
    t_lat = t_tot - t_ctx
    rows = t_lat // GRID_W
    win_r = min(NA_WIN_ROWS, rows)
    n_hp = C_WIDTH // LANES
    hpl = LANES // HEAD_DIM
    pos_r, pos_c, inv = _na_tables(t_ctx, t_lat)
    ang = np.concatenate([pos_r[:, None] * inv, pos_c[:, None] * inv], axis=-1)
    ang = np.concatenate([np.zeros((t_ctx, HEAD_DIM // 2), np.float32), ang], axis=0)
    cos = np.cos(ang)
    sin = np.sin(ang)
    cos_t = jnp.asarray(np.tile(np.concatenate([cos, cos], axis=-1), (1, hpl)), F32)
    sin_t = jnp.asarray(np.tile(np.concatenate([-sin, sin], axis=-1), (1, hpl)), F32)
    bias = bias.reshape(bias.shape[0], n_hp, hpl, win_r, GRID_W, win_r * GRID_W)

    def sec(s):
        return pl.BlockSpec((None, t_tot, LANES), lambda b, h, s=s: (b, 0, s * n_hp + h))

    gain = lambda g: jnp.tile(g.reshape(1, HEAD_DIM), (1, hpl))
    return pl.pallas_call(
        functools.partial(_na_body, t_ctx=t_ctx, need_ctx=need_ctx),
        grid=(bsz, n_hp),
        in_specs=[sec(0), sec(1), sec(2),
                  pl.BlockSpec((1, LANES), lambda b, h: (0, 0)),
                  pl.BlockSpec((1, LANES), lambda b, h: (0, 0)),
                  pl.BlockSpec((t_tot, LANES), lambda b, h: (0, 0)),
                  pl.BlockSpec((t_tot, LANES), lambda b, h: (0, 0)),
                  pl.BlockSpec((None, None, hpl, win_r, GRID_W, win_r * GRID_W),
                               lambda b, h: (layer, h, 0, 0, 0, 0))],
        out_specs=pl.BlockSpec((None, t_tot, LANES), lambda b, h: (b, 0, h)),
        out_shape=jax.ShapeDtypeStruct((bsz, t_tot, C_WIDTH), F32),
        scratch_shapes=[pltpu.VMEM((t_tot, LANES), BF16)] * 3,
        compiler_params=_cparams("arbitrary", "arbitrary"),
        name="na",
    )(pc, pc, pc, gain(q_gain), gain(k_gain), cos_t, sin_t, bias)


def _outproj_body(x_ref, a_ref, b_ref, c_ref, m_ref, g_ref, w_ref, wr_ref, br_ref, xo_ref, h_ref, route_ref):
    mix = (_mm(a_ref[...].astype(BF16), w_ref[0:A_WIDTH, :])
           + _mm(b_ref[...].astype(BF16), w_ref[A_WIDTH:A_WIDTH + B_WIDTH, :])
           + _mm(c_ref[...].astype(BF16), w_ref[A_WIDTH + B_WIDTH:, :]))
    x = x_ref[...] + m_ref[2:3, :] * mix
    xo_ref[...] = x
    h = x * lax.rsqrt(jnp.mean(x * x, axis=-1, keepdims=True) + EPS) * g_ref[...]
    h = h * (1.0 + m_ref[4:5, :]) + m_ref[3:4, :]
    h_ref[...] = h
    logits = _mm_hi(h, wr_ref[...]) + br_ref[...]
    lane = _iota(logits.shape, 1).astype(F32)
    big = float(LANES)
    is_g = (lane >= N_EXPERTS) & (lane < N_EXPERTS + N_GROUPS)
    gl = jnp.where(is_g, logits, -jnp.inf)
    gmax = jnp.max(gl, axis=-1, keepdims=True)
    g_w = 1.0 / jnp.sum(jnp.exp(gl - gmax), axis=-1, keepdims=True)
    g_sel = jnp.min(jnp.where(gl == gmax, lane, big), axis=-1, keepdims=True) - N_EXPERTS
    lo = g_sel * EXPERTS_PER_GROUP
    el = jnp.where((lane >= lo) & (lane < lo + EXPERTS_PER_GROUP), logits, -jnp.inf)
    m1 = jnp.max(el, axis=-1, keepdims=True)
    i1 = jnp.min(jnp.where(el == m1, lane, big), axis=-1, keepdims=True)
    el2 = jnp.where(lane == i1, -jnp.inf, el)
    m2 = jnp.max(el2, axis=-1, keepdims=True)
    i2 = jnp.min(jnp.where(el2 == m2, lane, big), axis=-1, keepdims=True)
    e2 = jnp.exp(m2 - m1)
    w1 = g_w / (1.0 + e2)
    route_ref[...] = (jnp.where(lane == 0.0, i1, 0.0) + jnp.where(lane == 1.0, i2, 0.0)
                      + jnp.where(lane == 2.0, w1, 0.0) + jnp.where(lane == 3.0, w1 * e2, 0.0))


def _outproj(x, a, b, c, mod_l, gain2, w_bf, w_router, b_router, *, bps, ctx_blocks, ctx_row):
    n = x.shape[0]
    tm = TOKEN_TILE
    row = functools.partial(_mod_row, blocks_per_seq=bps, ctx_blocks=ctx_blocks, ctx_row=ctx_row)
    tok = lambda w: pl.BlockSpec((tm, w), lambda i: (i, 0))
    full = lambda shape: pl.BlockSpec(shape, lambda i: (0,) * len(shape))
    return pl.pallas_call(
        _outproj_body,
        grid=(n // tm,),
        in_specs=[tok(D_MODEL), tok(A_WIDTH), tok(B_WIDTH), tok(C_WIDTH),
                  pl.BlockSpec((None, 6, D_MODEL), lambda i: (row(i), 0, 0)),
                  full((1, D_MODEL)), full((D_MODEL, D_MODEL)), full((D_MODEL, LANES)), full((1, LANES))],
        out_specs=[tok(D_MODEL), tok(D_MODEL), tok(LANES)],
        out_shape=[jax.ShapeDtypeStruct((n, D_MODEL), F32),
                   jax.ShapeDtypeStruct((n, D_MODEL), F32),
                   jax.ShapeDtypeStruct((n, LANES), F32)],
        compiler_params=_cparams("arbitrary"),
        name="outproj",
    )(x, a, b, c, mod_l, gain2.reshape(1, D_MODEL), w_bf, w_router, b_router)


def _route_tables(route, bsz, t_tot):
    n_pairs = TOP_K * t_tot
    p_len = n_pairs + 8 * N_EXPERTS + MOE_TILE
    ids = route[:, 0:TOP_K].astype(jnp.int32).reshape(bsz, t_tot, TOP_K)
    wts = route[:, TOP_K:2 * TOP_K].reshape(bsz, t_tot, TOP_K)
    ids = jnp.swapaxes(ids, 1, 2).reshape(bsz, n_pairs)
    wts = jnp.swapaxes(wts, 1, 2).reshape(bsz, n_pairs)
    toks = jnp.tile(jnp.arange(t_tot, dtype=jnp.int32), TOP_K)
    onehot = (ids[:, :, None] == jnp.arange(N_EXPERTS, dtype=jnp.int32)).astype(jnp.int32)
    rank = jnp.sum((jnp.cumsum(onehot, axis=1) - 1) * onehot, axis=-1)
    count = jnp.sum(onehot, axis=1)
    padded = (count + 7) // 8 * 8
    start = jnp.cumsum(padded, axis=1) - padded
    dest = jnp.take_along_axis(start, ids, axis=1) + rank
    b_idx = jnp.arange(bsz)[:, None]
    tok = jnp.zeros((bsz, p_len), jnp.int32).at[b_idx, dest].set(jnp.broadcast_to(toks, (bsz, n_pairs)))
    w = jnp.zeros((bsz, p_len), F32).at[b_idx, dest].set(wts)
    wcol = jnp.broadcast_to(w[:, :, None], (bsz, p_len, LANES))
    return tok.reshape(-1), start.reshape(-1), count.reshape(-1), wcol


def _moe_body(tok_ref, start_ref, count_ref, h_ref, w_ref, wg_ref, wu_ref, wd_ref, y_ref, hbuf, obuf):
    c = pl.program_id(0)
    e = pl.program_id(1)
    t_tot = h_ref.shape[0]
    p_len = w_ref.shape[0]
    tm = MOE_TILE

    @pl.when(jnp.logical_and(c == 0, e == 0))
    def _():
        hbuf[...] = jnp.zeros_like(hbuf)

    @pl.when(e == 0)
    def _():
        def zero(i, carry):
            y_ref[pl.ds(pl.multiple_of(i * TOKEN_TILE, TOKEN_TILE), TOKEN_TILE), :] = jnp.zeros(
                (TOKEN_TILE, D_MODEL), F32)
            return carry
        lax.fori_loop(0, t_tot // TOKEN_TILE, zero, 0)

    start = start_ref[c * N_EXPERTS + e]
    count = count_ref[c * N_EXPERTS + e]
    tok_base = c * p_len + start

    def tile(t, carry):
        base = pl.multiple_of(start + t * tm, 8)
        n_groups = (jnp.minimum(tm, count - t * tm) + 7) // 8
        pair0 = tok_base + t * tm

        def gather(g, carry):
            for j in range(8):
                tok = tok_ref[pair0 + g * 8 + j]
                hbuf[pl.ds(g * 8 + j, 1), :] = h_ref[pl.ds(tok, 1), :]
            return carry
        lax.fori_loop(0, n_groups, gather, 0)

        h = hbuf[...].astype(BF16)
        gate = _mm(h, wg_ref[...])
        up = _mm(h, wu_ref[...])
        w = w_ref[pl.ds(base, tm), :]
        hid = gate * _sigmoid(gate) * up * jnp.tile(w, (1, D_EXPERT // LANES))
        obuf[...] = _mm(hid.astype(BF16), wd_ref[...])

        def scatter(g, carry):
            for j in range(8):
                tok = tok_ref[pair0 + g * 8 + j]
                y_ref[pl.ds(tok, 1), :] += obuf[pl.ds(g * 8 + j, 1), :]
            return carry
        lax.fori_loop(0, n_groups, scatter, 0)
        return carry

    lax.fori_loop(0, (count + tm - 1) // tm, tile, 0)


def _moe(h, route, wg, wu, wd, layer, *, bsz, t_tot):
    tok, start, count, wcol = _route_tables(route, bsz, t_tot)
    p_len = wcol.shape[1]
    grid_spec = pltpu.PrefetchScalarGridSpec(
        num_scalar_prefetch=3,
        grid=(bsz, N_EXPERTS),
        in_specs=[
            pl.BlockSpec((None, t_tot, D_MODEL), lambda c, e, *_: (c, 0, 0)),
            pl.BlockSpec((None, p_len, LANES), lambda c, e, *_: (c, 0, 0)),
            pl.BlockSpec((None, None, D_MODEL, D_EXPERT), lambda c, e, *_: (layer, e, 0, 0)),
            pl.BlockSpec((None, None, D_MODEL, D_EXPERT), lambda c, e, *_: (layer, e, 0, 0)),
            pl.BlockSpec((None, None, D_EXPERT, D_MODEL), lambda c, e, *_: (layer, e, 0, 0)),
        ],
        out_specs=pl.BlockSpec((None, t_tot, D_MODEL), lambda c, e, *_: (c, 0, 0)),
        scratch_shapes=[pltpu.VMEM((MOE_TILE, D_MODEL), F32), pltpu.VMEM((MOE_TILE, D_MODEL), F32)],
    )
    y = pl.pallas_call(
        _moe_body,
        grid_spec=grid_spec,
        out_shape=jax.ShapeDtypeStruct((bsz, t_tot, D_MODEL), F32),
        compiler_params=_cparams("arbitrary", "arbitrary"),
        name="moe",
    )(tok, start, count, h.reshape(bsz, t_tot, D_MODEL), wcol, wg, wu, wd)
    return y.reshape(bsz * t_tot, D_MODEL)


def _residual_body(x_ref, y_ref, m_ref, o_ref):
    o_ref[...] = x_ref[...] + m_ref[5:6, :] * y_ref[...]


def _residual(x, y, mod_l, *, bps, ctx_blocks, ctx_row):
    n = x.shape[0]
    tm = TOKEN_TILE
    row = functools.partial(_mod_row, blocks_per_seq=bps, ctx_blocks=ctx_blocks, ctx_row=ctx_row)
    tok = pl.BlockSpec((tm, D_MODEL), lambda i: (i, 0))
    return pl.pallas_call(
        _residual_body,
        grid=(n // tm,),
        in_specs=[tok, tok, pl.BlockSpec((None, 6, D_MODEL), lambda i: (row(i), 0, 0))],
        out_specs=tok,
        out_shape=jax.ShapeDtypeStruct((n, D_MODEL), F32),
        compiler_params=_cparams("arbitrary"),
        name="residual",
    )(x, y, mod_l)


def kernel(x, c, ctx, c_ctx, norm1_gain, norm2_gain, w_ada, b_ada, w_in, w_out, hgrn_lb_logits, hgrn_gn_gain, rwkv_mu, rwkv_w0, rwkv_w_up, rwkv_a0, rwkv_a_up, rwkv_g_up, rwkv_kk_scale, rwkv_k_a, rwkv_r_k, rwkv_ln_gain, rwkv_ln_bias, na_q_gain, na_k_gain, na_rpb, w_router_group, b_router_group, w_router_expert, b_router_expert, w_exp_gate, w_exp_up, w_exp_down):
    bsz, t_lat, _ = x.shape
    t_ctx = ctx.shape[1]
    t_tot = t_ctx + t_lat
    n = bsz * t_tot
    assert t_ctx % TOKEN_TILE == 0 and t_lat % TOKEN_TILE == 0 and bsz < 16
    bps = t_tot // TOKEN_TILE
    tile_kw = dict(bps=bps, ctx_blocks=t_ctx // TOKEN_TILE, ctx_row=bsz)

    lb_p = jax.nn.softmax(hgrn_lb_logits.astype(F32), axis=1)
    lower_bounds = jnp.cumsum(lb_p, axis=1) - lb_p[:, :1]

    cc = jnp.zeros((16, D_MODEL), F32).at[:bsz].set(c.astype(F32)).at[bsz].set(c_ctx.astype(F32))
    mod = _ada_mod(cc, w_ada, b_ada).reshape(DEPTH, 16, 6, D_MODEL)

    w_in_bf = w_in.astype(BF16)
    w_out_bf = w_out.astype(BF16)
    wg_bf = w_exp_gate.astype(BF16).reshape(DEPTH, N_EXPERTS, D_MODEL, D_EXPERT)
    wu_bf = w_exp_up.astype(BF16).reshape(DEPTH, N_EXPERTS, D_MODEL, D_EXPERT)
    wd_bf = w_exp_down.astype(BF16).reshape(DEPTH, N_EXPERTS, D_EXPERT, D_MODEL)
    pad = LANES - N_EXPERTS - N_GROUPS
    w_router = jnp.concatenate([w_router_expert, w_router_group,
                                jnp.zeros((DEPTH, D_MODEL, pad), F32)], axis=-1)
    b_router = jnp.concatenate([b_router_expert, b_router_group, jnp.zeros((DEPTH, pad), F32)], axis=-1)

    na_bias = _na_bias_table(na_rpb, t_lat // GRID_W)
    xs = jnp.concatenate([ctx.astype(F32), x.astype(F32)], axis=1).reshape(n, D_MODEL)
    for layer in range(DEPTH):
        last = layer == DEPTH - 1
        pa, pb, pc = _inproj(xs, mod[layer], norm1_gain[layer], w_in_bf[layer], **tile_kw)
        a_mix = _hgrn(pa.reshape(bsz, t_tot, A_PROJ), lower_bounds[:, layer], hgrn_gn_gain[layer], t_ctx=t_ctx)
        prep = _rwkv_prep(pb.reshape(bsz, t_tot, B_PROJ), rwkv_mu[layer], rwkv_w0[layer], rwkv_w_up[layer],
                          rwkv_a0[layer], rwkv_a_up[layer], rwkv_g_up[layer], rwkv_kk_scale[layer],
                          rwkv_k_a[layer], rwkv_r_k[layer], t_ctx=t_ctx)
        y = _rwkv_scan(*prep[:9], t_ctx=t_ctx)
        b_mix = _rwkv_readout(y.reshape(n, B_WIDTH), prep[9].reshape(n, B_WIDTH), prep[10].reshape(n, B_WIDTH),
                              rwkv_ln_gain[layer], rwkv_ln_bias[layer])
        c_mix = _na(pc.reshape(bsz, t_tot, C_PROJ), na_q_gain[layer], na_k_gain[layer], na_bias, layer,
                    t_ctx=t_ctx, need_ctx=not last)
        xs, h2, route = _outproj(xs, a_mix.reshape(n, A_WIDTH), b_mix, c_mix.reshape(n, C_WIDTH), mod[layer],
                                norm2_gain[layer], w_out_bf[layer], w_router[layer],
                                b_router[layer].reshape(1, LANES), **tile_kw)
        y_moe = _moe(h2, route, wg_bf, wu_bf, wd_bf, layer, bsz=bsz, t_tot=t_tot)
        xs = _residual(xs, y_moe, mod[layer], **tile_kw)
    return xs.reshape(bsz, t_tot, D_MODEL)[:, t_ctx:].astype(x.dtype)
```

```python
import functools

import jax
import jax.numpy as jnp
import numpy as np
from jax import lax
from jax.experimental import pallas as pl
from jax.experimental.pallas import tpu as pltpu

F32 = jnp.float32
BF16 = jnp.bfloat16

D_MODEL = 1024
DEPTH = 4
GRID_W = 64
HEAD_DIM = 64
A_HEADS = 4
B_HEADS = 4
C_HEADS = 8
A_WIDTH = A_HEADS * HEAD_DIM
B_WIDTH = B_HEADS * HEAD_DIM
C_WIDTH = C_HEADS * HEAD_DIM
HGRN_CHUNK = 16
HGRN_F_FLOOR = 1e-20
RWKV_W_RANK = 64
RWKV_A_RANK = 64
RWKV_G_RANK = 128
RWKV_LN_EPS = 64e-5
NA_WIN_ROWS = 8
NA_WIN_COLS = 16
ROPE_THETA = 10000.0
N_GROUPS = 4
EXPERTS_PER_GROUP = 8
N_EXPERTS = N_GROUPS * EXPERTS_PER_GROUP
TOP_K = 2
D_EXPERT = 512
EPS = 1e-6
NEG_INF = -1e30
A_PROJ = 5 * A_WIDTH
B_PROJ = 3 * B_WIDTH + 2 * RWKV_W_RANK + 2 * RWKV_A_RANK + RWKV_G_RANK
C_PROJ = 3 * C_WIDTH
P_TOTAL = A_PROJ + B_PROJ + C_PROJ
B_TAIL = B_PROJ - 3 * B_WIDTH

LANES = 128
TOKEN_TILE = 256
RWKV_CHUNK = 64
RWKV_BLOCK = 128
HGRN_BLOCK = 128
NA_ROW_UNROLL = 4
HGRN_SCAN_UNROLL = 8
MOE_TILE = 192
VMEM_LIMIT = 56 * 1024 * 1024

_HI = lax.Precision.HIGHEST


def _cparams(*sem):
    return pltpu.CompilerParams(dimension_semantics=sem, vmem_limit_bytes=VMEM_LIMIT)


def _mm(a, b):
    return jnp.dot(a, b, preferred_element_type=F32)


def _mm_hi(a, b):
    return jnp.dot(a, b, preferred_element_type=F32, precision=_HI)


def _nt(a, b):
    return lax.dot_general(a, b, (((1,), (1,)), ((), ())), preferred_element_type=F32)


def _nt_hi(a, b):
    return lax.dot_general(a, b, (((1,), (1,)), ((), ())), preferred_element_type=F32, precision=_HI)


def _tn(a, b):
    return lax.dot_general(a, b, (((0,), (0,)), ((), ())), preferred_element_type=F32)


def _tn_hi(a, b):
    return lax.dot_general(a, b, (((0,), (0,)), ((), ())), preferred_element_type=F32, precision=_HI)


def _split3(a):
    a1 = a.astype(BF16)
    r1 = a - a1.astype(F32)
    a2 = r1.astype(BF16)
    a3 = (r1 - a2.astype(F32)).astype(BF16)
    return a1, a2, a3


def _mm_x3(a, m):
    a1, a2, a3 = _split3(a)
    return _mm(a1, m) + _mm(a2, m) + _mm(a3, m)


def _mm_l3(m, a):
    a1, a2, a3 = _split3(a)
    return _mm(m, a1) + _mm(m, a2) + _mm(m, a3)


def _split2(a):
    a1 = a.astype(BF16)
    return a1, (a - a1.astype(F32)).astype(BF16)


def _x3p(dot, a, b):
    return dot(a[0], b[0]) + (dot(a[0], b[1]) + dot(a[1], b[0]))


def _x3(dot, a, b):
    return _x3p(dot, _split2(a), _split2(b))


def _sigmoid(x):
    return 1.0 / (1.0 + jnp.exp(-x))


def _iota(shape, dim):
    return lax.broadcasted_iota(jnp.int32, shape, dim)


def _head_seg(n):
    return jnp.where(_iota((n, n), 0) // HEAD_DIM == _iota((n, n), 1) // HEAD_DIM, 1.0, 0.0).astype(BF16)


def _ada_body(c_ref, w_ref, b_ref, o_ref):
    c = c_ref[...]
    o_ref[...] = _mm_hi(c * _sigmoid(c), w_ref[...]) + b_ref[...]


def _ada_mod(cc, w_ada, b_ada):
    rows = cc.shape[0]
    tn = 1536
    return pl.pallas_call(
        _ada_body,
        grid=(DEPTH, 6 * D_MODEL // tn),
        in_specs=[
            pl.BlockSpec((rows, D_MODEL), lambda l, j: (0, 0)),
            pl.BlockSpec((None, D_MODEL, tn), lambda l, j: (l, 0, j)),
            pl.BlockSpec((None, 1, tn), lambda l, j: (l, 0, j)),
        ],
        out_specs=pl.BlockSpec((None, rows, tn), lambda l, j: (l, 0, j)),
        out_shape=jax.ShapeDtypeStruct((DEPTH, rows, 6 * D_MODEL), F32),
        compiler_params=_cparams("arbitrary", "arbitrary"),
        name="ada_mod",
    )(cc, w_ada, b_ada.reshape(DEPTH, 1, 6 * D_MODEL))


def _mod_row(i, blocks_per_seq, ctx_blocks, ctx_row):
    return jnp.where(i % blocks_per_seq < ctx_blocks, ctx_row, i // blocks_per_seq)


def _inproj_body(x_ref, m_ref, g_ref, w_ref, pa_ref, pb_ref, pc_ref):
    x = x_ref[...]
    h = x * lax.rsqrt(jnp.mean(x * x, axis=-1, keepdims=True) + EPS) * g_ref[...]
    h = (h * (1.0 + m_ref[1:2, :]) + m_ref[0:1, :]).astype(BF16)
    pa_ref[...] = _mm(h, w_ref[:, :A_PROJ])
    pb_ref[...] = _mm(h, w_ref[:, A_PROJ:A_PROJ + B_PROJ])
    pc_ref[...] = _mm(h, w_ref[:, A_PROJ + B_PROJ:])


def _inproj(x, mod_l, gain, w_bf, *, bps, ctx_blocks, ctx_row):
    n = x.shape[0]
    tm = TOKEN_TILE
    row = functools.partial(_mod_row, blocks_per_seq=bps, ctx_blocks=ctx_blocks, ctx_row=ctx_row)
    return pl.pallas_call(
        _inproj_body,
        grid=(n // tm,),
        in_specs=[
            pl.BlockSpec((tm, D_MODEL), lambda i: (i, 0)),
            pl.BlockSpec((None, 6, D_MODEL), lambda i: (row(i), 0, 0)),
            pl.BlockSpec((1, D_MODEL), lambda i: (0, 0)),
            pl.BlockSpec((D_MODEL, P_TOTAL), lambda i: (0, 0)),
        ],
        out_specs=[
            pl.BlockSpec((tm, A_PROJ), lambda i: (i, 0)),
            pl.BlockSpec((tm, B_PROJ), lambda i: (i, 0)),
            pl.BlockSpec((tm, C_PROJ), lambda i: (i, 0)),
        ],
        out_shape=[
            jax.ShapeDtypeStruct((n, A_PROJ), F32),
            jax.ShapeDtypeStruct((n, B_PROJ), F32),
            jax.ShapeDtypeStruct((n, C_PROJ), F32),
        ],
        compiler_params=_cparams("arbitrary"),
        name="inproj",
    )(x, mod_l, gain.reshape(1, D_MODEL), w_bf)


def _hgrn_body(q_ref, ff_ref, fb_ref, i_ref, g_ref, lb_ref, gn_ref, o_ref,
               qin_s, kout_s, cum_s, acc_s, *, t_ctx):
    t_tot = q_ref.shape[0]
    c = HGRN_CHUNK
    rb = HGRN_BLOCK
    n_blk = t_tot // rb
    n_chunk = t_tot // c
    n_chunk_ctx = t_ctx // c
    row = _iota((rb, rb), 0)
    col = _iota((rb, rb), 1)
    same = (row // c) == (col // c)
    same_bf = jnp.where(same, 1.0, 0.0).astype(BF16)
    seg = _head_seg(LANES)
    pos = _iota((rb, LANES), 0) % c
    blockdiag = _iota((LANES, LANES), 0) // HEAD_DIM == _iota((LANES, LANES), 1) // HEAD_DIM

    acc_s[...] = jnp.zeros_like(acc_s)

    for d in range(2):
        rev = d == 1
        f_ref = fb_ref if rev else ff_ref
        lb = lb_ref[d:d + 1, :]
        tri_bf = jnp.where(same & ((col >= row) if rev else (col <= row)), 1.0, 0.0).astype(BF16)

        def block(b, carry, rev=rev, f_ref=f_ref, lb=lb, tri_bf=tri_bf, d=d):
            sl = pl.ds(pl.multiple_of(b * rb, rb), rb)
            fpre = f_ref[sl, :]
            q = q_ref[sl, :]
            v = i_ref[sl, :]
            f = lb + (1.0 - lb) * _sigmoid(fpre)
            logf = jnp.log(jnp.maximum(f, HGRN_F_FLOOR))
            k = (1.0 - lb) * _sigmoid(-fpre)
            cum = _mm_l3(tri_bf, logf)
            tot = _mm_l3(same_bf, logf)
            o = jnp.zeros((rb, LANES), F32)
            for j in range(c):
                shift = (rb - j) % rb if rev else j
                valid = (pos <= c - 1 - j) if rev else (pos >= j)
                if j == 0:
                    ks, cs, vs = k, cum, v
                else:
                    ks = pltpu.roll(k, shift, 0)
                    cs = pltpu.roll(cum, shift, 0)
                    vs = pltpu.roll(v, shift, 0)
                dec = jnp.exp(jnp.where(valid, cum - cs, 0.0))
                prod = jnp.where(valid, q * ks * dec, 0.0)
                o = o + _mm(prod.astype(BF16), seg) * vs
            acc_s[sl, :] += o
            qin_s[d, sl, :] = q * jnp.exp(cum)
            kout_s[d, sl, :] = k * jnp.exp(tot - cum)
            cum_s[d, sl, :] = tot
            return carry

        lax.fori_loop(0, n_blk, block, 0)

    def step(it, carry):
        s_f, s_b = carry
        for j in range(HGRN_SCAN_UNROLL):
            i = it * HGRN_SCAN_UNROLL + j
            n_b = jnp.where(i < n_chunk_ctx, n_chunk_ctx - 1 - i, n_chunk - 1 - (i - n_chunk_ctx))
            outs = []
            for d, n, s in ((0, i, s_f), (1, n_b, s_b)):
                sl = pl.ds(pl.multiple_of(n * c, c), c)
                qin = qin_s[d, sl, :].astype(BF16)
                kout = kout_s[d, sl, :].astype(BF16)
                v = i_ref[sl, :].astype(BF16)
                dec = jnp.exp(cum_s[d, pl.ds(pl.multiple_of(n * c, c), 1), :])
                acc_s[sl, :] += _nt(qin, s.astype(BF16))
                kv = jnp.where(blockdiag, _tn(v, kout), 0.0)
                outs.append(s * dec + kv)
            s_f, s_b = outs
        return s_f, s_b

    zero = jnp.zeros((LANES, LANES), F32)
    lax.fori_loop(0, n_chunk // HGRN_SCAN_UNROLL, step, (zero, zero))

    def readout(b, carry):
        sl = pl.ds(pl.multiple_of(b * rb, rb), rb)
        o = acc_s[sl, :]
        ms = _mm_x3(o * o, seg) * (1.0 / HEAD_DIM)
        g = g_ref[sl, :]
        o_ref[sl, :] = o * lax.rsqrt(ms + EPS) * gn_ref[...] * (g * _sigmoid(g))
        return carry

    lax.fori_loop(0, n_blk, readout, 0)


def _hgrn(pa, lb, gn_gain, *, t_ctx):
    bsz, t_tot, _ = pa.shape
    n_hp = A_WIDTH // LANES

    def sec(s):
        return pl.BlockSpec((None, t_tot, LANES), lambda b, h, s=s: (b, 0, s * n_hp + h))

    return pl.pallas_call(
        functools.partial(_hgrn_body, t_ctx=t_ctx),
        grid=(bsz, n_hp),
        in_specs=[sec(0), sec(1), sec(2), sec(3), sec(4),
                  pl.BlockSpec((2, LANES), lambda b, h: (0, h)),
                  pl.BlockSpec((1, LANES), lambda b, h: (0, h))],
        out_specs=pl.BlockSpec((None, t_tot, LANES), lambda b, h: (b, 0, h)),
        out_shape=jax.ShapeDtypeStruct((bsz, t_tot, A_WIDTH), F32),
        scratch_shapes=[
            pltpu.VMEM((2, t_tot, LANES), F32),
            pltpu.VMEM((2, t_tot, LANES), F32),
            pltpu.VMEM((2, t_tot, LANES), F32),
            pltpu.VMEM((t_tot, LANES), F32),
        ],
        compiler_params=_cparams("arbitrary", "arbitrary"),
        name="hgrn",
    )(pa, pa, pa, pa, pa, lb, gn_gain.reshape(1, A_WIDTH))


def _rwkv_prep_body(x_ref, xp_ref, xn_ref, mu_ref, w0_ref, wup_ref, a0_ref, aup_ref, gup_ref,
                    kks_ref, ka_ref, rk_ref,
                    r_o, v_o, kk_o, kf_o, kb_o, bf_o, bb_o, lwf_o, lwb_o, g_o, bonus_o, *, t_ctx, t_tot):
    tm = x_ref.shape[0]
    r0 = pl.program_id(1) * tm
    prev_ok = jnp.logical_and(r0 != 0, r0 != t_ctx)
    next_ok = jnp.logical_and(r0 + tm != t_ctx, r0 + tm != t_tot)
    first = _iota((tm, 1), 0) == 0
    last = _iota((tm, 1), 0) == tm - 1

    def shifted(lo, hi):
        x = x_ref[:, lo:hi]
        p_row = jnp.where(prev_ok, xp_ref[7:8, lo:hi], 0.0)
        n_row = jnp.where(next_ok, xn_ref[0:1, lo:hi], 0.0)
        prev = jnp.where(first, p_row, pltpu.roll(x, 1, 0))
        nxt = jnp.where(last, n_row, pltpu.roll(x, tm - 1, 0))
        return x + (0.5 * (prev + nxt) - x) * mu_ref[:, lo:hi]

    bw = B_WIDTH
    r = shifted(0, bw)
    k = shifted(bw, 2 * bw)
    v = shifted(2 * bw, 3 * bw)
    tail = shifted(3 * bw, B_PROJ)
    seg = _head_seg(bw)

    kk = k * kks_ref[...]
    kk = kk / jnp.maximum(jnp.sqrt(_mm_x3(kk * kk, seg)), 1e-12)
    ksum = jnp.zeros_like(k)
    for d, (k_o, b_o, lw_o) in enumerate(((kf_o, bf_o, lwf_o), (kb_o, bb_o, lwb_o))):
        wd = tail[:, d * RWKV_W_RANK:(d + 1) * RWKV_W_RANK]
        ad = tail[:, 2 * RWKV_W_RANK + d * RWKV_A_RANK:2 * RWKV_W_RANK + (d + 1) * RWKV_A_RANK]
        u = -(w0_ref[d:d + 1, :] + _mm_hi(jnp.tanh(wd), wup_ref[d]))
        w = -(jnp.maximum(u, 0.0) + jnp.log(1.0 + jnp.exp(-jnp.abs(u)))) - 0.5
        a = _sigmoid(a0_ref[d:d + 1, :] + _mm_hi(ad, aup_ref[d]))
        k_d = k * (1.0 + (a - 1.0) * ka_ref[...])
        ksum = ksum + k_d
        k_o[...] = k_d
        b_o[...] = a * kk
        lw_o[...] = -jnp.exp(w)
    gd = tail[:, 2 * RWKV_W_RANK + 2 * RWKV_A_RANK:]
    r_o[...] = r
    v_o[...] = v
    kk_o[...] = kk
    g_o[...] = _mm_hi(_sigmoid(gd), gup_ref[...])
    bonus_o[...] = _mm_x3(r * ksum * rk_ref[...], seg) * v


def _rwkv_prep(pb, mu, w0, w_up, a0, a_up, g_up, kk_scale, k_a, r_k, *, t_ctx):
    bsz, t_tot, _ = pb.shape
    tm = TOKEN_TILE
    nb8 = t_tot // 8
    per8 = tm // 8
    bw = B_WIDTH
    full = lambda shape: pl.BlockSpec(shape, lambda b, i: (0,) * len(shape))
    out = jax.ShapeDtypeStruct((bsz, t_tot, bw), F32)
    return pl.pallas_call(
        functools.partial(_rwkv_prep_body, t_ctx=t_ctx, t_tot=t_tot),
        grid=(bsz, t_tot // tm),
        in_specs=[
            pl.BlockSpec((None, tm, B_PROJ), lambda b, i: (b, i, 0)),
            pl.BlockSpec((None, 8, B_PROJ), lambda b, i: (b, jnp.maximum(i * per8 - 1, 0), 0)),
            pl.BlockSpec((None, 8, B_PROJ), lambda b, i: (b, jnp.minimum((i + 1) * per8, nb8 - 1), 0)),
            full((1, B_PROJ)), full((2, bw)), full((2, RWKV_W_RANK, bw)), full((2, bw)),
            full((2, RWKV_A_RANK, bw)), full((RWKV_G_RANK, bw)), full((1, bw)), full((1, bw)), full((1, bw)),
        ],
        out_specs=[pl.BlockSpec((None, tm, bw), lambda b, i: (b, i, 0))] * 11,
        out_shape=[out] * 11,
        compiler_params=_cparams("arbitrary", "arbitrary"),
        name="rwkv_prep",
    )(pb, pb, pb, mu.reshape(1, B_PROJ), w0, w_up, a0, a_up, g_up,
      kk_scale.reshape(1, bw), k_a.reshape(1, bw), r_k.reshape(1, bw))


def _rwkv_scan_body(r_ref, v_ref, kk_ref, kf_ref, kb_ref, bf_ref, bb_ref, lwf_ref, lwb_ref, y_ref,
                    rq_s, y0_s, p_s, z_s, *, t_ctx):
    t_tot = r_ref.shape[0]
    c = RWKV_CHUNK
    rb = RWKV_BLOCK
    n_blk = t_tot // rb
    n_chunk = t_tot // c
    n_chunk_ctx = t_ctx // c
    cpb = rb // c
    n_double = c.bit_length() - 2
    row = _iota((rb, rb), 0)
    col = _iota((rb, rb), 1)
    same = (row // c) == (col // c)
    same_bf = jnp.where(same, 1.0, 0.0).astype(BF16)
    eye = jnp.where(row == col, 1.0, 0.0)
    lane = _iota((1, LANES), 1)
    head_masks = [jnp.where(lane // HEAD_DIM == h, 1.0, 0.0) for h in range(LANES // HEAD_DIM)]
    r128 = _iota((LANES, LANES), 0)
    c128 = _iota((LANES, LANES), 1)
    blockdiag = (r128 // HEAD_DIM) == (c128 // HEAD_DIM)
    diag128 = r128 == c128

    def block(bi, carry):
        sl = pl.ds(pl.multiple_of(bi * rb, rb), rb)
        for d in range(2):
            rev = d == 1
            k_ref, b_ref, lw_ref = (kb_ref, bb_ref, lwb_ref) if rev else (kf_ref, bf_ref, lwf_ref)
            incl = same & ((col >= row) if rev else (col <= row))
            strict = same & ((col > row) if rev else (col < row))
            incl_bf = jnp.where(incl, 1.0, 0.0).astype(BF16)
            lw = lw_ref[sl, :]
            r = r_ref[sl, :]
            v = v_ref[sl, :]
            kk = kk_ref[sl, :]
            k = k_ref[sl, :]
            b = b_ref[sl, :]
            cl = _mm_l3(incl_bf, lw)
            tot = _mm_l3(same_bf, lw)
            w_inv = jnp.exp(-cl)
            w_end = jnp.exp(tot - cl)
            kk_d = kk * jnp.exp(cl - lw)
            r_d = r * jnp.exp(cl)
            k_i = k * w_inv
            b_i = b * w_inv
            kb = jnp.concatenate([k_i, b_i], axis=0).astype(BF16)
            v_bf = v.astype(BF16)
            kkt = jnp.zeros((rb, LANES), F32)
            u = jnp.zeros((rb, LANES), F32)
            rq = jnp.zeros((rb, LANES), F32)
            y0 = jnp.zeros((rb, LANES), F32)
            for mh in head_masks:
                kkm = kk_d * mh
                rm = r_d * mh
                aa = _nt(jnp.concatenate([kkm, rm], axis=0).astype(BF16), kb)
                a_kb = jnp.where(strict, aa[:rb, :rb], 0.0)
                a_bb = jnp.where(strict, aa[:rb, rb:], 0.0)
                a_rk = jnp.where(incl, aa[rb:, :rb], 0.0)
                a_rb = jnp.where(incl, aa[rb:, rb:], 0.0)
                m_pow = -a_bb
                t_inv = eye + m_pow
                m_bf = m_pow.astype(BF16)
                m_pow = _mm(m_bf, m_bf)
                for _ in range(n_double - 1):
                    both = _mm(jnp.concatenate([t_inv, m_pow], axis=0).astype(BF16), m_pow.astype(BF16))
                    t_inv = t_inv + both[:rb]
                    m_pow = both[rb:]
                t_inv = t_inv + _mm(t_inv.astype(BF16), m_pow.astype(BF16))
                av = _mm(jnp.concatenate([a_kb, a_rk], axis=0).astype(BF16), v_bf)
                ku = _mm(t_inv.astype(BF16), jnp.concatenate([kkm, av[:rb]], axis=1).astype(BF16))
                rb_ku = _mm(a_rb.astype(BF16), ku.astype(BF16))
                kkt = kkt + ku[:, :LANES]
                u = u + ku[:, LANES:] * mh
                rq = rq + rm - rb_ku[:, :LANES]
                y0 = y0 + (av[rb:] - rb_ku[:, LANES:]) * mh
            rq_s[d, sl, :] = rq
            y0_s[d, sl, :] = y0
            b_e = (b * w_end).astype(BF16)
            k_e = (k * w_end).astype(BF16)
            ktu = jnp.concatenate([kkt, u], axis=1).astype(BF16)
            for n in range(cpb):
                rows = slice(n * c, (n + 1) * c)
                wc = jnp.exp(tot[n * c:n * c + 1, :])
                kub = _tn(ktu[rows], b_e[rows])
                p = jnp.where(diag128, wc, 0.0) - jnp.where(blockdiag, kub[:LANES], 0.0)
                z = jnp.where(blockdiag, _tn(v_bf[rows], k_e[rows]) - kub[LANES:], 0.0)
                p_s[d, bi * cpb + n] = p
                z_s[d, bi * cpb + n] = z
        return carry

    lax.fori_loop(0, n_blk, block, 0)

    y_ref[...] = jnp.zeros_like(y_ref)

    def step(i, carry):
        s_f, s_b = carry
        n_b = jnp.where(i < n_chunk_ctx, n_chunk_ctx - 1 - i, n_chunk - 1 - (i - n_chunk_ctx))
        outs = []
        for d, n, s in ((0, i, s_f), (1, n_b, s_b)):
            sl = pl.ds(pl.multiple_of(n * c, c), c)
            y_ref[sl, :] += _x3(_nt, rq_s[d, sl, :], s) + y0_s[d, sl, :]
            outs.append(_x3(_mm, s, p_s[d, n]) + z_s[d, n])
        return tuple(outs)

    zero = jnp.zeros((LANES, LANES), F32)
    lax.fori_loop(0, n_chunk, step, (zero, zero))


def _rwkv_scan(r, v, kk, kf, kb, bf, bb, lwf, lwb, *, t_ctx):
    bsz, t_tot, bw = r.shape
    n_chunk = t_tot // RWKV_CHUNK
    spec = pl.BlockSpec((None, t_tot, LANES), lambda b, h: (b, 0, h))
    return pl.pallas_call(
        functools.partial(_rwkv_scan_body, t_ctx=t_ctx),
        grid=(bsz, bw // LANES),
        in_specs=[spec] * 9,
        out_specs=spec,
        out_shape=jax.ShapeDtypeStruct((bsz, t_tot, bw), F32),
        scratch_shapes=[
            pltpu.VMEM((2, t_tot, LANES), F32),
            pltpu.VMEM((2, t_tot, LANES), F32),
            pltpu.VMEM((2, n_chunk, LANES, LANES), F32),
            pltpu.VMEM((2, n_chunk, LANES, LANES), F32),
        ],
        compiler_params=_cparams("arbitrary", "arbitrary"),
        name="rwkv_scan",
    )(r, v, kk, kf, kb, bf, bb, lwf, lwb)


def _rwkv_readout_body(y_ref, g_ref, bonus_ref, lng_ref, lnb_ref, o_ref):
    seg = _head_seg(B_WIDTH)
    y = y_ref[...]
    mean = _mm_x3(y, seg) * (1.0 / HEAD_DIM)
    yc = y - mean
    var = _mm_x3(yc * yc, seg) * (1.0 / HEAD_DIM)
    yn = yc * lax.rsqrt(var + RWKV_LN_EPS) * lng_ref[...] + lnb_ref[...]
    o_ref[...] = (yn + bonus_ref[...]) * g_ref[...]


def _rwkv_readout(y, g, bonus, ln_gain, ln_bias):
    n, bw = y.shape
    tm = 512
    spec = pl.BlockSpec((tm, bw), lambda i: (i, 0))
    vec = pl.BlockSpec((1, bw), lambda i: (0, 0))
    return pl.pallas_call(
        _rwkv_readout_body,
        grid=(n // tm,),
        in_specs=[spec, spec, spec, vec, vec],
        out_specs=spec,
        out_shape=jax.ShapeDtypeStruct((n, bw), F32),
        compiler_params=_cparams("arbitrary"),
        name="rwkv_readout",
    )(y, g, bonus, ln_gain.reshape(1, bw), ln_bias.reshape(1, bw))


def _na_body(q_ref, k_ref, v_ref, qg_ref, kg_ref, cos_ref, sin_ref, bias_ref, o_ref,
             qs, ks, vs, *, t_ctx, need_ctx):
    t_tot = q_ref.shape[0]
    t_lat = t_tot - t_ctx
    rows = t_lat // GRID_W
    win_r = min(NA_WIN_ROWS, rows)
    scale = HEAD_DIM ** -0.5
    seg = _head_seg(LANES)
    lane = _iota((1, LANES), 1)
    head_masks = [jnp.where(lane // HEAD_DIM == h, 1.0, 0.0) for h in range(LANES // HEAD_DIM)]
    half = (lane % HEAD_DIM) < HEAD_DIM // 2
    blk = 256

    def norm_block(i, carry):
        sl = pl.ds(pl.multiple_of(i * blk, blk), blk)
        q = q_ref[sl, :]
        k = k_ref[sl, :]
        q = q * lax.rsqrt(_mm_x3(q * q, seg) * (1.0 / HEAD_DIM) + EPS) * qg_ref[...]
        k = k * lax.rsqrt(_mm_x3(k * k, seg) * (1.0 / HEAD_DIM) + EPS) * kg_ref[...]
        cos = cos_ref[sl, :]
        sin = sin_ref[sl, :]

        def rope(t):
            swapped = jnp.where(half, pltpu.roll(t, LANES - HEAD_DIM // 2, 1), pltpu.roll(t, HEAD_DIM // 2, 1))
            return t * cos + swapped * sin

        qs[sl, :] = (rope(q) * scale).astype(BF16)
        ks[sl, :] = rope(k).astype(BF16)
        vs[sl, :] = v_ref[sl, :].astype(BF16)
        return carry

    lax.fori_loop(0, t_tot // blk, norm_block, 0)

    masks_bf = [mh.astype(BF16) for mh in head_masks]

    def by_head(q):
        return jnp.concatenate([q * mb for mb in masks_bf], axis=0)

    def merge_heads(o, n):
        return sum(o[h * n:(h + 1) * n] * mh for h, mh in enumerate(head_masks))

    if need_ctx:
        s = _nt(by_head(qs[0:t_ctx, :]), ks[0:t_ctx, :])
        p = jnp.exp(s - jnp.max(s, axis=-1, keepdims=True))
        o = _mm(p.astype(BF16), vs[0:t_ctx, :]) / jnp.sum(p, axis=-1, keepdims=True)
        o_ref[0:t_ctx, :] = merge_heads(o, t_ctx)
    else:
        o_ref[0:t_ctx, :] = jnp.zeros((t_ctx, LANES), F32)

    def q_rows(it, carry):
        for j in range(NA_ROW_UNROLL):
            r = it * NA_ROW_UNROLL + j
            r_start = jnp.clip(r - win_r // 2, 0, rows - win_r)
            q_sl = pl.ds(pl.multiple_of(t_ctx + r * GRID_W, GRID_W), GRID_W)
            k_sl = pl.ds(pl.multiple_of(t_ctx + r_start * GRID_W, GRID_W), win_r * GRID_W)
            q = by_head(qs[q_sl, :])
            s_win = _nt(q, ks[k_sl, :]) + bias_ref[r - r_start]
            s_ctx = _nt(q, ks[0:t_ctx, :])
            m = jnp.maximum(jnp.max(s_win, axis=-1, keepdims=True), jnp.max(s_ctx, axis=-1, keepdims=True))
            p_win = jnp.exp(s_win - m)
            p_ctx = jnp.exp(s_ctx - m)
            den = jnp.sum(p_win, axis=-1, keepdims=True) + jnp.sum(p_ctx, axis=-1, keepdims=True)
            o = (_mm(p_win.astype(BF16), vs[k_sl, :]) + _mm(p_ctx.astype(BF16), vs[0:t_ctx, :])) / den
            o_ref[q_sl, :] = merge_heads(o, GRID_W)
        return carry

    lax.fori_loop(0, rows // NA_ROW_UNROLL, q_rows, 0)


def _na_tables(t_ctx, t_lat):
    quarter = HEAD_DIM // 4
    pos = np.arange(t_lat)
    inv = ROPE_THETA ** (-np.arange(quarter, dtype=np.float32) / quarter)
    pos_r = (pos // GRID_W).astype(np.float32)
    pos_c = (pos % GRID_W).astype(np.float32)
    return pos_r, pos_c, inv


def _na_bias_table(rpb, rows):
    win_r = min(NA_WIN_ROWS, rows)
    c = np.arange(GRID_W)
    w_start = np.clip(c - NA_WIN_COLS // 2, 0, GRID_W - NA_WIN_COLS)
    kc = np.arange(GRID_W)
    in_win = (kc[None, :] >= w_start[:, None]) & (kc[None, :] < w_start[:, None] + NA_WIN_COLS)
    col_idx = np.clip(kc[None, :] - c[:, None] + NA_WIN_COLS - 1, 0, 2 * NA_WIN_COLS - 2)
    n_col = 2 * NA_WIN_COLS - 1
    onehot = jnp.asarray(col_idx[None] == np.arange(n_col)[:, None, None], F32)
    band = jnp.einsum('...hrj,jck->...hrck', rpb.astype(F32), onehot, precision=_HI)
    band = jnp.where(in_win, band, NEG_INF)
    per_off = [band[..., NA_WIN_ROWS - 1 - off:NA_WIN_ROWS - 1 - off + win_r, :, :] for off in range(win_r)]
    g = jnp.stack(per_off, axis=-4)
    g = jnp.swapaxes(g, -3, -2)
    return g.reshape(g.shape[:-2] + (win_r * GRID_W,))


def _na(pc, q_gain, k_gain, bias, layer, *, t_ctx, need_ctx):
    bsz, t_tot, _ = pc.shape
    t_lat = t_tot - t_ctx
    rows = t_lat // GRID_W
    win_r = min(NA_WIN_ROWS, rows)
    n_hp = C_WIDTH // LANES
    hpl = LANES // HEAD_DIM
    pos_r, pos_c, inv = _na_tables(t_ctx, t_lat)
    ang = np.concatenate([pos_r[:, None] * inv, pos_c[:, None] * inv], axis=-1)
    ang = np.concatenate([np.zeros((t_ctx, HEAD_DIM // 2), np.float32), ang], axis=0)
    cos = np.cos(ang)
    sin = np.sin(ang)
    cos_t = jnp.asarray(np.tile(np.concatenate([cos, cos], axis=-1), (1, hpl)), F32)
    sin_t = jnp.asarray(np.tile(np.concatenate([-sin, sin], axis=-1), (1, hpl)), F32)
    bias = bias.reshape(bias.shape[0], n_hp, hpl, win_r, GRID_W, win_r * GRID_W)
    bias = jnp.swapaxes(bias, 2, 3).reshape(bias.shape[0], n_hp, win_r, hpl * GRID_W, win_r * GRID_W)

    def sec(s):
        return pl.BlockSpec((None, t_tot, LANES), lambda b, h, s=s: (b, 0, s * n_hp + h))

    gain = lambda g: jnp.tile(g.reshape(1, HEAD_DIM), (1, hpl))
    return pl.pallas_call(
        functools.partial(_na_body, t_ctx=t_ctx, need_ctx=need_ctx),
        grid=(bsz, n_hp),
        in_specs=[sec(0), sec(1), sec(2),
                  pl.BlockSpec((1, LANES), lambda b, h: (0, 0)),
                  pl.BlockSpec((1, LANES), lambda b, h: (0, 0)),
                  pl.BlockSpec((t_tot, LANES), lambda b, h: (0, 0)),
                  pl.BlockSpec((t_tot, LANES), lambda b, h: (0, 0)),
                  pl.BlockSpec((None, None, win_r, hpl * GRID_W, win_r * GRID_W),
                               lambda b, h: (layer, h, 0, 0, 0))],
        out_specs=pl.BlockSpec((None, t_tot, LANES), lambda b, h: (b, 0, h)),
        out_shape=jax.ShapeDtypeStruct((bsz, t_tot, C_WIDTH), F32),
        scratch_shapes=[pltpu.VMEM((t_tot, LANES), BF16)] * 3,
        compiler_params=_cparams("arbitrary", "arbitrary"),
        name="na",
    )(pc, pc, pc, gain(q_gain), gain(k_gain), cos_t, sin_t, bias)


def _outproj_body(x_ref, a_ref, b_ref, c_ref, m_ref, g_ref, w_ref, wr_ref, br_ref, xo_ref, h_ref, route_ref):
    mix = (_mm(a_ref[...].astype(BF16), w_ref[0:A_WIDTH, :])
           + _mm(b_ref[...].astype(BF16), w_ref[A_WIDTH:A_WIDTH + B_WIDTH, :])
           + _mm(c_ref[...].astype(BF16), w_ref[A_WIDTH + B_WIDTH:, :]))
    x = x_ref[...] + m_ref[2:3, :] * mix
    xo_ref[...] = x
    h = x * lax.rsqrt(jnp.mean(x * x, axis=-1, keepdims=True) + EPS) * g_ref[...]
    h = h * (1.0 + m_ref[4:5, :]) + m_ref[3:4, :]
    h_ref[...] = h
    logits = _mm_hi(h, wr_ref[...]) + br_ref[...]
    lane = _iota(logits.shape, 1).astype(F32)
    big = float(LANES)
    is_g = (lane >= N_EXPERTS) & (lane < N_EXPERTS + N_GROUPS)
    gl = jnp.where(is_g, logits, -jnp.inf)
    gmax = jnp.max(gl, axis=-1, keepdims=True)
    g_w = 1.0 / jnp.sum(jnp.exp(gl - gmax), axis=-1, keepdims=True)
    g_sel = jnp.min(jnp.where(gl == gmax, lane, big), axis=-1, keepdims=True) - N_EXPERTS
    lo = g_sel * EXPERTS_PER_GROUP
    el = jnp.where((lane >= lo) & (lane < lo + EXPERTS_PER_GROUP), logits, -jnp.inf)
    m1 = jnp.max(el, axis=-1, keepdims=True)
    i1 = jnp.min(jnp.where(el == m1, lane, big), axis=-1, keepdims=True)
    el2 = jnp.where(lane == i1, -jnp.inf, el)
    m2 = jnp.max(el2, axis=-1, keepdims=True)
    i2 = jnp.min(jnp.where(el2 == m2, lane, big), axis=-1, keepdims=True)
    e2 = jnp.exp(m2 - m1)
    w1 = g_w / (1.0 + e2)
    route_ref[...] = (jnp.where(lane == 0.0, i1, 0.0) + jnp.where(lane == 1.0, i2, 0.0)
                      + jnp.where(lane == 2.0, w1, 0.0) + jnp.where(lane == 3.0, w1 * e2, 0.0))


def _outproj(x, a, b, c, mod_l, gain2, w_bf, w_router, b_router, *, bps, ctx_blocks, ctx_row):
    n = x.shape[0]
    tm = TOKEN_TILE
    row = functools.partial(_mod_row, blocks_per_seq=bps, ctx_blocks=ctx_blocks, ctx_row=ctx_row)
    tok = lambda w: pl.BlockSpec((tm, w), lambda i: (i, 0))
    full = lambda shape: pl.BlockSpec(shape, lambda i: (0,) * len(shape))
    return pl.pallas_call(
        _outproj_body,
        grid=(n // tm,),
        in_specs=[tok(D_MODEL), tok(A_WIDTH), tok(B_WIDTH), tok(C_WIDTH),
                  pl.BlockSpec((None, 6, D_MODEL), lambda i: (row(i), 0, 0)),
                  full((1, D_MODEL)), full((D_MODEL, D_MODEL)), full((D_MODEL, LANES)), full((1, LANES))],
        out_specs=[tok(D_MODEL), tok(D_MODEL), tok(LANES)],
        out_shape=[jax.ShapeDtypeStruct((n, D_MODEL), F32),
                   jax.ShapeDtypeStruct((n, D_MODEL), F32),
                   jax.ShapeDtypeStruct((n, LANES), F32)],
        compiler_params=_cparams("arbitrary"),
        name="outproj",
    )(x, a, b, c, mod_l, gain2.reshape(1, D_MODEL), w_bf, w_router, b_router)


def _route_tables(route, bsz, t_tot):
    n_pairs = TOP_K * t_tot
    ids = route[:, 0:TOP_K].astype(jnp.int32).reshape(bsz, t_tot, TOP_K)
    wts = route[:, TOP_K:2 * TOP_K].reshape(bsz, t_tot, TOP_K)
    ids = jnp.swapaxes(ids, 1, 2).reshape(bsz, n_pairs)
    wts = jnp.swapaxes(wts, 1, 2).reshape(bsz, n_pairs)
    key = ids * n_pairs + jnp.arange(n_pairs, dtype=jnp.int32)
    key, wts = lax.sort((key, wts), dimension=1, num_keys=1)
    tok = (key % n_pairs) % t_tot
    count = jnp.sum((ids[:, :, None] == jnp.arange(N_EXPERTS, dtype=jnp.int32)).astype(jnp.int32), axis=1)
    start = jnp.cumsum(count, axis=1) - count
    pad = ((0, 0), (0, MOE_TILE))
    return (jnp.pad(tok, pad).reshape(-1), jnp.pad(wts, pad).reshape(-1), start.reshape(-1), count.reshape(-1))


def _moe_body(tok_ref, w_ref, start_ref, count_ref, h_ref, wg_ref, wu_ref, wd_ref, y_ref, hbuf, obuf):
    c = pl.program_id(0)
    e = pl.program_id(1)
    t_tot = h_ref.shape[0]
    p_len = TOP_K * t_tot + MOE_TILE
    tm = MOE_TILE

    @pl.when(jnp.logical_and(c == 0, e == 0))
    def _():
        hbuf[...] = jnp.zeros_like(hbuf)

    @pl.when(e == 0)
    def _():
        def zero(i, carry):
            y_ref[pl.ds(pl.multiple_of(i * TOKEN_TILE, TOKEN_TILE), TOKEN_TILE), :] = jnp.zeros(
                (TOKEN_TILE, D_MODEL), F32)
            return carry
        lax.fori_loop(0, t_tot // TOKEN_TILE, zero, 0)

    count = count_ref[c * N_EXPERTS + e]
    pair_base = c * p_len + start_ref[c * N_EXPERTS + e]

    def tile(t, carry):
        n_valid = jnp.minimum(tm, count - t * tm)
        pair0 = pair_base + t * tm

        def gather(g, carry):
            for j in range(8):
                tok = tok_ref[pair0 + g * 8 + j]
                hbuf[pl.ds(g * 8 + j, 1), :] = h_ref[pl.ds(tok, 1), :]
            return carry
        lax.fori_loop(0, (n_valid + 7) // 8, gather, 0)

        h = hbuf[...].astype(BF16)
        gate = _mm(h, wg_ref[...])
        up = _mm(h, wu_ref[...])
        hid = gate * _sigmoid(gate) * up
        obuf[...] = _mm(hid.astype(BF16), wd_ref[...])

        def add_row(r):
            tok = tok_ref[pair0 + r]
            y_ref[pl.ds(tok, 1), :] += w_ref[pair0 + r] * obuf[pl.ds(r, 1), :]

        def scatter(g, carry):
            for j in range(8):
                add_row(g * 8 + j)
            return carry
        lax.fori_loop(0, n_valid // 8, scatter, 0)

        def scatter_tail(r, carry):
            add_row(r)
            return carry
        lax.fori_loop(n_valid // 8 * 8, n_valid, scatter_tail, 0)
        return carry

    lax.fori_loop(0, (count + tm - 1) // tm, tile, 0)


def _moe(h, route, wg, wu, wd, layer, *, bsz, t_tot):
    tok, w, start, count = _route_tables(route, bsz, t_tot)
    grid_spec = pltpu.PrefetchScalarGridSpec(
        num_scalar_prefetch=4,
        grid=(bsz, N_EXPERTS),
        in_specs=[
            pl.BlockSpec((None, t_tot, D_MODEL), lambda c, e, *_: (c, 0, 0)),
            pl.BlockSpec((None, None, D_MODEL, D_EXPERT), lambda c, e, *_: (layer, e, 0, 0)),
            pl.BlockSpec((None, None, D_MODEL, D_EXPERT), lambda c, e, *_: (layer, e, 0, 0)),
            pl.BlockSpec((None, None, D_EXPERT, D_MODEL), lambda c, e, *_: (layer, e, 0, 0)),
        ],
        out_specs=pl.BlockSpec((None, t_tot, D_MODEL), lambda c, e, *_: (c, 0, 0)),
        scratch_shapes=[pltpu.VMEM((MOE_TILE, D_MODEL), F32), pltpu.VMEM((MOE_TILE, D_MODEL), F32)],
    )
    y = pl.pallas_call(
        _moe_body,
        grid_spec=grid_spec,
        out_shape=jax.ShapeDtypeStruct((bsz, t_tot, D_MODEL), F32),
        compiler_params=_cparams("arbitrary", "arbitrary"),
        name="moe",
    )(tok, w, start, count, h.reshape(bsz, t_tot, D_MODEL), wg, wu, wd)
    return y.reshape(bsz * t_tot, D_MODEL)


def _residual_body(x_ref, y_ref, m_ref, o_ref):
    o_ref[...] = x_ref[...] + m_ref[5:6, :] * y_ref[...]


def _residual(x, y, mod_l, *, bps, ctx_blocks, ctx_row):
    n = x.shape[0]
    tm = TOKEN_TILE
    row = functools.partial(_mod_row, blocks_per_seq=bps, ctx_blocks=ctx_blocks, ctx_row=ctx_row)
    tok = pl.BlockSpec((tm, D_MODEL), lambda i: (i, 0))
    return pl.pallas_call(
        _residual_body,
        grid=(n // tm,),
        in_specs=[tok, tok, pl.BlockSpec((None, 6, D_MODEL), lambda i: (row(i), 0, 0))],
        out_specs=tok,
        out_shape=jax.ShapeDtypeStruct((n, D_MODEL), F32),
        compiler_params=_cparams("arbitrary"),
        name="residual",
    )(x, y, mod_l)


def kernel(x, c, ctx, c_ctx, norm1_gain, norm2_gain, w_ada, b_ada, w_in, w_out, hgrn_lb_logits, hgrn_gn_gain, rwkv_mu, rwkv_w0, rwkv_w_up, rwkv_a0, rwkv_a_up, rwkv_g_up, rwkv_kk_scale, rwkv_k_a, rwkv_r_k, rwkv_ln_gain, rwkv_ln_bias, na_q_gain, na_k_gain, na_rpb, w_router_group, b_router_group, w_router_expert, b_router_expert, w_exp_gate, w_exp_up, w_exp_down):
    bsz, t_lat, _ = x.shape
    t_ctx = ctx.shape[1]
    t_tot = t_ctx + t_lat
    n = bsz * t_tot
    assert t_ctx % TOKEN_TILE == 0 and t_lat % TOKEN_TILE == 0 and bsz < 16
    bps = t_tot // TOKEN_TILE
    tile_kw = dict(bps=bps, ctx_blocks=t_ctx // TOKEN_TILE, ctx_row=bsz)

    lb_p = jax.nn.softmax(hgrn_lb_logits.astype(F32), axis=1)
    lower_bounds = jnp.cumsum(lb_p, axis=1) - lb_p[:, :1]

    cc = jnp.zeros((16, D_MODEL), F32).at[:bsz].set(c.astype(F32)).at[bsz].set(c_ctx.astype(F32))
    mod = _ada_mod(cc, w_ada, b_ada).reshape(DEPTH, 16, 6, D_MODEL)

    w_in_bf = w_in.astype(BF16)
    w_out_bf = w_out.astype(BF16)
    wg_bf = w_exp_gate.astype(BF16).reshape(DEPTH, N_EXPERTS, D_MODEL, D_EXPERT)
    wu_bf = w_exp_up.astype(BF16).reshape(DEPTH, N_EXPERTS, D_MODEL, D_EXPERT)
    wd_bf = w_exp_down.astype(BF16).reshape(DEPTH, N_EXPERTS, D_EXPERT, D_MODEL)
    pad = LANES - N_EXPERTS - N_GROUPS
    w_router = jnp.concatenate([w_router_expert, w_router_group,
                                jnp.zeros((DEPTH, D_MODEL, pad), F32)], axis=-1)
    b_router = jnp.concatenate([b_router_expert, b_router_group, jnp.zeros((DEPTH, pad), F32)], axis=-1)

    na_bias = _na_bias_table(na_rpb, t_lat // GRID_W)
    xs = jnp.concatenate([ctx.astype(F32), x.astype(F32)], axis=1).reshape(n, D_MODEL)
    for layer in range(DEPTH):
        last = layer == DEPTH - 1
        pa, pb, pc = _inproj(xs, mod[layer], norm1_gain[layer], w_in_bf[layer], **tile_kw)
        a_mix = _hgrn(pa.reshape(bsz, t_tot, A_PROJ), lower_bounds[:, layer], hgrn_gn_gain[layer], t_ctx=t_ctx)
        prep = _rwkv_prep(pb.reshape(bsz, t_tot, B_PROJ), rwkv_mu[layer], rwkv_w0[layer], rwkv_w_up[layer],
                          rwkv_a0[layer], rwkv_a_up[layer], rwkv_g_up[layer], rwkv_kk_scale[layer],
                          rwkv_k_a[layer], rwkv_r_k[layer], t_ctx=t_ctx)
        y = _rwkv_scan(*prep[:9], t_ctx=t_ctx)
        b_mix = _rwkv_readout(y.reshape(n, B_WIDTH), prep[9].reshape(n, B_WIDTH), prep[10].reshape(n, B_WIDTH),
                              rwkv_ln_gain[layer], rwkv_ln_bias[layer])
        c_mix = _na(pc.reshape(bsz, t_tot, C_PROJ), na_q_gain[layer], na_k_gain[layer], na_bias, layer,
                    t_ctx=t_ctx, need_ctx=not last)
        xs, h2, route = _outproj(xs, a_mix.reshape(n, A_WIDTH), b_mix, c_mix.reshape(n, C_WIDTH), mod[layer],
                                norm2_gain[layer], w_out_bf[layer], w_router[layer],
                                b_router[layer].reshape(1, LANES), **tile_kw)
        y_moe = _moe(h2, route, wg_bf, wu_bf, wd_bf, layer, bsz=bsz, t_tot=t_tot)
        xs = _residual(xs, y_moe, mod[layer], **tile_kw)
    return xs.reshape(bsz, t_tot, D_MODEL)[:, t_ctx:].astype(x.dtype)
```

```python
import functools

import jax
import jax.numpy as jnp
import numpy as np
from jax import lax
from jax.experimental import pallas as pl
from jax.experimental.pallas import tpu as pltpu

F32 = jnp.float32
BF16 = jnp.bfloat16

D_MODEL = 1024
DEPTH = 4
GRID_W = 64
HEAD_DIM = 64
A_HEADS = 4
B_HEADS = 4
C_HEADS = 8
A_WIDTH = A_HEADS * HEAD_DIM
B_WIDTH = B_HEADS * HEAD_DIM
C_WIDTH = C_HEADS * HEAD_DIM
HGRN_CHUNK = 16
HGRN_F_FLOOR = 1e-20
RWKV_W_RANK = 64
RWKV_A_RANK = 64
RWKV_G_RANK = 128
RWKV_LN_EPS = 64e-5
NA_WIN_ROWS = 8
NA_WIN_COLS = 16
ROPE_THETA = 10000.0
N_GROUPS = 4
EXPERTS_PER_GROUP = 8
N_EXPERTS = N_GROUPS * EXPERTS_PER_GROUP
TOP_K = 2
D_EXPERT = 512
EPS = 1e-6
NEG_INF = -1e30
A_PROJ = 5 * A_WIDTH
B_PROJ = 3 * B_WIDTH + 2 * RWKV_W_RANK + 2 * RWKV_A_RANK + RWKV_G_RANK
C_PROJ = 3 * C_WIDTH
P_TOTAL = A_PROJ + B_PROJ + C_PROJ
B_TAIL = B_PROJ - 3 * B_WIDTH

LANES = 128
TOKEN_TILE = 256
RWKV_CHUNK = 64
RWKV_BLOCK = 128
HGRN_BLOCK = 128
NA_ROW_UNROLL = 4
HGRN_SCAN_UNROLL = 8
MOE_TILE = 192
VMEM_LIMIT = 56 * 1024 * 1024

_HI = lax.Precision.HIGHEST


def _cparams(*sem):
    return pltpu.CompilerParams(dimension_semantics=sem, vmem_limit_bytes=VMEM_LIMIT)


def _mm(a, b):
    return jnp.dot(a, b, preferred_element_type=F32)


def _mm_hi(a, b):
    return jnp.dot(a, b, preferred_element_type=F32, precision=_HI)


def _nt(a, b):
    return lax.dot_general(a, b, (((1,), (1,)), ((), ())), preferred_element_type=F32)


def _tn(a, b):
    return lax.dot_general(a, b, (((0,), (0,)), ((), ())), preferred_element_type=F32)


def _split3(a):
    a1 = a.astype(BF16)
    r1 = a - a1.astype(F32)
    a2 = r1.astype(BF16)
    a3 = (r1 - a2.astype(F32)).astype(BF16)
    return a1, a2, a3


def _mm_x3(a, m):
    a1, a2, a3 = _split3(a)
    return _mm(a1, m) + _mm(a2, m) + _mm(a3, m)


def _mm_l3(m, a):
    a1, a2, a3 = _split3(a)
    return _mm(m, a1) + _mm(m, a2) + _mm(m, a3)


def _sigmoid(x):
    return 1.0 / (1.0 + jnp.exp(-x))


def _iota(shape, dim):
    return lax.broadcasted_iota(jnp.int32, shape, dim)


def _head_seg(n):
    return jnp.where(_iota((n, n), 0) // HEAD_DIM == _iota((n, n), 1) // HEAD_DIM, 1.0, 0.0).astype(BF16)


def _ada_body(c_ref, w_ref, b_ref, o_ref):
    c = c_ref[...]
    o_ref[...] = _mm_hi(c * _sigmoid(c), w_ref[...]) + b_ref[...]


def _ada_mod(cc, w_ada, b_ada):
    rows = cc.shape[0]
    tn = 1536
    return pl.pallas_call(
        _ada_body,
        grid=(DEPTH, 6 * D_MODEL // tn),
        in_specs=[
            pl.BlockSpec((rows, D_MODEL), lambda l, j: (0, 0)),
            pl.BlockSpec((None, D_MODEL, tn), lambda l, j: (l, 0, j)),
            pl.BlockSpec((None, 1, tn), lambda l, j: (l, 0, j)),
        ],
        out_specs=pl.BlockSpec((None, rows, tn), lambda l, j: (l, 0, j)),
        out_shape=jax.ShapeDtypeStruct((DEPTH, rows, 6 * D_MODEL), F32),
        compiler_params=_cparams("arbitrary", "arbitrary"),
        name="ada_mod",
    )(cc, w_ada, b_ada.reshape(DEPTH, 1, 6 * D_MODEL))


def _mod_row(i, blocks_per_seq, ctx_blocks, ctx_row):
    return jnp.where(i % blocks_per_seq < ctx_blocks, ctx_row, i // blocks_per_seq)


def _inproj_body(x_ref, m_ref, g_ref, w_ref, pa_ref, pb_ref, pc_ref):
    x = x_ref[...]
    h = x * lax.rsqrt(jnp.mean(x * x, axis=-1, keepdims=True) + EPS) * g_ref[...]
    h = (h * (1.0 + m_ref[1:2, :]) + m_ref[0:1, :]).astype(BF16)
    pa_ref[...] = _mm(h, w_ref[:, :A_PROJ])
    pb_ref[...] = _mm(h, w_ref[:, A_PROJ:A_PROJ + B_PROJ])
    pc_ref[...] = _mm(h, w_ref[:, A_PROJ + B_PROJ:])


def _inproj(x, mod_l, gain, w_bf, *, bps, ctx_blocks, ctx_row):
    n = x.shape[0]
    tm = TOKEN_TILE
    row = functools.partial(_mod_row, blocks_per_seq=bps, ctx_blocks=ctx_blocks, ctx_row=ctx_row)
    return pl.pallas_call(
        _inproj_body,
        grid=(n // tm,),
        in_specs=[
            pl.BlockSpec((tm, D_MODEL), lambda i: (i, 0)),
            pl.BlockSpec((None, 6, D_MODEL), lambda i: (row(i), 0, 0)),
            pl.BlockSpec((1, D_MODEL), lambda i: (0, 0)),
            pl.BlockSpec((D_MODEL, P_TOTAL), lambda i: (0, 0)),
        ],
        out_specs=[
            pl.BlockSpec((tm, A_PROJ), lambda i: (i, 0)),
            pl.BlockSpec((tm, B_PROJ), lambda i: (i, 0)),
            pl.BlockSpec((tm, C_PROJ), lambda i: (i, 0)),
        ],
        out_shape=[
            jax.ShapeDtypeStruct((n, A_PROJ), F32),
            jax.ShapeDtypeStruct((n, B_PROJ), F32),
            jax.ShapeDtypeStruct((n, C_PROJ), F32),
        ],
        compiler_params=_cparams("arbitrary"),
        name="inproj",
    )(x, mod_l, gain.reshape(1, D_MODEL), w_bf)


def _hgrn_body(q_ref, ff_ref, fb_ref, i_ref, g_ref, lb_ref, gn_ref, o_ref,
               qin_s, kout_s, cum_s, acc_s, *, t_ctx):
    t_tot = q_ref.shape[0]
    c = HGRN_CHUNK
    rb = HGRN_BLOCK
    n_blk = t_tot // rb
    n_chunk = t_tot // c
    n_chunk_ctx = t_ctx // c
    row = _iota((rb, rb), 0)
    col = _iota((rb, rb), 1)
    same = (row // c) == (col // c)
    same_bf = jnp.where(same, 1.0, 0.0).astype(BF16)
    seg = _head_seg(LANES)
    pos = _iota((rb, LANES), 0) % c
    blockdiag = _iota((LANES, LANES), 0) // HEAD_DIM == _iota((LANES, LANES), 1) // HEAD_DIM

    acc_s[...] = jnp.zeros_like(acc_s)

    for d in range(2):
        rev = d == 1
        f_ref = fb_ref if rev else ff_ref
        lb = lb_ref[d:d + 1, :]
        tri_bf = jnp.where(same & ((col >= row) if rev else (col <= row)), 1.0, 0.0).astype(BF16)

        def block(b, carry, rev=rev, f_ref=f_ref, lb=lb, tri_bf=tri_bf, d=d):
            sl = pl.ds(pl.multiple_of(b * rb, rb), rb)
            fpre = f_ref[sl, :]
            q = q_ref[sl, :]
            v = i_ref[sl, :]
            f = lb + (1.0 - lb) * _sigmoid(fpre)
            logf = jnp.log(jnp.maximum(f, HGRN_F_FLOOR))
            k = (1.0 - lb) * _sigmoid(-fpre)
            cum = _mm_l3(tri_bf, logf)
            tot = _mm_l3(same_bf, logf)
            o = jnp.zeros((rb, LANES), F32)
            for j in range(c):
                shift = (rb - j) % rb if rev else j
                valid = (pos <= c - 1 - j) if rev else (pos >= j)
                if j == 0:
                    ks, cs, vs = k, cum, v
                else:
                    ks = pltpu.roll(k, shift, 0)
                    cs = pltpu.roll(cum, shift, 0)
                    vs = pltpu.roll(v, shift, 0)
                dec = jnp.exp(jnp.where(valid, cum - cs, 0.0))
                prod = jnp.where(valid, q * ks * dec, 0.0)
                o = o + _mm(prod.astype(BF16), seg) * vs
            acc_s[sl, :] += o
            qin_s[d, sl, :] = q * jnp.exp(cum)
            kout_s[d, sl, :] = k * jnp.exp(tot - cum)
            cum_s[d, sl, :] = tot
            return carry

        lax.fori_loop(0, n_blk, block, 0)

    def step(it, carry):
        idx = [it * HGRN_SCAN_UNROLL + j for j in range(HGRN_SCAN_UNROLL)]
        order = [(i, jnp.where(i < n_chunk_ctx, n_chunk_ctx - 1 - i, n_chunk - 1 - (i - n_chunk_ctx))) for i in idx]
        sls = [[pl.ds(pl.multiple_of(n[d] * c, c), c) for d in range(2)] for n in order]
        kvs = [[jnp.where(blockdiag, _tn(i_ref[sl[d], :].astype(BF16), kout_s[d, sl[d], :].astype(BF16)), 0.0)
                for d in range(2)] for sl in sls]
        decs = [[jnp.exp(cum_s[d, pl.ds(pl.multiple_of(n[d] * c, c), 1), :]) for d in range(2)] for n in order]
        states = list(carry)
        for j in range(HGRN_SCAN_UNROLL):
            for d in range(2):
                acc_s[sls[j][d], :] += _nt(qin_s[d, sls[j][d], :].astype(BF16), states[d].astype(BF16))
                states[d] = states[d] * decs[j][d] + kvs[j][d]
        return tuple(states)

    zero = jnp.zeros((LANES, LANES), F32)
    lax.fori_loop(0, n_chunk // HGRN_SCAN_UNROLL, step, (zero, zero))

    def readout(b, carry):
        sl = pl.ds(pl.multiple_of(b * rb, rb), rb)
        o = acc_s[sl, :]
        ms = _mm_x3(o * o, seg) * (1.0 / HEAD_DIM)
        g = g_ref[sl, :]
        o_ref[sl, :] = o * lax.rsqrt(ms + EPS) * gn_ref[...] * (g * _sigmoid(g))
        return carry

    lax.fori_loop(0, n_blk, readout, 0)


def _hgrn(pa, lb, gn_gain, *, t_ctx):
    bsz, t_tot, _ = pa.shape
    n_hp = A_WIDTH // LANES

    def sec(s):
        return pl.BlockSpec((None, t_tot, LANES), lambda b, h, s=s: (b, 0, s * n_hp + h))

    return pl.pallas_call(
        functools.partial(_hgrn_body, t_ctx=t_ctx),
        grid=(bsz, n_hp),
        in_specs=[sec(0), sec(1), sec(2), sec(3), sec(4),
                  pl.BlockSpec((2, LANES), lambda b, h: (0, h)),
                  pl.BlockSpec((1, LANES), lambda b, h: (0, h))],
        out_specs=pl.BlockSpec((None, t_tot, LANES), lambda b, h: (b, 0, h)),
        out_shape=jax.ShapeDtypeStruct((bsz, t_tot, A_WIDTH), F32),
        scratch_shapes=[
            pltpu.VMEM((2, t_tot, LANES), F32),
            pltpu.VMEM((2, t_tot, LANES), F32),
            pltpu.VMEM((2, t_tot, LANES), F32),
            pltpu.VMEM((t_tot, LANES), F32),
        ],
        compiler_params=_cparams("arbitrary", "arbitrary"),
        name="hgrn",
    )(pa, pa, pa, pa, pa, lb, gn_gain.reshape(1, A_WIDTH))


def _rwkv_prep_body(x_ref, xp_ref, xn_ref, mu_ref, w0_ref, wup_ref, a0_ref, aup_ref, gup_ref,
                    kks_ref, ka_ref, rk_ref,
                    r_o, v_o, kk_o, kf_o, kb_o, bf_o, bb_o, lwf_o, lwb_o, g_o, bonus_o, *, t_ctx, t_tot):
    tm = x_ref.shape[0]
    r0 = pl.program_id(1) * tm
    prev_ok = jnp.logical_and(r0 != 0, r0 != t_ctx)
    next_ok = jnp.logical_and(r0 + tm != t_ctx, r0 + tm != t_tot)
    first = _iota((tm, 1), 0) == 0
    last = _iota((tm, 1), 0) == tm - 1

    def shifted(lo, hi):
        x = x_ref[:, lo:hi]
        p_row = jnp.where(prev_ok, xp_ref[7:8, lo:hi], 0.0)
        n_row = jnp.where(next_ok, xn_ref[0:1, lo:hi], 0.0)
        prev = jnp.where(first, p_row, pltpu.roll(x, 1, 0))
        nxt = jnp.where(last, n_row, pltpu.roll(x, tm - 1, 0))
        return x + (0.5 * (prev + nxt) - x) * mu_ref[:, lo:hi]

    bw = B_WIDTH
    r = shifted(0, bw)
    k = shifted(bw, 2 * bw)
    v = shifted(2 * bw, 3 * bw)
    tail = shifted(3 * bw, B_PROJ)
    seg = _head_seg(bw)

    kk = k * kks_ref[...]
    kk = kk / jnp.maximum(jnp.sqrt(_mm_x3(kk * kk, seg)), 1e-12)
    ksum = jnp.zeros_like(k)
    for d, (k_o, b_o, lw_o) in enumerate(((kf_o, bf_o, lwf_o), (kb_o, bb_o, lwb_o))):
        wd = tail[:, d * RWKV_W_RANK:(d + 1) * RWKV_W_RANK]
        ad = tail[:, 2 * RWKV_W_RANK + d * RWKV_A_RANK:2 * RWKV_W_RANK + (d + 1) * RWKV_A_RANK]
        u = -(w0_ref[d:d + 1, :] + _mm_hi(jnp.tanh(wd), wup_ref[d]))
        w = -(jnp.maximum(u, 0.0) + jnp.log(1.0 + jnp.exp(-jnp.abs(u)))) - 0.5
        a = _sigmoid(a0_ref[d:d + 1, :] + _mm_hi(ad, aup_ref[d]))
        k_d = k * (1.0 + (a - 1.0) * ka_ref[...])
        ksum = ksum + k_d
        k_o[...] = k_d
        b_o[...] = a * kk
        lw_o[...] = -jnp.exp(w)
    gd = tail[:, 2 * RWKV_W_RANK + 2 * RWKV_A_RANK:]
    r_o[...] = r
    v_o[...] = v
    kk_o[...] = kk
    g_o[...] = _mm_hi(_sigmoid(gd), gup_ref[...])
    bonus_o[...] = _mm_x3(r * ksum * rk_ref[...], seg) * v


def _rwkv_prep(pb, mu, w0, w_up, a0, a_up, g_up, kk_scale, k_a, r_k, *, t_ctx):
    bsz, t_tot, _ = pb.shape
    tm = TOKEN_TILE
    nb8 = t_tot // 8
    per8 = tm // 8
    bw = B_WIDTH
    full = lambda shape: pl.BlockSpec(shape, lambda b, i: (0,) * len(shape))
    out = jax.ShapeDtypeStruct((bsz, t_tot, bw), F32)
    return pl.pallas_call(
        functools.partial(_rwkv_prep_body, t_ctx=t_ctx, t_tot=t_tot),
        grid=(bsz, t_tot // tm),
        in_specs=[
            pl.BlockSpec((None, tm, B_PROJ), lambda b, i: (b, i, 0)),
            pl.BlockSpec((None, 8, B_PROJ), lambda b, i: (b, jnp.maximum(i * per8 - 1, 0), 0)),
            pl.BlockSpec((None, 8, B_PROJ), lambda b, i: (b, jnp.minimum((i + 1) * per8, nb8 - 1), 0)),
            full((1, B_PROJ)), full((2, bw)), full((2, RWKV_W_RANK, bw)), full((2, bw)),
            full((2, RWKV_A_RANK, bw)), full((RWKV_G_RANK, bw)), full((1, bw)), full((1, bw)), full((1, bw)),
        ],
        out_specs=[pl.BlockSpec((None, tm, bw), lambda b, i: (b, i, 0))] * 11,
        out_shape=[out] * 11,
        compiler_params=_cparams("arbitrary", "arbitrary"),
        name="rwkv_prep",
    )(pb, pb, pb, mu.reshape(1, B_PROJ), w0, w_up, a0, a_up, g_up,
      kk_scale.reshape(1, bw), k_a.reshape(1, bw), r_k.reshape(1, bw))


def _rwkv_scan_body(r_ref, v_ref, kk_ref, kf_ref, kb_ref, bf_ref, bb_ref, lwf_ref, lwb_ref, y_ref,
                    rq_s, y0_s, p_s, z_s, *, t_ctx):
    t_tot = r_ref.shape[0]
    c = RWKV_CHUNK
    rb = RWKV_BLOCK
    n_blk = t_tot // rb
    n_chunk = t_tot // c
    n_chunk_ctx = t_ctx // c
    cpb = rb // c
    n_double = c.bit_length() - 2
    row = _iota((rb, rb), 0)
    col = _iota((rb, rb), 1)
    same = (row // c) == (col // c)
    same_bf = jnp.where(same, 1.0, 0.0).astype(BF16)
    eye = jnp.where(row == col, 1.0, 0.0)
    lane = _iota((1, LANES), 1)
    head_masks = [jnp.where(lane // HEAD_DIM == h, 1.0, 0.0) for h in range(LANES // HEAD_DIM)]
    r128 = _iota((LANES, LANES), 0)
    c128 = _iota((LANES, LANES), 1)
    blockdiag = (r128 // HEAD_DIM) == (c128 // HEAD_DIM)
    diag128 = r128 == c128

    def block(bi, carry):
        sl = pl.ds(pl.multiple_of(bi * rb, rb), rb)
        r = r_ref[sl, :]
        v = v_ref[sl, :]
        kk = kk_ref[sl, :]
        v_bf = v.astype(BF16)
        pre = []
        for d in range(2):
            rev = d == 1
            k_ref, b_ref, lw_ref = (kb_ref, bb_ref, lwb_ref) if rev else (kf_ref, bf_ref, lwf_ref)
            incl = same & ((col >= row) if rev else (col <= row))
            strict = same & ((col > row) if rev else (col < row))
            incl_bf = jnp.where(incl, 1.0, 0.0).astype(BF16)
            lw = lw_ref[sl, :]
            k = k_ref[sl, :]
            b = b_ref[sl, :]
            cl = _mm_l3(incl_bf, lw)
            tot = _mm_l3(same_bf, lw)
            w_inv = jnp.exp(-cl)
            w_end = jnp.exp(tot - cl)
            pre.append(dict(incl=incl, strict=strict, tot=tot, kk_d=kk * jnp.exp(cl - lw), r_d=r * jnp.exp(cl),
                            kb=jnp.concatenate([k * w_inv, b * w_inv], axis=0).astype(BF16),
                            b_e=(b * w_end).astype(BF16), k_e=(k * w_end).astype(BF16)))
        chains = [dict(d=d, mh=mh, kkm=pre[d]['kk_d'] * mh, rm=pre[d]['r_d'] * mh)
                  for d in range(2) for mh in head_masks]
        for ch in chains:
            p = pre[ch['d']]
            aa = _nt(jnp.concatenate([ch['kkm'], ch['rm']], axis=0).astype(BF16), p['kb'])
            ch['a_kb'] = jnp.where(p['strict'], aa[:rb, :rb], 0.0)
            ch['a_rk'] = jnp.where(p['incl'], aa[rb:, :rb], 0.0)
            ch['a_rb'] = jnp.where(p['incl'], aa[rb:, rb:], 0.0)
            m = -jnp.where(p['strict'], aa[:rb, rb:], 0.0)
            ch['t'] = eye + m
            ch['m'] = m
        for ch in chains:
            m_bf = ch['m'].astype(BF16)
            ch['m'] = _mm(m_bf, m_bf)
        for _ in range(n_double - 1):
            for ch in chains:
                both = _mm(jnp.concatenate([ch['t'], ch['m']], axis=0).astype(BF16), ch['m'].astype(BF16))
                ch['t'] = ch['t'] + both[:rb]
                ch['m'] = both[rb:]
        for ch in chains:
            ch['t'] = ch['t'] + _mm(ch['t'].astype(BF16), ch['m'].astype(BF16))
            ch['av'] = _mm(jnp.concatenate([ch['a_kb'], ch['a_rk']], axis=0).astype(BF16), v_bf)
        for ch in chains:
            ch['ku'] = _mm(ch['t'].astype(BF16), jnp.concatenate([ch['kkm'], ch['av'][:rb]], axis=1).astype(BF16))
        for ch in chains:
            ch['rb_ku'] = _mm(ch['a_rb'].astype(BF16), ch['ku'].astype(BF16))
        for d in range(2):
            p = pre[d]
            mine = [ch for ch in chains if ch['d'] == d]
            kkt = sum(ch['ku'][:, :LANES] for ch in mine)
            u = sum(ch['ku'][:, LANES:] * ch['mh'] for ch in mine)
            rq_s[d, sl, :] = sum(ch['rm'] - ch['rb_ku'][:, :LANES] for ch in mine)
            y0_s[d, sl, :] = sum((ch['av'][rb:] - ch['rb_ku'][:, LANES:]) * ch['mh'] for ch in mine)
            ktu = jnp.concatenate([kkt, u], axis=1).astype(BF16)
            for n in range(cpb):
                rows = slice(n * c, (n + 1) * c)
                wc = jnp.exp(p['tot'][n * c:n * c + 1, :])
                kub = _tn(ktu[rows], p['b_e'][rows])
                p_s[d, bi * cpb + n] = jnp.where(diag128, wc, 0.0) - jnp.where(blockdiag, kub[:LANES], 0.0)
                p_z = jnp.where(blockdiag, _tn(v_bf[rows], p['k_e'][rows]) - kub[LANES:], 0.0)
                z_s[d, bi * cpb + n] = p_z
        return carry

    lax.fori_loop(0, n_blk, block, 0)

    y_ref[...] = jnp.zeros_like(y_ref)

    def step(i, carry):
        n_b = jnp.where(i < n_chunk_ctx, n_chunk_ctx - 1 - i, n_chunk - 1 - (i - n_chunk_ctx))
        chunk = (i, n_b)
        s_bf = [s.astype(BF16) for s in carry]
        ys = [_nt(rq_s[d, pl.ds(pl.multiple_of(chunk[d] * c, c), c), :].astype(BF16), s_bf[d]) for d in range(2)]
        states = [_mm(s_bf[d], p_s[d, chunk[d]].astype(BF16)) + z_s[d, chunk[d]] for d in range(2)]
        for d in range(2):
            sl = pl.ds(pl.multiple_of(chunk[d] * c, c), c)
            y_ref[sl, :] += ys[d] + y0_s[d, sl, :]
        return tuple(states)

    zero = jnp.zeros((LANES, LANES), F32)
    lax.fori_loop(0, n_chunk, step, (zero, zero))


def _rwkv_scan(r, v, kk, kf, kb, bf, bb, lwf, lwb, *, t_ctx):
    bsz, t_tot, bw = r.shape
    n_chunk = t_tot // RWKV_CHUNK
    spec = pl.BlockSpec((None, t_tot, LANES), lambda b, h: (b, 0, h))
    return pl.pallas_call(
        functools.partial(_rwkv_scan_body, t_ctx=t_ctx),
        grid=(bsz, bw // LANES),
        in_specs=[spec] * 9,
        out_specs=spec,
        out_shape=jax.ShapeDtypeStruct((bsz, t_tot, bw), F32),
        scratch_shapes=[
            pltpu.VMEM((2, t_tot, LANES), F32),
            pltpu.VMEM((2, t_tot, LANES), F32),
            pltpu.VMEM((2, n_chunk, LANES, LANES), F32),
            pltpu.VMEM((2, n_chunk, LANES, LANES), F32),
        ],
        compiler_params=_cparams("arbitrary", "arbitrary"),
        name="rwkv_scan",
    )(r, v, kk, kf, kb, bf, bb, lwf, lwb)


def _rwkv_readout_body(y_ref, g_ref, bonus_ref, lng_ref, lnb_ref, o_ref):
    seg = _head_seg(B_WIDTH)
    y = y_ref[...]
    mean = _mm_x3(y, seg) * (1.0 / HEAD_DIM)
    yc = y - mean
    var = _mm_x3(yc * yc, seg) * (1.0 / HEAD_DIM)
    yn = yc * lax.rsqrt(var + RWKV_LN_EPS) * lng_ref[...] + lnb_ref[...]
    o_ref[...] = (yn + bonus_ref[...]) * g_ref[...]


def _rwkv_readout(y, g, bonus, ln_gain, ln_bias):
    n, bw = y.shape
    tm = 512
    spec = pl.BlockSpec((tm, bw), lambda i: (i, 0))
    vec = pl.BlockSpec((1, bw), lambda i: (0, 0))
    return pl.pallas_call(
        _rwkv_readout_body,
        grid=(n // tm,),
        in_specs=[spec, spec, spec, vec, vec],
        out_specs=spec,
        out_shape=jax.ShapeDtypeStruct((n, bw), F32),
        compiler_params=_cparams("arbitrary"),
        name="rwkv_readout",
    )(y, g, bonus, ln_gain.reshape(1, bw), ln_bias.reshape(1, bw))


def _na_body(q_ref, k_ref, v_ref, qg_ref, kg_ref, cos_ref, sin_ref, bias_ref, o_ref,
             qs, ks, vs, *, t_ctx, need_ctx):
    t_tot = q_ref.shape[0]
    t_lat = t_tot - t_ctx
    rows = t_lat // GRID_W
    win_r = min(NA_WIN_ROWS, rows)
    scale = HEAD_DIM ** -0.5
    seg = _head_seg(LANES)
    lane = _iota((1, LANES), 1)
    head_masks = [jnp.where(lane // HEAD_DIM == h, 1.0, 0.0) for h in range(LANES // HEAD_DIM)]
    half = (lane % HEAD_DIM) < HEAD_DIM // 2
    blk = 256

    def norm_block(i, carry):
        sl = pl.ds(pl.multiple_of(i * blk, blk), blk)
        q = q_ref[sl, :]
        k = k_ref[sl, :]
        q = q * lax.rsqrt(_mm((q * q).astype(BF16), seg) * (1.0 / HEAD_DIM) + EPS) * qg_ref[...]
        k = k * lax.rsqrt(_mm((k * k).astype(BF16), seg) * (1.0 / HEAD_DIM) + EPS) * kg_ref[...]
        cos = cos_ref[sl, :]
        sin = sin_ref[sl, :]

        def rope(t):
            swapped = jnp.where(half, pltpu.roll(t, LANES - HEAD_DIM // 2, 1), pltpu.roll(t, HEAD_DIM // 2, 1))
            return t * cos + swapped * sin

        qs[sl, :] = (rope(q) * scale).astype(BF16)
        ks[sl, :] = rope(k).astype(BF16)
        vs[sl, :] = v_ref[sl, :].astype(BF16)
        return carry

    lax.fori_loop(0, t_tot // blk, norm_block, 0)

    masks_bf = [mh.astype(BF16) for mh in head_masks]

    def by_head(q):
        return jnp.concatenate([q * mb for mb in masks_bf], axis=0)

    def merge_heads(o, n):
        return sum(o[h * n:(h + 1) * n] * mh for h, mh in enumerate(head_masks))

    if need_ctx:
        s = _nt(by_head(qs[0:t_ctx, :]), ks[0:t_ctx, :])
        p = jnp.exp(s - jnp.max(s, axis=-1, keepdims=True))
        o = _mm(p.astype(BF16), vs[0:t_ctx, :]) / jnp.sum(p, axis=-1, keepdims=True)
        o_ref[0:t_ctx, :] = merge_heads(o, t_ctx)
    else:
        o_ref[0:t_ctx, :] = jnp.zeros((t_ctx, LANES), F32)

    def q_rows(it, carry):
        rs = [it * NA_ROW_UNROLL + j for j in range(NA_ROW_UNROLL)]
        starts = [jnp.clip(r - win_r // 2, 0, rows - win_r) for r in rs]
        q_sl = [pl.ds(pl.multiple_of(t_ctx + r * GRID_W, GRID_W), GRID_W) for r in rs]
        k_sl = [pl.ds(pl.multiple_of(t_ctx + r0 * GRID_W, GRID_W), win_r * GRID_W) for r0 in starts]
        qs_ = [by_head(qs[sl, :]) for sl in q_sl]
        s_win = [_nt(q, ks[sl, :]) + bias_ref[r - r0] for q, sl, r, r0 in zip(qs_, k_sl, rs, starts)]
        s_ctx = [_nt(q, ks[0:t_ctx, :]) for q in qs_]
        ms = [jnp.maximum(jnp.max(a, axis=-1, keepdims=True), jnp.max(b, axis=-1, keepdims=True))
              for a, b in zip(s_win, s_ctx)]
        p_win = [jnp.exp(a - m) for a, m in zip(s_win, ms)]
        p_ctx = [jnp.exp(b - m) for b, m in zip(s_ctx, ms)]
        den = [jnp.sum(a, axis=-1, keepdims=True) + jnp.sum(b, axis=-1, keepdims=True) for a, b in zip(p_win, p_ctx)]
        o_win = [_mm(a.astype(BF16), vs[sl, :]) for a, sl in zip(p_win, k_sl)]
        o_ctx = [_mm(b.astype(BF16), vs[0:t_ctx, :]) for b in p_ctx]
        for sl, a, b, d in zip(q_sl, o_win, o_ctx, den):
            o_ref[sl, :] = merge_heads((a + b) / d, GRID_W)
        return carry

    lax.fori_loop(0, rows // NA_ROW_UNROLL, q_rows, 0)


def _na_tables(t_ctx, t_lat):
    quarter = HEAD_DIM // 4
    pos = np.arange(t_lat)
    inv = ROPE_THETA ** (-np.arange(quarter, dtype=np.float32) / quarter)
    pos_r = (pos // GRID_W).astype(np.float32)
    pos_c = (pos % GRID_W).astype(np.float32)
    return pos_r, pos_c, inv


def _na_bias_table(rpb, rows):
    win_r = min(NA_WIN_ROWS, rows)
    c = np.arange(GRID_W)
    w_start = np.clip(c - NA_WIN_COLS // 2, 0, GRID_W - NA_WIN_COLS)
    kc = np.arange(GRID_W)
    in_win = (kc[None, :] >= w_start[:, None]) & (kc[None, :] < w_start[:, None] + NA_WIN_COLS)
    col_idx = np.clip(kc[None, :] - c[:, None] + NA_WIN_COLS - 1, 0, 2 * NA_WIN_COLS - 2)
    n_col = 2 * NA_WIN_COLS - 1
    onehot = jnp.asarray(col_idx[None] == np.arange(n_col)[:, None, None], F32)
    band = jnp.einsum('...hrj,jck->...hrck', rpb.astype(F32), onehot, precision=_HI)
    band = jnp.where(in_win, band, NEG_INF)
    per_off = [band[..., NA_WIN_ROWS - 1 - off:NA_WIN_ROWS - 1 - off + win_r, :, :] for off in range(win_r)]
    g = jnp.stack(per_off, axis=-4)
    g = jnp.swapaxes(g, -3, -2)
    return g.reshape(g.shape[:-2] + (win_r * GRID_W,))


def _na(pc, q_gain, k_gain, bias, layer, *, t_ctx, need_ctx):
    bsz, t_tot, _ = pc.shape
    t_lat = t_tot - t_ctx
    rows = t_lat // GRID_W
    win_r = min(NA_WIN_ROWS, rows)
    n_hp = C_WIDTH // LANES
    hpl = LANES // HEAD_DIM
    pos_r, pos_c, inv = _na_tables(t_ctx, t_lat)
    ang = np.concatenate([pos_r[:, None] * inv, pos_c[:, None] * inv], axis=-1)
    ang = np.concatenate([np.zeros((t_ctx, HEAD_DIM // 2), np.float32), ang], axis=0)
    cos = np.cos(ang)
    sin = np.sin(ang)
    cos_t = jnp.asarray(np.tile(np.concatenate([cos, cos], axis=-1), (1, hpl)), F32)
    sin_t = jnp.asarray(np.tile(np.concatenate([-sin, sin], axis=-1), (1, hpl)), F32)
    bias = bias.reshape(bias.shape[0], n_hp, hpl, win_r, GRID_W, win_r * GRID_W)
    bias = jnp.swapaxes(bias, 2, 3).reshape(bias.shape[0], n_hp, win_r, hpl * GRID_W, win_r * GRID_W)

    def sec(s):
        return pl.BlockSpec((None, t_tot, LANES), lambda b, h, s=s: (b, 0, s * n_hp + h))

    gain = lambda g: jnp.tile(g.reshape(1, HEAD_DIM), (1, hpl))
    return pl.pallas_call(
        functools.partial(_na_body, t_ctx=t_ctx, need_ctx=need_ctx),
        grid=(bsz, n_hp),
        in_specs=[sec(0), sec(1), sec(2),
                  pl.BlockSpec((1, LANES), lambda b, h: (0, 0)),
                  pl.BlockSpec((1, LANES), lambda b, h: (0, 0)),
                  pl.BlockSpec((t_tot, LANES), lambda b, h: (0, 0)),
                  pl.BlockSpec((t_tot, LANES), lambda b, h: (0, 0)),
                  pl.BlockSpec((None, None, win_r, hpl * GRID_W, win_r * GRID_W),
                               lambda b, h: (layer, h, 0, 0, 0))],
        out_specs=pl.BlockSpec((None, t_tot, LANES), lambda b, h: (b, 0, h)),
        out_shape=jax.ShapeDtypeStruct((bsz, t_tot, C_WIDTH), F32),
        scratch_shapes=[pltpu.VMEM((t_tot, LANES), BF16)] * 3,
        compiler_params=_cparams("arbitrary", "arbitrary"),
        name="na",
    )(pc, pc, pc, gain(q_gain), gain(k_gain), cos_t, sin_t, bias)


def _outproj_body(x_ref, a_ref, b_ref, c_ref, m_ref, g_ref, w_ref, wr_ref, br_ref, xo_ref, h_ref, route_ref):
    mix = (_mm(a_ref[...].astype(BF16), w_ref[0:A_WIDTH, :])
           + _mm(b_ref[...].astype(BF16), w_ref[A_WIDTH:A_WIDTH + B_WIDTH, :])
           + _mm(c_ref[...].astype(BF16), w_ref[A_WIDTH + B_WIDTH:, :]))
    x = x_ref[...] + m_ref[2:3, :] * mix
    xo_ref[...] = x
    h = x * lax.rsqrt(jnp.mean(x * x, axis=-1, keepdims=True) + EPS) * g_ref[...]
    h = h * (1.0 + m_ref[4:5, :]) + m_ref[3:4, :]
    h_ref[...] = h
    logits = _mm_hi(h, wr_ref[...]) + br_ref[...]
    lane = _iota(logits.shape, 1).astype(F32)
    big = float(LANES)
    is_g = (lane >= N_EXPERTS) & (lane < N_EXPERTS + N_GROUPS)
    gl = jnp.where(is_g, logits, -jnp.inf)
    gmax = jnp.max(gl, axis=-1, keepdims=True)
    g_w = 1.0 / jnp.sum(jnp.exp(gl - gmax), axis=-1, keepdims=True)
    g_sel = jnp.min(jnp.where(gl == gmax, lane, big), axis=-1, keepdims=True) - N_EXPERTS
    lo = g_sel * EXPERTS_PER_GROUP
    el = jnp.where((lane >= lo) & (lane < lo + EXPERTS_PER_GROUP), logits, -jnp.inf)
    m1 = jnp.max(el, axis=-1, keepdims=True)
    i1 = jnp.min(jnp.where(el == m1, lane, big), axis=-1, keepdims=True)
    el2 = jnp.where(lane == i1, -jnp.inf, el)
    m2 = jnp.max(el2, axis=-1, keepdims=True)
    i2 = jnp.min(jnp.where(el2 == m2, lane, big), axis=-1, keepdims=True)
    e2 = jnp.exp(m2 - m1)
    w1 = g_w / (1.0 + e2)
    route_ref[...] = (jnp.where(lane == 0.0, i1, 0.0) + jnp.where(lane == 1.0, i2, 0.0)
                      + jnp.where(lane == 2.0, w1, 0.0) + jnp.where(lane == 3.0, w1 * e2, 0.0))


def _outproj(x, a, b, c, mod_l, gain2, w_bf, w_router, b_router, *, bps, ctx_blocks, ctx_row):
    n = x.shape[0]
    tm = TOKEN_TILE
    row = functools.partial(_mod_row, blocks_per_seq=bps, ctx_blocks=ctx_blocks, ctx_row=ctx_row)
    tok = lambda w: pl.BlockSpec((tm, w), lambda i: (i, 0))
    full = lambda shape: pl.BlockSpec(shape, lambda i: (0,) * len(shape))
    return pl.pallas_call(
        _outproj_body,
        grid=(n // tm,),
        in_specs=[tok(D_MODEL), tok(A_WIDTH), tok(B_WIDTH), tok(C_WIDTH),
                  pl.BlockSpec((None, 6, D_MODEL), lambda i: (row(i), 0, 0)),
                  full((1, D_MODEL)), full((D_MODEL, D_MODEL)), full((D_MODEL, LANES)), full((1, LANES))],
        out_specs=[tok(D_MODEL), tok(D_MODEL), tok(LANES)],
        out_shape=[jax.ShapeDtypeStruct((n, D_MODEL), F32),
                   jax.ShapeDtypeStruct((n, D_MODEL), F32),
                   jax.ShapeDtypeStruct((n, LANES), F32)],
        compiler_params=_cparams("arbitrary"),
        name="outproj",
    )(x, a, b, c, mod_l, gain2.reshape(1, D_MODEL), w_bf, w_router, b_router)


def _route_tables(route, bsz, t_tot):
    n_pairs = TOP_K * t_tot
    ids = route[:, 0:TOP_K].astype(jnp.int32).reshape(bsz, t_tot, TOP_K)
    wts = route[:, TOP_K:2 * TOP_K].reshape(bsz, t_tot, TOP_K)
    ids = jnp.swapaxes(ids, 1, 2).reshape(bsz, n_pairs)
    wts = jnp.swapaxes(wts, 1, 2).reshape(bsz, n_pairs)
    key = ids * n_pairs + jnp.arange(n_pairs, dtype=jnp.int32)
    key, wts = lax.sort((key, wts), dimension=1, num_keys=1)
    tok = (key % n_pairs) % t_tot
    count = jnp.sum((ids[:, :, None] == jnp.arange(N_EXPERTS, dtype=jnp.int32)).astype(jnp.int32), axis=1)
    start = jnp.cumsum(count, axis=1) - count
    pad = ((0, 0), (0, MOE_TILE))
    return (jnp.pad(tok, pad).reshape(-1), jnp.pad(wts, pad).reshape(-1), start.reshape(-1), count.reshape(-1))


def _moe_body(tok_ref, w_ref, start_ref, count_ref, h_ref, wg_ref, wu_ref, wd_ref, y_ref, hbuf, obuf):
    c = pl.program_id(0)
    e = pl.program_id(1)
    t_tot = h_ref.shape[0]
    p_len = TOP_K * t_tot + MOE_TILE
    tm = MOE_TILE

    @pl.when(jnp.logical_and(c == 0, e == 0))
    def _():
        hbuf[...] = jnp.zeros_like(hbuf)

    @pl.when(e == 0)
    def _():
        def zero(i, carry):
            y_ref[pl.ds(pl.multiple_of(i * TOKEN_TILE, TOKEN_TILE), TOKEN_TILE), :] = jnp.zeros(
                (TOKEN_TILE, D_MODEL), F32)
            return carry
        lax.fori_loop(0, t_tot // TOKEN_TILE, zero, 0)

    count = count_ref[c * N_EXPERTS + e]
    pair_base = c * p_len + start_ref[c * N_EXPERTS + e]

    def tile(t, carry):
        n_valid = jnp.minimum(tm, count - t * tm)
        pair0 = pair_base + t * tm

        def gather(g, carry):
            for j in range(8):
                tok = tok_ref[pair0 + g * 8 + j]
                hbuf[pl.ds(g * 8 + j, 1), :] = h_ref[pl.ds(tok, 1), :]
            return carry
        lax.fori_loop(0, (n_valid + 7) // 8, gather, 0)

        h = hbuf[...].astype(BF16)
        gate = _mm(h, wg_ref[...])
        up = _mm(h, wu_ref[...])
        hid = gate * _sigmoid(gate) * up
        obuf[...] = _mm(hid.astype(BF16), wd_ref[...])

        def add_row(r):
            tok = tok_ref[pair0 + r]
            y_ref[pl.ds(tok, 1), :] += w_ref[pair0 + r] * obuf[pl.ds(r, 1), :]

        def scatter(g, carry):
            toks = [tok_ref[pair0 + g * 8 + j] for j in range(8)]
            new = [y_ref[pl.ds(toks[j], 1), :] + w_ref[pair0 + g * 8 + j] * obuf[pl.ds(g * 8 + j, 1), :]
                   for j in range(8)]
            for j in range(8):
                y_ref[pl.ds(toks[j], 1), :] = new[j]
            return carry
        lax.fori_loop(0, n_valid // 8, scatter, 0)

        def scatter_tail(r, carry):
            add_row(r)
            return carry
        lax.fori_loop(n_valid // 8 * 8, n_valid, scatter_tail, 0)
        return carry

    lax.fori_loop(0, (count + tm - 1) // tm, tile, 0)


def _moe(h, route, wg, wu, wd, layer, *, bsz, t_tot):
    tok, w, start, count = _route_tables(route, bsz, t_tot)
    grid_spec = pltpu.PrefetchScalarGridSpec(
        num_scalar_prefetch=4,
        grid=(bsz, N_EXPERTS),
        in_specs=[
            pl.BlockSpec((None, t_tot, D_MODEL), lambda c, e, *_: (c, 0, 0)),
            pl.BlockSpec((None, None, D_MODEL, D_EXPERT), lambda c, e, *_: (layer, e, 0, 0)),
            pl.BlockSpec((None, None, D_MODEL, D_EXPERT), lambda c, e, *_: (layer, e, 0, 0)),
            pl.BlockSpec((None, None, D_EXPERT, D_MODEL), lambda c, e, *_: (layer, e, 0, 0)),
        ],
        out_specs=pl.BlockSpec((None, t_tot, D_MODEL), lambda c, e, *_: (c, 0, 0)),
        scratch_shapes=[pltpu.VMEM((MOE_TILE, D_MODEL), F32), pltpu.VMEM((MOE_TILE, D_MODEL), F32)],
    )
    y = pl.pallas_call(
        _moe_body,
        grid_spec=grid_spec,
        out_shape=jax.ShapeDtypeStruct((bsz, t_tot, D_MODEL), F32),
        compiler_params=_cparams("arbitrary", "arbitrary"),
        name="moe",
    )(tok, w, start, count, h.reshape(bsz, t_tot, D_MODEL), wg, wu, wd)
    return y.reshape(bsz * t_tot, D_MODEL)


def _residual_body(x_ref, y_ref, m_ref, o_ref):
    o_ref[...] = x_ref[...] + m_ref[5:6, :] * y_ref[...]


def _residual(x, y, mod_l, *, bps, ctx_blocks, ctx_row):
    n = x.shape[0]
    tm = TOKEN_TILE
    row = functools.partial(_mod_row, blocks_per_seq=bps, ctx_blocks=ctx_blocks, ctx_row=ctx_row)
    tok = pl.BlockSpec((tm, D_MODEL), lambda i: (i, 0))
    return pl.pallas_call(
        _residual_body,
        grid=(n // tm,),
        in_specs=[tok, tok, pl.BlockSpec((None, 6, D_MODEL), lambda i: (row(i), 0, 0))],
        out_specs=tok,
        out_shape=jax.ShapeDtypeStruct((n, D_MODEL), F32),
        compiler_params=_cparams("arbitrary"),
        name="residual",
    )(x, y, mod_l)


def kernel(x, c, ctx, c_ctx, norm1_gain, norm2_gain, w_ada, b_ada, w_in, w_out, hgrn_lb_logits, hgrn_gn_gain, rwkv_mu, rwkv_w0, rwkv_w_up, rwkv_a0, rwkv_a_up, rwkv_g_up, rwkv_kk_scale, rwkv_k_a, rwkv_r_k, rwkv_ln_gain, rwkv_ln_bias, na_q_gain, na_k_gain, na_rpb, w_router_group, b_router_group, w_router_expert, b_router_expert, w_exp_gate, w_exp_up, w_exp_down):
    bsz, t_lat, _ = x.shape
    t_ctx = ctx.shape[1]
    t_tot = t_ctx + t_lat
    n = bsz * t_tot
    assert t_ctx % TOKEN_TILE == 0 and t_lat % TOKEN_TILE == 0 and bsz < 16
    bps = t_tot // TOKEN_TILE
    tile_kw = dict(bps=bps, ctx_blocks=t_ctx // TOKEN_TILE, ctx_row=bsz)

    lb_p = jax.nn.softmax(hgrn_lb_logits.astype(F32), axis=1)
    lower_bounds = jnp.cumsum(lb_p, axis=1) - lb_p[:, :1]

    cc = jnp.zeros((16, D_MODEL), F32).at[:bsz].set(c.astype(F32)).at[bsz].set(c_ctx.astype(F32))
    mod = _ada_mod(cc, w_ada, b_ada).reshape(DEPTH, 16, 6, D_MODEL)

    w_in_bf = w_in.astype(BF16)
    w_out_bf = w_out.astype(BF16)
    wg_bf = w_exp_gate.astype(BF16).reshape(DEPTH, N_EXPERTS, D_MODEL, D_EXPERT)
    wu_bf = w_exp_up.astype(BF16).reshape(DEPTH, N_EXPERTS, D_MODEL, D_EXPERT)
    wd_bf = w_exp_down.astype(BF16).reshape(DEPTH, N_EXPERTS, D_EXPERT, D_MODEL)
    pad = LANES - N_EXPERTS - N_GROUPS
    w_router = jnp.concatenate([w_router_expert, w_router_group,
                                jnp.zeros((DEPTH, D_MODEL, pad), F32)], axis=-1)
    b_router = jnp.concatenate([b_router_expert, b_router_group, jnp.zeros((DEPTH, pad), F32)], axis=-1)

    na_bias = _na_bias_table(na_rpb, t_lat // GRID_W)
    xs = jnp.concatenate([ctx.astype(F32), x.astype(F32)], axis=1).reshape(n, D_MODEL)
    for layer in range(DEPTH):
        last = layer == DEPTH - 1
        pa, pb, pc = _inproj(xs, mod[layer], norm1_gain[layer], w_in_bf[layer], **tile_kw)
        a_mix = _hgrn(pa.reshape(bsz, t_tot, A_PROJ), lower_bounds[:, layer], hgrn_gn_gain[layer], t_ctx=t_ctx)
        prep = _rwkv_prep(pb.reshape(bsz, t_tot, B_PROJ), rwkv_mu[layer], rwkv_w0[layer], rwkv_w_up[layer],
                          rwkv_a0[layer], rwkv_a_up[layer], rwkv_g_up[layer], rwkv_kk_scale[layer],
                          rwkv_k_a[layer], rwkv_r_k[layer], t_ctx=t_ctx)
        y = _rwkv_scan(*prep[:9], t_ctx=t_ctx)
        b_mix = _rwkv_readout(y.reshape(n, B_WIDTH), prep[9].reshape(n, B_WIDTH), prep[10].reshape(n, B_WIDTH),
                              rwkv_ln_gain[layer], rwkv_ln_bias[layer])
        c_mix = _na(pc.reshape(bsz, t_tot, C_PROJ), na_q_gain[layer], na_k_gain[layer], na_bias, layer,
                    t_ctx=t_ctx, need_ctx=not last)
        xs, h2, route = _outproj(xs, a_mix.reshape(n, A_WIDTH), b_mix, c_mix.reshape(n, C_WIDTH), mod[layer],
                                norm2_gain[layer], w_out_bf[layer], w_router[layer],
                                b_router[layer].reshape(1, LANES), **tile_kw)
        y_moe = _moe(h2, route, wg_bf, wu_bf, wd_bf, layer, bsz=bsz, t_tot=t_tot)
        xs = _residual(xs, y_moe, mod[layer], **tile_kw)
    return xs.reshape(bsz, t_tot, D_MODEL)[:, t_ctx:].astype(x.dtype)
```

```python
import functools

import jax
import jax.numpy as jnp
import numpy as np
from jax import lax
from jax.experimental import pallas as pl
from jax.experimental.pallas import tpu as pltpu

F32 = jnp.float32
BF16 = jnp.bfloat16

D_MODEL = 1024
DEPTH = 4
GRID_W = 64
HEAD_DIM = 64
A_HEADS = 4
B_HEADS = 4
C_HEADS = 8
A_WIDTH = A_HEADS * HEAD_DIM
B_WIDTH = B_HEADS * HEAD_DIM
C_WIDTH = C_HEADS * HEAD_DIM
HGRN_CHUNK = 16
HGRN_F_FLOOR = 1e-20
RWKV_W_RANK = 64
RWKV_A_RANK = 64
RWKV_G_RANK = 128
RWKV_LN_EPS = 64e-5
NA_WIN_ROWS = 8
NA_WIN_COLS = 16
ROPE_THETA = 10000.0
N_GROUPS = 4
EXPERTS_PER_GROUP = 8
N_EXPERTS = N_GROUPS * EXPERTS_PER_GROUP
TOP_K = 2
D_EXPERT = 512
EPS = 1e-6
LOG2E = 1.4426950408889634
NEG_INF = -1e30
A_PROJ = 5 * A_WIDTH
B_PROJ = 3 * B_WIDTH + 2 * RWKV_W_RANK + 2 * RWKV_A_RANK + RWKV_G_RANK
C_PROJ = 3 * C_WIDTH
P_TOTAL = A_PROJ + B_PROJ + C_PROJ
B_TAIL = B_PROJ - 3 * B_WIDTH

LANES = 128
TOKEN_TILE = 256
RWKV_CHUNK = 64
RWKV_BLOCKS_PER_ITER = 2
RWKV_BLOCK = 128
HGRN_BLOCK = 128
NA_ROW_UNROLL = 8
HGRN_SCAN_UNROLL = 8
MOE_TILE = 192
VMEM_LIMIT = 56 * 1024 * 1024

_HI = lax.Precision.HIGHEST


def _cparams(*sem):
    return pltpu.CompilerParams(dimension_semantics=sem, vmem_limit_bytes=VMEM_LIMIT)


def _mm(a, b):
    return jnp.dot(a, b, preferred_element_type=F32)


def _mm_hi(a, b):
    return jnp.dot(a, b, preferred_element_type=F32, precision=_HI)


def _nt(a, b):
    return lax.dot_general(a, b, (((1,), (1,)), ((), ())), preferred_element_type=F32)


def _tn(a, b):
    return lax.dot_general(a, b, (((0,), (0,)), ((), ())), preferred_element_type=F32)


def _split3(a):
    a1 = a.astype(BF16)
    r1 = a - a1.astype(F32)
    a2 = r1.astype(BF16)
    a3 = (r1 - a2.astype(F32)).astype(BF16)
    return a1, a2, a3


def _mm_x3(a, m):
    a1, a2, a3 = _split3(a)
    return _mm(a1, m) + _mm(a2, m) + _mm(a3, m)


def _mm_l3(m, a):
    a1, a2, a3 = _split3(a)
    return _mm(m, a1) + _mm(m, a2) + _mm(m, a3)


def _sigmoid(x):
    return 1.0 / (1.0 + jnp.exp(-x))


def _iota(shape, dim):
    return lax.broadcasted_iota(jnp.int32, shape, dim)


def _head_seg(n):
    return jnp.where(_iota((n, n), 0) // HEAD_DIM == _iota((n, n), 1) // HEAD_DIM, 1.0, 0.0).astype(BF16)


def _ada_body(c_ref, w_ref, b_ref, o_ref):
    c = c_ref[...]
    o_ref[...] = _mm_hi(c * _sigmoid(c), w_ref[...]) + b_ref[...]


def _ada_mod(cc, w_ada, b_ada):
    rows = cc.shape[0]
    tn = 1536
    return pl.pallas_call(
        _ada_body,
        grid=(DEPTH, 6 * D_MODEL // tn),
        in_specs=[
            pl.BlockSpec((rows, D_MODEL), lambda l, j: (0, 0)),
            pl.BlockSpec((None, D_MODEL, tn), lambda l, j: (l, 0, j)),
            pl.BlockSpec((None, 1, tn), lambda l, j: (l, 0, j)),
        ],
        out_specs=pl.BlockSpec((None, rows, tn), lambda l, j: (l, 0, j)),
        out_shape=jax.ShapeDtypeStruct((DEPTH, rows, 6 * D_MODEL), F32),
        compiler_params=_cparams("arbitrary", "arbitrary"),
        name="ada_mod",
    )(cc, w_ada, b_ada.reshape(DEPTH, 1, 6 * D_MODEL))


def _mod_row(i, blocks_per_seq, ctx_blocks, ctx_row):
    return jnp.where(i % blocks_per_seq < ctx_blocks, ctx_row, i // blocks_per_seq)


def _inproj_body(x_ref, m_ref, g_ref, w_ref, pa_ref, pb_ref, pc_ref):
    x = x_ref[...]
    h = x * lax.rsqrt(jnp.mean(x * x, axis=-1, keepdims=True) + EPS) * g_ref[...]
    h = (h * (1.0 + m_ref[1:2, :]) + m_ref[0:1, :]).astype(BF16)
    pa_ref[...] = _mm(h, w_ref[:, :A_PROJ])
    pb_ref[...] = _mm(h, w_ref[:, A_PROJ:A_PROJ + B_PROJ])
    pc_ref[...] = _mm(h, w_ref[:, A_PROJ + B_PROJ:])


def _inproj(x, mod_l, gain, w_bf, *, bps, ctx_blocks, ctx_row):
    n = x.shape[0]
    tm = TOKEN_TILE
    row = functools.partial(_mod_row, blocks_per_seq=bps, ctx_blocks=ctx_blocks, ctx_row=ctx_row)
    return pl.pallas_call(
        _inproj_body,
        grid=(n // tm,),
        in_specs=[
            pl.BlockSpec((tm, D_MODEL), lambda i: (i, 0)),
            pl.BlockSpec((None, 6, D_MODEL), lambda i: (row(i), 0, 0)),
            pl.BlockSpec((1, D_MODEL), lambda i: (0, 0)),
            pl.BlockSpec((D_MODEL, P_TOTAL), lambda i: (0, 0)),
        ],
        out_specs=[
            pl.BlockSpec((tm, A_PROJ), lambda i: (i, 0)),
            pl.BlockSpec((tm, B_PROJ), lambda i: (i, 0)),
            pl.BlockSpec((tm, C_PROJ), lambda i: (i, 0)),
        ],
        out_shape=[
            jax.ShapeDtypeStruct((n, A_PROJ), F32),
            jax.ShapeDtypeStruct((n, B_PROJ), F32),
            jax.ShapeDtypeStruct((n, C_PROJ), F32),
        ],
        compiler_params=_cparams("arbitrary"),
        name="inproj",
    )(x, mod_l, gain.reshape(1, D_MODEL), w_bf)


def _hgrn_body(q_ref, ff_ref, fb_ref, i_ref, g_ref, lb_ref, gn_ref, o_ref,
               qin_s, kout_s, cum_s, acc_s, kpad_s, cpad_s, vpad_s, *, t_ctx):
    t_tot = q_ref.shape[0]
    c = HGRN_CHUNK
    rb = HGRN_BLOCK
    n_blk = t_tot // rb
    n_chunk = t_tot // c
    n_chunk_ctx = t_ctx // c
    row = _iota((rb, rb), 0)
    col = _iota((rb, rb), 1)
    same = (row // c) == (col // c)
    same_bf = jnp.where(same, 1.0, 0.0).astype(BF16)
    seg = _head_seg(LANES)
    pos = _iota((rb, LANES), 0) % c
    blockdiag = _iota((LANES, LANES), 0) // HEAD_DIM == _iota((LANES, LANES), 1) // HEAD_DIM

    pad = HGRN_CHUNK
    zpad = jnp.zeros((pad, LANES), F32)
    for d in range(2):
        for ref in (kpad_s, cpad_s):
            ref[d, 0:pad, :] = zpad
            ref[d, pad + t_tot:, :] = zpad
    vpad_s[0:pad, :] = zpad
    vpad_s[pad + t_tot:, :] = zpad
    tris = [jnp.where(same & ((col >= row) if d == 1 else (col <= row)), 1.0, 0.0).astype(BF16) for d in range(2)]

    def gates(b, carry):
        sl = pl.ds(pl.multiple_of(b * rb, rb), rb)
        slp = pl.ds(pl.multiple_of(b * rb, rb) + pad, rb)
        q = q_ref[sl, :]
        vpad_s[slp, :] = i_ref[sl, :]
        for d, f_ref in enumerate((ff_ref, fb_ref)):
            lb = lb_ref[d:d + 1, :]
            fpre = f_ref[sl, :]
            f = lb + (1.0 - lb) * _sigmoid(fpre)
            logf = jnp.log(jnp.maximum(f, HGRN_F_FLOOR))
            k = (1.0 - lb) * _sigmoid(-fpre)
            cum = _mm_l3(tris[d], logf)
            tot = _mm_l3(same_bf, logf)
            kpad_s[d, slp, :] = k
            cpad_s[d, slp, :] = cum * LOG2E
            qin_s[d, sl, :] = q * jnp.exp(cum)
            kout_s[d, sl, :] = k * jnp.exp(tot - cum)
            cum_s[d, sl, :] = tot
        return carry

    lax.fori_loop(0, n_blk, gates, 0)

    def intra(b, carry):
        r0 = pl.multiple_of(b * rb, rb)
        sl = pl.ds(r0, rb)
        q = q_ref[sl, :]
        cums = [cpad_s[d, pl.ds(r0 + pad, rb), :] for d in range(2)]
        o = jnp.zeros((rb, LANES), F32)
        for j in range(c):
            for d in range(2):
                off = r0 + pad + (j if d == 1 else -j)
                valid = (pos <= c - 1 - j) if d == 1 else (pos >= j)
                ks = kpad_s[d, pl.ds(off, rb), :]
                cs = cpad_s[d, pl.ds(off, rb), :]
                vs = vpad_s[pl.ds(off, rb), :]
                prod = jnp.where(valid, q * ks * jnp.exp2(cums[d] - cs), 0.0)
                o = o + _mm(prod.astype(BF16), seg) * vs
        acc_s[sl, :] = o
        return carry

    lax.fori_loop(0, n_blk, intra, 0)

    def step(it, carry):
        idx = [it * HGRN_SCAN_UNROLL + j for j in range(HGRN_SCAN_UNROLL)]
        order = [(i, jnp.where(i < n_chunk_ctx, n_chunk_ctx - 1 - i, n_chunk - 1 - (i - n_chunk_ctx))) for i in idx]
        sls = [[pl.ds(pl.multiple_of(n[d] * c, c), c) for d in range(2)] for n in order]
        kvs = [[jnp.where(blockdiag, _tn(i_ref[sl[d], :].astype(BF16), kout_s[d, sl[d], :].astype(BF16)), 0.0)
                for d in range(2)] for sl in sls]
        decs = [[jnp.exp(cum_s[d, pl.ds(pl.multiple_of(n[d] * c, c), 1), :]) for d in range(2)] for n in order]
        states = list(carry)
        for j in range(HGRN_SCAN_UNROLL):
            for d in range(2):
                acc_s[sls[j][d], :] += _nt(qin_s[d, sls[j][d], :].astype(BF16), states[d].astype(BF16))
                states[d] = states[d] * decs[j][d] + kvs[j][d]
        return tuple(states)

    zero = jnp.zeros((LANES, LANES), F32)
    lax.fori_loop(0, n_chunk // HGRN_SCAN_UNROLL, step, (zero, zero))

    def readout(b, carry):
        sl = pl.ds(pl.multiple_of(b * rb, rb), rb)
        o = acc_s[sl, :]
        ms = _mm_x3(o * o, seg) * (1.0 / HEAD_DIM)
        g = g_ref[sl, :]
        o_ref[sl, :] = o * lax.rsqrt(ms + EPS) * gn_ref[...] * (g * _sigmoid(g))
        return carry

    lax.fori_loop(0, n_blk, readout, 0)


def _hgrn(pa, lb, gn_gain, *, t_ctx):
    bsz, t_tot, _ = pa.shape
    n_hp = A_WIDTH // LANES

    def sec(s):
        return pl.BlockSpec((None, t_tot, LANES), lambda b, h, s=s: (b, 0, s * n_hp + h))

    return pl.pallas_call(
        functools.partial(_hgrn_body, t_ctx=t_ctx),
        grid=(bsz, n_hp),
        in_specs=[sec(0), sec(1), sec(2), sec(3), sec(4),
                  pl.BlockSpec((2, LANES), lambda b, h: (0, h)),
                  pl.BlockSpec((1, LANES), lambda b, h: (0, h))],
        out_specs=pl.BlockSpec((None, t_tot, LANES), lambda b, h: (b, 0, h)),
        out_shape=jax.ShapeDtypeStruct((bsz, t_tot, A_WIDTH), F32),
        scratch_shapes=[
            pltpu.VMEM((2, t_tot, LANES), F32),
            pltpu.VMEM((2, t_tot, LANES), F32),
            pltpu.VMEM((2, t_tot, LANES), F32),
            pltpu.VMEM((t_tot, LANES), F32),
            pltpu.VMEM((2, t_tot + 2 * HGRN_CHUNK, LANES), F32),
            pltpu.VMEM((2, t_tot + 2 * HGRN_CHUNK, LANES), F32),
            pltpu.VMEM((t_tot + 2 * HGRN_CHUNK, LANES), F32),
        ],
        compiler_params=_cparams("arbitrary", "arbitrary"),
        name="hgrn",
    )(pa, pa, pa, pa, pa, lb, gn_gain.reshape(1, A_WIDTH))


def _rwkv_prep_body(x_ref, xp_ref, xn_ref, mu_ref, w0_ref, wup_ref, a0_ref, aup_ref, gup_ref,
                    kks_ref, ka_ref, rk_ref,
                    r_o, v_o, kk_o, kf_o, kb_o, bf_o, bb_o, lwf_o, lwb_o, g_o, bonus_o, *, t_ctx, t_tot):
    tm = x_ref.shape[0]
    r0 = pl.program_id(1) * tm
    prev_ok = jnp.logical_and(r0 != 0, r0 != t_ctx)
    next_ok = jnp.logical_and(r0 + tm != t_ctx, r0 + tm != t_tot)
    first = _iota((tm, 1), 0) == 0
    last = _iota((tm, 1), 0) == tm - 1

    def shifted(lo, hi):
        x = x_ref[:, lo:hi]
        p_row = jnp.where(prev_ok, xp_ref[7:8, lo:hi], 0.0)
        n_row = jnp.where(next_ok, xn_ref[0:1, lo:hi], 0.0)
        prev = jnp.where(first, p_row, pltpu.roll(x, 1, 0))
        nxt = jnp.where(last, n_row, pltpu.roll(x, tm - 1, 0))
        return x + (0.5 * (prev + nxt) - x) * mu_ref[:, lo:hi]

    bw = B_WIDTH
    r = shifted(0, bw)
    k = shifted(bw, 2 * bw)
    v = shifted(2 * bw, 3 * bw)
    tail = shifted(3 * bw, B_PROJ)
    seg = _head_seg(bw)

    kk = k * kks_ref[...]
    kk = kk / jnp.maximum(jnp.sqrt(_mm_x3(kk * kk, seg)), 1e-12)
    ksum = jnp.zeros_like(k)
    for d, (k_o, b_o, lw_o) in enumerate(((kf_o, bf_o, lwf_o), (kb_o, bb_o, lwb_o))):
        wd = tail[:, d * RWKV_W_RANK:(d + 1) * RWKV_W_RANK]
        ad = tail[:, 2 * RWKV_W_RANK + d * RWKV_A_RANK:2 * RWKV_W_RANK + (d + 1) * RWKV_A_RANK]
        u = -(w0_ref[d:d + 1, :] + _mm_hi(jnp.tanh(wd), wup_ref[d]))
        w = -(jnp.maximum(u, 0.0) + jnp.log(1.0 + jnp.exp(-jnp.abs(u)))) - 0.5
        a = _sigmoid(a0_ref[d:d + 1, :] + _mm_hi(ad, aup_ref[d]))
        k_d = k * (1.0 + (a - 1.0) * ka_ref[...])
        ksum = ksum + k_d
        k_o[...] = k_d
        b_o[...] = a * kk
        lw_o[...] = -jnp.exp(w)
    gd = tail[:, 2 * RWKV_W_RANK + 2 * RWKV_A_RANK:]
    r_o[...] = r
    v_o[...] = v
    kk_o[...] = kk
    g_o[...] = _mm_hi(_sigmoid(gd), gup_ref[...])
    bonus_o[...] = _mm_x3(r * ksum * rk_ref[...], seg) * v


def _rwkv_prep(pb, mu, w0, w_up, a0, a_up, g_up, kk_scale, k_a, r_k, *, t_ctx):
    bsz, t_tot, _ = pb.shape
    tm = TOKEN_TILE
    nb8 = t_tot // 8
    per8 = tm // 8
    bw = B_WIDTH
    full = lambda shape: pl.BlockSpec(shape, lambda b, i: (0,) * len(shape))
    out = jax.ShapeDtypeStruct((bsz, t_tot, bw), F32)
    return pl.pallas_call(
        functools.partial(_rwkv_prep_body, t_ctx=t_ctx, t_tot=t_tot),
        grid=(bsz, t_tot // tm),
        in_specs=[
            pl.BlockSpec((None, tm, B_PROJ), lambda b, i: (b, i, 0)),
            pl.BlockSpec((None, 8, B_PROJ), lambda b, i: (b, jnp.maximum(i * per8 - 1, 0), 0)),
            pl.BlockSpec((None, 8, B_PROJ), lambda b, i: (b, jnp.minimum((i + 1) * per8, nb8 - 1), 0)),
            full((1, B_PROJ)), full((2, bw)), full((2, RWKV_W_RANK, bw)), full((2, bw)),
            full((2, RWKV_A_RANK, bw)), full((RWKV_G_RANK, bw)), full((1, bw)), full((1, bw)), full((1, bw)),
        ],
        out_specs=[pl.BlockSpec((None, tm, bw), lambda b, i: (b, i, 0))] * 11,
        out_shape=[out] * 11,
        compiler_params=_cparams("arbitrary", "arbitrary"),
        name="rwkv_prep",
    )(pb, pb, pb, mu.reshape(1, B_PROJ), w0, w_up, a0, a_up, g_up,
      kk_scale.reshape(1, bw), k_a.reshape(1, bw), r_k.reshape(1, bw))


def _rwkv_scan_body(r_ref, v_ref, kk_ref, kf_ref, kb_ref, bf_ref, bb_ref, lwf_ref, lwb_ref, y_ref,
                    rq_s, y0_s, p_s, z_s, *, t_ctx):
    t_tot = r_ref.shape[0]
    c = RWKV_CHUNK
    rb = RWKV_BLOCK
    n_blk = t_tot // rb
    n_chunk = t_tot // c
    n_chunk_ctx = t_ctx // c
    cpb = rb // c
    n_double = c.bit_length() - 2
    row = _iota((rb, rb), 0)
    col = _iota((rb, rb), 1)
    same = (row // c) == (col // c)
    same_bf = jnp.where(same, 1.0, 0.0).astype(BF16)
    eye = jnp.where(row == col, 1.0, 0.0)
    lane = _iota((1, LANES), 1)
    head_masks = [jnp.where(lane // HEAD_DIM == h, 1.0, 0.0) for h in range(LANES // HEAD_DIM)]
    r128 = _iota((LANES, LANES), 0)
    c128 = _iota((LANES, LANES), 1)
    blockdiag = (r128 // HEAD_DIM) == (c128 // HEAD_DIM)
    diag128 = r128 == c128

    def block(it, carry):
        bis = [it * RWKV_BLOCKS_PER_ITER + j for j in range(RWKV_BLOCKS_PER_ITER)]
        sls = [pl.ds(pl.multiple_of(bi * rb, rb), rb) for bi in bis]
        v_bfs = [v_ref[sl, :].astype(BF16) for sl in sls]
        pre = {}
        for j, sl in enumerate(sls):
            r = r_ref[sl, :]
            kk = kk_ref[sl, :]
            for d in range(2):
                rev = d == 1
                k_ref, b_ref, lw_ref = (kb_ref, bb_ref, lwb_ref) if rev else (kf_ref, bf_ref, lwf_ref)
                incl = same & ((col >= row) if rev else (col <= row))
                strict = same & ((col > row) if rev else (col < row))
                incl_bf = jnp.where(incl, 1.0, 0.0).astype(BF16)
                lw = lw_ref[sl, :]
                k = k_ref[sl, :]
                b = b_ref[sl, :]
                cl = _mm_l3(incl_bf, lw)
                tot = _mm_l3(same_bf, lw)
                w_inv = jnp.exp(-cl)
                w_end = jnp.exp(tot - cl)
                pre[j, d] = dict(incl=incl, strict=strict, tot=tot, kk_d=kk * jnp.exp(cl - lw), r_d=r * jnp.exp(cl),
                                 kb=jnp.concatenate([k * w_inv, b * w_inv], axis=0).astype(BF16),
                                 b_e=(b * w_end).astype(BF16), k_e=(k * w_end).astype(BF16))
        chains = [dict(j=j, d=d, mh=mh, kkm=pre[j, d]['kk_d'] * mh, rm=pre[j, d]['r_d'] * mh)
                  for j in range(len(bis)) for d in range(2) for mh in head_masks]
        for ch in chains:
            p = pre[ch['j'], ch['d']]
            aa = _nt(jnp.concatenate([ch['kkm'], ch['rm']], axis=0).astype(BF16), p['kb'])
            ch['a_kb'] = jnp.where(p['strict'], aa[:rb, :rb], 0.0)
            ch['a_rk'] = jnp.where(p['incl'], aa[rb:, :rb], 0.0)
            ch['a_rb'] = jnp.where(p['incl'], aa[rb:, rb:], 0.0)
            m = -jnp.where(p['strict'], aa[:rb, rb:], 0.0)
            ch['t'] = eye + m
            ch['m'] = m
        for ch in chains:
            m_bf = ch['m'].astype(BF16)
            ch['m'] = _mm(m_bf, m_bf)
        for _ in range(n_double - 1):
            for ch in chains:
                both = _mm(jnp.concatenate([ch['t'], ch['m']], axis=0).astype(BF16), ch['m'].astype(BF16))
                ch['t'] = ch['t'] + both[:rb]
                ch['m'] = both[rb:]
        for ch in chains:
            ch['t'] = ch['t'] + _mm(ch['t'].astype(BF16), ch['m'].astype(BF16))
            ch['av'] = _mm(jnp.concatenate([ch['a_kb'], ch['a_rk']], axis=0).astype(BF16), v_bfs[ch['j']])
        for ch in chains:
            ch['ku'] = _mm(ch['t'].astype(BF16), jnp.concatenate([ch['kkm'], ch['av'][:rb]], axis=1).astype(BF16))
        for ch in chains:
            ch['rb_ku'] = _mm(ch['a_rb'].astype(BF16), ch['ku'].astype(BF16))
        for j, (bi, sl) in enumerate(zip(bis, sls)):
            for d in range(2):
                p = pre[j, d]
                mine = [ch for ch in chains if ch['j'] == j and ch['d'] == d]
                kkt = sum(ch['ku'][:, :LANES] for ch in mine)
                u = sum(ch['ku'][:, LANES:] * ch['mh'] for ch in mine)
                rq_s[d, sl, :] = sum(ch['rm'] - ch['rb_ku'][:, :LANES] for ch in mine)
                y0_s[d, sl, :] = sum((ch['av'][rb:] - ch['rb_ku'][:, LANES:]) * ch['mh'] for ch in mine)
                ktu = jnp.concatenate([kkt, u], axis=1).astype(BF16)
                for n in range(cpb):
                    rows = slice(n * c, (n + 1) * c)
                    wc = jnp.exp(p['tot'][n * c:n * c + 1, :])
                    kub = _tn(ktu[rows], p['b_e'][rows])
                    p_s[d, bi * cpb + n] = jnp.where(diag128, wc, 0.0) - jnp.where(blockdiag, kub[:LANES], 0.0)
                    p_z = jnp.where(blockdiag, _tn(v_bfs[j][rows], p['k_e'][rows]) - kub[LANES:], 0.0)
                    z_s[d, bi * cpb + n] = p_z
        return carry

    lax.fori_loop(0, n_blk // RWKV_BLOCKS_PER_ITER, block, 0)

    y_ref[...] = jnp.zeros_like(y_ref)

    def step(i, carry):
        n_b = jnp.where(i < n_chunk_ctx, n_chunk_ctx - 1 - i, n_chunk - 1 - (i - n_chunk_ctx))
        chunk = (i, n_b)
        s_bf = [s.astype(BF16) for s in carry]
        ys = [_nt(rq_s[d, pl.ds(pl.multiple_of(chunk[d] * c, c), c), :].astype(BF16), s_bf[d]) for d in range(2)]
        states = [_mm(s_bf[d], p_s[d, chunk[d]].astype(BF16)) + z_s[d, chunk[d]] for d in range(2)]
        for d in range(2):
            sl = pl.ds(pl.multiple_of(chunk[d] * c, c), c)
            y_ref[sl, :] += ys[d] + y0_s[d, sl, :]
        return tuple(states)

    zero = jnp.zeros((LANES, LANES), F32)
    lax.fori_loop(0, n_chunk, step, (zero, zero))


def _rwkv_scan(r, v, kk, kf, kb, bf, bb, lwf, lwb, *, t_ctx):
    bsz, t_tot, bw = r.shape
    n_chunk = t_tot // RWKV_CHUNK
    spec = pl.BlockSpec((None, t_tot, LANES), lambda b, h: (b, 0, h))
    return pl.pallas_call(
        functools.partial(_rwkv_scan_body, t_ctx=t_ctx),
        grid=(bsz, bw // LANES),
        in_specs=[spec] * 9,
        out_specs=spec,
        out_shape=jax.ShapeDtypeStruct((bsz, t_tot, bw), F32),
        scratch_shapes=[
            pltpu.VMEM((2, t_tot, LANES), F32),
            pltpu.VMEM((2, t_tot, LANES), F32),
            pltpu.VMEM((2, n_chunk, LANES, LANES), F32),
            pltpu.VMEM((2, n_chunk, LANES, LANES), F32),
        ],
        compiler_params=_cparams("arbitrary", "arbitrary"),
        name="rwkv_scan",
    )(r, v, kk, kf, kb, bf, bb, lwf, lwb)


def _rwkv_readout_body(y_ref, g_ref, bonus_ref, lng_ref, lnb_ref, o_ref):
    seg = _head_seg(B_WIDTH)
    y = y_ref[...]
    mean = _mm_x3(y, seg) * (1.0 / HEAD_DIM)
    yc = y - mean
    var = _mm_x3(yc * yc, seg) * (1.0 / HEAD_DIM)
    yn = yc * lax.rsqrt(var + RWKV_LN_EPS) * lng_ref[...] + lnb_ref[...]
    o_ref[...] = (yn + bonus_ref[...]) * g_ref[...]


def _rwkv_readout(y, g, bonus, ln_gain, ln_bias):
    n, bw = y.shape
    tm = 512
    spec = pl.BlockSpec((tm, bw), lambda i: (i, 0))
    vec = pl.BlockSpec((1, bw), lambda i: (0, 0))
    return pl.pallas_call(
        _rwkv_readout_body,
        grid=(n // tm,),
        in_specs=[spec, spec, spec, vec, vec],
        out_specs=spec,
        out_shape=jax.ShapeDtypeStruct((n, bw), F32),
        compiler_params=_cparams("arbitrary"),
        name="rwkv_readout",
    )(y, g, bonus, ln_gain.reshape(1, bw), ln_bias.reshape(1, bw))


def _na_body(q_ref, k_ref, v_ref, qg_ref, kg_ref, cos_ref, sin_ref, bias_ref, o_ref,
             qs, ks, vs, *, t_ctx, need_ctx):
    t_tot = q_ref.shape[0]
    t_lat = t_tot - t_ctx
    rows = t_lat // GRID_W
    win_r = min(NA_WIN_ROWS, rows)
    scale = HEAD_DIM ** -0.5
    seg = _head_seg(LANES)
    lane = _iota((1, LANES), 1)
    head_masks = [jnp.where(lane // HEAD_DIM == h, 1.0, 0.0) for h in range(LANES // HEAD_DIM)]
    half = (lane % HEAD_DIM) < HEAD_DIM // 2
    blk = 256

    def norm_block(i, carry):
        sl = pl.ds(pl.multiple_of(i * blk, blk), blk)
        q = q_ref[sl, :]
        k = k_ref[sl, :]
        q = q * lax.rsqrt(_mm((q * q).astype(BF16), seg) * (1.0 / HEAD_DIM) + EPS) * qg_ref[...]
        k = k * lax.rsqrt(_mm((k * k).astype(BF16), seg) * (1.0 / HEAD_DIM) + EPS) * kg_ref[...]
        cos = cos_ref[sl, :]
        sin = sin_ref[sl, :]

        def rope(t):
            swapped = jnp.where(half, pltpu.roll(t, LANES - HEAD_DIM // 2, 1), pltpu.roll(t, HEAD_DIM // 2, 1))
            return t * cos + swapped * sin

        qs[sl, :] = (rope(q) * scale).astype(BF16)
        ks[sl, :] = rope(k).astype(BF16)
        vs[sl, :] = v_ref[sl, :].astype(BF16)
        return carry

    lax.fori_loop(0, t_tot // blk, norm_block, 0)

    masks_bf = [mh.astype(BF16) for mh in head_masks]

    def by_head(q):
        return jnp.concatenate([q * mb for mb in masks_bf], axis=0)

    def merge_heads(o, n):
        return sum(o[h * n:(h + 1) * n] * mh for h, mh in enumerate(head_masks))

    if need_ctx:
        s = _nt(by_head(qs[0:t_ctx, :]), ks[0:t_ctx, :])
        p = jnp.exp(s - jnp.max(s, axis=-1, keepdims=True))
        o = _mm(p.astype(BF16), vs[0:t_ctx, :]) / jnp.sum(p, axis=-1, keepdims=True)
        o_ref[0:t_ctx, :] = merge_heads(o, t_ctx)
    else:
        o_ref[0:t_ctx, :] = jnp.zeros((t_ctx, LANES), F32)

    def q_rows(it, carry):
        rs = [it * NA_ROW_UNROLL + j for j in range(NA_ROW_UNROLL)]
        starts = [jnp.clip(r - win_r // 2, 0, rows - win_r) for r in rs]
        q_sl = [pl.ds(pl.multiple_of(t_ctx + r * GRID_W, GRID_W), GRID_W) for r in rs]
        k_sl = [pl.ds(pl.multiple_of(t_ctx + r0 * GRID_W, GRID_W), win_r * GRID_W) for r0 in starts]
        qs_ = [by_head(qs[sl, :]) for sl in q_sl]
        s_win = [_nt(q, ks[sl, :]) + bias_ref[r - r0] for q, sl, r, r0 in zip(qs_, k_sl, rs, starts)]
        s_ctx = [_nt(q, ks[0:t_ctx, :]) for q in qs_]
        ms = [jnp.maximum(jnp.max(a, axis=-1, keepdims=True), jnp.max(b, axis=-1, keepdims=True))
              for a, b in zip(s_win, s_ctx)]
        p_win = [jnp.exp(a - m) for a, m in zip(s_win, ms)]
        p_ctx = [jnp.exp(b - m) for b, m in zip(s_ctx, ms)]
        den = [jnp.sum(a, axis=-1, keepdims=True) + jnp.sum(b, axis=-1, keepdims=True) for a, b in zip(p_win, p_ctx)]
        o_win = [_mm(a.astype(BF16), vs[sl, :]) for a, sl in zip(p_win, k_sl)]
        o_ctx = [_mm(b.astype(BF16), vs[0:t_ctx, :]) for b in p_ctx]
        for sl, a, b, d in zip(q_sl, o_win, o_ctx, den):
            o_ref[sl, :] = merge_heads((a + b) / d, GRID_W)
        return carry

    lax.fori_loop(0, rows // NA_ROW_UNROLL, q_rows, 0)


def _na_tables(t_ctx, t_lat):
    quarter = HEAD_DIM // 4
    pos = np.arange(t_lat)
    inv = ROPE_THETA ** (-np.arange(quarter, dtype=np.float32) / quarter)
    pos_r = (pos // GRID_W).astype(np.float32)
    pos_c = (pos % GRID_W).astype(np.float32)
    return pos_r, pos_c, inv


def _na_bias_table(rpb, rows):
    win_r = min(NA_WIN_ROWS, rows)
    c = np.arange(GRID_W)
    w_start = np.clip(c - NA_WIN_COLS // 2, 0, GRID_W - NA_WIN_COLS)
    kc = np.arange(GRID_W)
    in_win = (kc[None, :] >= w_start[:, None]) & (kc[None, :] < w_start[:, None] + NA_WIN_COLS)
    col_idx = np.clip(kc[None, :] - c[:, None] + NA_WIN_COLS - 1, 0, 2 * NA_WIN_COLS - 2)
    n_col = 2 * NA_WIN_COLS - 1
    onehot = jnp.asarray(col_idx[None] == np.arange(n_col)[:, None, None], F32)
    band = jnp.einsum('...hrj,jck->...hrck', rpb.astype(F32), onehot, precision=_HI)
    band = jnp.where(in_win, band, NEG_INF)
    per_off = [band[..., NA_WIN_ROWS - 1 - off:NA_WIN_ROWS - 1 - off + win_r, :, :] for off in range(win_r)]
    g = jnp.stack(per_off, axis=-4)
    g = jnp.swapaxes(g, -3, -2)
    return g.reshape(g.shape[:-2] + (win_r * GRID_W,))


def _na(pc, q_gain, k_gain, bias, layer, *, t_ctx, need_ctx):
    bsz, t_tot, _ = pc.shape
    t_lat = t_tot - t_ctx
    rows = t_lat // GRID_W
    win_r = min(NA_WIN_ROWS, rows)
    n_hp = C_WIDTH // LANES
    hpl = LANES // HEAD_DIM
    pos_r, pos_c, inv = _na_tables(t_ctx, t_lat)
    ang = np.concatenate([pos_r[:, None] * inv, pos_c[:, None] * inv], axis=-1)
    ang = np.concatenate([np.zeros((t_ctx, HEAD_DIM // 2), np.float32), ang], axis=0)
    cos = np.cos(ang)
    sin = np.sin(ang)
    cos_t = jnp.asarray(np.tile(np.concatenate([cos, cos], axis=-1), (1, hpl)), F32)
    sin_t = jnp.asarray(np.tile(np.concatenate([-sin, sin], axis=-1), (1, hpl)), F32)
    bias = bias.reshape(bias.shape[0], n_hp, hpl, win_r, GRID_W, win_r * GRID_W)
    bias = jnp.swapaxes(bias, 2, 3).reshape(bias.shape[0], n_hp, win_r, hpl * GRID_W, win_r * GRID_W)

    def sec(s):
        return pl.BlockSpec((None, t_tot, LANES), lambda b, h, s=s: (b, 0, s * n_hp + h))

    gain = lambda g: jnp.tile(g.reshape(1, HEAD_DIM), (1, hpl))
    return pl.pallas_call(
        functools.partial(_na_body, t_ctx=t_ctx, need_ctx=need_ctx),
        grid=(bsz, n_hp),
        in_specs=[sec(0), sec(1), sec(2),
                  pl.BlockSpec((1, LANES), lambda b, h: (0, 0)),
                  pl.BlockSpec((1, LANES), lambda b, h: (0, 0)),
                  pl.BlockSpec((t_tot, LANES), lambda b, h: (0, 0)),
                  pl.BlockSpec((t_tot, LANES), lambda b, h: (0, 0)),
                  pl.BlockSpec((None, None, win_r, hpl * GRID_W, win_r * GRID_W),
                               lambda b, h: (layer, h, 0, 0, 0))],
        out_specs=pl.BlockSpec((None, t_tot, LANES), lambda b, h: (b, 0, h)),
        out_shape=jax.ShapeDtypeStruct((bsz, t_tot, C_WIDTH), F32),
        scratch_shapes=[pltpu.VMEM((t_tot, LANES), BF16)] * 3,
        compiler_params=_cparams("arbitrary", "arbitrary"),
        name="na",
    )(pc, pc, pc, gain(q_gain), gain(k_gain), cos_t, sin_t, bias)


def _outproj_body(x_ref, a_ref, b_ref, c_ref, m_ref, g_ref, w_ref, wr_ref, br_ref, xo_ref, h_ref, route_ref):
    mix = (_mm(a_ref[...].astype(BF16), w_ref[0:A_WIDTH, :])
           + _mm(b_ref[...].astype(BF16), w_ref[A_WIDTH:A_WIDTH + B_WIDTH, :])
           + _mm(c_ref[...].astype(BF16), w_ref[A_WIDTH + B_WIDTH:, :]))
    x = x_ref[...] + m_ref[2:3, :] * mix
    xo_ref[...] = x
    h = x * lax.rsqrt(jnp.mean(x * x, axis=-1, keepdims=True) + EPS) * g_ref[...]
    h = h * (1.0 + m_ref[4:5, :]) + m_ref[3:4, :]
    h_ref[...] = h
    logits = _mm_hi(h, wr_ref[...]) + br_ref[...]
    lane = _iota(logits.shape, 1).astype(F32)
    big = float(LANES)
    is_g = (lane >= N_EXPERTS) & (lane < N_EXPERTS + N_GROUPS)
    gl = jnp.where(is_g, logits, -jnp.inf)
    gmax = jnp.max(gl, axis=-1, keepdims=True)
    g_w = 1.0 / jnp.sum(jnp.exp(gl - gmax), axis=-1, keepdims=True)
    g_sel = jnp.min(jnp.where(gl == gmax, lane, big), axis=-1, keepdims=True) - N_EXPERTS
    lo = g_sel * EXPERTS_PER_GROUP
    el = jnp.where((lane >= lo) & (lane < lo + EXPERTS_PER_GROUP), logits, -jnp.inf)
    m1 = jnp.max(el, axis=-1, keepdims=True)
    i1 = jnp.min(jnp.where(el == m1, lane, big), axis=-1, keepdims=True)
    el2 = jnp.where(lane == i1, -jnp.inf, el)
    m2 = jnp.max(el2, axis=-1, keepdims=True)
    i2 = jnp.min(jnp.where(el2 == m2, lane, big), axis=-1, keepdims=True)
    e2 = jnp.exp(m2 - m1)
    w1 = g_w / (1.0 + e2)
    route_ref[...] = (jnp.where(lane == 0.0, i1, 0.0) + jnp.where(lane == 1.0, i2, 0.0)
                      + jnp.where(lane == 2.0, w1, 0.0) + jnp.where(lane == 3.0, w1 * e2, 0.0))


def _outproj(x, a, b, c, mod_l, gain2, w_bf, w_router, b_router, *, bps, ctx_blocks, ctx_row):
    n = x.shape[0]
    tm = TOKEN_TILE
    row = functools.partial(_mod_row, blocks_per_seq=bps, ctx_blocks=ctx_blocks, ctx_row=ctx_row)
    tok = lambda w: pl.BlockSpec((tm, w), lambda i: (i, 0))
    full = lambda shape: pl.BlockSpec(shape, lambda i: (0,) * len(shape))
    return pl.pallas_call(
        _outproj_body,
        grid=(n // tm,),
        in_specs=[tok(D_MODEL), tok(A_WIDTH), tok(B_WIDTH), tok(C_WIDTH),
                  pl.BlockSpec((None, 6, D_MODEL), lambda i: (row(i), 0, 0)),
                  full((1, D_MODEL)), full((D_MODEL, D_MODEL)), full((D_MODEL, LANES)), full((1, LANES))],
        out_specs=[tok(D_MODEL), tok(D_MODEL), tok(LANES)],
        out_shape=[jax.ShapeDtypeStruct((n, D_MODEL), F32),
                   jax.ShapeDtypeStruct((n, D_MODEL), F32),
                   jax.ShapeDtypeStruct((n, LANES), F32)],
        compiler_params=_cparams("arbitrary"),
        name="outproj",
    )(x, a, b, c, mod_l, gain2.reshape(1, D_MODEL), w_bf, w_router, b_router)


def _route_tables(route, bsz, t_tot):
    n_pairs = TOP_K * t_tot
    ids = route[:, 0:TOP_K].astype(jnp.int32).reshape(bsz, t_tot, TOP_K)
    wts = route[:, TOP_K:2 * TOP_K].reshape(bsz, t_tot, TOP_K)
    ids = jnp.swapaxes(ids, 1, 2).reshape(bsz, n_pairs)
    wts = jnp.swapaxes(wts, 1, 2).reshape(bsz, n_pairs)
    key = ids * n_pairs + jnp.arange(n_pairs, dtype=jnp.int32)
    key, wts = lax.sort((key, wts), dimension=1, num_keys=1)
    tok = (key % n_pairs) % t_tot
    count = jnp.sum((ids[:, :, None] == jnp.arange(N_EXPERTS, dtype=jnp.int32)).astype(jnp.int32), axis=1)
    start = jnp.cumsum(count, axis=1) - count
    pad = ((0, 0), (0, MOE_TILE))
    return (jnp.pad(tok, pad).reshape(-1), jnp.pad(wts, pad).reshape(-1), start.reshape(-1), count.reshape(-1))


def _moe_body(tok_ref, w_ref, start_ref, count_ref, h_ref, wg_ref, wu_ref, wd_ref, y_ref, hbuf, obuf):
    c = pl.program_id(0)
    e = pl.program_id(1)
    t_tot = h_ref.shape[0]
    p_len = TOP_K * t_tot + MOE_TILE
    tm = MOE_TILE

    @pl.when(jnp.logical_and(c == 0, e == 0))
    def _():
        hbuf[...] = jnp.zeros_like(hbuf)

    @pl.when(e == 0)
    def _():
        def zero(i, carry):
            y_ref[pl.ds(pl.multiple_of(i * TOKEN_TILE, TOKEN_TILE), TOKEN_TILE), :] = jnp.zeros(
                (TOKEN_TILE, D_MODEL), F32)
            return carry
        lax.fori_loop(0, t_tot // TOKEN_TILE, zero, 0)

    count = count_ref[c * N_EXPERTS + e]
    pair_base = c * p_len + start_ref[c * N_EXPERTS + e]

    def tile(t, carry):
        n_valid = jnp.minimum(tm, count - t * tm)
        pair0 = pair_base + t * tm

        def gather(g, carry):
            for j in range(8):
                tok = tok_ref[pair0 + g * 8 + j]
                hbuf[pl.ds(g * 8 + j, 1), :] = h_ref[pl.ds(tok, 1), :]
            return carry
        lax.fori_loop(0, (n_valid + 7) // 8, gather, 0)

        h = hbuf[...].astype(BF16)
        gate = _mm(h, wg_ref[...])
        up = _mm(h, wu_ref[...])
        hid = gate * _sigmoid(gate) * up
        obuf[...] = _mm(hid.astype(BF16), wd_ref[...])

        def add_row(r):
            tok = tok_ref[pair0 + r]
            y_ref[pl.ds(tok, 1), :] += w_ref[pair0 + r] * obuf[pl.ds(r, 1), :]

        def scatter(g, carry):
            toks = [tok_ref[pair0 + g * 8 + j] for j in range(8)]
            new = [y_ref[pl.ds(toks[j], 1), :] + w_ref[pair0 + g * 8 + j] * obuf[pl.ds(g * 8 + j, 1), :]
                   for j in range(8)]
            for j in range(8):
                y_ref[pl.ds(toks[j], 1), :] = new[j]
            return carry
        lax.fori_loop(0, n_valid // 8, scatter, 0)

        def scatter_tail(r, carry):
            add_row(r)
            return carry
        lax.fori_loop(n_valid // 8 * 8, n_valid, scatter_tail, 0)
        return carry

    lax.fori_loop(0, (count + tm - 1) // tm, tile, 0)


def _moe(h, route, wg, wu, wd, layer, *, bsz, t_tot):
    tok, w, start, count = _route_tables(route, bsz, t_tot)
    grid_spec = pltpu.PrefetchScalarGridSpec(
        num_scalar_prefetch=4,
        grid=(bsz, N_EXPERTS),
        in_specs=[
            pl.BlockSpec((None, t_tot, D_MODEL), lambda c, e, *_: (c, 0, 0)),
            pl.BlockSpec((None, None, D_MODEL, D_EXPERT), lambda c, e, *_: (layer, e, 0, 0)),
            pl.BlockSpec((None, None, D_MODEL, D_EXPERT), lambda c, e, *_: (layer, e, 0, 0)),
            pl.BlockSpec((None, None, D_EXPERT, D_MODEL), lambda c, e, *_: (layer, e, 0, 0)),
        ],
        out_specs=pl.BlockSpec((None, t_tot, D_MODEL), lambda c, e, *_: (c, 0, 0)),
        scratch_shapes=[pltpu.VMEM((MOE_TILE, D_MODEL), F32), pltpu.VMEM((MOE_TILE, D_MODEL), F32)],
    )
    y = pl.pallas_call(
        _moe_body,
        grid_spec=grid_spec,
        out_shape=jax.ShapeDtypeStruct((bsz, t_tot, D_MODEL), F32),
        compiler_params=_cparams("arbitrary", "arbitrary"),
        name="moe",
    )(tok, w, start, count, h.reshape(bsz, t_tot, D_MODEL), wg, wu, wd)
    return y.reshape(bsz * t_tot, D_MODEL)


def _residual_body(x_ref, y_ref, m_ref, o_ref):
    o_ref[...] = x_ref[...] + m_ref[5:6, :] * y_ref[...]


def _residual(x, y, mod_l, *, bps, ctx_blocks, ctx_row):
    n = x.shape[0]
    tm = TOKEN_TILE
    row = functools.partial(_mod_row, blocks_per_seq=bps, ctx_blocks=ctx_blocks, ctx_row=ctx_row)
    tok = pl.BlockSpec((tm, D_MODEL), lambda i: (i, 0))
    return pl.pallas_call(
        _residual_body,
        grid=(n // tm,),
        in_specs=[tok, tok, pl.BlockSpec((None, 6, D_MODEL), lambda i: (row(i), 0, 0))],
        out_specs=tok,
        out_shape=jax.ShapeDtypeStruct((n, D_MODEL), F32),
        compiler_params=_cparams("arbitrary"),
        name="residual",
    )(x, y, mod_l)


def kernel(x, c, ctx, c_ctx, norm1_gain, norm2_gain, w_ada, b_ada, w_in, w_out, hgrn_lb_logits, hgrn_gn_gain, rwkv_mu, rwkv_w0, rwkv_w_up, rwkv_a0, rwkv_a_up, rwkv_g_up, rwkv_kk_scale, rwkv_k_a, rwkv_r_k, rwkv_ln_gain, rwkv_ln_bias, na_q_gain, na_k_gain, na_rpb, w_router_group, b_router_group, w_router_expert, b_router_expert, w_exp_gate, w_exp_up, w_exp_down):
    bsz, t_lat, _ = x.shape
    t_ctx = ctx.shape[1]
    t_tot = t_ctx + t_lat
    n = bsz * t_tot
    assert t_ctx % TOKEN_TILE == 0 and t_lat % TOKEN_TILE == 0 and bsz < 16
    bps = t_tot // TOKEN_TILE
    tile_kw = dict(bps=bps, ctx_blocks=t_ctx // TOKEN_TILE, ctx_row=bsz)

    lb_p = jax.nn.softmax(hgrn_lb_logits.astype(F32), axis=1)
    lower_bounds = jnp.cumsum(lb_p, axis=1) - lb_p[:, :1]

    cc = jnp.zeros((16, D_MODEL), F32).at[:bsz].set(c.astype(F32)).at[bsz].set(c_ctx.astype(F32))
    mod = _ada_mod(cc, w_ada, b_ada).reshape(DEPTH, 16, 6, D_MODEL)

    w_in_bf = w_in.astype(BF16)
    w_out_bf = w_out.astype(BF16)
    wg_bf = w_exp_gate.astype(BF16).reshape(DEPTH, N_EXPERTS, D_MODEL, D_EXPERT)
    wu_bf = w_exp_up.astype(BF16).reshape(DEPTH, N_EXPERTS, D_MODEL, D_EXPERT)
    wd_bf = w_exp_down.astype(BF16).reshape(DEPTH, N_EXPERTS, D_EXPERT, D_MODEL)
    pad = LANES - N_EXPERTS - N_GROUPS
    w_router = jnp.concatenate([w_router_expert, w_router_group,
                                jnp.zeros((DEPTH, D_MODEL, pad), F32)], axis=-1)
    b_router = jnp.concatenate([b_router_expert, b_router_group, jnp.zeros((DEPTH, pad), F32)], axis=-1)

    na_bias = _na_bias_table(na_rpb, t_lat // GRID_W)
    xs = jnp.concatenate([ctx.astype(F32), x.astype(F32)], axis=1).reshape(n, D_MODEL)
    for layer in range(DEPTH):
        last = layer == DEPTH - 1
        pa, pb, pc = _inproj(xs, mod[layer], norm1_gain[layer], w_in_bf[layer], **tile_kw)
        a_mix = _hgrn(pa.reshape(bsz, t_tot, A_PROJ), lower_bounds[:, layer], hgrn_gn_gain[layer], t_ctx=t_ctx)
        prep = _rwkv_prep(pb.reshape(bsz, t_tot, B_PROJ), rwkv_mu[layer], rwkv_w0[layer], rwkv_w_up[layer],
                          rwkv_a0[layer], rwkv_a_up[layer], rwkv_g_up[layer], rwkv_kk_scale[layer],
                          rwkv_k_a[layer], rwkv_r_k[layer], t_ctx=t_ctx)
        y = _rwkv_scan(*prep[:9], t_ctx=t_ctx)
        b_mix = _rwkv_readout(y.reshape(n, B_WIDTH), prep[9].reshape(n, B_WIDTH), prep[10].reshape(n, B_WIDTH),
                              rwkv_ln_gain[layer], rwkv_ln_bias[layer])
        c_mix = _na(pc.reshape(bsz, t_tot, C_PROJ), na_q_gain[layer], na_k_gain[layer], na_bias, layer,
                    t_ctx=t_ctx, need_ctx=not last)
        xs, h2, route = _outproj(xs, a_mix.reshape(n, A_WIDTH), b_mix, c_mix.reshape(n, C_WIDTH), mod[layer],
                                norm2_gain[layer], w_out_bf[layer], w_router[layer],
                                b_router[layer].reshape(1, LANES), **tile_kw)
        y_moe = _moe(h2, route, wg_bf, wu_bf, wd_bf, layer, bsz=bsz, t_tot=t_tot)
        xs = _residual(xs, y_moe, mod[layer], **tile_kw)
    return xs.reshape(bsz, t_tot, D_MODEL)[:, t_ctx:].astype(x.dtype)
```

```python
import functools

import jax
import jax.numpy as jnp
import numpy as np
from jax import lax
from jax.experimental import pallas as pl
from jax.experimental.pallas import tpu as pltpu

F32 = jnp.float32
BF16 = jnp.bfloat16

D_MODEL = 1024
DEPTH = 4
GRID_W = 64
HEAD_DIM = 64
A_HEADS = 4
B_HEADS = 4
C_HEADS = 8
A_WIDTH = A_HEADS * HEAD_DIM
B_WIDTH = B_HEADS * HEAD_DIM
C_WIDTH = C_HEADS * HEAD_DIM
HGRN_CHUNK = 16
HGRN_F_FLOOR = 1e-20
RWKV_W_RANK = 64
RWKV_A_RANK = 64
RWKV_G_RANK = 128
RWKV_LN_EPS = 64e-5
NA_WIN_ROWS = 8
NA_WIN_COLS = 16
ROPE_THETA = 10000.0
N_GROUPS = 4
EXPERTS_PER_GROUP = 8
N_EXPERTS = N_GROUPS * EXPERTS_PER_GROUP
TOP_K = 2
D_EXPERT = 512
EPS = 1e-6
LOG2E = 1.4426950408889634
NEG_INF = -1e30
A_PROJ = 5 * A_WIDTH
B_PROJ = 3 * B_WIDTH + 2 * RWKV_W_RANK + 2 * RWKV_A_RANK + RWKV_G_RANK
C_PROJ = 3 * C_WIDTH
P_TOTAL = A_PROJ + B_PROJ + C_PROJ
B_TAIL = B_PROJ - 3 * B_WIDTH

LANES = 128
TOKEN_TILE = 256
RWKV_CHUNK = 64
RWKV_BLOCKS_PER_ITER = 2
RWKV_BLOCK = 128
HGRN_BLOCK = 128
NA_ROW_UNROLL = 8
HGRN_SCAN_UNROLL = 8
MOE_TILE = 192
VMEM_LIMIT = 56 * 1024 * 1024

_HI = lax.Precision.HIGHEST


def _cparams(*sem):
    return pltpu.CompilerParams(dimension_semantics=sem, vmem_limit_bytes=VMEM_LIMIT)


def _mm(a, b):
    return jnp.dot(a, b, preferred_element_type=F32)


def _mm_hi(a, b):
    return jnp.dot(a, b, preferred_element_type=F32, precision=_HI)


def _nt(a, b):
    return lax.dot_general(a, b, (((1,), (1,)), ((), ())), preferred_element_type=F32)


def _tn(a, b):
    return lax.dot_general(a, b, (((0,), (0,)), ((), ())), preferred_element_type=F32)


def _split3(a):
    a1 = a.astype(BF16)
    r1 = a - a1.astype(F32)
    a2 = r1.astype(BF16)
    a3 = (r1 - a2.astype(F32)).astype(BF16)
    return a1, a2, a3


def _mm_x3(a, m):
    a1, a2, a3 = _split3(a)
    return _mm(a1, m) + _mm(a2, m) + _mm(a3, m)


def _mm_l3(m, a):
    a1, a2, a3 = _split3(a)
    return _mm(m, a1) + _mm(m, a2) + _mm(m, a3)


def _sigmoid(x):
    return 1.0 / (1.0 + jnp.exp(-x))


def _iota(shape, dim):
    return lax.broadcasted_iota(jnp.int32, shape, dim)


def _head_seg(n):
    return jnp.where(_iota((n, n), 0) // HEAD_DIM == _iota((n, n), 1) // HEAD_DIM, 1.0, 0.0).astype(BF16)


def _ada_body(c_ref, w_ref, b_ref, o_ref):
    c = c_ref[...]
    o_ref[...] = _mm_hi(c * _sigmoid(c), w_ref[...]) + b_ref[...]


def _ada_mod(cc, w_ada, b_ada):
    rows = cc.shape[0]
    tn = 1536
    return pl.pallas_call(
        _ada_body,
        grid=(DEPTH, 6 * D_MODEL // tn),
        in_specs=[
            pl.BlockSpec((rows, D_MODEL), lambda l, j: (0, 0)),
            pl.BlockSpec((None, D_MODEL, tn), lambda l, j: (l, 0, j)),
            pl.BlockSpec((None, 1, tn), lambda l, j: (l, 0, j)),
        ],
        out_specs=pl.BlockSpec((None, rows, tn), lambda l, j: (l, 0, j)),
        out_shape=jax.ShapeDtypeStruct((DEPTH, rows, 6 * D_MODEL), F32),
        compiler_params=_cparams("arbitrary", "arbitrary"),
        name="ada_mod",
    )(cc, w_ada, b_ada.reshape(DEPTH, 1, 6 * D_MODEL))


def _mod_row(i, blocks_per_seq, ctx_blocks, ctx_row):
    return jnp.where(i % blocks_per_seq < ctx_blocks, ctx_row, i // blocks_per_seq)


def _inproj_body(*refs, with_moe):
    if with_moe:
        x_ref, y_ref, mp_ref, m_ref, g_ref, w_ref, xo_ref, pa_ref, pb_ref, pc_ref = refs
        x = x_ref[...] + mp_ref[5:6, :] * y_ref[...]
        xo_ref[...] = x
    else:
        x_ref, m_ref, g_ref, w_ref, pa_ref, pb_ref, pc_ref = refs
        x = x_ref[...]
    h = x * lax.rsqrt(jnp.mean(x * x, axis=-1, keepdims=True) + EPS) * g_ref[...]
    h = (h * (1.0 + m_ref[1:2, :]) + m_ref[0:1, :]).astype(BF16)
    pa_ref[...] = _mm(h, w_ref[:, :A_PROJ])
    pb_ref[...] = _mm(h, w_ref[:, A_PROJ:A_PROJ + B_PROJ])
    pc_ref[...] = _mm(h, w_ref[:, A_PROJ + B_PROJ:])


def _inproj(x, mod_l, gain, w_bf, moe=None, *, bps, ctx_blocks, ctx_row):
    n = x.shape[0]
    tm = TOKEN_TILE
    row = functools.partial(_mod_row, blocks_per_seq=bps, ctx_blocks=ctx_blocks, ctx_row=ctx_row)
    tok = lambda w: pl.BlockSpec((tm, w), lambda i: (i, 0))
    mod_spec = pl.BlockSpec((None, 6, D_MODEL), lambda i: (row(i), 0, 0))
    with_moe = moe is not None
    ins = (x, *moe) if with_moe else (x,)
    return pl.pallas_call(
        functools.partial(_inproj_body, with_moe=with_moe),
        grid=(n // tm,),
        in_specs=[tok(D_MODEL)] + ([tok(D_MODEL), mod_spec] if with_moe else []) + [
            mod_spec,
            pl.BlockSpec((1, D_MODEL), lambda i: (0, 0)),
            pl.BlockSpec((D_MODEL, P_TOTAL), lambda i: (0, 0)),
        ],
        out_specs=([tok(D_MODEL)] if with_moe else []) + [tok(A_PROJ), tok(B_PROJ), tok(C_PROJ)],
        out_shape=([jax.ShapeDtypeStruct((n, D_MODEL), F32)] if with_moe else []) + [
            jax.ShapeDtypeStruct((n, A_PROJ), F32),
            jax.ShapeDtypeStruct((n, B_PROJ), F32),
            jax.ShapeDtypeStruct((n, C_PROJ), F32),
        ],
        compiler_params=_cparams("arbitrary"),
        name="inproj",
    )(*ins, mod_l, gain.reshape(1, D_MODEL), w_bf)


def _hgrn_body(q_ref, ff_ref, fb_ref, i_ref, g_ref, lb_ref, gn_ref, o_ref,
               qin_s, kout_s, cum_s, acc_s, kpad_s, cpad_s, vpad_s, *, t_ctx):
    t_tot = q_ref.shape[0]
    c = HGRN_CHUNK
    rb = HGRN_BLOCK
    n_blk = t_tot // rb
    n_chunk = t_tot // c
    n_chunk_ctx = t_ctx // c
    row = _iota((rb, rb), 0)
    col = _iota((rb, rb), 1)
    same = (row // c) == (col // c)
    same_bf = jnp.where(same, 1.0, 0.0).astype(BF16)
    seg = _head_seg(LANES)
    pos = _iota((rb, LANES), 0) % c
    blockdiag = _iota((LANES, LANES), 0) // HEAD_DIM == _iota((LANES, LANES), 1) // HEAD_DIM

    pad = HGRN_CHUNK
    zpad = jnp.zeros((pad, LANES), F32)
    for d in range(2):
        for ref in (kpad_s, cpad_s):
            ref[d, 0:pad, :] = zpad
            ref[d, pad + t_tot:, :] = zpad
    vpad_s[0:pad, :] = zpad
    vpad_s[pad + t_tot:, :] = zpad
    tris = [jnp.where(same & ((col >= row) if d == 1 else (col <= row)), 1.0, 0.0).astype(BF16) for d in range(2)]

    def gates(b, carry):
        sl = pl.ds(pl.multiple_of(b * rb, rb), rb)
        slp = pl.ds(pl.multiple_of(b * rb, rb) + pad, rb)
        q = q_ref[sl, :]
        vpad_s[slp, :] = i_ref[sl, :]
        for d, f_ref in enumerate((ff_ref, fb_ref)):
            lb = lb_ref[d:d + 1, :]
            fpre = f_ref[sl, :]
            f = lb + (1.0 - lb) * _sigmoid(fpre)
            logf = jnp.log(jnp.maximum(f, HGRN_F_FLOOR))
            k = (1.0 - lb) * _sigmoid(-fpre)
            cum = _mm_l3(tris[d], logf)
            tot = _mm_l3(same_bf, logf)
            kpad_s[d, slp, :] = k
            cpad_s[d, slp, :] = cum * LOG2E
            qin_s[d, sl, :] = q * jnp.exp(cum)
            kout_s[d, sl, :] = k * jnp.exp(tot - cum)
            cum_s[d, sl, :] = tot
        return carry

    lax.fori_loop(0, n_blk, gates, 0)

    def intra(b, carry):
        r0 = pl.multiple_of(b * rb, rb)
        sl = pl.ds(r0, rb)
        q = q_ref[sl, :]
        cums = [cpad_s[d, pl.ds(r0 + pad, rb), :] for d in range(2)]
        o = jnp.zeros((rb, LANES), F32)
        for j in range(c):
            for d in range(2):
                off = r0 + pad + (j if d == 1 else -j)
                valid = (pos <= c - 1 - j) if d == 1 else (pos >= j)
                ks = kpad_s[d, pl.ds(off, rb), :]
                cs = cpad_s[d, pl.ds(off, rb), :]
                vs = vpad_s[pl.ds(off, rb), :]
                prod = jnp.where(valid, q * ks * jnp.exp2(cums[d] - cs), 0.0)
                o = o + _mm(prod.astype(BF16), seg) * vs
        acc_s[sl, :] = o
        return carry

    lax.fori_loop(0, n_blk, intra, 0)

    def step(it, carry):
        idx = [it * HGRN_SCAN_UNROLL + j for j in range(HGRN_SCAN_UNROLL)]
        order = [(i, jnp.where(i < n_chunk_ctx, n_chunk_ctx - 1 - i, n_chunk - 1 - (i - n_chunk_ctx))) for i in idx]
        sls = [[pl.ds(pl.multiple_of(n[d] * c, c), c) for d in range(2)] for n in order]
        kvs = [[jnp.where(blockdiag, _tn(i_ref[sl[d], :].astype(BF16), kout_s[d, sl[d], :].astype(BF16)), 0.0)
                for d in range(2)] for sl in sls]
        decs = [[jnp.exp(cum_s[d, pl.ds(pl.multiple_of(n[d] * c, c), 1), :]) for d in range(2)] for n in order]
        states = list(carry)
        for j in range(HGRN_SCAN_UNROLL):
            for d in range(2):
                acc_s[sls[j][d], :] += _nt(qin_s[d, sls[j][d], :].astype(BF16), states[d].astype(BF16))
                states[d] = states[d] * decs[j][d] + kvs[j][d]
        return tuple(states)

    zero = jnp.zeros((LANES, LANES), F32)
    lax.fori_loop(0, n_chunk // HGRN_SCAN_UNROLL, step, (zero, zero))

    def readout(b, carry):
        sl = pl.ds(pl.multiple_of(b * rb, rb), rb)
        o = acc_s[sl, :]
        ms = _mm_x3(o * o, seg) * (1.0 / HEAD_DIM)
        g = g_ref[sl, :]
        o_ref[sl, :] = o * lax.rsqrt(ms + EPS) * gn_ref[...] * (g * _sigmoid(g))
        return carry

    lax.fori_loop(0, n_blk, readout, 0)


def _hgrn(pa, lb, gn_gain, *, t_ctx):
    bsz, t_tot, _ = pa.shape
    n_hp = A_WIDTH // LANES

    def sec(s):
        return pl.BlockSpec((None, t_tot, LANES), lambda b, h, s=s: (b, 0, s * n_hp + h))

    return pl.pallas_call(
        functools.partial(_hgrn_body, t_ctx=t_ctx),
        grid=(bsz, n_hp),
        in_specs=[sec(0), sec(1), sec(2), sec(3), sec(4),
                  pl.BlockSpec((2, LANES), lambda b, h: (0, h)),
                  pl.BlockSpec((1, LANES), lambda b, h: (0, h))],
        out_specs=pl.BlockSpec((None, t_tot, LANES), lambda b, h: (b, 0, h)),
        out_shape=jax.ShapeDtypeStruct((bsz, t_tot, A_WIDTH), F32),
        scratch_shapes=[
            pltpu.VMEM((2, t_tot, LANES), F32),
            pltpu.VMEM((2, t_tot, LANES), F32),
            pltpu.VMEM((2, t_tot, LANES), F32),
            pltpu.VMEM((t_tot, LANES), F32),
            pltpu.VMEM((2, t_tot + 2 * HGRN_CHUNK, LANES), F32),
            pltpu.VMEM((2, t_tot + 2 * HGRN_CHUNK, LANES), F32),
            pltpu.VMEM((t_tot + 2 * HGRN_CHUNK, LANES), F32),
        ],
        compiler_params=_cparams("arbitrary", "arbitrary"),
        name="hgrn",
    )(pa, pa, pa, pa, pa, lb, gn_gain.reshape(1, A_WIDTH))


def _rwkv_prep_body(x_ref, xp_ref, xn_ref, mu_ref, w0_ref, wup_ref, a0_ref, aup_ref, gup_ref,
                    kks_ref, ka_ref, rk_ref,
                    r_o, v_o, kk_o, kf_o, kb_o, bf_o, bb_o, lwf_o, lwb_o, g_o, bonus_o, *, t_ctx, t_tot):
    tm = x_ref.shape[0]
    r0 = pl.program_id(1) * tm
    prev_ok = jnp.logical_and(r0 != 0, r0 != t_ctx)
    next_ok = jnp.logical_and(r0 + tm != t_ctx, r0 + tm != t_tot)
    first = _iota((tm, 1), 0) == 0
    last = _iota((tm, 1), 0) == tm - 1

    def shifted(lo, hi):
        x = x_ref[:, lo:hi]
        p_row = jnp.where(prev_ok, xp_ref[7:8, lo:hi], 0.0)
        n_row = jnp.where(next_ok, xn_ref[0:1, lo:hi], 0.0)
        prev = jnp.where(first, p_row, pltpu.roll(x, 1, 0))
        nxt = jnp.where(last, n_row, pltpu.roll(x, tm - 1, 0))
        return x + (0.5 * (prev + nxt) - x) * mu_ref[:, lo:hi]

    bw = B_WIDTH
    r = shifted(0, bw)
    k = shifted(bw, 2 * bw)
    v = shifted(2 * bw, 3 * bw)
    tail = shifted(3 * bw, B_PROJ)
    seg = _head_seg(bw)

    kk = k * kks_ref[...]
    kk = kk / jnp.maximum(jnp.sqrt(_mm_x3(kk * kk, seg)), 1e-12)
    ksum = jnp.zeros_like(k)
    for d, (k_o, b_o, lw_o) in enumerate(((kf_o, bf_o, lwf_o), (kb_o, bb_o, lwb_o))):
        wd = tail[:, d * RWKV_W_RANK:(d + 1) * RWKV_W_RANK]
        ad = tail[:, 2 * RWKV_W_RANK + d * RWKV_A_RANK:2 * RWKV_W_RANK + (d + 1) * RWKV_A_RANK]
        u = -(w0_ref[d:d + 1, :] + _mm_hi(jnp.tanh(wd), wup_ref[d]))
        w = -(jnp.maximum(u, 0.0) + jnp.log(1.0 + jnp.exp(-jnp.abs(u)))) - 0.5
        a = _sigmoid(a0_ref[d:d + 1, :] + _mm_hi(ad, aup_ref[d]))
        k_d = k * (1.0 + (a - 1.0) * ka_ref[...])
        ksum = ksum + k_d
        k_o[...] = k_d
        b_o[...] = a * kk
        lw_o[...] = -jnp.exp(w)
    gd = tail[:, 2 * RWKV_W_RANK + 2 * RWKV_A_RANK:]
    r_o[...] = r
    v_o[...] = v
    kk_o[...] = kk
    g_o[...] = _mm_hi(_sigmoid(gd), gup_ref[...])
    bonus_o[...] = _mm_x3(r * ksum * rk_ref[...], seg) * v


def _rwkv_prep(pb, mu, w0, w_up, a0, a_up, g_up, kk_scale, k_a, r_k, *, t_ctx):
    bsz, t_tot, _ = pb.shape
    tm = TOKEN_TILE
    nb8 = t_tot // 8
    per8 = tm // 8
    bw = B_WIDTH
    full = lambda shape: pl.BlockSpec(shape, lambda b, i: (0,) * len(shape))
    out = jax.ShapeDtypeStruct((bsz, t_tot, bw), F32)
    return pl.pallas_call(
        functools.partial(_rwkv_prep_body, t_ctx=t_ctx, t_tot=t_tot),
        grid=(bsz, t_tot // tm),
        in_specs=[
            pl.BlockSpec((None, tm, B_PROJ), lambda b, i: (b, i, 0)),
            pl.BlockSpec((None, 8, B_PROJ), lambda b, i: (b, jnp.maximum(i * per8 - 1, 0), 0)),
            pl.BlockSpec((None, 8, B_PROJ), lambda b, i: (b, jnp.minimum((i + 1) * per8, nb8 - 1), 0)),
            full((1, B_PROJ)), full((2, bw)), full((2, RWKV_W_RANK, bw)), full((2, bw)),
            full((2, RWKV_A_RANK, bw)), full((RWKV_G_RANK, bw)), full((1, bw)), full((1, bw)), full((1, bw)),
        ],
        out_specs=[pl.BlockSpec((None, tm, bw), lambda b, i: (b, i, 0))] * 11,
        out_shape=[out] * 11,
        compiler_params=_cparams("arbitrary", "arbitrary"),
        name="rwkv_prep",
    )(pb, pb, pb, mu.reshape(1, B_PROJ), w0, w_up, a0, a_up, g_up,
      kk_scale.reshape(1, bw), k_a.reshape(1, bw), r_k.reshape(1, bw))


def _rwkv_scan_body(r_ref, v_ref, kk_ref, kf_ref, kb_ref, bf_ref, bb_ref, lwf_ref, lwb_ref, y_ref,
                    rq_s, y0_s, p_s, z_s, *, t_ctx):
    t_tot = r_ref.shape[0]
    c = RWKV_CHUNK
    rb = RWKV_BLOCK
    n_blk = t_tot // rb
    n_chunk = t_tot // c
    n_chunk_ctx = t_ctx // c
    cpb = rb // c
    n_double = c.bit_length() - 2
    row = _iota((rb, rb), 0)
    col = _iota((rb, rb), 1)
    same = (row // c) == (col // c)
    same_bf = jnp.where(same, 1.0, 0.0).astype(BF16)
    eye = jnp.where(row == col, 1.0, 0.0)
    lane = _iota((1, LANES), 1)
    head_masks = [jnp.where(lane // HEAD_DIM == h, 1.0, 0.0) for h in range(LANES // HEAD_DIM)]
    r128 = _iota((LANES, LANES), 0)
    c128 = _iota((LANES, LANES), 1)
    blockdiag = (r128 // HEAD_DIM) == (c128 // HEAD_DIM)
    diag128 = r128 == c128

    def block(it, carry):
        bis = [it * RWKV_BLOCKS_PER_ITER + j for j in range(RWKV_BLOCKS_PER_ITER)]
        sls = [pl.ds(pl.multiple_of(bi * rb, rb), rb) for bi in bis]
        v_bfs = [v_ref[sl, :].astype(BF16) for sl in sls]
        pre = {}
        for j, sl in enumerate(sls):
            r = r_ref[sl, :]
            kk = kk_ref[sl, :]
            for d in range(2):
                rev = d == 1
                k_ref, b_ref, lw_ref = (kb_ref, bb_ref, lwb_ref) if rev else (kf_ref, bf_ref, lwf_ref)
                incl = same & ((col >= row) if rev else (col <= row))
                strict = same & ((col > row) if rev else (col < row))
                incl_bf = jnp.where(incl, 1.0, 0.0).astype(BF16)
                lw = lw_ref[sl, :]
                k = k_ref[sl, :]
                b = b_ref[sl, :]
                cl = _mm_l3(incl_bf, lw)
                tot = _mm_l3(same_bf, lw)
                w_inv = jnp.exp(-cl)
                w_end = jnp.exp(tot - cl)
                pre[j, d] = dict(incl=incl, strict=strict, tot=tot, kk_d=kk * jnp.exp(cl - lw), r_d=r * jnp.exp(cl),
                                 kb=jnp.concatenate([k * w_inv, b * w_inv], axis=0).astype(BF16),
                                 b_e=(b * w_end).astype(BF16), k_e=(k * w_end).astype(BF16))
        chains = [dict(j=j, d=d, mh=mh, kkm=pre[j, d]['kk_d'] * mh, rm=pre[j, d]['r_d'] * mh)
                  for j in range(len(bis)) for d in range(2) for mh in head_masks]
        for ch in chains:
            p = pre[ch['j'], ch['d']]
            aa = _nt(jnp.concatenate([ch['kkm'], ch['rm']], axis=0).astype(BF16), p['kb'])
            ch['a_kb'] = jnp.where(p['strict'], aa[:rb, :rb], 0.0)
            ch['a_rk'] = jnp.where(p['incl'], aa[rb:, :rb], 0.0)
            ch['a_rb'] = jnp.where(p['incl'], aa[rb:, rb:], 0.0)
            m = -jnp.where(p['strict'], aa[:rb, rb:], 0.0)
            ch['t'] = eye + m
            ch['m'] = m
        for ch in chains:
            m_bf = ch['m'].astype(BF16)
            ch['m'] = _mm(m_bf, m_bf)
        for _ in range(n_double - 1):
            for ch in chains:
                both = _mm(jnp.concatenate([ch['t'], ch['m']], axis=0).astype(BF16), ch['m'].astype(BF16))
                ch['t'] = ch['t'] + both[:rb]
                ch['m'] = both[rb:]
        for ch in chains:
            ch['t'] = ch['t'] + _mm(ch['t'].astype(BF16), ch['m'].astype(BF16))
            ch['av'] = _mm(jnp.concatenate([ch['a_kb'], ch['a_rk']], axis=0).astype(BF16), v_bfs[ch['j']])
        for ch in chains:
            ch['ku'] = _mm(ch['t'].astype(BF16), jnp.concatenate([ch['kkm'], ch['av'][:rb]], axis=1).astype(BF16))
        for ch in chains:
            ch['rb_ku'] = _mm(ch['a_rb'].astype(BF16), ch['ku'].astype(BF16))
        for j, (bi, sl) in enumerate(zip(bis, sls)):
            for d in range(2):
                p = pre[j, d]
                mine = [ch for ch in chains if ch['j'] == j and ch['d'] == d]
                kkt = sum(ch['ku'][:, :LANES] for ch in mine)
                u = sum(ch['ku'][:, LANES:] * ch['mh'] for ch in mine)
                rq_s[d, sl, :] = sum(ch['rm'] - ch['rb_ku'][:, :LANES] for ch in mine)
                y0_s[d, sl, :] = sum((ch['av'][rb:] - ch['rb_ku'][:, LANES:]) * ch['mh'] for ch in mine)
                ktu = jnp.concatenate([kkt, u], axis=1).astype(BF16)
                for n in range(cpb):
                    rows = slice(n * c, (n + 1) * c)
                    wc = jnp.exp(p['tot'][n * c:n * c + 1, :])
                    kub = _tn(ktu[rows], p['b_e'][rows])
                    p_s[d, bi * cpb + n] = jnp.where(diag128, wc, 0.0) - jnp.where(blockdiag, kub[:LANES], 0.0)
                    p_z = jnp.where(blockdiag, _tn(v_bfs[j][rows], p['k_e'][rows]) - kub[LANES:], 0.0)
                    z_s[d, bi * cpb + n] = p_z
        return carry

    lax.fori_loop(0, n_blk // RWKV_BLOCKS_PER_ITER, block, 0)

    y_ref[...] = jnp.zeros_like(y_ref)

    def step(i, carry):
        n_b = jnp.where(i < n_chunk_ctx, n_chunk_ctx - 1 - i, n_chunk - 1 - (i - n_chunk_ctx))
        chunk = (i, n_b)
        s_bf = [s.astype(BF16) for s in carry]
        ys = [_nt(rq_s[d, pl.ds(pl.multiple_of(chunk[d] * c, c), c), :].astype(BF16), s_bf[d]) for d in range(2)]
        states = [_mm(s_bf[d], p_s[d, chunk[d]].astype(BF16)) + z_s[d, chunk[d]] for d in range(2)]
        for d in range(2):
            sl = pl.ds(pl.multiple_of(chunk[d] * c, c), c)
            y_ref[sl, :] += ys[d] + y0_s[d, sl, :]
        return tuple(states)

    zero = jnp.zeros((LANES, LANES), F32)
    lax.fori_loop(0, n_chunk, step, (zero, zero))


def _rwkv_scan(r, v, kk, kf, kb, bf, bb, lwf, lwb, *, t_ctx):
    bsz, t_tot, bw = r.shape
    n_chunk = t_tot // RWKV_CHUNK
    spec = pl.BlockSpec((None, t_tot, LANES), lambda b, h: (b, 0, h))
    return pl.pallas_call(
        functools.partial(_rwkv_scan_body, t_ctx=t_ctx),
        grid=(bsz, bw // LANES),
        in_specs=[spec] * 9,
        out_specs=spec,
        out_shape=jax.ShapeDtypeStruct((bsz, t_tot, bw), F32),
        scratch_shapes=[
            pltpu.VMEM((2, t_tot, LANES), F32),
            pltpu.VMEM((2, t_tot, LANES), F32),
            pltpu.VMEM((2, n_chunk, LANES, LANES), F32),
            pltpu.VMEM((2, n_chunk, LANES, LANES), F32),
        ],
        compiler_params=_cparams("arbitrary", "arbitrary"),
        name="rwkv_scan",
    )(r, v, kk, kf, kb, bf, bb, lwf, lwb)


def _rwkv_readout(y, g, bonus, ln_gain, ln_bias):
    seg = _head_seg(B_WIDTH)
    mean = _mm_x3(y, seg) * (1.0 / HEAD_DIM)
    yc = y - mean
    var = _mm_x3(yc * yc, seg) * (1.0 / HEAD_DIM)
    return (yc * lax.rsqrt(var + RWKV_LN_EPS) * ln_gain + ln_bias + bonus) * g


def _na_body(q_ref, k_ref, v_ref, qg_ref, kg_ref, cos_ref, sin_ref, bias_ref, o_ref,
             qs, ks, vs, *, t_ctx, need_ctx):
    t_tot = q_ref.shape[0]
    t_lat = t_tot - t_ctx
    rows = t_lat // GRID_W
    win_r = min(NA_WIN_ROWS, rows)
    scale = HEAD_DIM ** -0.5
    seg = _head_seg(LANES)
    lane = _iota((1, LANES), 1)
    head_masks = [jnp.where(lane // HEAD_DIM == h, 1.0, 0.0) for h in range(LANES // HEAD_DIM)]
    half = (lane % HEAD_DIM) < HEAD_DIM // 2
    blk = 256

    def norm_block(i, carry):
        sl = pl.ds(pl.multiple_of(i * blk, blk), blk)
        q = q_ref[sl, :]
        k = k_ref[sl, :]
        q = q * lax.rsqrt(_mm((q * q).astype(BF16), seg) * (1.0 / HEAD_DIM) + EPS) * qg_ref[...]
        k = k * lax.rsqrt(_mm((k * k).astype(BF16), seg) * (1.0 / HEAD_DIM) + EPS) * kg_ref[...]
        cos = cos_ref[sl, :]
        sin = sin_ref[sl, :]

        def rope(t):
            swapped = jnp.where(half, pltpu.roll(t, LANES - HEAD_DIM // 2, 1), pltpu.roll(t, HEAD_DIM // 2, 1))
            return t * cos + swapped * sin

        qs[sl, :] = (rope(q) * scale).astype(BF16)
        ks[sl, :] = rope(k).astype(BF16)
        vs[sl, :] = v_ref[sl, :].astype(BF16)
        return carry

    lax.fori_loop(0, t_tot // blk, norm_block, 0)

    masks_bf = [mh.astype(BF16) for mh in head_masks]

    def by_head(q):
        return jnp.concatenate([q * mb for mb in masks_bf], axis=0)

    def merge_heads(o, n):
        return sum(o[h * n:(h + 1) * n] * mh for h, mh in enumerate(head_masks))

    if need_ctx:
        s = _nt(by_head(qs[0:t_ctx, :]), ks[0:t_ctx, :])
        p = jnp.exp(s - jnp.max(s, axis=-1, keepdims=True))
        o = _mm(p.astype(BF16), vs[0:t_ctx, :]) / jnp.sum(p, axis=-1, keepdims=True)
        o_ref[0:t_ctx, :] = merge_heads(o, t_ctx)
    else:
        o_ref[0:t_ctx, :] = jnp.zeros((t_ctx, LANES), F32)

    def q_rows(it, carry):
        rs = [it * NA_ROW_UNROLL + j for j in range(NA_ROW_UNROLL)]
        starts = [jnp.clip(r - win_r // 2, 0, rows - win_r) for r in rs]
        q_sl = [pl.ds(pl.multiple_of(t_ctx + r * GRID_W, GRID_W), GRID_W) for r in rs]
        k_sl = [pl.ds(pl.multiple_of(t_ctx + r0 * GRID_W, GRID_W), win_r * GRID_W) for r0 in starts]
        qs_ = [by_head(qs[sl, :]) for sl in q_sl]
        s_win = [_nt(q, ks[sl, :]) + bias_ref[r - r0] for q, sl, r, r0 in zip(qs_, k_sl, rs, starts)]
        s_ctx = [_nt(q, ks[0:t_ctx, :]) for q in qs_]
        ms = [jnp.maximum(jnp.max(a, axis=-1, keepdims=True), jnp.max(b, axis=-1, keepdims=True))
              for a, b in zip(s_win, s_ctx)]
        p_win = [jnp.exp(a - m) for a, m in zip(s_win, ms)]
        p_ctx = [jnp.exp(b - m) for b, m in zip(s_ctx, ms)]
        den = [jnp.sum(a, axis=-1, keepdims=True) + jnp.sum(b, axis=-1, keepdims=True) for a, b in zip(p_win, p_ctx)]
        o_win = [_mm(a.astype(BF16), vs[sl, :]) for a, sl in zip(p_win, k_sl)]
        o_ctx = [_mm(b.astype(BF16), vs[0:t_ctx, :]) for b in p_ctx]
        for sl, a, b, d in zip(q_sl, o_win, o_ctx, den):
            o_ref[sl, :] = merge_heads((a + b) / d, GRID_W)
        return carry

    lax.fori_loop(0, rows // NA_ROW_UNROLL, q_rows, 0)


def _na_tables(t_ctx, t_lat):
    quarter = HEAD_DIM // 4
    pos = np.arange(t_lat)
    inv = ROPE_THETA ** (-np.arange(quarter, dtype=np.float32) / quarter)
    pos_r = (pos // GRID_W).astype(np.float32)
    pos_c = (pos % GRID_W).astype(np.float32)
    return pos_r, pos_c, inv


def _na_bias_table(rpb, rows):
    win_r = min(NA_WIN_ROWS, rows)
    c = np.arange(GRID_W)
    w_start = np.clip(c - NA_WIN_COLS // 2, 0, GRID_W - NA_WIN_COLS)
    kc = np.arange(GRID_W)
    in_win = (kc[None, :] >= w_start[:, None]) & (kc[None, :] < w_start[:, None] + NA_WIN_COLS)
    col_idx = np.clip(kc[None, :] - c[:, None] + NA_WIN_COLS - 1, 0, 2 * NA_WIN_COLS - 2)
    n_col = 2 * NA_WIN_COLS - 1
    onehot = jnp.asarray(col_idx[None] == np.arange(n_col)[:, None, None], F32)
    band = jnp.einsum('...hrj,jck->...hrck', rpb.astype(F32), onehot, precision=_HI)
    band = jnp.where(in_win, band, NEG_INF)
    per_off = [band[..., NA_WIN_ROWS - 1 - off:NA_WIN_ROWS - 1 - off + win_r, :, :] for off in range(win_r)]
    g = jnp.stack(per_off, axis=-4)
    g = jnp.swapaxes(g, -3, -2)
    return g.reshape(g.shape[:-2] + (win_r * GRID_W,))


def _na(pc, q_gain, k_gain, bias, layer, *, t_ctx, need_ctx):
    bsz, t_tot, _ = pc.shape
    t_lat = t_tot - t_ctx
    rows = t_lat // GRID_W
    win_r = min(NA_WIN_ROWS, rows)
    n_hp = C_WIDTH // LANES
    hpl = LANES // HEAD_DIM
    pos_r, pos_c, inv = _na_tables(t_ctx, t_lat)
    ang = np.concatenate([pos_r[:, None] * inv, pos_c[:, None] * inv], axis=-1)
    ang = np.concatenate([np.zeros((t_ctx, HEAD_DIM // 2), np.float32), ang], axis=0)
    cos = np.cos(ang)
    sin = np.sin(ang)
    cos_t = jnp.asarray(np.tile(np.concatenate([cos, cos], axis=-1), (1, hpl)), F32)
    sin_t = jnp.asarray(np.tile(np.concatenate([-sin, sin], axis=-1), (1, hpl)), F32)
    bias = bias.reshape(bias.shape[0], n_hp, hpl, win_r, GRID_W, win_r * GRID_W)
    bias = jnp.swapaxes(bias, 2, 3).reshape(bias.shape[0], n_hp, win_r, hpl * GRID_W, win_r * GRID_W)

    def sec(s):
        return pl.BlockSpec((None, t_tot, LANES), lambda b, h, s=s: (b, 0, s * n_hp + h))

    gain = lambda g: jnp.tile(g.reshape(1, HEAD_DIM), (1, hpl))
    return pl.pallas_call(
        functools.partial(_na_body, t_ctx=t_ctx, need_ctx=need_ctx),
        grid=(bsz, n_hp),
        in_specs=[sec(0), sec(1), sec(2),
                  pl.BlockSpec((1, LANES), lambda b, h: (0, 0)),
                  pl.BlockSpec((1, LANES), lambda b, h: (0, 0)),
                  pl.BlockSpec((t_tot, LANES), lambda b, h: (0, 0)),
                  pl.BlockSpec((t_tot, LANES), lambda b, h: (0, 0)),
                  pl.BlockSpec((None, None, win_r, hpl * GRID_W, win_r * GRID_W),
                               lambda b, h: (layer, h, 0, 0, 0))],
        out_specs=pl.BlockSpec((None, t_tot, LANES), lambda b, h: (b, 0, h)),
        out_shape=jax.ShapeDtypeStruct((bsz, t_tot, C_WIDTH), F32),
        scratch_shapes=[pltpu.VMEM((t_tot, LANES), BF16)] * 3,
        compiler_params=_cparams("arbitrary", "arbitrary"),
        name="na",
    )(pc, pc, pc, gain(q_gain), gain(k_gain), cos_t, sin_t, bias)


def _outproj_body(x_ref, a_ref, by_ref, bg_ref, bb_ref, lng_ref, lnb_ref, c_ref, m_ref, g_ref, w_ref, wr_ref, br_ref,
                  xo_ref, h_ref, route_ref):
    b_mix = _rwkv_readout(by_ref[...], bg_ref[...], bb_ref[...], lng_ref[...], lnb_ref[...])
    mix = (_mm(a_ref[...].astype(BF16), w_ref[0:A_WIDTH, :])
           + _mm(b_mix.astype(BF16), w_ref[A_WIDTH:A_WIDTH + B_WIDTH, :])
           + _mm(c_ref[...].astype(BF16), w_ref[A_WIDTH + B_WIDTH:, :]))
    x = x_ref[...] + m_ref[2:3, :] * mix
    xo_ref[...] = x
    h = x * lax.rsqrt(jnp.mean(x * x, axis=-1, keepdims=True) + EPS) * g_ref[...]
    h = h * (1.0 + m_ref[4:5, :]) + m_ref[3:4, :]
    h_ref[...] = h
    logits = _mm_hi(h, wr_ref[...]) + br_ref[...]
    lane = _iota(logits.shape, 1).astype(F32)
    big = float(LANES)
    is_g = (lane >= N_EXPERTS) & (lane < N_EXPERTS + N_GROUPS)
    gl = jnp.where(is_g, logits, -jnp.inf)
    gmax = jnp.max(gl, axis=-1, keepdims=True)
    g_w = 1.0 / jnp.sum(jnp.exp(gl - gmax), axis=-1, keepdims=True)
    g_sel = jnp.min(jnp.where(gl == gmax, lane, big), axis=-1, keepdims=True) - N_EXPERTS
    lo = g_sel * EXPERTS_PER_GROUP
    el = jnp.where((lane >= lo) & (lane < lo + EXPERTS_PER_GROUP), logits, -jnp.inf)
    m1 = jnp.max(el, axis=-1, keepdims=True)
    i1 = jnp.min(jnp.where(el == m1, lane, big), axis=-1, keepdims=True)
    el2 = jnp.where(lane == i1, -jnp.inf, el)
    m2 = jnp.max(el2, axis=-1, keepdims=True)
    i2 = jnp.min(jnp.where(el2 == m2, lane, big), axis=-1, keepdims=True)
    e2 = jnp.exp(m2 - m1)
    w1 = g_w / (1.0 + e2)
    route_ref[...] = (jnp.where(lane == 0.0, i1, 0.0) + jnp.where(lane == 1.0, i2, 0.0)
                      + jnp.where(lane == 2.0, w1, 0.0) + jnp.where(lane == 3.0, w1 * e2, 0.0))


def _outproj(x, a, b_scan, b_gate, b_bonus, ln_gain, ln_bias, c, mod_l, gain2, w_bf, w_router, b_router, *,
             bps, ctx_blocks, ctx_row):
    n = x.shape[0]
    tm = TOKEN_TILE
    row = functools.partial(_mod_row, blocks_per_seq=bps, ctx_blocks=ctx_blocks, ctx_row=ctx_row)
    tok = lambda w: pl.BlockSpec((tm, w), lambda i: (i, 0))
    full = lambda shape: pl.BlockSpec(shape, lambda i: (0,) * len(shape))
    return pl.pallas_call(
        _outproj_body,
        grid=(n // tm,),
        in_specs=[tok(D_MODEL), tok(A_WIDTH), tok(B_WIDTH), tok(B_WIDTH), tok(B_WIDTH),
                  full((1, B_WIDTH)), full((1, B_WIDTH)), tok(C_WIDTH),
                  pl.BlockSpec((None, 6, D_MODEL), lambda i: (row(i), 0, 0)),
                  full((1, D_MODEL)), full((D_MODEL, D_MODEL)), full((D_MODEL, LANES)), full((1, LANES))],
        out_specs=[tok(D_MODEL), tok(D_MODEL), tok(LANES)],
        out_shape=[jax.ShapeDtypeStruct((n, D_MODEL), F32),
                   jax.ShapeDtypeStruct((n, D_MODEL), F32),
                   jax.ShapeDtypeStruct((n, LANES), F32)],
        compiler_params=_cparams("arbitrary"),
        name="outproj",
    )(x, a, b_scan, b_gate, b_bonus, ln_gain.reshape(1, B_WIDTH), ln_bias.reshape(1, B_WIDTH), c, mod_l,
      gain2.reshape(1, D_MODEL), w_bf, w_router, b_router)


def _route_tables(route, bsz, t_tot):
    n_pairs = TOP_K * t_tot
    ids = route[:, 0:TOP_K].astype(jnp.int32).reshape(bsz, t_tot, TOP_K)
    wts = route[:, TOP_K:2 * TOP_K].reshape(bsz, t_tot, TOP_K)
    ids = jnp.swapaxes(ids, 1, 2).reshape(bsz, n_pairs)
    wts = jnp.swapaxes(wts, 1, 2).reshape(bsz, n_pairs)
    key = ids * n_pairs + jnp.arange(n_pairs, dtype=jnp.int32)
    key, wts = lax.sort((key, wts), dimension=1, num_keys=1)
    tok = (key % n_pairs) % t_tot
    count = jnp.sum((ids[:, :, None] == jnp.arange(N_EXPERTS, dtype=jnp.int32)).astype(jnp.int32), axis=1)
    start = jnp.cumsum(count, axis=1) - count
    pad = ((0, 0), (0, MOE_TILE))
    return (jnp.pad(tok, pad).reshape(-1), jnp.pad(wts, pad).reshape(-1), start.reshape(-1), count.reshape(-1))


def _moe_body(tok_ref, w_ref, start_ref, count_ref, h_ref, wg_ref, wu_ref, wd_ref, y_ref, hbuf, obuf):
    c = pl.program_id(0)
    e = pl.program_id(1)
    t_tot = h_ref.shape[0]
    p_len = TOP_K * t_tot + MOE_TILE
    tm = MOE_TILE

    @pl.when(jnp.logical_and(c == 0, e == 0))
    def _():
        hbuf[...] = jnp.zeros_like(hbuf)

    @pl.when(e == 0)
    def _():
        def zero(i, carry):
            y_ref[pl.ds(pl.multiple_of(i * TOKEN_TILE, TOKEN_TILE), TOKEN_TILE), :] = jnp.zeros(
                (TOKEN_TILE, D_MODEL), F32)
            return carry
        lax.fori_loop(0, t_tot // TOKEN_TILE, zero, 0)

    count = count_ref[c * N_EXPERTS + e]
    pair_base = c * p_len + start_ref[c * N_EXPERTS + e]

    def tile(t, carry):
        n_valid = jnp.minimum(tm, count - t * tm)
        pair0 = pair_base + t * tm

        def gather(g, carry):
            for j in range(8):
                tok = tok_ref[pair0 + g * 8 + j]
                hbuf[g, pl.ds(j, 1), :] = h_ref[pl.ds(tok, 1), :]
            return carry
        lax.fori_loop(0, (n_valid + 7) // 8, gather, 0)

        h = hbuf[...].reshape(tm, D_MODEL).astype(BF16)
        gate = _mm(h, wg_ref[...])
        up = _mm(h, wu_ref[...])
        hid = gate * _sigmoid(gate) * up
        obuf[...] = _mm(hid.astype(BF16), wd_ref[...]).reshape(tm // 8, 8, D_MODEL)

        def add_row(r):
            tok = tok_ref[pair0 + r]
            y_ref[pl.ds(tok, 1), :] += w_ref[pair0 + r] * obuf[r // 8, pl.ds(r % 8, 1), :]

        def scatter(g, carry):
            toks = [tok_ref[pair0 + g * 8 + j] for j in range(8)]
            new = [y_ref[pl.ds(toks[j], 1), :] + w_ref[pair0 + g * 8 + j] * obuf[g, pl.ds(j, 1), :]
                   for j in range(8)]
            for j in range(8):
                y_ref[pl.ds(toks[j], 1), :] = new[j]
            return carry
        lax.fori_loop(0, n_valid // 8, scatter, 0)

        def scatter_tail(r, carry):
            add_row(r)
            return carry
        lax.fori_loop(n_valid // 8 * 8, n_valid, scatter_tail, 0)
        return carry

    lax.fori_loop(0, (count + tm - 1) // tm, tile, 0)


def _moe(h, route, wg, wu, wd, layer, *, bsz, t_tot):
    tok, w, start, count = _route_tables(route, bsz, t_tot)
    grid_spec = pltpu.PrefetchScalarGridSpec(
        num_scalar_prefetch=4,
        grid=(bsz, N_EXPERTS),
        in_specs=[
            pl.BlockSpec((None, t_tot, D_MODEL), lambda c, e, *_: (c, 0, 0)),
            pl.BlockSpec((None, None, D_MODEL, D_EXPERT), lambda c, e, *_: (layer, e, 0, 0)),
            pl.BlockSpec((None, None, D_MODEL, D_EXPERT), lambda c, e, *_: (layer, e, 0, 0)),
            pl.BlockSpec((None, None, D_EXPERT, D_MODEL), lambda c, e, *_: (layer, e, 0, 0)),
        ],
        out_specs=pl.BlockSpec((None, t_tot, D_MODEL), lambda c, e, *_: (c, 0, 0)),
        scratch_shapes=[pltpu.VMEM((MOE_TILE // 8, 8, D_MODEL), F32)] * 2,
    )
    y = pl.pallas_call(
        _moe_body,
        grid_spec=grid_spec,
        out_shape=jax.ShapeDtypeStruct((bsz, t_tot, D_MODEL), F32),
        compiler_params=_cparams("arbitrary", "arbitrary"),
        name="moe",
    )(tok, w, start, count, h.reshape(bsz, t_tot, D_MODEL), wg, wu, wd)
    return y.reshape(bsz * t_tot, D_MODEL)


def _residual_body(x_ref, y_ref, m_ref, o_ref):
    o_ref[...] = x_ref[...] + m_ref[5:6, :] * y_ref[...]


def _residual(x, y, mod_l, *, bps, ctx_blocks, ctx_row):
    n = x.shape[0]
    tm = TOKEN_TILE
    row = functools.partial(_mod_row, blocks_per_seq=bps, ctx_blocks=ctx_blocks, ctx_row=ctx_row)
    tok = pl.BlockSpec((tm, D_MODEL), lambda i: (i, 0))
    return pl.pallas_call(
        _residual_body,
        grid=(n // tm,),
        in_specs=[tok, tok, pl.BlockSpec((None, 6, D_MODEL), lambda i: (row(i), 0, 0))],
        out_specs=tok,
        out_shape=jax.ShapeDtypeStruct((n, D_MODEL), F32),
        compiler_params=_cparams("arbitrary"),
        name="residual",
    )(x, y, mod_l)


def kernel(x, c, ctx, c_ctx, norm1_gain, norm2_gain, w_ada, b_ada, w_in, w_out, hgrn_lb_logits, hgrn_gn_gain, rwkv_mu, rwkv_w0, rwkv_w_up, rwkv_a0, rwkv_a_up, rwkv_g_up, rwkv_kk_scale, rwkv_k_a, rwkv_r_k, rwkv_ln_gain, rwkv_ln_bias, na_q_gain, na_k_gain, na_rpb, w_router_group, b_router_group, w_router_expert, b_router_expert, w_exp_gate, w_exp_up, w_exp_down):
    bsz, t_lat, _ = x.shape
    t_ctx = ctx.shape[1]
    t_tot = t_ctx + t_lat
    n = bsz * t_tot
    assert t_ctx % TOKEN_TILE == 0 and t_lat % TOKEN_TILE == 0 and bsz < 16
    bps = t_tot // TOKEN_TILE
    tile_kw = dict(bps=bps, ctx_blocks=t_ctx // TOKEN_TILE, ctx_row=bsz)

    lb_p = jax.nn.softmax(hgrn_lb_logits.astype(F32), axis=1)
    lower_bounds = jnp.cumsum(lb_p, axis=1) - lb_p[:, :1]

    cc = jnp.zeros((16, D_MODEL), F32).at[:bsz].set(c.astype(F32)).at[bsz].set(c_ctx.astype(F32))
    mod = _ada_mod(cc, w_ada, b_ada).reshape(DEPTH, 16, 6, D_MODEL)

    w_in_bf = w_in.astype(BF16)
    w_out_bf = w_out.astype(BF16)
    wg_bf = w_exp_gate.astype(BF16).reshape(DEPTH, N_EXPERTS, D_MODEL, D_EXPERT)
    wu_bf = w_exp_up.astype(BF16).reshape(DEPTH, N_EXPERTS, D_MODEL, D_EXPERT)
    wd_bf = w_exp_down.astype(BF16).reshape(DEPTH, N_EXPERTS, D_EXPERT, D_MODEL)
    pad = LANES - N_EXPERTS - N_GROUPS
    w_router = jnp.concatenate([w_router_expert, w_router_group,
                                jnp.zeros((DEPTH, D_MODEL, pad), F32)], axis=-1)
    b_router = jnp.concatenate([b_router_expert, b_router_group, jnp.zeros((DEPTH, pad), F32)], axis=-1)

    na_bias = _na_bias_table(na_rpb, t_lat // GRID_W)
    xs = jnp.concatenate([ctx.astype(F32), x.astype(F32)], axis=1).reshape(n, D_MODEL)
    y_moe = None
    for layer in range(DEPTH):
        last = layer == DEPTH - 1
        if layer == 0:
            pa, pb, pc = _inproj(xs, mod[layer], norm1_gain[layer], w_in_bf[layer], **tile_kw)
        else:
            xs, pa, pb, pc = _inproj(xs, mod[layer], norm1_gain[layer], w_in_bf[layer],
                                     (y_moe, mod[layer - 1]), **tile_kw)
        a_mix = _hgrn(pa.reshape(bsz, t_tot, A_PROJ), lower_bounds[:, layer], hgrn_gn_gain[layer], t_ctx=t_ctx)
        prep = _rwkv_prep(pb.reshape(bsz, t_tot, B_PROJ), rwkv_mu[layer], rwkv_w0[layer], rwkv_w_up[layer],
                          rwkv_a0[layer], rwkv_a_up[layer], rwkv_g_up[layer], rwkv_kk_scale[layer],
                          rwkv_k_a[layer], rwkv_r_k[layer], t_ctx=t_ctx)
        y = _rwkv_scan(*prep[:9], t_ctx=t_ctx)
        c_mix = _na(pc.reshape(bsz, t_tot, C_PROJ), na_q_gain[layer], na_k_gain[layer], na_bias, layer,
                    t_ctx=t_ctx, need_ctx=not last)
        xs, h2, route = _outproj(xs, a_mix.reshape(n, A_WIDTH), y.reshape(n, B_WIDTH), prep[9].reshape(n, B_WIDTH),
                                 prep[10].reshape(n, B_WIDTH), rwkv_ln_gain[layer], rwkv_ln_bias[layer],
                                 c_mix.reshape(n, C_WIDTH), mod[layer], norm2_gain[layer], w_out_bf[layer],
                                 w_router[layer], b_router[layer].reshape(1, LANES), **tile_kw)
        y_moe = _moe(h2, route, wg_bf, wu_bf, wd_bf, layer, bsz=bsz, t_tot=t_tot)
    xs = _residual(xs, y_moe, mod[DEPTH - 1], **tile_kw)
    return xs.reshape(bsz, t_tot, D_MODEL)[:, t_ctx:].astype(x.dtype)
```

```python
import functools

import jax
import jax.numpy as jnp
import numpy as np
from jax import lax
from jax.experimental import pallas as pl
from jax.experimental.pallas import tpu as pltpu

F32 = jnp.float32
BF16 = jnp.bfloat16

D_MODEL = 1024
DEPTH = 4
GRID_W = 64
HEAD_DIM = 64
A_HEADS = 4
B_HEADS = 4
C_HEADS = 8
A_WIDTH = A_HEADS * HEAD_DIM
B_WIDTH = B_HEADS * HEAD_DIM
C_WIDTH = C_HEADS * HEAD_DIM
HGRN_CHUNK = 16
HGRN_F_FLOOR = 1e-20
RWKV_W_RANK = 64
RWKV_A_RANK = 64
RWKV_G_RANK = 128
RWKV_LN_EPS = 64e-5
NA_WIN_ROWS = 8
NA_WIN_COLS = 16
ROPE_THETA = 10000.0
N_GROUPS = 4
EXPERTS_PER_GROUP = 8
N_EXPERTS = N_GROUPS * EXPERTS_PER_GROUP
TOP_K = 2
D_EXPERT = 512
EPS = 1e-6
LOG2E = 1.4426950408889634
NEG_INF = -1e30
A_PROJ = 5 * A_WIDTH
B_PROJ = 3 * B_WIDTH + 2 * RWKV_W_RANK + 2 * RWKV_A_RANK + RWKV_G_RANK
C_PROJ = 3 * C_WIDTH
P_TOTAL = A_PROJ + B_PROJ + C_PROJ
B_TAIL = B_PROJ - 3 * B_WIDTH

LANES = 128
TOKEN_TILE = 256
RWKV_CHUNK = 64
RWKV_BLOCKS_PER_ITER = 2
RWKV_BLOCK = 128
HGRN_BLOCK = 128
NA_ROW_UNROLL = 8
HGRN_SCAN_UNROLL = 8
MOE_EXPERTS_PER_STEP = 2
MOE_TILE = 192
VMEM_LIMIT = 56 * 1024 * 1024

_HI = lax.Precision.HIGHEST


def _cparams(*sem):
    return pltpu.CompilerParams(dimension_semantics=sem, vmem_limit_bytes=VMEM_LIMIT)


def _mm(a, b):
    return jnp.dot(a, b, preferred_element_type=F32)


def _mm_hi(a, b):
    return jnp.dot(a, b, preferred_element_type=F32, precision=_HI)


def _nt(a, b):
    return lax.dot_general(a, b, (((1,), (1,)), ((), ())), preferred_element_type=F32)


def _tn(a, b):
    return lax.dot_general(a, b, (((0,), (0,)), ((), ())), preferred_element_type=F32)


def _split3(a):
    a1 = a.astype(BF16)
    r1 = a - a1.astype(F32)
    a2 = r1.astype(BF16)
    a3 = (r1 - a2.astype(F32)).astype(BF16)
    return a1, a2, a3


def _mm_x3(a, m):
    a1, a2, a3 = _split3(a)
    return _mm(a1, m) + _mm(a2, m) + _mm(a3, m)


def _mm_l3(m, a):
    a1, a2, a3 = _split3(a)
    return _mm(m, a1) + _mm(m, a2) + _mm(m, a3)


def _mm_3pass(a, b):
    a1 = a.astype(BF16)
    a2 = (a - a1.astype(F32)).astype(BF16)
    b1 = b.astype(BF16)
    b2 = (b - b1.astype(F32)).astype(BF16)
    return _mm(a1, b1) + (_mm(a1, b2) + _mm(a2, b1))


def _sigmoid(x):
    return 1.0 / (1.0 + jnp.exp(-x))


def _iota(shape, dim):
    return lax.broadcasted_iota(jnp.int32, shape, dim)


def _head_seg(n):
    return jnp.where(_iota((n, n), 0) // HEAD_DIM == _iota((n, n), 1) // HEAD_DIM, 1.0, 0.0).astype(BF16)


def _ada_body(c_ref, w_ref, b_ref, o_ref):
    c = c_ref[...]
    o_ref[...] = _mm_hi(c * _sigmoid(c), w_ref[...]) + b_ref[...]


def _ada_mod(cc, w_ada, b_ada):
    rows = cc.shape[0]
    tn = 1536
    return pl.pallas_call(
        _ada_body,
        grid=(DEPTH, 6 * D_MODEL // tn),
        in_specs=[
            pl.BlockSpec((rows, D_MODEL), lambda l, j: (0, 0)),
            pl.BlockSpec((None, D_MODEL, tn), lambda l, j: (l, 0, j)),
            pl.BlockSpec((None, 1, tn), lambda l, j: (l, 0, j)),
        ],
        out_specs=pl.BlockSpec((None, rows, tn), lambda l, j: (l, 0, j)),
        out_shape=jax.ShapeDtypeStruct((DEPTH, rows, 6 * D_MODEL), F32),
        compiler_params=_cparams("arbitrary", "arbitrary"),
        name="ada_mod",
    )(cc, w_ada, b_ada.reshape(DEPTH, 1, 6 * D_MODEL))


def _mod_row(i, blocks_per_seq, ctx_blocks, ctx_row):
    return jnp.where(i % blocks_per_seq < ctx_blocks, ctx_row, i // blocks_per_seq)


def _inproj_body(*refs, with_moe):
    if with_moe:
        x_ref, y_ref, mp_ref, m_ref, g_ref, w_ref, xo_ref, pa_ref, pb_ref, pc_ref = refs
        x = x_ref[...] + mp_ref[5:6, :] * y_ref[...]
        xo_ref[...] = x
    else:
        x_ref, m_ref, g_ref, w_ref, pa_ref, pb_ref, pc_ref = refs
        x = x_ref[...]
    h = x * lax.rsqrt(jnp.mean(x * x, axis=-1, keepdims=True) + EPS) * g_ref[...]
    h = (h * (1.0 + m_ref[1:2, :]) + m_ref[0:1, :]).astype(BF16)
    pa_ref[...] = _mm(h, w_ref[:, :A_PROJ])
    pb_ref[...] = _mm(h, w_ref[:, A_PROJ:A_PROJ + B_PROJ])
    pc_ref[...] = _mm(h, w_ref[:, A_PROJ + B_PROJ:])


def _inproj(x, mod_l, gain, w_bf, moe=None, *, bps, ctx_blocks, ctx_row):
    n = x.shape[0]
    tm = TOKEN_TILE
    row = functools.partial(_mod_row, blocks_per_seq=bps, ctx_blocks=ctx_blocks, ctx_row=ctx_row)
    tok = lambda w: pl.BlockSpec((tm, w), lambda i: (i, 0))
    mod_spec = pl.BlockSpec((None, 6, D_MODEL), lambda i: (row(i), 0, 0))
    with_moe = moe is not None
    ins = (x, *moe) if with_moe else (x,)
    return pl.pallas_call(
        functools.partial(_inproj_body, with_moe=with_moe),
        grid=(n // tm,),
        in_specs=[tok(D_MODEL)] + ([tok(D_MODEL), mod_spec] if with_moe else []) + [
            mod_spec,
            pl.BlockSpec((1, D_MODEL), lambda i: (0, 0)),
            pl.BlockSpec((D_MODEL, P_TOTAL), lambda i: (0, 0)),
        ],
        out_specs=([tok(D_MODEL)] if with_moe else []) + [tok(A_PROJ), tok(B_PROJ), tok(C_PROJ)],
        out_shape=([jax.ShapeDtypeStruct((n, D_MODEL), F32)] if with_moe else []) + [
            jax.ShapeDtypeStruct((n, A_PROJ), F32),
            jax.ShapeDtypeStruct((n, B_PROJ), F32),
            jax.ShapeDtypeStruct((n, C_PROJ), F32),
        ],
        compiler_params=_cparams("arbitrary"),
        name="inproj",
    )(*ins, mod_l, gain.reshape(1, D_MODEL), w_bf)


def _hgrn_body(q_ref, ff_ref, fb_ref, i_ref, g_ref, lb_ref, gn_ref, o_ref,
               qin_s, kout_s, cum_s, acc_s, kpad_s, cpad_s, vpad_s, *, t_ctx):
    t_tot = q_ref.shape[0]
    c = HGRN_CHUNK
    rb = HGRN_BLOCK
    n_blk = t_tot // rb
    n_chunk = t_tot // c
    n_chunk_ctx = t_ctx // c
    row = _iota((rb, rb), 0)
    col = _iota((rb, rb), 1)
    same = (row // c) == (col // c)
    same_bf = jnp.where(same, 1.0, 0.0).astype(BF16)
    seg = _head_seg(LANES)
    pos = _iota((rb, LANES), 0) % c
    blockdiag = _iota((LANES, LANES), 0) // HEAD_DIM == _iota((LANES, LANES), 1) // HEAD_DIM

    pad = HGRN_CHUNK
    zpad = jnp.zeros((pad, LANES), F32)
    for d in range(2):
        for ref in (kpad_s, cpad_s):
            ref[d, 0:pad, :] = zpad
            ref[d, pad + t_tot:, :] = zpad
    vpad_s[0:pad, :] = zpad
    vpad_s[pad + t_tot:, :] = zpad
    tris = [jnp.where(same & ((col >= row) if d == 1 else (col <= row)), 1.0, 0.0).astype(BF16) for d in range(2)]

    def gates(b, carry):
        sl = pl.ds(pl.multiple_of(b * rb, rb), rb)
        slp = pl.ds(pl.multiple_of(b * rb, rb) + pad, rb)
        q = q_ref[sl, :]
        vpad_s[slp, :] = i_ref[sl, :]
        for d, f_ref in enumerate((ff_ref, fb_ref)):
            lb = lb_ref[d:d + 1, :]
            fpre = f_ref[sl, :]
            f = lb + (1.0 - lb) * _sigmoid(fpre)
            logf = jnp.log(jnp.maximum(f, HGRN_F_FLOOR))
            k = (1.0 - lb) * _sigmoid(-fpre)
            cum = _mm_l3(tris[d], logf)
            tot = _mm_l3(same_bf, logf)
            kpad_s[d, slp, :] = k
            cpad_s[d, slp, :] = cum * LOG2E
            qin_s[d, sl, :] = q * jnp.exp(cum)
            kout_s[d, sl, :] = k * jnp.exp(tot - cum)
            cum_s[d, sl, :] = tot
        return carry

    lax.fori_loop(0, n_blk, gates, 0)

    def intra(b, carry):
        r0 = pl.multiple_of(b * rb, rb)
        sl = pl.ds(r0, rb)
        q = q_ref[sl, :]
        cums = [cpad_s[d, pl.ds(r0 + pad, rb), :] for d in range(2)]
        o = jnp.zeros((rb, LANES), F32)
        for j in range(c):
            for d in range(2):
                off = r0 + pad + (j if d == 1 else -j)
                valid = (pos <= c - 1 - j) if d == 1 else (pos >= j)
                ks = kpad_s[d, pl.ds(off, rb), :]
                cs = cpad_s[d, pl.ds(off, rb), :]
                vs = vpad_s[pl.ds(off, rb), :]
                prod = jnp.where(valid, q * ks * jnp.exp2(cums[d] - cs), 0.0)
                o = o + _mm(prod.astype(BF16), seg) * vs
        acc_s[sl, :] = o
        return carry

    lax.fori_loop(0, n_blk, intra, 0)

    def step(it, carry):
        idx = [it * HGRN_SCAN_UNROLL + j for j in range(HGRN_SCAN_UNROLL)]
        order = [(i, jnp.where(i < n_chunk_ctx, n_chunk_ctx - 1 - i, n_chunk - 1 - (i - n_chunk_ctx))) for i in idx]
        sls = [[pl.ds(pl.multiple_of(n[d] * c, c), c) for d in range(2)] for n in order]
        kvs = [[jnp.where(blockdiag, _tn(i_ref[sl[d], :].astype(BF16), kout_s[d, sl[d], :].astype(BF16)), 0.0)
                for d in range(2)] for sl in sls]
        decs = [[jnp.exp(cum_s[d, pl.ds(pl.multiple_of(n[d] * c, c), 1), :]) for d in range(2)] for n in order]
        states = list(carry)
        for j in range(HGRN_SCAN_UNROLL):
            for d in range(2):
                acc_s[sls[j][d], :] += _nt(qin_s[d, sls[j][d], :].astype(BF16), states[d].astype(BF16))
                states[d] = states[d] * decs[j][d] + kvs[j][d]
        return tuple(states)

    zero = jnp.zeros((LANES, LANES), F32)
    lax.fori_loop(0, n_chunk // HGRN_SCAN_UNROLL, step, (zero, zero))

    def readout(b, carry):
        sl = pl.ds(pl.multiple_of(b * rb, rb), rb)
        o = acc_s[sl, :]
        ms = _mm_x3(o * o, seg) * (1.0 / HEAD_DIM)
        g = g_ref[sl, :]
        o_ref[sl, :] = o * lax.rsqrt(ms + EPS) * gn_ref[...] * (g * _sigmoid(g))
        return carry

    lax.fori_loop(0, n_blk, readout, 0)


def _hgrn(pa, lb, gn_gain, *, t_ctx):
    bsz, t_tot, _ = pa.shape
    n_hp = A_WIDTH // LANES

    def sec(s):
        return pl.BlockSpec((None, t_tot, LANES), lambda b, h, s=s: (b, 0, s * n_hp + h))

    return pl.pallas_call(
        functools.partial(_hgrn_body, t_ctx=t_ctx),
        grid=(bsz, n_hp),
        in_specs=[sec(0), sec(1), sec(2), sec(3), sec(4),
                  pl.BlockSpec((2, LANES), lambda b, h: (0, h)),
                  pl.BlockSpec((1, LANES), lambda b, h: (0, h))],
        out_specs=pl.BlockSpec((None, t_tot, LANES), lambda b, h: (b, 0, h)),
        out_shape=jax.ShapeDtypeStruct((bsz, t_tot, A_WIDTH), F32),
        scratch_shapes=[
            pltpu.VMEM((2, t_tot, LANES), F32),
            pltpu.VMEM((2, t_tot, LANES), F32),
            pltpu.VMEM((2, t_tot, LANES), F32),
            pltpu.VMEM((t_tot, LANES), F32),
            pltpu.VMEM((2, t_tot + 2 * HGRN_CHUNK, LANES), F32),
            pltpu.VMEM((2, t_tot + 2 * HGRN_CHUNK, LANES), F32),
            pltpu.VMEM((t_tot + 2 * HGRN_CHUNK, LANES), F32),
        ],
        compiler_params=_cparams("arbitrary", "arbitrary"),
        name="hgrn",
    )(pa, pa, pa, pa, pa, lb, gn_gain.reshape(1, A_WIDTH))


def _rwkv_prep_body(x_ref, xp_ref, xn_ref, mu_ref, w0_ref, wup_ref, a0_ref, aup_ref, gup_ref,
                    kks_ref, ka_ref, rk_ref,
                    r_o, v_o, kk_o, kf_o, kb_o, bf_o, bb_o, lwf_o, lwb_o, g_o, bonus_o, *, t_ctx, t_tot):
    tm = x_ref.shape[0]
    r0 = pl.program_id(1) * tm
    prev_ok = jnp.logical_and(r0 != 0, r0 != t_ctx)
    next_ok = jnp.logical_and(r0 + tm != t_ctx, r0 + tm != t_tot)
    first = _iota((tm, 1), 0) == 0
    last = _iota((tm, 1), 0) == tm - 1

    def shifted(lo, hi):
        x = x_ref[:, lo:hi]
        p_row = jnp.where(prev_ok, xp_ref[7:8, lo:hi], 0.0)
        n_row = jnp.where(next_ok, xn_ref[0:1, lo:hi], 0.0)
        prev = jnp.where(first, p_row, pltpu.roll(x, 1, 0))
        nxt = jnp.where(last, n_row, pltpu.roll(x, tm - 1, 0))
        return x + (0.5 * (prev + nxt) - x) * mu_ref[:, lo:hi]

    bw = B_WIDTH
    r = shifted(0, bw)
    k = shifted(bw, 2 * bw)
    v = shifted(2 * bw, 3 * bw)
    tail = shifted(3 * bw, B_PROJ)
    seg = _head_seg(bw)

    kk = k * kks_ref[...]
    kk = kk / jnp.maximum(jnp.sqrt(_mm_x3(kk * kk, seg)), 1e-12)
    ksum = jnp.zeros_like(k)
    for d, (k_o, b_o, lw_o) in enumerate(((kf_o, bf_o, lwf_o), (kb_o, bb_o, lwb_o))):
        wd = tail[:, d * RWKV_W_RANK:(d + 1) * RWKV_W_RANK]
        ad = tail[:, 2 * RWKV_W_RANK + d * RWKV_A_RANK:2 * RWKV_W_RANK + (d + 1) * RWKV_A_RANK]
        u = -(w0_ref[d:d + 1, :] + _mm_hi(jnp.tanh(wd), wup_ref[d]))
        w = -(jnp.maximum(u, 0.0) + jnp.log(1.0 + jnp.exp(-jnp.abs(u)))) - 0.5
        a = _sigmoid(a0_ref[d:d + 1, :] + _mm_hi(ad, aup_ref[d]))
        k_d = k * (1.0 + (a - 1.0) * ka_ref[...])
        ksum = ksum + k_d
        k_o[...] = k_d
        b_o[...] = a * kk
        lw_o[...] = -jnp.exp(w)
    gd = tail[:, 2 * RWKV_W_RANK + 2 * RWKV_A_RANK:]
    r_o[...] = r
    v_o[...] = v
    kk_o[...] = kk
    g_o[...] = _mm_hi(_sigmoid(gd), gup_ref[...])
    bonus_o[...] = _mm((r * ksum * rk_ref[...]).astype(BF16), seg) * v


def _rwkv_prep(pb, mu, w0, w_up, a0, a_up, g_up, kk_scale, k_a, r_k, *, t_ctx):
    bsz, t_tot, _ = pb.shape
    tm = TOKEN_TILE
    nb8 = t_tot // 8
    per8 = tm // 8
    bw = B_WIDTH
    full = lambda shape: pl.BlockSpec(shape, lambda b, i: (0,) * len(shape))
    out = jax.ShapeDtypeStruct((bsz, t_tot, bw), F32)
    return pl.pallas_call(
        functools.partial(_rwkv_prep_body, t_ctx=t_ctx, t_tot=t_tot),
        grid=(bsz, t_tot // tm),
        in_specs=[
            pl.BlockSpec((None, tm, B_PROJ), lambda b, i: (b, i, 0)),
            pl.BlockSpec((None, 8, B_PROJ), lambda b, i: (b, jnp.maximum(i * per8 - 1, 0), 0)),
            pl.BlockSpec((None, 8, B_PROJ), lambda b, i: (b, jnp.minimum((i + 1) * per8, nb8 - 1), 0)),
            full((1, B_PROJ)), full((2, bw)), full((2, RWKV_W_RANK, bw)), full((2, bw)),
            full((2, RWKV_A_RANK, bw)), full((RWKV_G_RANK, bw)), full((1, bw)), full((1, bw)), full((1, bw)),
        ],
        out_specs=[pl.BlockSpec((None, tm, bw), lambda b, i: (b, i, 0))] * 11,
        out_shape=[out] * 11,
        compiler_params=_cparams("arbitrary", "arbitrary"),
        name="rwkv_prep",
    )(pb, pb, pb, mu.reshape(1, B_PROJ), w0, w_up, a0, a_up, g_up,
      kk_scale.reshape(1, bw), k_a.reshape(1, bw), r_k.reshape(1, bw))


def _rwkv_scan_body(r_ref, v_ref, kk_ref, kf_ref, kb_ref, bf_ref, bb_ref, lwf_ref, lwb_ref, y_ref,
                    rq_s, y0_s, p_s, z_s, *, t_ctx):
    t_tot = r_ref.shape[0]
    c = RWKV_CHUNK
    rb = RWKV_BLOCK
    n_blk = t_tot // rb
    n_chunk = t_tot // c
    n_chunk_ctx = t_ctx // c
    cpb = rb // c
    n_double = c.bit_length() - 2
    row = _iota((rb, rb), 0)
    col = _iota((rb, rb), 1)
    same = (row // c) == (col // c)
    same_bf = jnp.where(same, 1.0, 0.0).astype(BF16)
    eye = jnp.where(row == col, 1.0, 0.0)
    lane = _iota((1, LANES), 1)
    head_masks = [jnp.where(lane // HEAD_DIM == h, 1.0, 0.0) for h in range(LANES // HEAD_DIM)]
    r128 = _iota((LANES, LANES), 0)
    c128 = _iota((LANES, LANES), 1)
    blockdiag = (r128 // HEAD_DIM) == (c128 // HEAD_DIM)
    diag128 = r128 == c128

    def block(it, carry):
        bis = [it * RWKV_BLOCKS_PER_ITER + j for j in range(RWKV_BLOCKS_PER_ITER)]
        sls = [pl.ds(pl.multiple_of(bi * rb, rb), rb) for bi in bis]
        v_bfs = [v_ref[sl, :].astype(BF16) for sl in sls]
        pre = {}
        for j, sl in enumerate(sls):
            r = r_ref[sl, :]
            kk = kk_ref[sl, :]
            for d in range(2):
                rev = d == 1
                k_ref, b_ref, lw_ref = (kb_ref, bb_ref, lwb_ref) if rev else (kf_ref, bf_ref, lwf_ref)
                incl = same & ((col >= row) if rev else (col <= row))
                strict = same & ((col > row) if rev else (col < row))
                incl_bf = jnp.where(incl, 1.0, 0.0).astype(BF16)
                lw = lw_ref[sl, :]
                k = k_ref[sl, :]
                b = b_ref[sl, :]
                cl = _mm_l3(incl_bf, lw)
                tot = _mm_l3(same_bf, lw)
                w_inv = jnp.exp(-cl)
                w_end = jnp.exp(tot - cl)
                pre[j, d] = dict(incl=incl, strict=strict, tot=tot, kk_d=kk * jnp.exp(cl - lw), r_d=r * jnp.exp(cl),
                                 kb=jnp.concatenate([k * w_inv, b * w_inv], axis=0).astype(BF16),
                                 b_e=(b * w_end).astype(BF16), k_e=(k * w_end).astype(BF16))
        chains = [dict(j=j, d=d, mh=mh, kkm=pre[j, d]['kk_d'] * mh, rm=pre[j, d]['r_d'] * mh)
                  for j in range(len(bis)) for d in range(2) for mh in head_masks]
        for ch in chains:
            p = pre[ch['j'], ch['d']]
            aa = _nt(jnp.concatenate([ch['kkm'], ch['rm']], axis=0).astype(BF16), p['kb'])
            ch['a_kb'] = jnp.where(p['strict'], aa[:rb, :rb], 0.0)
            ch['a_rk'] = jnp.where(p['incl'], aa[rb:, :rb], 0.0)
            ch['a_rb'] = jnp.where(p['incl'], aa[rb:, rb:], 0.0)
            m = -jnp.where(p['strict'], aa[:rb, rb:], 0.0)
            ch['t'] = eye + m
            ch['m'] = m
        for ch in chains:
            m_bf = ch['m'].astype(BF16)
            ch['m'] = _mm(m_bf, m_bf)
        for _ in range(n_double - 1):
            for ch in chains:
                both = _mm(jnp.concatenate([ch['t'], ch['m']], axis=0).astype(BF16), ch['m'].astype(BF16))
                ch['t'] = ch['t'] + both[:rb]
                ch['m'] = both[rb:]
        for ch in chains:
            ch['t'] = ch['t'] + _mm(ch['t'].astype(BF16), ch['m'].astype(BF16))
            ch['av'] = _mm(jnp.concatenate([ch['a_kb'], ch['a_rk']], axis=0).astype(BF16), v_bfs[ch['j']])
        for ch in chains:
            ch['ku'] = _mm(ch['t'].astype(BF16), jnp.concatenate([ch['kkm'], ch['av'][:rb]], axis=1).astype(BF16))
        for ch in chains:
            ch['rb_ku'] = _mm(ch['a_rb'].astype(BF16), ch['ku'].astype(BF16))
        for j, (bi, sl) in enumerate(zip(bis, sls)):
            for d in range(2):
                p = pre[j, d]
                mine = [ch for ch in chains if ch['j'] == j and ch['d'] == d]
                kkt = sum(ch['ku'][:, :LANES] for ch in mine)
                u = sum(ch['ku'][:, LANES:] * ch['mh'] for ch in mine)
                rq_s[d, sl, :] = sum(ch['rm'] - ch['rb_ku'][:, :LANES] for ch in mine)
                y0_s[d, sl, :] = sum((ch['av'][rb:] - ch['rb_ku'][:, LANES:]) * ch['mh'] for ch in mine)
                ktu = jnp.concatenate([kkt, u], axis=1).astype(BF16)
                for n in range(cpb):
                    rows = slice(n * c, (n + 1) * c)
                    wc = jnp.exp(p['tot'][n * c:n * c + 1, :])
                    kub = _tn(ktu[rows], p['b_e'][rows])
                    p_s[d, bi * cpb + n] = jnp.where(diag128, wc, 0.0) - jnp.where(blockdiag, kub[:LANES], 0.0)
                    p_z = jnp.where(blockdiag, _tn(v_bfs[j][rows], p['k_e'][rows]) - kub[LANES:], 0.0)
                    z_s[d, bi * cpb + n] = p_z
        return carry

    lax.fori_loop(0, n_blk // RWKV_BLOCKS_PER_ITER, block, 0)

    y_ref[...] = jnp.zeros_like(y_ref)

    def step(i, carry):
        n_b = jnp.where(i < n_chunk_ctx, n_chunk_ctx - 1 - i, n_chunk - 1 - (i - n_chunk_ctx))
        chunk = (i, n_b)
        s_bf = [s.astype(BF16) for s in carry]
        ys = [_nt(rq_s[d, pl.ds(pl.multiple_of(chunk[d] * c, c), c), :].astype(BF16), s_bf[d]) for d in range(2)]
        states = [_mm(s_bf[d], p_s[d, chunk[d]].astype(BF16)) + z_s[d, chunk[d]] for d in range(2)]
        for d in range(2):
            sl = pl.ds(pl.multiple_of(chunk[d] * c, c), c)
            y_ref[sl, :] += ys[d] + y0_s[d, sl, :]
        return tuple(states)

    zero = jnp.zeros((LANES, LANES), F32)
    lax.fori_loop(0, n_chunk, step, (zero, zero))


def _rwkv_scan(r, v, kk, kf, kb, bf, bb, lwf, lwb, *, t_ctx):
    bsz, t_tot, bw = r.shape
    n_chunk = t_tot // RWKV_CHUNK
    spec = pl.BlockSpec((None, t_tot, LANES), lambda b, h: (b, 0, h))
    return pl.pallas_call(
        functools.partial(_rwkv_scan_body, t_ctx=t_ctx),
        grid=(bsz, bw // LANES),
        in_specs=[spec] * 9,
        out_specs=spec,
        out_shape=jax.ShapeDtypeStruct((bsz, t_tot, bw), F32),
        scratch_shapes=[
            pltpu.VMEM((2, t_tot, LANES), F32),
            pltpu.VMEM((2, t_tot, LANES), F32),
            pltpu.VMEM((2, n_chunk, LANES, LANES), F32),
            pltpu.VMEM((2, n_chunk, LANES, LANES), F32),
        ],
        compiler_params=_cparams("arbitrary", "arbitrary"),
        name="rwkv_scan",
    )(r, v, kk, kf, kb, bf, bb, lwf, lwb)


def _rwkv_readout(y, g, bonus, ln_gain, ln_bias):
    seg = _head_seg(B_WIDTH)
    mean = _mm_x3(y, seg) * (1.0 / HEAD_DIM)
    yc = y - mean
    var = _mm((yc * yc).astype(BF16), seg) * (1.0 / HEAD_DIM)
    return (yc * lax.rsqrt(var + RWKV_LN_EPS) * ln_gain + ln_bias + bonus) * g


def _na_body(q_ref, k_ref, v_ref, qg_ref, kg_ref, cos_ref, sin_ref, bias_ref, o_ref,
             qs, ks, vs, *, t_ctx, need_ctx):
    t_tot = q_ref.shape[0]
    t_lat = t_tot - t_ctx
    rows = t_lat // GRID_W
    win_r = min(NA_WIN_ROWS, rows)
    scale = HEAD_DIM ** -0.5
    seg = _head_seg(LANES)
    lane = _iota((1, LANES), 1)
    head_masks = [jnp.where(lane // HEAD_DIM == h, 1.0, 0.0) for h in range(LANES // HEAD_DIM)]
    half = (lane % HEAD_DIM) < HEAD_DIM // 2
    blk = 256

    def norm_block(i, carry):
        sl = pl.ds(pl.multiple_of(i * blk, blk), blk)
        q = q_ref[sl, :]
        k = k_ref[sl, :]
        q = q * lax.rsqrt(_mm((q * q).astype(BF16), seg) * (1.0 / HEAD_DIM) + EPS) * qg_ref[...]
        k = k * lax.rsqrt(_mm((k * k).astype(BF16), seg) * (1.0 / HEAD_DIM) + EPS) * kg_ref[...]
        cos = cos_ref[sl, :]
        sin = sin_ref[sl, :]

        def rope(t):
            swapped = jnp.where(half, pltpu.roll(t, LANES - HEAD_DIM // 2, 1), pltpu.roll(t, HEAD_DIM // 2, 1))
            return t * cos + swapped * sin

        qs[sl, :] = (rope(q) * scale).astype(BF16)
        ks[sl, :] = rope(k).astype(BF16)
        vs[sl, :] = v_ref[sl, :].astype(BF16)
        return carry

    lax.fori_loop(0, t_tot // blk, norm_block, 0)

    masks_bf = [mh.astype(BF16) for mh in head_masks]

    def by_head(q):
        return jnp.concatenate([q * mb for mb in masks_bf], axis=0)

    def merge_heads(o, n):
        return sum(o[h * n:(h + 1) * n] * mh for h, mh in enumerate(head_masks))

    if need_ctx:
        s = _nt(by_head(qs[0:t_ctx, :]), ks[0:t_ctx, :])
        p = jnp.exp(s - jnp.max(s, axis=-1, keepdims=True))
        o = _mm(p.astype(BF16), vs[0:t_ctx, :]) / jnp.sum(p, axis=-1, keepdims=True)
        o_ref[0:t_ctx, :] = merge_heads(o, t_ctx)
    else:
        o_ref[0:t_ctx, :] = jnp.zeros((t_ctx, LANES), F32)

    def q_rows(it, carry):
        rs = [it * NA_ROW_UNROLL + j for j in range(NA_ROW_UNROLL)]
        starts = [jnp.clip(r - win_r // 2, 0, rows - win_r) for r in rs]
        q_sl = [pl.ds(pl.multiple_of(t_ctx + r * GRID_W, GRID_W), GRID_W) for r in rs]
        k_sl = [pl.ds(pl.multiple_of(t_ctx + r0 * GRID_W, GRID_W), win_r * GRID_W) for r0 in starts]
        qs_ = [by_head(qs[sl, :]) for sl in q_sl]
        s_win = [_nt(q, ks[sl, :]) + bias_ref[r - r0] for q, sl, r, r0 in zip(qs_, k_sl, rs, starts)]
        s_ctx = [_nt(q, ks[0:t_ctx, :]) for q in qs_]
        ms = [jnp.maximum(jnp.max(a, axis=-1, keepdims=True), jnp.max(b, axis=-1, keepdims=True))
              for a, b in zip(s_win, s_ctx)]
        p_win = [jnp.exp(a - m) for a, m in zip(s_win, ms)]
        p_ctx = [jnp.exp(b - m) for b, m in zip(s_ctx, ms)]
        den = [jnp.sum(a, axis=-1, keepdims=True) + jnp.sum(b, axis=-1, keepdims=True) for a, b in zip(p_win, p_ctx)]
        o_win = [_mm(a.astype(BF16), vs[sl, :]) for a, sl in zip(p_win, k_sl)]
        o_ctx = [_mm(b.astype(BF16), vs[0:t_ctx, :]) for b in p_ctx]
        for sl, a, b, d in zip(q_sl, o_win, o_ctx, den):
            o_ref[sl, :] = merge_heads((a + b) / d, GRID_W)
        return carry

    lax.fori_loop(0, rows // NA_ROW_UNROLL, q_rows, 0)


def _na_tables(t_ctx, t_lat):
    quarter = HEAD_DIM // 4
    pos = np.arange(t_lat)
    inv = ROPE_THETA ** (-np.arange(quarter, dtype=np.float32) / quarter)
    pos_r = (pos // GRID_W).astype(np.float32)
    pos_c = (pos % GRID_W).astype(np.float32)
    return pos_r, pos_c, inv


def _na_bias_table(rpb, rows):
    win_r = min(NA_WIN_ROWS, rows)
    c = np.arange(GRID_W)
    w_start = np.clip(c - NA_WIN_COLS // 2, 0, GRID_W - NA_WIN_COLS)
    kc = np.arange(GRID_W)
    in_win = (kc[None, :] >= w_start[:, None]) & (kc[None, :] < w_start[:, None] + NA_WIN_COLS)
    col_idx = np.clip(kc[None, :] - c[:, None] + NA_WIN_COLS - 1, 0, 2 * NA_WIN_COLS - 2)
    n_col = 2 * NA_WIN_COLS - 1
    onehot = jnp.asarray(col_idx[None] == np.arange(n_col)[:, None, None], F32)
    band = jnp.einsum('...hrj,jck->...hrck', rpb.astype(F32), onehot, precision=_HI)
    band = jnp.where(in_win, band, NEG_INF)
    per_off = [band[..., NA_WIN_ROWS - 1 - off:NA_WIN_ROWS - 1 - off + win_r, :, :] for off in range(win_r)]
    g = jnp.stack(per_off, axis=-4)
    g = jnp.swapaxes(g, -3, -2)
    return g.reshape(g.shape[:-2] + (win_r * GRID_W,))


def _na(pc, q_gain, k_gain, bias, layer, *, t_ctx, need_ctx):
    bsz, t_tot, _ = pc.shape
    t_lat = t_tot - t_ctx
    rows = t_lat // GRID_W
    win_r = min(NA_WIN_ROWS, rows)
    n_hp = C_WIDTH // LANES
    hpl = LANES // HEAD_DIM
    pos_r, pos_c, inv = _na_tables(t_ctx, t_lat)
    ang = np.concatenate([pos_r[:, None] * inv, pos_c[:, None] * inv], axis=-1)
    ang = np.concatenate([np.zeros((t_ctx, HEAD_DIM // 2), np.float32), ang], axis=0)
    cos = np.cos(ang)
    sin = np.sin(ang)
    cos_t = jnp.asarray(np.tile(np.concatenate([cos, cos], axis=-1), (1, hpl)), F32)
    sin_t = jnp.asarray(np.tile(np.concatenate([-sin, sin], axis=-1), (1, hpl)), F32)
    bias = bias.reshape(bias.shape[0], n_hp, hpl, win_r, GRID_W, win_r * GRID_W)
    bias = jnp.swapaxes(bias, 2, 3).reshape(bias.shape[0], n_hp, win_r, hpl * GRID_W, win_r * GRID_W)

    def sec(s):
        return pl.BlockSpec((None, t_tot, LANES), lambda b, h, s=s: (b, 0, s * n_hp + h))

    gain = lambda g: jnp.tile(g.reshape(1, HEAD_DIM), (1, hpl))
    return pl.pallas_call(
        functools.partial(_na_body, t_ctx=t_ctx, need_ctx=need_ctx),
        grid=(bsz, n_hp),
        in_specs=[sec(0), sec(1), sec(2),
                  pl.BlockSpec((1, LANES), lambda b, h: (0, 0)),
                  pl.BlockSpec((1, LANES), lambda b, h: (0, 0)),
                  pl.BlockSpec((t_tot, LANES), lambda b, h: (0, 0)),
                  pl.BlockSpec((t_tot, LANES), lambda b, h: (0, 0)),
                  pl.BlockSpec((None, None, win_r, hpl * GRID_W, win_r * GRID_W),
                               lambda b, h: (layer, h, 0, 0, 0))],
        out_specs=pl.BlockSpec((None, t_tot, LANES), lambda b, h: (b, 0, h)),
        out_shape=jax.ShapeDtypeStruct((bsz, t_tot, C_WIDTH), F32),
        scratch_shapes=[pltpu.VMEM((t_tot, LANES), BF16)] * 3,
        compiler_params=_cparams("arbitrary", "arbitrary"),
        name="na",
    )(pc, pc, pc, gain(q_gain), gain(k_gain), cos_t, sin_t, bias)


def _outproj_body(x_ref, a_ref, by_ref, bg_ref, bb_ref, lng_ref, lnb_ref, c_ref, m_ref, g_ref, w_ref, wr_ref, br_ref,
                  xo_ref, h_ref, route_ref):
    b_mix = _rwkv_readout(by_ref[...], bg_ref[...], bb_ref[...], lng_ref[...], lnb_ref[...])
    mix = (_mm(a_ref[...].astype(BF16), w_ref[0:A_WIDTH, :])
           + _mm(b_mix.astype(BF16), w_ref[A_WIDTH:A_WIDTH + B_WIDTH, :])
           + _mm(c_ref[...].astype(BF16), w_ref[A_WIDTH + B_WIDTH:, :]))
    x = x_ref[...] + m_ref[2:3, :] * mix
    xo_ref[...] = x
    h = x * lax.rsqrt(jnp.mean(x * x, axis=-1, keepdims=True) + EPS) * g_ref[...]
    h = h * (1.0 + m_ref[4:5, :]) + m_ref[3:4, :]
    h_ref[...] = h
    logits = _mm_3pass(h, wr_ref[...]) + br_ref[...]
    lane = _iota(logits.shape, 1).astype(F32)
    big = float(LANES)
    is_g = (lane >= N_EXPERTS) & (lane < N_EXPERTS + N_GROUPS)
    gl = jnp.where(is_g, logits, -jnp.inf)
    gmax = jnp.max(gl, axis=-1, keepdims=True)
    g_w = 1.0 / jnp.sum(jnp.exp(gl - gmax), axis=-1, keepdims=True)
    g_sel = jnp.min(jnp.where(gl == gmax, lane, big), axis=-1, keepdims=True) - N_EXPERTS
    lo = g_sel * EXPERTS_PER_GROUP
    el = jnp.where((lane >= lo) & (lane < lo + EXPERTS_PER_GROUP), logits, -jnp.inf)
    m1 = jnp.max(el, axis=-1, keepdims=True)
    i1 = jnp.min(jnp.where(el == m1, lane, big), axis=-1, keepdims=True)
    el2 = jnp.where(lane == i1, -jnp.inf, el)
    m2 = jnp.max(el2, axis=-1, keepdims=True)
    i2 = jnp.min(jnp.where(el2 == m2, lane, big), axis=-1, keepdims=True)
    e2 = jnp.exp(m2 - m1)
    w1 = g_w / (1.0 + e2)
    route_ref[...] = (jnp.where(lane == 0.0, i1, 0.0) + jnp.where(lane == 1.0, i2, 0.0)
                      + jnp.where(lane == 2.0, w1, 0.0) + jnp.where(lane == 3.0, w1 * e2, 0.0))


def _outproj(x, a, b_scan, b_gate, b_bonus, ln_gain, ln_bias, c, mod_l, gain2, w_bf, w_router, b_router, *,
             bps, ctx_blocks, ctx_row):
    n = x.shape[0]
    tm = TOKEN_TILE
    row = functools.partial(_mod_row, blocks_per_seq=bps, ctx_blocks=ctx_blocks, ctx_row=ctx_row)
    tok = lambda w: pl.BlockSpec((tm, w), lambda i: (i, 0))
    full = lambda shape: pl.BlockSpec(shape, lambda i: (0,) * len(shape))
    return pl.pallas_call(
        _outproj_body,
        grid=(n // tm,),
        in_specs=[tok(D_MODEL), tok(A_WIDTH), tok(B_WIDTH), tok(B_WIDTH), tok(B_WIDTH),
                  full((1, B_WIDTH)), full((1, B_WIDTH)), tok(C_WIDTH),
                  pl.BlockSpec((None, 6, D_MODEL), lambda i: (row(i), 0, 0)),
                  full((1, D_MODEL)), full((D_MODEL, D_MODEL)), full((D_MODEL, LANES)), full((1, LANES))],
        out_specs=[tok(D_MODEL), tok(D_MODEL), tok(LANES)],
        out_shape=[jax.ShapeDtypeStruct((n, D_MODEL), F32),
                   jax.ShapeDtypeStruct((n, D_MODEL), F32),
                   jax.ShapeDtypeStruct((n, LANES), F32)],
        compiler_params=_cparams("arbitrary"),
        name="outproj",
    )(x, a, b_scan, b_gate, b_bonus, ln_gain.reshape(1, B_WIDTH), ln_bias.reshape(1, B_WIDTH), c, mod_l,
      gain2.reshape(1, D_MODEL), w_bf, w_router, b_router)


def _route_tables(route, bsz, t_tot):
    n_pairs = TOP_K * t_tot
    ids = route[:, 0:TOP_K].astype(jnp.int32).reshape(bsz, t_tot, TOP_K)
    wts = route[:, TOP_K:2 * TOP_K].reshape(bsz, t_tot, TOP_K)
    ids = jnp.swapaxes(ids, 1, 2).reshape(bsz, n_pairs)
    wts = jnp.swapaxes(wts, 1, 2).reshape(bsz, n_pairs)
    key = ids * n_pairs + jnp.arange(n_pairs, dtype=jnp.int32)
    key, wts = lax.sort((key, wts), dimension=1, num_keys=1)
    tok = (key % n_pairs) % t_tot
    count = jnp.sum((ids[:, :, None] == jnp.arange(N_EXPERTS, dtype=jnp.int32)).astype(jnp.int32), axis=1)
    start = jnp.cumsum(count, axis=1) - count
    pad = ((0, 0), (0, MOE_TILE))
    return (jnp.pad(tok, pad).reshape(-1), jnp.pad(wts, pad).reshape(-1), start.reshape(-1), count.reshape(-1))


def _moe_body(tok_ref, w_ref, start_ref, count_ref, h_ref, wg_ref, wu_ref, wd_ref, y_ref, hbuf, obuf):
    c = pl.program_id(0)
    step = pl.program_id(1)
    t_tot = h_ref.shape[0]
    p_len = TOP_K * t_tot + MOE_TILE

    @pl.when(jnp.logical_and(c == 0, step == 0))
    def _():
        hbuf[...] = jnp.zeros_like(hbuf)

    @pl.when(step == 0)
    def _():
        def zero(i, carry):
            y_ref[pl.ds(pl.multiple_of(i * TOKEN_TILE, TOKEN_TILE), TOKEN_TILE), :] = jnp.zeros(
                (TOKEN_TILE, D_MODEL), F32)
            return carry
        lax.fori_loop(0, t_tot // TOKEN_TILE, zero, 0)

    for k in range(MOE_EXPERTS_PER_STEP):
        e = step * MOE_EXPERTS_PER_STEP + k
        _moe_expert(tok_ref, w_ref, h_ref, wg_ref.at[k], wu_ref.at[k], wd_ref.at[k], y_ref, hbuf, obuf,
                    count=count_ref[c * N_EXPERTS + e], pair_base=c * p_len + start_ref[c * N_EXPERTS + e])


def _moe_expert(tok_ref, w_ref, h_ref, wg_ref, wu_ref, wd_ref, y_ref, hbuf, obuf, *, count, pair_base):
    tm = MOE_TILE

    def tile(t, carry):
        n_valid = jnp.minimum(tm, count - t * tm)
        pair0 = pair_base + t * tm

        def gather(g, carry):
            for j in range(8):
                tok = tok_ref[pair0 + g * 8 + j]
                hbuf[g, pl.ds(j, 1), :] = h_ref[pl.ds(tok, 1), :]
            return carry
        lax.fori_loop(0, (n_valid + 7) // 8, gather, 0)

        h = hbuf[...].reshape(tm, D_MODEL).astype(BF16)
        gate = _mm(h, wg_ref[...])
        up = _mm(h, wu_ref[...])
        hid = gate * _sigmoid(gate) * up
        obuf[...] = _mm(hid.astype(BF16), wd_ref[...]).reshape(tm // 8, 8, D_MODEL)

        def add_row(r):
            tok = tok_ref[pair0 + r]
            y_ref[pl.ds(tok, 1), :] += w_ref[pair0 + r] * obuf[r // 8, pl.ds(r % 8, 1), :]

        def scatter(g, carry):
            toks = [tok_ref[pair0 + g * 8 + j] for j in range(8)]
            new = [y_ref[pl.ds(toks[j], 1), :] + w_ref[pair0 + g * 8 + j] * obuf[g, pl.ds(j, 1), :]
                   for j in range(8)]
            for j in range(8):
                y_ref[pl.ds(toks[j], 1), :] = new[j]
            return carry
        lax.fori_loop(0, n_valid // 8, scatter, 0)

        def scatter_tail(r, carry):
            add_row(r)
            return carry
        lax.fori_loop(n_valid // 8 * 8, n_valid, scatter_tail, 0)
        return carry

    lax.fori_loop(0, (count + tm - 1) // tm, tile, 0)


def _moe(h, route, wg, wu, wd, layer, *, bsz, t_tot):
    tok, w, start, count = _route_tables(route, bsz, t_tot)
    grid_spec = pltpu.PrefetchScalarGridSpec(
        num_scalar_prefetch=4,
        grid=(bsz, N_EXPERTS // MOE_EXPERTS_PER_STEP),
        in_specs=[
            pl.BlockSpec((None, t_tot, D_MODEL), lambda c, e, *_: (c, 0, 0)),
            pl.BlockSpec((None, MOE_EXPERTS_PER_STEP, D_MODEL, D_EXPERT), lambda c, e, *_: (layer, e, 0, 0)),
            pl.BlockSpec((None, MOE_EXPERTS_PER_STEP, D_MODEL, D_EXPERT), lambda c, e, *_: (layer, e, 0, 0)),
            pl.BlockSpec((None, MOE_EXPERTS_PER_STEP, D_EXPERT, D_MODEL), lambda c, e, *_: (layer, e, 0, 0)),
        ],
        out_specs=pl.BlockSpec((None, t_tot, D_MODEL), lambda c, e, *_: (c, 0, 0)),
        scratch_shapes=[pltpu.VMEM((MOE_TILE // 8, 8, D_MODEL), F32)] * 2,
    )
    y = pl.pallas_call(
        _moe_body,
        grid_spec=grid_spec,
        out_shape=jax.ShapeDtypeStruct((bsz, t_tot, D_MODEL), F32),
        compiler_params=_cparams("arbitrary", "arbitrary"),
        name="moe",
    )(tok, w, start, count, h.reshape(bsz, t_tot, D_MODEL), wg, wu, wd)
    return y.reshape(bsz * t_tot, D_MODEL)


def _residual_body(x_ref, y_ref, m_ref, o_ref):
    o_ref[...] = (x_ref[...] + m_ref[5:6, :] * y_ref[...]).astype(o_ref.dtype)


def _final_residual(x, y, mod_l, dtype, *, bsz, t_ctx, t_tot):
    tm = TOKEN_TILE
    tok = pl.BlockSpec((None, tm, D_MODEL), lambda b, i: (b, t_ctx // tm + i, 0))
    return pl.pallas_call(
        _residual_body,
        grid=(bsz, (t_tot - t_ctx) // tm),
        in_specs=[tok, tok, pl.BlockSpec((None, 6, D_MODEL), lambda b, i: (b, 0, 0))],
        out_specs=pl.BlockSpec((None, tm, D_MODEL), lambda b, i: (b, i, 0)),
        out_shape=jax.ShapeDtypeStruct((bsz, t_tot - t_ctx, D_MODEL), dtype),
        compiler_params=_cparams("arbitrary", "arbitrary"),
        name="residual",
    )(x.reshape(bsz, t_tot, D_MODEL), y.reshape(bsz, t_tot, D_MODEL), mod_l)


def kernel(x, c, ctx, c_ctx, norm1_gain, norm2_gain, w_ada, b_ada, w_in, w_out, hgrn_lb_logits, hgrn_gn_gain, rwkv_mu, rwkv_w0, rwkv_w_up, rwkv_a0, rwkv_a_up, rwkv_g_up, rwkv_kk_scale, rwkv_k_a, rwkv_r_k, rwkv_ln_gain, rwkv_ln_bias, na_q_gain, na_k_gain, na_rpb, w_router_group, b_router_group, w_router_expert, b_router_expert, w_exp_gate, w_exp_up, w_exp_down):
    bsz, t_lat, _ = x.shape
    t_ctx = ctx.shape[1]
    t_tot = t_ctx + t_lat
    n = bsz * t_tot
    assert t_ctx % TOKEN_TILE == 0 and t_lat % TOKEN_TILE == 0 and bsz < 16
    bps = t_tot // TOKEN_TILE
    tile_kw = dict(bps=bps, ctx_blocks=t_ctx // TOKEN_TILE, ctx_row=bsz)

    lb_p = jax.nn.softmax(hgrn_lb_logits.astype(F32), axis=1)
    lower_bounds = jnp.cumsum(lb_p, axis=1) - lb_p[:, :1]

    cc = jnp.zeros((16, D_MODEL), F32).at[:bsz].set(c.astype(F32)).at[bsz].set(c_ctx.astype(F32))
    mod = _ada_mod(cc, w_ada, b_ada).reshape(DEPTH, 16, 6, D_MODEL)

    w_in_bf = w_in.astype(BF16)
    w_out_bf = w_out.astype(BF16)
    wg_bf = w_exp_gate.astype(BF16).reshape(DEPTH, N_EXPERTS, D_MODEL, D_EXPERT)
    wu_bf = w_exp_up.astype(BF16).reshape(DEPTH, N_EXPERTS, D_MODEL, D_EXPERT)
    wd_bf = w_exp_down.astype(BF16).reshape(DEPTH, N_EXPERTS, D_EXPERT, D_MODEL)
    pad = LANES - N_EXPERTS - N_GROUPS
    w_router = jnp.concatenate([w_router_expert, w_router_group,
                                jnp.zeros((DEPTH, D_MODEL, pad), F32)], axis=-1)
    b_router = jnp.concatenate([b_router_expert, b_router_group, jnp.zeros((DEPTH, pad), F32)], axis=-1)

    na_bias = _na_bias_table(na_rpb, t_lat // GRID_W)
    xs = jnp.concatenate([ctx.astype(F32), x.astype(F32)], axis=1).reshape(n, D_MODEL)
    y_moe = None
    for layer in range(DEPTH):
        last = layer == DEPTH - 1
        if layer == 0:
            pa, pb, pc = _inproj(xs, mod[layer], norm1_gain[layer], w_in_bf[layer], **tile_kw)
        else:
            xs, pa, pb, pc = _inproj(xs, mod[layer], norm1_gain[layer], w_in_bf[layer],
                                     (y_moe, mod[layer - 1]), **tile_kw)
        a_mix = _hgrn(pa.reshape(bsz, t_tot, A_PROJ), lower_bounds[:, layer], hgrn_gn_gain[layer], t_ctx=t_ctx)
        prep = _rwkv_prep(pb.reshape(bsz, t_tot, B_PROJ), rwkv_mu[layer], rwkv_w0[layer], rwkv_w_up[layer],
                          rwkv_a0[layer], rwkv_a_up[layer], rwkv_g_up[layer], rwkv_kk_scale[layer],
                          rwkv_k_a[layer], rwkv_r_k[layer], t_ctx=t_ctx)
        y = _rwkv_scan(*prep[:9], t_ctx=t_ctx)
        c_mix = _na(pc.reshape(bsz, t_tot, C_PROJ), na_q_gain[layer], na_k_gain[layer], na_bias, layer,
                    t_ctx=t_ctx, need_ctx=not last)
        xs, h2, route = _outproj(xs, a_mix.reshape(n, A_WIDTH), y.reshape(n, B_WIDTH), prep[9].reshape(n, B_WIDTH),
                                 prep[10].reshape(n, B_WIDTH), rwkv_ln_gain[layer], rwkv_ln_bias[layer],
                                 c_mix.reshape(n, C_WIDTH), mod[layer], norm2_gain[layer], w_out_bf[layer],
                                 w_router[layer], b_router[layer].reshape(1, LANES), **tile_kw)
        y_moe = _moe(h2, route, wg_bf, wu_bf, wd_bf, layer, bsz=bsz, t_tot=t_tot)
    return _final_residual(xs, y_moe, mod[DEPTH - 1], x.dtype, bsz=bsz, t_ctx=t_ctx, t_tot=t_tot)
```

```python
import functools

import jax
import jax.numpy as jnp
import numpy as np
from jax import lax
from jax.experimental import pallas as pl
from jax.experimental.pallas import tpu as pltpu

F32 = jnp.float32
BF16 = jnp.bfloat16

D_MODEL = 1024
DEPTH = 4
GRID_W = 64
HEAD_DIM = 64
A_HEADS = 4
B_HEADS = 4
C_HEADS = 8
A_WIDTH = A_HEADS * HEAD_DIM
B_WIDTH = B_HEADS * HEAD_DIM
C_WIDTH = C_HEADS * HEAD_DIM
HGRN_CHUNK = 16
HGRN_F_FLOOR = 1e-20
RWKV_W_RANK = 64
RWKV_A_RANK = 64
RWKV_G_RANK = 128
RWKV_LN_EPS = 64e-5
NA_WIN_ROWS = 8
NA_WIN_COLS = 16
ROPE_THETA = 10000.0
N_GROUPS = 4
EXPERTS_PER_GROUP = 8
N_EXPERTS = N_GROUPS * EXPERTS_PER_GROUP
TOP_K = 2
D_EXPERT = 512
EPS = 1e-6
LOG2E = 1.4426950408889634
NEG_INF = -1e30
A_PROJ = 5 * A_WIDTH
B_PROJ = 3 * B_WIDTH + 2 * RWKV_W_RANK + 2 * RWKV_A_RANK + RWKV_G_RANK
C_PROJ = 3 * C_WIDTH
P_TOTAL = A_PROJ + B_PROJ + C_PROJ
B_TAIL = B_PROJ - 3 * B_WIDTH

LANES = 128
PROJ_TILES = 2
TOKEN_TILE = 256
RWKV_CHUNK = 64
RWKV_BLOCKS_PER_ITER = 2
RWKV_BLOCK = 128
HGRN_BLOCK = 128
NA_ROW_UNROLL = 8
HGRN_SCAN_UNROLL = 8
MOE_EXPERTS_PER_STEP = 2
MOE_TILE = 192
VMEM_LIMIT = 56 * 1024 * 1024

_HI = lax.Precision.HIGHEST


def _cparams(*sem):
    return pltpu.CompilerParams(dimension_semantics=sem, vmem_limit_bytes=VMEM_LIMIT)


def _mm(a, b):
    return jnp.dot(a, b, preferred_element_type=F32)


def _mm_hi(a, b):
    return jnp.dot(a, b, preferred_element_type=F32, precision=_HI)


def _nt(a, b):
    return lax.dot_general(a, b, (((1,), (1,)), ((), ())), preferred_element_type=F32)


def _tn(a, b):
    return lax.dot_general(a, b, (((0,), (0,)), ((), ())), preferred_element_type=F32)


def _split3(a):
    a1 = a.astype(BF16)
    r1 = a - a1.astype(F32)
    a2 = r1.astype(BF16)
    a3 = (r1 - a2.astype(F32)).astype(BF16)
    return a1, a2, a3


def _mm_x3(a, m):
    a1, a2, a3 = _split3(a)
    return _mm(a1, m) + _mm(a2, m) + _mm(a3, m)


def _mm_l3(m, a):
    a1, a2, a3 = _split3(a)
    return _mm(m, a1) + _mm(m, a2) + _mm(m, a3)


def _mm_3pass(a, b):
    a1 = a.astype(BF16)
    a2 = (a - a1.astype(F32)).astype(BF16)
    b1 = b.astype(BF16)
    b2 = (b - b1.astype(F32)).astype(BF16)
    return _mm(a1, b1) + (_mm(a1, b2) + _mm(a2, b1))


def _sigmoid(x):
    return 1.0 / (1.0 + jnp.exp(-x))


def _iota(shape, dim):
    return lax.broadcasted_iota(jnp.int32, shape, dim)


def _head_seg(n):
    return jnp.where(_iota((n, n), 0) // HEAD_DIM == _iota((n, n), 1) // HEAD_DIM, 1.0, 0.0).astype(BF16)


def _ada_body(c_ref, w_ref, b_ref, o_ref):
    c = c_ref[...]
    o_ref[...] = _mm_hi(c * _sigmoid(c), w_ref[...]) + b_ref[...]


def _ada_mod(cc, w_ada, b_ada):
    rows = cc.shape[0]
    tn = 1536
    return pl.pallas_call(
        _ada_body,
        grid=(DEPTH, 6 * D_MODEL // tn),
        in_specs=[
            pl.BlockSpec((rows, D_MODEL), lambda l, j: (0, 0)),
            pl.BlockSpec((None, D_MODEL, tn), lambda l, j: (l, 0, j)),
            pl.BlockSpec((None, 1, tn), lambda l, j: (l, 0, j)),
        ],
        out_specs=pl.BlockSpec((None, rows, tn), lambda l, j: (l, 0, j)),
        out_shape=jax.ShapeDtypeStruct((DEPTH, rows, 6 * D_MODEL), F32),
        compiler_params=_cparams("arbitrary", "arbitrary"),
        name="ada_mod",
    )(cc, w_ada, b_ada.reshape(DEPTH, 1, 6 * D_MODEL))


def _mod_row(i, blocks_per_seq, ctx_blocks, ctx_row):
    return jnp.where(i % blocks_per_seq < ctx_blocks, ctx_row, i // blocks_per_seq)


def _inproj_body(*refs, with_moe, per):
    tt = TOKEN_TILE
    if with_moe:
        x_ref, y_ref = refs[:2]
        mp_refs, m_refs = refs[2:2 + per], refs[2 + per:2 + 2 * per]
        g_ref, w_ref, xo_ref, pa_ref, pb_ref, pc_ref = refs[2 + 2 * per:]
    else:
        x_ref = refs[0]
        m_refs = refs[1:1 + per]
        g_ref, w_ref, pa_ref, pb_ref, pc_ref = refs[1 + per:]
    hs = []
    for j, m_ref in enumerate(m_refs):
        rows = slice(j * tt, (j + 1) * tt)
        x = x_ref[rows, :]
        if with_moe:
            x = x + mp_refs[j][5:6, :] * y_ref[rows, :]
            xo_ref[rows, :] = x
        h = x * lax.rsqrt(jnp.mean(x * x, axis=-1, keepdims=True) + EPS) * g_ref[...]
        hs.append((h * (1.0 + m_ref[1:2, :]) + m_ref[0:1, :]).astype(BF16))
    h = jnp.concatenate(hs, axis=0)
    pa_ref[...] = _mm(h, w_ref[:, :A_PROJ])
    pb_ref[...] = _mm(h, w_ref[:, A_PROJ:A_PROJ + B_PROJ])
    pc_ref[...] = _mm(h, w_ref[:, A_PROJ + B_PROJ:])


def _mod_specs(per, bps, ctx_blocks, ctx_row):
    row = functools.partial(_mod_row, blocks_per_seq=bps, ctx_blocks=ctx_blocks, ctx_row=ctx_row)
    return [pl.BlockSpec((None, 6, D_MODEL), lambda i, j=j: (row(i * per + j), 0, 0)) for j in range(per)]


def _inproj(x, mod_l, gain, w_bf, moe=None, *, bps, ctx_blocks, ctx_row):
    n = x.shape[0]
    per = PROJ_TILES
    tm = per * TOKEN_TILE
    tok = lambda w: pl.BlockSpec((tm, w), lambda i: (i, 0))
    mods = _mod_specs(per, bps, ctx_blocks, ctx_row)
    with_moe = moe is not None
    ins = (x, moe[0], *([moe[1]] * per)) if with_moe else (x,)
    return pl.pallas_call(
        functools.partial(_inproj_body, with_moe=with_moe, per=per),
        grid=(n // tm,),
        in_specs=[tok(D_MODEL)] + ([tok(D_MODEL)] + mods if with_moe else []) + mods + [
            pl.BlockSpec((1, D_MODEL), lambda i: (0, 0)),
            pl.BlockSpec((D_MODEL, P_TOTAL), lambda i: (0, 0)),
        ],
        out_specs=([tok(D_MODEL)] if with_moe else []) + [tok(A_PROJ), tok(B_PROJ), tok(C_PROJ)],
        out_shape=([jax.ShapeDtypeStruct((n, D_MODEL), F32)] if with_moe else []) + [
            jax.ShapeDtypeStruct((n, A_PROJ), F32),
            jax.ShapeDtypeStruct((n, B_PROJ), F32),
            jax.ShapeDtypeStruct((n, C_PROJ), F32),
        ],
        compiler_params=_cparams("arbitrary"),
        name="inproj",
    )(*ins, *([mod_l] * per), gain.reshape(1, D_MODEL), w_bf)


def _hgrn_body(q_ref, ff_ref, fb_ref, i_ref, g_ref, lb_ref, gn_ref, o_ref,
               qin_s, kout_s, cum_s, acc_s, kpad_s, cpad_s, vpad_s, *, t_ctx):
    t_tot = q_ref.shape[0]
    c = HGRN_CHUNK
    rb = HGRN_BLOCK
    n_blk = t_tot // rb
    n_chunk = t_tot // c
    n_chunk_ctx = t_ctx // c
    row = _iota((rb, rb), 0)
    col = _iota((rb, rb), 1)
    same = (row // c) == (col // c)
    same_bf = jnp.where(same, 1.0, 0.0).astype(BF16)
    seg = _head_seg(LANES)
    pos = _iota((rb, LANES), 0) % c
    blockdiag = _iota((LANES, LANES), 0) // HEAD_DIM == _iota((LANES, LANES), 1) // HEAD_DIM

    pad = HGRN_CHUNK
    zpad = jnp.zeros((pad, LANES), F32)
    for d in range(2):
        for ref in (kpad_s, cpad_s):
            ref[d, 0:pad, :] = zpad
            ref[d, pad + t_tot:, :] = zpad
    vpad_s[0:pad, :] = zpad
    vpad_s[pad + t_tot:, :] = zpad
    tris = [jnp.where(same & ((col >= row) if d == 1 else (col <= row)), 1.0, 0.0).astype(BF16) for d in range(2)]

    def gates(b, carry):
        sl = pl.ds(pl.multiple_of(b * rb, rb), rb)
        slp = pl.ds(pl.multiple_of(b * rb, rb) + pad, rb)
        q = q_ref[sl, :]
        vpad_s[slp, :] = i_ref[sl, :]
        for d, f_ref in enumerate((ff_ref, fb_ref)):
            lb = lb_ref[d:d + 1, :]
            fpre = f_ref[sl, :]
            f = lb + (1.0 - lb) * _sigmoid(fpre)
            logf = jnp.log(jnp.maximum(f, HGRN_F_FLOOR))
            k = (1.0 - lb) * _sigmoid(-fpre)
            cum = _mm_l3(tris[d], logf)
            tot = _mm_l3(same_bf, logf)
            kpad_s[d, slp, :] = k
            cpad_s[d, slp, :] = cum * LOG2E
            qin_s[d, sl, :] = q * jnp.exp(cum)
            kout_s[d, sl, :] = k * jnp.exp(tot - cum)
            cum_s[d, sl, :] = tot
        return carry

    lax.fori_loop(0, n_blk, gates, 0)

    def intra(b, carry):
        r0 = pl.multiple_of(b * rb, rb)
        sl = pl.ds(r0, rb)
        q = q_ref[sl, :]
        cums = [cpad_s[d, pl.ds(r0 + pad, rb), :] for d in range(2)]
        o = jnp.zeros((rb, LANES), F32)
        for j in range(c):
            for d in range(2):
                off = r0 + pad + (j if d == 1 else -j)
                valid = (pos <= c - 1 - j) if d == 1 else (pos >= j)
                ks = kpad_s[d, pl.ds(off, rb), :]
                cs = cpad_s[d, pl.ds(off, rb), :]
                vs = vpad_s[pl.ds(off, rb), :]
                prod = jnp.where(valid, q * ks * jnp.exp2(cums[d] - cs), 0.0)
                o = o + _mm(prod.astype(BF16), seg) * vs
        acc_s[sl, :] = o
        return carry

    lax.fori_loop(0, n_blk, intra, 0)

    def step(it, carry):
        idx = [it * HGRN_SCAN_UNROLL + j for j in range(HGRN_SCAN_UNROLL)]
        order = [(i, jnp.where(i < n_chunk_ctx, n_chunk_ctx - 1 - i, n_chunk - 1 - (i - n_chunk_ctx))) for i in idx]
        sls = [[pl.ds(pl.multiple_of(n[d] * c, c), c) for d in range(2)] for n in order]
        kvs = [[jnp.where(blockdiag, _tn(i_ref[sl[d], :].astype(BF16), kout_s[d, sl[d], :].astype(BF16)), 0.0)
                for d in range(2)] for sl in sls]
        decs = [[jnp.exp(cum_s[d, pl.ds(pl.multiple_of(n[d] * c, c), 1), :]) for d in range(2)] for n in order]
        states = list(carry)
        for j in range(HGRN_SCAN_UNROLL):
            for d in range(2):
                acc_s[sls[j][d], :] += _nt(qin_s[d, sls[j][d], :].astype(BF16), states[d].astype(BF16))
                states[d] = states[d] * decs[j][d] + kvs[j][d]
        return tuple(states)

    zero = jnp.zeros((LANES, LANES), F32)
    lax.fori_loop(0, n_chunk // HGRN_SCAN_UNROLL, step, (zero, zero))

    def readout(b, carry):
        sl = pl.ds(pl.multiple_of(b * rb, rb), rb)
        o = acc_s[sl, :]
        ms = _mm_x3(o * o, seg) * (1.0 / HEAD_DIM)
        g = g_ref[sl, :]
        o_ref[sl, :] = o * lax.rsqrt(ms + EPS) * gn_ref[...] * (g * _sigmoid(g))
        return carry

    lax.fori_loop(0, n_blk, readout, 0)


def _hgrn(pa, lb, gn_gain, *, t_ctx):
    bsz, t_tot, _ = pa.shape
    n_hp = A_WIDTH // LANES

    def sec(s):
        return pl.BlockSpec((None, t_tot, LANES), lambda b, h, s=s: (b, 0, s * n_hp + h))

    return pl.pallas_call(
        functools.partial(_hgrn_body, t_ctx=t_ctx),
        grid=(bsz, n_hp),
        in_specs=[sec(0), sec(1), sec(2), sec(3), sec(4),
                  pl.BlockSpec((2, LANES), lambda b, h: (0, h)),
                  pl.BlockSpec((1, LANES), lambda b, h: (0, h))],
        out_specs=pl.BlockSpec((None, t_tot, LANES), lambda b, h: (b, 0, h)),
        out_shape=jax.ShapeDtypeStruct((bsz, t_tot, A_WIDTH), F32),
        scratch_shapes=[
            pltpu.VMEM((2, t_tot, LANES), F32),
            pltpu.VMEM((2, t_tot, LANES), F32),
            pltpu.VMEM((2, t_tot, LANES), F32),
            pltpu.VMEM((t_tot, LANES), F32),
            pltpu.VMEM((2, t_tot + 2 * HGRN_CHUNK, LANES), F32),
            pltpu.VMEM((2, t_tot + 2 * HGRN_CHUNK, LANES), F32),
            pltpu.VMEM((t_tot + 2 * HGRN_CHUNK, LANES), F32),
        ],
        compiler_params=_cparams("arbitrary", "arbitrary"),
        name="hgrn",
    )(pa, pa, pa, pa, pa, lb, gn_gain.reshape(1, A_WIDTH))


def _rwkv_prep_body(x_ref, xp_ref, xn_ref, mu_ref, w0_ref, wup_ref, a0_ref, aup_ref, gup_ref,
                    kks_ref, ka_ref, rk_ref,
                    r_o, v_o, kk_o, kf_o, kb_o, bf_o, bb_o, lwf_o, lwb_o, g_o, bonus_o, *, t_ctx, t_tot):
    tm = x_ref.shape[0]
    r0 = pl.program_id(1) * tm
    prev_ok = jnp.logical_and(r0 != 0, r0 != t_ctx)
    next_ok = jnp.logical_and(r0 + tm != t_ctx, r0 + tm != t_tot)
    first = _iota((tm, 1), 0) == 0
    last = _iota((tm, 1), 0) == tm - 1

    def shifted(lo, hi):
        x = x_ref[:, lo:hi]
        p_row = jnp.where(prev_ok, xp_ref[7:8, lo:hi], 0.0)
        n_row = jnp.where(next_ok, xn_ref[0:1, lo:hi], 0.0)
        prev = jnp.where(first, p_row, pltpu.roll(x, 1, 0))
        nxt = jnp.where(last, n_row, pltpu.roll(x, tm - 1, 0))
        return x + (0.5 * (prev + nxt) - x) * mu_ref[:, lo:hi]

    bw = B_WIDTH
    r = shifted(0, bw)
    k = shifted(bw, 2 * bw)
    v = shifted(2 * bw, 3 * bw)
    tail = shifted(3 * bw, B_PROJ)
    seg = _head_seg(bw)

    kk = k * kks_ref[...]
    kk = kk / jnp.maximum(jnp.sqrt(_mm_x3(kk * kk, seg)), 1e-12)
    ksum = jnp.zeros_like(k)
    for d, (k_o, b_o, lw_o) in enumerate(((kf_o, bf_o, lwf_o), (kb_o, bb_o, lwb_o))):
        wd = tail[:, d * RWKV_W_RANK:(d + 1) * RWKV_W_RANK]
        ad = tail[:, 2 * RWKV_W_RANK + d * RWKV_A_RANK:2 * RWKV_W_RANK + (d + 1) * RWKV_A_RANK]
        u = -(w0_ref[d:d + 1, :] + _mm_3pass(jnp.tanh(wd), wup_ref[d]))
        w = -(jnp.maximum(u, 0.0) + jnp.log(1.0 + jnp.exp(-jnp.abs(u)))) - 0.5
        a = _sigmoid(a0_ref[d:d + 1, :] + _mm_3pass(ad, aup_ref[d]))
        k_d = k * (1.0 + (a - 1.0) * ka_ref[...])
        ksum = ksum + k_d
        k_o[...] = k_d
        b_o[...] = a * kk
        lw_o[...] = -jnp.exp(w)
    gd = tail[:, 2 * RWKV_W_RANK + 2 * RWKV_A_RANK:]
    r_o[...] = r
    v_o[...] = v
    kk_o[...] = kk
    g_o[...] = _mm_3pass(_sigmoid(gd), gup_ref[...])
    bonus_o[...] = _mm((r * ksum * rk_ref[...]).astype(BF16), seg) * v


def _rwkv_prep(pb, mu, w0, w_up, a0, a_up, g_up, kk_scale, k_a, r_k, *, t_ctx):
    bsz, t_tot, _ = pb.shape
    tm = TOKEN_TILE
    nb8 = t_tot // 8
    per8 = tm // 8
    bw = B_WIDTH
    full = lambda shape: pl.BlockSpec(shape, lambda b, i: (0,) * len(shape))
    out = jax.ShapeDtypeStruct((bsz, t_tot, bw), F32)
    return pl.pallas_call(
        functools.partial(_rwkv_prep_body, t_ctx=t_ctx, t_tot=t_tot),
        grid=(bsz, t_tot // tm),
        in_specs=[
            pl.BlockSpec((None, tm, B_PROJ), lambda b, i: (b, i, 0)),
            pl.BlockSpec((None, 8, B_PROJ), lambda b, i: (b, jnp.maximum(i * per8 - 1, 0), 0)),
            pl.BlockSpec((None, 8, B_PROJ), lambda b, i: (b, jnp.minimum((i + 1) * per8, nb8 - 1), 0)),
            full((1, B_PROJ)), full((2, bw)), full((2, RWKV_W_RANK, bw)), full((2, bw)),
            full((2, RWKV_A_RANK, bw)), full((RWKV_G_RANK, bw)), full((1, bw)), full((1, bw)), full((1, bw)),
        ],
        out_specs=[pl.BlockSpec((None, tm, bw), lambda b, i: (b, i, 0))] * 11,
        out_shape=[out] * 11,
        compiler_params=_cparams("arbitrary", "arbitrary"),
        name="rwkv_prep",
    )(pb, pb, pb, mu.reshape(1, B_PROJ), w0, w_up, a0, a_up, g_up,
      kk_scale.reshape(1, bw), k_a.reshape(1, bw), r_k.reshape(1, bw))


def _rwkv_scan_body(r_ref, v_ref, kk_ref, kf_ref, kb_ref, bf_ref, bb_ref, lwf_ref, lwb_ref, y_ref,
                    rq_s, y0_s, p_s, z_s, *, t_ctx):
    t_tot = r_ref.shape[0]
    c = RWKV_CHUNK
    rb = RWKV_BLOCK
    n_blk = t_tot // rb
    n_chunk = t_tot // c
    n_chunk_ctx = t_ctx // c
    cpb = rb // c
    n_double = c.bit_length() - 2
    row = _iota((rb, rb), 0)
    col = _iota((rb, rb), 1)
    same = (row // c) == (col // c)
    same_bf = jnp.where(same, 1.0, 0.0).astype(BF16)
    eye = jnp.where(row == col, 1.0, 0.0)
    lane = _iota((1, LANES), 1)
    head_masks = [jnp.where(lane // HEAD_DIM == h, 1.0, 0.0) for h in range(LANES // HEAD_DIM)]
    r128 = _iota((LANES, LANES), 0)
    c128 = _iota((LANES, LANES), 1)
    blockdiag = (r128 // HEAD_DIM) == (c128 // HEAD_DIM)
    diag128 = r128 == c128

    def block(it, carry):
        bis = [it * RWKV_BLOCKS_PER_ITER + j for j in range(RWKV_BLOCKS_PER_ITER)]
        sls = [pl.ds(pl.multiple_of(bi * rb, rb), rb) for bi in bis]
        v_bfs = [v_ref[sl, :].astype(BF16) for sl in sls]
        pre = {}
        for j, sl in enumerate(sls):
            r = r_ref[sl, :]
            kk = kk_ref[sl, :]
            for d in range(2):
                rev = d == 1
                k_ref, b_ref, lw_ref = (kb_ref, bb_ref, lwb_ref) if rev else (kf_ref, bf_ref, lwf_ref)
                incl = same & ((col >= row) if rev else (col <= row))
                strict = same & ((col > row) if rev else (col < row))
                incl_bf = jnp.where(incl, 1.0, 0.0).astype(BF16)
                lw = lw_ref[sl, :]
                k = k_ref[sl, :]
                b = b_ref[sl, :]
                cl = _mm_l3(incl_bf, lw)
                tot = _mm_l3(same_bf, lw)
                w_inv = jnp.exp(-cl)
                w_end = jnp.exp(tot - cl)
                pre[j, d] = dict(incl=incl, strict=strict, tot=tot, kk_d=kk * jnp.exp(cl - lw), r_d=r * jnp.exp(cl),
                                 kb=jnp.concatenate([k * w_inv, b * w_inv], axis=0).astype(BF16),
                                 b_e=(b * w_end).astype(BF16), k_e=(k * w_end).astype(BF16))
        chains = [dict(j=j, d=d, mh=mh, kkm=pre[j, d]['kk_d'] * mh, rm=pre[j, d]['r_d'] * mh)
                  for j in range(len(bis)) for d in range(2) for mh in head_masks]
        for ch in chains:
            p = pre[ch['j'], ch['d']]
            aa = _nt(jnp.concatenate([ch['kkm'], ch['rm']], axis=0).astype(BF16), p['kb'])
            ch['a_kb'] = jnp.where(p['strict'], aa[:rb, :rb], 0.0)
            ch['a_rk'] = jnp.where(p['incl'], aa[rb:, :rb], 0.0)
            ch['a_rb'] = jnp.where(p['incl'], aa[rb:, rb:], 0.0)
            m = -jnp.where(p['strict'], aa[:rb, rb:], 0.0)
            ch['t'] = eye + m
            ch['m'] = m
        for ch in chains:
            m_bf = ch['m'].astype(BF16)
            ch['m'] = _mm(m_bf, m_bf)
        for _ in range(n_double - 1):
            for ch in chains:
                both = _mm(jnp.concatenate([ch['t'], ch['m']], axis=0).astype(BF16), ch['m'].astype(BF16))
                ch['t'] = ch['t'] + both[:rb]
                ch['m'] = both[rb:]
        for ch in chains:
            ch['t'] = ch['t'] + _mm(ch['t'].astype(BF16), ch['m'].astype(BF16))
            ch['av'] = _mm(jnp.concatenate([ch['a_kb'], ch['a_rk']], axis=0).astype(BF16), v_bfs[ch['j']])
        for ch in chains:
            ch['ku'] = _mm(ch['t'].astype(BF16), jnp.concatenate([ch['kkm'], ch['av'][:rb]], axis=1).astype(BF16))
        for ch in chains:
            ch['rb_ku'] = _mm(ch['a_rb'].astype(BF16), ch['ku'].astype(BF16))
        for j, (bi, sl) in enumerate(zip(bis, sls)):
            for d in range(2):
                p = pre[j, d]
                mine = [ch for ch in chains if ch['j'] == j and ch['d'] == d]
                kkt = sum(ch['ku'][:, :LANES] for ch in mine)
                u = sum(ch['ku'][:, LANES:] * ch['mh'] for ch in mine)
                rq_s[d, sl, :] = sum(ch['rm'] - ch['rb_ku'][:, :LANES] for ch in mine)
                y0_s[d, sl, :] = sum((ch['av'][rb:] - ch['rb_ku'][:, LANES:]) * ch['mh'] for ch in mine)
                ktu = jnp.concatenate([kkt, u], axis=1).astype(BF16)
                for n in range(cpb):
                    rows = slice(n * c, (n + 1) * c)
                    wc = jnp.exp(p['tot'][n * c:n * c + 1, :])
                    kub = _tn(ktu[rows], p['b_e'][rows])
                    p_s[d, bi * cpb + n] = jnp.where(diag128, wc, 0.0) - jnp.where(blockdiag, kub[:LANES], 0.0)
                    p_z = jnp.where(blockdiag, _tn(v_bfs[j][rows], p['k_e'][rows]) - kub[LANES:], 0.0)
                    z_s[d, bi * cpb + n] = p_z
        return carry

    lax.fori_loop(0, n_blk // RWKV_BLOCKS_PER_ITER, block, 0)

    y_ref[...] = jnp.zeros_like(y_ref)

    def step(i, carry):
        n_b = jnp.where(i < n_chunk_ctx, n_chunk_ctx - 1 - i, n_chunk - 1 - (i - n_chunk_ctx))
        chunk = (i, n_b)
        s_bf = [s.astype(BF16) for s in carry]
        ys = [_nt(rq_s[d, pl.ds(pl.multiple_of(chunk[d] * c, c), c), :].astype(BF16), s_bf[d]) for d in range(2)]
        states = [_mm(s_bf[d], p_s[d, chunk[d]].astype(BF16)) + z_s[d, chunk[d]] for d in range(2)]
        for d in range(2):
            sl = pl.ds(pl.multiple_of(chunk[d] * c, c), c)
            y_ref[sl, :] += ys[d] + y0_s[d, sl, :]
        return tuple(states)

    zero = jnp.zeros((LANES, LANES), F32)
    lax.fori_loop(0, n_chunk, step, (zero, zero))


def _rwkv_scan(r, v, kk, kf, kb, bf, bb, lwf, lwb, *, t_ctx):
    bsz, t_tot, bw = r.shape
    n_chunk = t_tot // RWKV_CHUNK
    spec = pl.BlockSpec((None, t_tot, LANES), lambda b, h: (b, 0, h))
    return pl.pallas_call(
        functools.partial(_rwkv_scan_body, t_ctx=t_ctx),
        grid=(bsz, bw // LANES),
        in_specs=[spec] * 9,
        out_specs=spec,
        out_shape=jax.ShapeDtypeStruct((bsz, t_tot, bw), F32),
        scratch_shapes=[
            pltpu.VMEM((2, t_tot, LANES), F32),
            pltpu.VMEM((2, t_tot, LANES), F32),
            pltpu.VMEM((2, n_chunk, LANES, LANES), F32),
            pltpu.VMEM((2, n_chunk, LANES, LANES), F32),
        ],
        compiler_params=_cparams("arbitrary", "arbitrary"),
        name="rwkv_scan",
    )(r, v, kk, kf, kb, bf, bb, lwf, lwb)


def _rwkv_readout(y, g, bonus, ln_gain, ln_bias):
    seg = _head_seg(B_WIDTH)
    mean = _mm_x3(y, seg) * (1.0 / HEAD_DIM)
    yc = y - mean
    var = _mm((yc * yc).astype(BF16), seg) * (1.0 / HEAD_DIM)
    return (yc * lax.rsqrt(var + RWKV_LN_EPS) * ln_gain + ln_bias + bonus) * g


def _na_body(q_ref, k_ref, v_ref, qg_ref, kg_ref, cos_ref, sin_ref, bias_ref, o_ref,
             qs, ks, vs, *, t_ctx, need_ctx):
    t_tot = q_ref.shape[0]
    t_lat = t_tot - t_ctx
    rows = t_lat // GRID_W
    win_r = min(NA_WIN_ROWS, rows)
    scale = HEAD_DIM ** -0.5
    seg = _head_seg(LANES)
    lane = _iota((1, LANES), 1)
    head_masks = [jnp.where(lane // HEAD_DIM == h, 1.0, 0.0) for h in range(LANES // HEAD_DIM)]
    half = (lane % HEAD_DIM) < HEAD_DIM // 2
    blk = 256

    def norm_block(i, carry):
        sl = pl.ds(pl.multiple_of(i * blk, blk), blk)
        q = q_ref[sl, :]
        k = k_ref[sl, :]
        q = q * lax.rsqrt(_mm((q * q).astype(BF16), seg) * (1.0 / HEAD_DIM) + EPS) * qg_ref[...]
        k = k * lax.rsqrt(_mm((k * k).astype(BF16), seg) * (1.0 / HEAD_DIM) + EPS) * kg_ref[...]
        cos = cos_ref[sl, :]
        sin = sin_ref[sl, :]

        def rope(t):
            swapped = jnp.where(half, pltpu.roll(t, LANES - HEAD_DIM // 2, 1), pltpu.roll(t, HEAD_DIM // 2, 1))
            return t * cos + swapped * sin

        qs[sl, :] = (rope(q) * scale).astype(BF16)
        ks[sl, :] = rope(k).astype(BF16)
        vs[sl, :] = v_ref[sl, :].astype(BF16)
        return carry

    lax.fori_loop(0, t_tot // blk, norm_block, 0)

    masks_bf = [mh.astype(BF16) for mh in head_masks]

    def by_head(q):
        return jnp.concatenate([q * mb for mb in masks_bf], axis=0)

    def merge_heads(o, n):
        return sum(o[h * n:(h + 1) * n] * mh for h, mh in enumerate(head_masks))

    if need_ctx:
        s = _nt(by_head(qs[0:t_ctx, :]), ks[0:t_ctx, :])
        p = jnp.exp(s - jnp.max(s, axis=-1, keepdims=True))
        o = _mm(p.astype(BF16), vs[0:t_ctx, :]) / jnp.sum(p, axis=-1, keepdims=True)
        o_ref[0:t_ctx, :] = merge_heads(o, t_ctx)
    else:
        o_ref[0:t_ctx, :] = jnp.zeros((t_ctx, LANES), F32)

    def q_rows(it, carry):
        rs = [it * NA_ROW_UNROLL + j for j in range(NA_ROW_UNROLL)]
        starts = [jnp.clip(r - win_r // 2, 0, rows - win_r) for r in rs]
        q_sl = [pl.ds(pl.multiple_of(t_ctx + r * GRID_W, GRID_W), GRID_W) for r in rs]
        k_sl = [pl.ds(pl.multiple_of(t_ctx + r0 * GRID_W, GRID_W), win_r * GRID_W) for r0 in starts]
        qs_ = [by_head(qs[sl, :]) for sl in q_sl]
        s_win = [_nt(q, ks[sl, :]) + bias_ref[r - r0] for q, sl, r, r0 in zip(qs_, k_sl, rs, starts)]
        s_ctx = [_nt(q, ks[0:t_ctx, :]) for q in qs_]
        ms = [jnp.maximum(jnp.max(a, axis=-1, keepdims=True), jnp.max(b, axis=-1, keepdims=True))
              for a, b in zip(s_win, s_ctx)]
        p_win = [jnp.exp(a - m) for a, m in zip(s_win, ms)]
        p_ctx = [jnp.exp(b - m) for b, m in zip(s_ctx, ms)]
        den = [jnp.sum(a, axis=-1, keepdims=True) + jnp.sum(b, axis=-1, keepdims=True) for a, b in zip(p_win, p_ctx)]
        o_win = [_mm(a.astype(BF16), vs[sl, :]) for a, sl in zip(p_win, k_sl)]
        o_ctx = [_mm(b.astype(BF16), vs[0:t_ctx, :]) for b in p_ctx]
        for sl, a, b, d in zip(q_sl, o_win, o_ctx, den):
            o_ref[sl, :] = merge_heads((a + b) / d, GRID_W)
        return carry

    lax.fori_loop(0, rows // NA_ROW_UNROLL, q_rows, 0)


def _na_tables(t_ctx, t_lat):
    quarter = HEAD_DIM // 4
    pos = np.arange(t_lat)
    inv = ROPE_THETA ** (-np.arange(quarter, dtype=np.float32) / quarter)
    pos_r = (pos // GRID_W).astype(np.float32)
    pos_c = (pos % GRID_W).astype(np.float32)
    return pos_r, pos_c, inv


def _na_bias_table(rpb, rows):
    win_r = min(NA_WIN_ROWS, rows)
    c = np.arange(GRID_W)
    w_start = np.clip(c - NA_WIN_COLS // 2, 0, GRID_W - NA_WIN_COLS)
    kc = np.arange(GRID_W)
    in_win = (kc[None, :] >= w_start[:, None]) & (kc[None, :] < w_start[:, None] + NA_WIN_COLS)
    col_idx = np.clip(kc[None, :] - c[:, None] + NA_WIN_COLS - 1, 0, 2 * NA_WIN_COLS - 2)
    n_col = 2 * NA_WIN_COLS - 1
    onehot = jnp.asarray(col_idx[None] == np.arange(n_col)[:, None, None], F32)
    band = jnp.einsum('...hrj,jck->...hrck', rpb.astype(F32), onehot, precision=_HI)
    band = jnp.where(in_win, band, NEG_INF)
    per_off = [band[..., NA_WIN_ROWS - 1 - off:NA_WIN_ROWS - 1 - off + win_r, :, :] for off in range(win_r)]
    g = jnp.stack(per_off, axis=-4)
    g = jnp.swapaxes(g, -3, -2)
    return g.reshape(g.shape[:-2] + (win_r * GRID_W,))


def _na(pc, q_gain, k_gain, bias, layer, *, t_ctx, need_ctx):
    bsz, t_tot, _ = pc.shape
    t_lat = t_tot - t_ctx
    rows = t_lat // GRID_W
    win_r = min(NA_WIN_ROWS, rows)
    n_hp = C_WIDTH // LANES
    hpl = LANES // HEAD_DIM
    pos_r, pos_c, inv = _na_tables(t_ctx, t_lat)
    ang = np.concatenate([pos_r[:, None] * inv, pos_c[:, None] * inv], axis=-1)
    ang = np.concatenate([np.zeros((t_ctx, HEAD_DIM // 2), np.float32), ang], axis=0)
    cos = np.cos(ang)
    sin = np.sin(ang)
    cos_t = jnp.asarray(np.tile(np.concatenate([cos, cos], axis=-1), (1, hpl)), F32)
    sin_t = jnp.asarray(np.tile(np.concatenate([-sin, sin], axis=-1), (1, hpl)), F32)
    bias = bias.reshape(bias.shape[0], n_hp, hpl, win_r, GRID_W, win_r * GRID_W)
    bias = jnp.swapaxes(bias, 2, 3).reshape(bias.shape[0], n_hp, win_r, hpl * GRID_W, win_r * GRID_W)

    def sec(s):
        return pl.BlockSpec((None, t_tot, LANES), lambda b, h, s=s: (b, 0, s * n_hp + h))

    gain = lambda g: jnp.tile(g.reshape(1, HEAD_DIM), (1, hpl))
    return pl.pallas_call(
        functools.partial(_na_body, t_ctx=t_ctx, need_ctx=need_ctx),
        grid=(bsz, n_hp),
        in_specs=[sec(0), sec(1), sec(2),
                  pl.BlockSpec((1, LANES), lambda b, h: (0, 0)),
                  pl.BlockSpec((1, LANES), lambda b, h: (0, 0)),
                  pl.BlockSpec((t_tot, LANES), lambda b, h: (0, 0)),
                  pl.BlockSpec((t_tot, LANES), lambda b, h: (0, 0)),
                  pl.BlockSpec((None, None, win_r, hpl * GRID_W, win_r * GRID_W),
                               lambda b, h: (layer, h, 0, 0, 0))],
        out_specs=pl.BlockSpec((None, t_tot, LANES), lambda b, h: (b, 0, h)),
        out_shape=jax.ShapeDtypeStruct((bsz, t_tot, C_WIDTH), F32),
        scratch_shapes=[pltpu.VMEM((t_tot, LANES), BF16)] * 3,
        compiler_params=_cparams("arbitrary", "arbitrary"),
        name="na",
    )(pc, pc, pc, gain(q_gain), gain(k_gain), cos_t, sin_t, bias)


def _outproj_body(x_ref, a_ref, by_ref, bg_ref, bb_ref, lng_ref, lnb_ref, c_ref, *refs, per):
    m_refs = refs[:per]
    g_ref, w_ref, wr_ref, br_ref, xo_ref, h_ref, route_ref = refs[per:]
    tt = TOKEN_TILE
    b_mix = _rwkv_readout(by_ref[...], bg_ref[...], bb_ref[...], lng_ref[...], lnb_ref[...])
    mix = (_mm(a_ref[...].astype(BF16), w_ref[0:A_WIDTH, :])
           + _mm(b_mix.astype(BF16), w_ref[A_WIDTH:A_WIDTH + B_WIDTH, :])
           + _mm(c_ref[...].astype(BF16), w_ref[A_WIDTH + B_WIDTH:, :]))
    hs = []
    for j, m_ref in enumerate(m_refs):
        rows = slice(j * tt, (j + 1) * tt)
        x = x_ref[rows, :] + m_ref[2:3, :] * mix[rows]
        xo_ref[rows, :] = x
        h = x * lax.rsqrt(jnp.mean(x * x, axis=-1, keepdims=True) + EPS) * g_ref[...]
        h = h * (1.0 + m_ref[4:5, :]) + m_ref[3:4, :]
        h_ref[rows, :] = h
        hs.append(h)
    h = jnp.concatenate(hs, axis=0)
    logits = _mm_3pass(h, wr_ref[...]) + br_ref[...]
    lane = _iota(logits.shape, 1).astype(F32)
    big = float(LANES)
    is_g = (lane >= N_EXPERTS) & (lane < N_EXPERTS + N_GROUPS)
    gl = jnp.where(is_g, logits, -jnp.inf)
    gmax = jnp.max(gl, axis=-1, keepdims=True)
    g_w = 1.0 / jnp.sum(jnp.exp(gl - gmax), axis=-1, keepdims=True)
    g_sel = jnp.min(jnp.where(gl == gmax, lane, big), axis=-1, keepdims=True) - N_EXPERTS
    lo = g_sel * EXPERTS_PER_GROUP
    el = jnp.where((lane >= lo) & (lane < lo + EXPERTS_PER_GROUP), logits, -jnp.inf)
    m1 = jnp.max(el, axis=-1, keepdims=True)
    i1 = jnp.min(jnp.where(el == m1, lane, big), axis=-1, keepdims=True)
    el2 = jnp.where(lane == i1, -jnp.inf, el)
    m2 = jnp.max(el2, axis=-1, keepdims=True)
    i2 = jnp.min(jnp.where(el2 == m2, lane, big), axis=-1, keepdims=True)
    e2 = jnp.exp(m2 - m1)
    w1 = g_w / (1.0 + e2)
    route_ref[...] = (jnp.where(lane == 0.0, i1, 0.0) + jnp.where(lane == 1.0, i2, 0.0)
                      + jnp.where(lane == 2.0, w1, 0.0) + jnp.where(lane == 3.0, w1 * e2, 0.0))


def _outproj(x, a, b_scan, b_gate, b_bonus, ln_gain, ln_bias, c, mod_l, gain2, w_bf, w_router, b_router, *,
             bps, ctx_blocks, ctx_row):
    n = x.shape[0]
    per = PROJ_TILES
    tm = per * TOKEN_TILE
    tok = lambda w: pl.BlockSpec((tm, w), lambda i: (i, 0))
    full = lambda shape: pl.BlockSpec(shape, lambda i: (0,) * len(shape))
    return pl.pallas_call(
        functools.partial(_outproj_body, per=per),
        grid=(n // tm,),
        in_specs=[tok(D_MODEL), tok(A_WIDTH), tok(B_WIDTH), tok(B_WIDTH), tok(B_WIDTH),
                  full((1, B_WIDTH)), full((1, B_WIDTH)), tok(C_WIDTH),
                  *_mod_specs(per, bps, ctx_blocks, ctx_row),
                  full((1, D_MODEL)), full((D_MODEL, D_MODEL)), full((D_MODEL, LANES)), full((1, LANES))],
        out_specs=[tok(D_MODEL), tok(D_MODEL), tok(LANES)],
        out_shape=[jax.ShapeDtypeStruct((n, D_MODEL), F32),
                   jax.ShapeDtypeStruct((n, D_MODEL), F32),
                   jax.ShapeDtypeStruct((n, LANES), F32)],
        compiler_params=_cparams("arbitrary"),
        name="outproj",
    )(x, a, b_scan, b_gate, b_bonus, ln_gain.reshape(1, B_WIDTH), ln_bias.reshape(1, B_WIDTH), c,
      *([mod_l] * per), gain2.reshape(1, D_MODEL), w_bf, w_router, b_router)


def _route_tables(route, bsz, t_tot):
    n_pairs = TOP_K * t_tot
    ids = route[:, 0:TOP_K].astype(jnp.int32).reshape(bsz, t_tot, TOP_K)
    wts = route[:, TOP_K:2 * TOP_K].reshape(bsz, t_tot, TOP_K)
    ids = jnp.swapaxes(ids, 1, 2).reshape(bsz, n_pairs)
    wts = jnp.swapaxes(wts, 1, 2).reshape(bsz, n_pairs)
    key = ids * n_pairs + jnp.arange(n_pairs, dtype=jnp.int32)
    key, wts = lax.sort((key, wts), dimension=1, num_keys=1)
    tok = (key % n_pairs) % t_tot
    count = jnp.sum((ids[:, :, None] == jnp.arange(N_EXPERTS, dtype=jnp.int32)).astype(jnp.int32), axis=1)
    start = jnp.cumsum(count, axis=1) - count
    pad = ((0, 0), (0, MOE_TILE))
    return (jnp.pad(tok, pad).reshape(-1), jnp.pad(wts, pad).reshape(-1), start.reshape(-1), count.reshape(-1))


def _moe_body(tok_ref, w_ref, start_ref, count_ref, h_ref, wg_ref, wu_ref, wd_ref, y_ref, hbuf, obuf):
    c = pl.program_id(0)
    step = pl.program_id(1)
    t_tot = h_ref.shape[0]
    p_len = TOP_K * t_tot + MOE_TILE

    @pl.when(jnp.logical_and(c == 0, step == 0))
    def _():
        hbuf[...] = jnp.zeros_like(hbuf)

    @pl.when(step == 0)
    def _():
        def zero(i, carry):
            y_ref[pl.ds(pl.multiple_of(i * TOKEN_TILE, TOKEN_TILE), TOKEN_TILE), :] = jnp.zeros(
                (TOKEN_TILE, D_MODEL), F32)
            return carry
        lax.fori_loop(0, t_tot // TOKEN_TILE, zero, 0)

    for k in range(MOE_EXPERTS_PER_STEP):
        e = step * MOE_EXPERTS_PER_STEP + k
        _moe_expert(tok_ref, w_ref, h_ref, wg_ref.at[k], wu_ref.at[k], wd_ref.at[k], y_ref, hbuf, obuf,
                    count=count_ref[c * N_EXPERTS + e], pair_base=c * p_len + start_ref[c * N_EXPERTS + e])


def _moe_expert(tok_ref, w_ref, h_ref, wg_ref, wu_ref, wd_ref, y_ref, hbuf, obuf, *, count, pair_base):
    tm = MOE_TILE

    def tile(t, carry):
        n_valid = jnp.minimum(tm, count - t * tm)
        pair0 = pair_base + t * tm

        def gather(g, carry):
            for j in range(8):
                tok = tok_ref[pair0 + g * 8 + j]
                hbuf[g, pl.ds(j, 1), :] = h_ref[pl.ds(tok, 1), :]
            return carry
        lax.fori_loop(0, (n_valid + 7) // 8, gather, 0)

        h = hbuf[...].reshape(tm, D_MODEL).astype(BF16)
        gate = _mm(h, wg_ref[...])
        up = _mm(h, wu_ref[...])
        hid = gate * _sigmoid(gate) * up
        obuf[...] = _mm(hid.astype(BF16), wd_ref[...]).reshape(tm // 8, 8, D_MODEL)

        def add_row(r):
            tok = tok_ref[pair0 + r]
            y_ref[pl.ds(tok, 1), :] += w_ref[pair0 + r] * obuf[r // 8, pl.ds(r % 8, 1), :]

        def scatter(g, carry):
            toks = [tok_ref[pair0 + g * 8 + j] for j in range(8)]
            new = [y_ref[pl.ds(toks[j], 1), :] + w_ref[pair0 + g * 8 + j] * obuf[g, pl.ds(j, 1), :]
                   for j in range(8)]
            for j in range(8):
                y_ref[pl.ds(toks[j], 1), :] = new[j]
            return carry
        lax.fori_loop(0, n_valid // 8, scatter, 0)

        def scatter_tail(r, carry):
            add_row(r)
            return carry
        lax.fori_loop(n_valid // 8 * 8, n_valid, scatter_tail, 0)
        return carry

    lax.fori_loop(0, (count + tm - 1) // tm, tile, 0)


def _moe(h, route, wg, wu, wd, layer, *, bsz, t_tot):
    tok, w, start, count = _route_tables(route, bsz, t_tot)
    grid_spec = pltpu.PrefetchScalarGridSpec(
        num_scalar_prefetch=4,
        grid=(bsz, N_EXPERTS // MOE_EXPERTS_PER_STEP),
        in_specs=[
            pl.BlockSpec((None, t_tot, D_MODEL), lambda c, e, *_: (c, 0, 0)),
            pl.BlockSpec((None, MOE_EXPERTS_PER_STEP, D_MODEL, D_EXPERT), lambda c, e, *_: (layer, e, 0, 0)),
            pl.BlockSpec((None, MOE_EXPERTS_PER_STEP, D_MODEL, D_EXPERT), lambda c, e, *_: (layer, e, 0, 0)),
            pl.BlockSpec((None, MOE_EXPERTS_PER_STEP, D_EXPERT, D_MODEL), lambda c, e, *_: (layer, e, 0, 0)),
        ],
        out_specs=pl.BlockSpec((None, t_tot, D_MODEL), lambda c, e, *_: (c, 0, 0)),
        scratch_shapes=[pltpu.VMEM((MOE_TILE // 8, 8, D_MODEL), F32)] * 2,
    )
    y = pl.pallas_call(
        _moe_body,
        grid_spec=grid_spec,
        out_shape=jax.ShapeDtypeStruct((bsz, t_tot, D_MODEL), F32),
        compiler_params=_cparams("arbitrary", "arbitrary"),
        name="moe",
    )(tok, w, start, count, h.reshape(bsz, t_tot, D_MODEL), wg, wu, wd)
    return y.reshape(bsz * t_tot, D_MODEL)


def _residual_body(x_ref, y_ref, m_ref, o_ref):
    o_ref[...] = (x_ref[...] + m_ref[5:6, :] * y_ref[...]).astype(o_ref.dtype)


def _final_residual(x, y, mod_l, dtype, *, bsz, t_ctx, t_tot):
    tm = TOKEN_TILE
    tok = pl.BlockSpec((None, tm, D_MODEL), lambda b, i: (b, t_ctx // tm + i, 0))
    return pl.pallas_call(
        _residual_body,
        grid=(bsz, (t_tot - t_ctx) // tm),
        in_specs=[tok, tok, pl.BlockSpec((None, 6, D_MODEL), lambda b, i: (b, 0, 0))],
        out_specs=pl.BlockSpec((None, tm, D_MODEL), lambda b, i: (b, i, 0)),
        out_shape=jax.ShapeDtypeStruct((bsz, t_tot - t_ctx, D_MODEL), dtype),
        compiler_params=_cparams("arbitrary", "arbitrary"),
        name="residual",
    )(x.reshape(bsz, t_tot, D_MODEL), y.reshape(bsz, t_tot, D_MODEL), mod_l)


def kernel(x, c, ctx, c_ctx, norm1_gain, norm2_gain, w_ada, b_ada, w_in, w_out, hgrn_lb_logits, hgrn_gn_gain, rwkv_mu, rwkv_w0, rwkv_w_up, rwkv_a0, rwkv_a_up, rwkv_g_up, rwkv_kk_scale, rwkv_k_a, rwkv_r_k, rwkv_ln_gain, rwkv_ln_bias, na_q_gain, na_k_gain, na_rpb, w_router_group, b_router_group, w_router_expert, b_router_expert, w_exp_gate, w_exp_up, w_exp_down):
    bsz, t_lat, _ = x.shape
    t_ctx = ctx.shape[1]
    t_tot = t_ctx + t_lat
    n = bsz * t_tot
    assert t_ctx % TOKEN_TILE == 0 and t_lat % TOKEN_TILE == 0 and bsz < 16
    bps = t_tot // TOKEN_TILE
    tile_kw = dict(bps=bps, ctx_blocks=t_ctx // TOKEN_TILE, ctx_row=bsz)

    lb_p = jax.nn.softmax(hgrn_lb_logits.astype(F32), axis=1)
    lower_bounds = jnp.cumsum(lb_p, axis=1) - lb_p[:, :1]

    cc = jnp.zeros((16, D_MODEL), F32).at[:bsz].set(c.astype(F32)).at[bsz].set(c_ctx.astype(F32))
    mod = _ada_mod(cc, w_ada, b_ada).reshape(DEPTH, 16, 6, D_MODEL)

    w_in_bf = w_in.astype(BF16)
    w_out_bf = w_out.astype(BF16)
    wg_bf = w_exp_gate.astype(BF16).reshape(DEPTH, N_EXPERTS, D_MODEL, D_EXPERT)
    wu_bf = w_exp_up.astype(BF16).reshape(DEPTH, N_EXPERTS, D_MODEL, D_EXPERT)
    wd_bf = w_exp_down.astype(BF16).reshape(DEPTH, N_EXPERTS, D_EXPERT, D_MODEL)
    pad = LANES - N_EXPERTS - N_GROUPS
    w_router = jnp.concatenate([w_router_expert, w_router_group,
                                jnp.zeros((DEPTH, D_MODEL, pad), F32)], axis=-1)
    b_router = jnp.concatenate([b_router_expert, b_router_group, jnp.zeros((DEPTH, pad), F32)], axis=-1)

    na_bias = _na_bias_table(na_rpb, t_lat // GRID_W)
    xs = jnp.concatenate([ctx.astype(F32), x.astype(F32)], axis=1).reshape(n, D_MODEL)
    y_moe = None
    for layer in range(DEPTH):
        last = layer == DEPTH - 1
        if layer == 0:
            pa, pb, pc = _inproj(xs, mod[layer], norm1_gain[layer], w_in_bf[layer], **tile_kw)
        else:
            xs, pa, pb, pc = _inproj(xs, mod[layer], norm1_gain[layer], w_in_bf[layer],
                                     (y_moe, mod[layer - 1]), **tile_kw)
        a_mix = _hgrn(pa.reshape(bsz, t_tot, A_PROJ), lower_bounds[:, layer], hgrn_gn_gain[layer], t_ctx=t_ctx)
        prep = _rwkv_prep(pb.reshape(bsz, t_tot, B_PROJ), rwkv_mu[layer], rwkv_w0[layer], rwkv_w_up[layer],
                          rwkv_a0[layer], rwkv_a_up[layer], rwkv_g_up[layer], rwkv_kk_scale[layer],
                          rwkv_k_a[layer], rwkv_r_k[layer], t_ctx=t_ctx)
        y = _rwkv_scan(*prep[:9], t_ctx=t_ctx)
        c_mix = _na(pc.reshape(bsz, t_tot, C_PROJ), na_q_gain[layer], na_k_gain[layer], na_bias, layer,
                    t_ctx=t_ctx, need_ctx=not last)
        xs, h2, route = _outproj(xs, a_mix.reshape(n, A_WIDTH), y.reshape(n, B_WIDTH), prep[9].reshape(n, B_WIDTH),
                                 prep[10].reshape(n, B_WIDTH), rwkv_ln_gain[layer], rwkv_ln_bias[layer],
                                 c_mix.reshape(n, C_WIDTH), mod[layer], norm2_gain[layer], w_out_bf[layer],
                                 w_router[layer], b_router[layer].reshape(1, LANES), **tile_kw)
        y_moe = _moe(h2, route, wg_bf, wu_bf, wd_bf, layer, bsz=bsz, t_tot=t_tot)
    return _final_residual(xs, y_moe, mod[DEPTH - 1], x.dtype, bsz=bsz, t_ctx=t_ctx, t_tot=t_tot)
```

```python
import functools

import jax
import jax.numpy as jnp
import numpy as np
from jax import lax
from jax.experimental import pallas as pl
from jax.experimental.pallas import tpu as pltpu

F32 = jnp.float32
BF16 = jnp.bfloat16

D_MODEL = 1024
DEPTH = 4
GRID_W = 64
HEAD_DIM = 64
A_HEADS = 4
B_HEADS = 4
C_HEADS = 8
A_WIDTH = A_HEADS * HEAD_DIM
B_WIDTH = B_HEADS * HEAD_DIM
C_WIDTH = C_HEADS * HEAD_DIM
HGRN_CHUNK = 16
HGRN_F_FLOOR = 1e-20
RWKV_W_RANK = 64
RWKV_A_RANK = 64
RWKV_G_RANK = 128
RWKV_LN_EPS = 64e-5
NA_WIN_ROWS = 8
NA_WIN_COLS = 16
ROPE_THETA = 10000.0
N_GROUPS = 4
EXPERTS_PER_GROUP = 8
N_EXPERTS = N_GROUPS * EXPERTS_PER_GROUP
TOP_K = 2
D_EXPERT = 512
EPS = 1e-6
LOG2E = 1.4426950408889634
NEG_INF = -1e30
A_PROJ = 5 * A_WIDTH
B_PROJ = 3 * B_WIDTH + 2 * RWKV_W_RANK + 2 * RWKV_A_RANK + RWKV_G_RANK
C_PROJ = 3 * C_WIDTH
P_TOTAL = A_PROJ + B_PROJ + C_PROJ
B_TAIL = B_PROJ - 3 * B_WIDTH

LANES = 128
PROJ_TILES = 2
TOKEN_TILE = 256
RWKV_CHUNK = 64
RWKV_BLOCKS_PER_ITER = 2
RWKV_BLOCK = 128
HGRN_BLOCK = 128
NA_ROW_UNROLL = 8
HGRN_SCAN_UNROLL = 8
MOE_EXPERTS_PER_STEP = 2
MOE_TILE = 176
VMEM_LIMIT = 56 * 1024 * 1024

_HI = lax.Precision.HIGHEST


def _cparams(*sem):
    return pltpu.CompilerParams(dimension_semantics=sem, vmem_limit_bytes=VMEM_LIMIT)


def _mm(a, b):
    return jnp.dot(a, b, preferred_element_type=F32)


def _mm_hi(a, b):
    return jnp.dot(a, b, preferred_element_type=F32, precision=_HI)


def _nt(a, b):
    return lax.dot_general(a, b, (((1,), (1,)), ((), ())), preferred_element_type=F32)


def _tn(a, b):
    return lax.dot_general(a, b, (((0,), (0,)), ((), ())), preferred_element_type=F32)


def _split3(a):
    a1 = a.astype(BF16)
    r1 = a - a1.astype(F32)
    a2 = r1.astype(BF16)
    a3 = (r1 - a2.astype(F32)).astype(BF16)
    return a1, a2, a3


def _mm_x3(a, m):
    a1, a2, a3 = _split3(a)
    return _mm(a1, m) + _mm(a2, m) + _mm(a3, m)


def _mm_l3(m, a):
    a1, a2, a3 = _split3(a)
    return _mm(m, a1) + _mm(m, a2) + _mm(m, a3)


def _mm_3pass(a, b):
    a1 = a.astype(BF16)
    a2 = (a - a1.astype(F32)).astype(BF16)
    b1 = b.astype(BF16)
    b2 = (b - b1.astype(F32)).astype(BF16)
    return _mm(a1, b1) + (_mm(a1, b2) + _mm(a2, b1))


def _sigmoid(x):
    return 1.0 / (1.0 + jnp.exp(-x))


def _iota(shape, dim):
    return lax.broadcasted_iota(jnp.int32, shape, dim)


def _head_seg(n):
    return jnp.where(_iota((n, n), 0) // HEAD_DIM == _iota((n, n), 1) // HEAD_DIM, 1.0, 0.0).astype(BF16)


def _ada_body(c_ref, w_ref, b_ref, o_ref):
    c = c_ref[...]
    o_ref[...] = _mm_hi(c * _sigmoid(c), w_ref[...]) + b_ref[...]


def _ada_mod(cc, w_ada, b_ada):
    rows = cc.shape[0]
    tn = 1536
    return pl.pallas_call(
        _ada_body,
        grid=(DEPTH, 6 * D_MODEL // tn),
        in_specs=[
            pl.BlockSpec((rows, D_MODEL), lambda l, j: (0, 0)),
            pl.BlockSpec((None, D_MODEL, tn), lambda l, j: (l, 0, j)),
            pl.BlockSpec((None, 1, tn), lambda l, j: (l, 0, j)),
        ],
        out_specs=pl.BlockSpec((None, rows, tn), lambda l, j: (l, 0, j)),
        out_shape=jax.ShapeDtypeStruct((DEPTH, rows, 6 * D_MODEL), F32),
        compiler_params=_cparams("arbitrary", "arbitrary"),
        name="ada_mod",
    )(cc, w_ada, b_ada.reshape(DEPTH, 1, 6 * D_MODEL))


def _mod_row(i, blocks_per_seq, ctx_blocks, ctx_row):
    return jnp.where(i % blocks_per_seq < ctx_blocks, ctx_row, i // blocks_per_seq)


def _inproj_body(*refs, with_moe, per):
    tt = TOKEN_TILE
    if with_moe:
        x_ref, y_ref = refs[:2]
        mp_refs, m_refs = refs[2:2 + per], refs[2 + per:2 + 2 * per]
        g_ref, w_ref, xo_ref, pa_ref, pb_ref, pc_ref = refs[2 + 2 * per:]
    else:
        x_ref = refs[0]
        m_refs = refs[1:1 + per]
        g_ref, w_ref, pa_ref, pb_ref, pc_ref = refs[1 + per:]
    hs = []
    for j, m_ref in enumerate(m_refs):
        rows = slice(j * tt, (j + 1) * tt)
        x = x_ref[rows, :]
        if with_moe:
            x = x + mp_refs[j][5:6, :] * y_ref[rows, :]
            xo_ref[rows, :] = x
        h = x * lax.rsqrt(jnp.mean(x * x, axis=-1, keepdims=True) + EPS) * g_ref[...]
        hs.append((h * (1.0 + m_ref[1:2, :]) + m_ref[0:1, :]).astype(BF16))
    h = jnp.concatenate(hs, axis=0)
    pa_ref[...] = _mm(h, w_ref[:, :A_PROJ])
    pb_ref[...] = _mm(h, w_ref[:, A_PROJ:A_PROJ + B_PROJ])
    pc_ref[...] = _mm(h, w_ref[:, A_PROJ + B_PROJ:])


def _mod_specs(per, bps, ctx_blocks, ctx_row):
    row = functools.partial(_mod_row, blocks_per_seq=bps, ctx_blocks=ctx_blocks, ctx_row=ctx_row)
    return [pl.BlockSpec((None, 6, D_MODEL), lambda i, j=j: (row(i * per + j), 0, 0)) for j in range(per)]


def _inproj(x, mod_l, gain, w_bf, moe=None, *, bps, ctx_blocks, ctx_row):
    n = x.shape[0]
    per = PROJ_TILES
    tm = per * TOKEN_TILE
    tok = lambda w: pl.BlockSpec((tm, w), lambda i: (i, 0))
    mods = _mod_specs(per, bps, ctx_blocks, ctx_row)
    with_moe = moe is not None
    ins = (x, moe[0], *([moe[1]] * per)) if with_moe else (x,)
    return pl.pallas_call(
        functools.partial(_inproj_body, with_moe=with_moe, per=per),
        grid=(n // tm,),
        in_specs=[tok(D_MODEL)] + ([tok(D_MODEL)] + mods if with_moe else []) + mods + [
            pl.BlockSpec((1, D_MODEL), lambda i: (0, 0)),
            pl.BlockSpec((D_MODEL, P_TOTAL), lambda i: (0, 0)),
        ],
        out_specs=([tok(D_MODEL)] if with_moe else []) + [tok(A_PROJ), tok(B_PROJ), tok(C_PROJ)],
        out_shape=([jax.ShapeDtypeStruct((n, D_MODEL), F32)] if with_moe else []) + [
            jax.ShapeDtypeStruct((n, A_PROJ), F32),
            jax.ShapeDtypeStruct((n, B_PROJ), F32),
            jax.ShapeDtypeStruct((n, C_PROJ), F32),
        ],
        compiler_params=_cparams("arbitrary"),
        name="inproj",
    )(*ins, *([mod_l] * per), gain.reshape(1, D_MODEL), w_bf)


def _hgrn_body(q_ref, ff_ref, fb_ref, i_ref, g_ref, lb_ref, gn_ref, o_ref,
               qin_s, kout_s, cum_s, acc_s, kpad_s, cpad_s, vpad_s, *, t_ctx):
    t_tot = q_ref.shape[0]
    c = HGRN_CHUNK
    rb = HGRN_BLOCK
    n_blk = t_tot // rb
    n_chunk = t_tot // c
    n_chunk_ctx = t_ctx // c
    row = _iota((rb, rb), 0)
    col = _iota((rb, rb), 1)
    same = (row // c) == (col // c)
    same_bf = jnp.where(same, 1.0, 0.0).astype(BF16)
    seg = _head_seg(LANES)
    pos = _iota((rb, LANES), 0) % c
    blockdiag = _iota((LANES, LANES), 0) // HEAD_DIM == _iota((LANES, LANES), 1) // HEAD_DIM

    pad = HGRN_CHUNK
    zpad = jnp.zeros((pad, LANES), F32)
    for d in range(2):
        for ref in (kpad_s, cpad_s):
            ref[d, 0:pad, :] = zpad
            ref[d, pad + t_tot:, :] = zpad
    vpad_s[0:pad, :] = zpad
    vpad_s[pad + t_tot:, :] = zpad
    tris = [jnp.where(same & ((col >= row) if d == 1 else (col <= row)), 1.0, 0.0).astype(BF16) for d in range(2)]

    def gates(b, carry):
        sl = pl.ds(pl.multiple_of(b * rb, rb), rb)
        slp = pl.ds(pl.multiple_of(b * rb, rb) + pad, rb)
        q = q_ref[sl, :]
        vpad_s[slp, :] = i_ref[sl, :]
        for d, f_ref in enumerate((ff_ref, fb_ref)):
            lb = lb_ref[d:d + 1, :]
            fpre = f_ref[sl, :]
            f = lb + (1.0 - lb) * _sigmoid(fpre)
            logf = jnp.log(jnp.maximum(f, HGRN_F_FLOOR))
            k = (1.0 - lb) * _sigmoid(-fpre)
            cum = _mm_l3(tris[d], logf)
            tot = _mm_l3(same_bf, logf)
            kpad_s[d, slp, :] = k
            cpad_s[d, slp, :] = cum * LOG2E
            qin_s[d, sl, :] = q * jnp.exp(cum)
            kout_s[d, sl, :] = k * jnp.exp(tot - cum)
            cum_s[d, sl, :] = tot
        return carry

    lax.fori_loop(0, n_blk, gates, 0)

    def intra(b, carry):
        r0 = pl.multiple_of(b * rb, rb)
        sl = pl.ds(r0, rb)
        q = q_ref[sl, :]
        cums = [cpad_s[d, pl.ds(r0 + pad, rb), :] for d in range(2)]
        o = jnp.zeros((rb, LANES), F32)
        for j in range(c):
            for d in range(2):
                off = r0 + pad + (j if d == 1 else -j)
                valid = (pos <= c - 1 - j) if d == 1 else (pos >= j)
                ks = kpad_s[d, pl.ds(off, rb), :]
                cs = cpad_s[d, pl.ds(off, rb), :]
                vs = vpad_s[pl.ds(off, rb), :]
                prod = jnp.where(valid, q * ks * jnp.exp2(cums[d] - cs), 0.0)
                o = o + _mm(prod.astype(BF16), seg) * vs
        acc_s[sl, :] = o
        return carry

    lax.fori_loop(0, n_blk, intra, 0)

    def step(it, carry):
        idx = [it * HGRN_SCAN_UNROLL + j for j in range(HGRN_SCAN_UNROLL)]
        order = [(i, jnp.where(i < n_chunk_ctx, n_chunk_ctx - 1 - i, n_chunk - 1 - (i - n_chunk_ctx))) for i in idx]
        sls = [[pl.ds(pl.multiple_of(n[d] * c, c), c) for d in range(2)] for n in order]
        kvs = [[jnp.where(blockdiag, _tn(i_ref[sl[d], :].astype(BF16), kout_s[d, sl[d], :].astype(BF16)), 0.0)
                for d in range(2)] for sl in sls]
        decs = [[jnp.exp(cum_s[d, pl.ds(pl.multiple_of(n[d] * c, c), 1), :]) for d in range(2)] for n in order]
        states = list(carry)
        for j in range(HGRN_SCAN_UNROLL):
            for d in range(2):
                acc_s[sls[j][d], :] += _nt(qin_s[d, sls[j][d], :].astype(BF16), states[d].astype(BF16))
                states[d] = states[d] * decs[j][d] + kvs[j][d]
        return tuple(states)

    zero = jnp.zeros((LANES, LANES), F32)
    lax.fori_loop(0, n_chunk // HGRN_SCAN_UNROLL, step, (zero, zero))

    def readout(b, carry):
        sl = pl.ds(pl.multiple_of(b * rb, rb), rb)
        o = acc_s[sl, :]
        ms = _mm_x3(o * o, seg) * (1.0 / HEAD_DIM)
        g = g_ref[sl, :]
        o_ref[sl, :] = o * lax.rsqrt(ms + EPS) * gn_ref[...] * (g * _sigmoid(g))
        return carry

    lax.fori_loop(0, n_blk, readout, 0)


def _hgrn(pa, lb, gn_gain, *, t_ctx):
    bsz, t_tot, _ = pa.shape
    n_hp = A_WIDTH // LANES

    def sec(s):
        return pl.BlockSpec((None, t_tot, LANES), lambda b, h, s=s: (b, 0, s * n_hp + h))

    return pl.pallas_call(
        functools.partial(_hgrn_body, t_ctx=t_ctx),
        grid=(bsz, n_hp),
        in_specs=[sec(0), sec(1), sec(2), sec(3), sec(4),
                  pl.BlockSpec((2, LANES), lambda b, h: (0, h)),
                  pl.BlockSpec((1, LANES), lambda b, h: (0, h))],
        out_specs=pl.BlockSpec((None, t_tot, LANES), lambda b, h: (b, 0, h)),
        out_shape=jax.ShapeDtypeStruct((bsz, t_tot, A_WIDTH), F32),
        scratch_shapes=[
            pltpu.VMEM((2, t_tot, LANES), F32),
            pltpu.VMEM((2, t_tot, LANES), F32),
            pltpu.VMEM((2, t_tot, LANES), F32),
            pltpu.VMEM((t_tot, LANES), F32),
            pltpu.VMEM((2, t_tot + 2 * HGRN_CHUNK, LANES), F32),
            pltpu.VMEM((2, t_tot + 2 * HGRN_CHUNK, LANES), F32),
            pltpu.VMEM((t_tot + 2 * HGRN_CHUNK, LANES), F32),
        ],
        compiler_params=_cparams("arbitrary", "arbitrary"),
        name="hgrn",
    )(pa, pa, pa, pa, pa, lb, gn_gain.reshape(1, A_WIDTH))


def _rwkv_prep_body(x_ref, xp_ref, xn_ref, mu_ref, w0_ref, wup_ref, a0_ref, aup_ref, gup_ref,
                    kks_ref, ka_ref, rk_ref,
                    r_o, v_o, kk_o, kf_o, kb_o, bf_o, bb_o, lwf_o, lwb_o, g_o, bonus_o, *, t_ctx, t_tot):
    tm = x_ref.shape[0]
    r0 = pl.program_id(1) * tm
    prev_ok = jnp.logical_and(r0 != 0, r0 != t_ctx)
    next_ok = jnp.logical_and(r0 + tm != t_ctx, r0 + tm != t_tot)
    first = _iota((tm, 1), 0) == 0
    last = _iota((tm, 1), 0) == tm - 1

    def shifted(lo, hi):
        x = x_ref[:, lo:hi]
        p_row = jnp.where(prev_ok, xp_ref[7:8, lo:hi], 0.0)
        n_row = jnp.where(next_ok, xn_ref[0:1, lo:hi], 0.0)
        prev = jnp.where(first, p_row, pltpu.roll(x, 1, 0))
        nxt = jnp.where(last, n_row, pltpu.roll(x, tm - 1, 0))
        return x + (0.5 * (prev + nxt) - x) * mu_ref[:, lo:hi]

    bw = B_WIDTH
    r = shifted(0, bw)
    k = shifted(bw, 2 * bw)
    v = shifted(2 * bw, 3 * bw)
    tail = shifted(3 * bw, B_PROJ)
    seg = _head_seg(bw)

    kk = k * kks_ref[...]
    kk = kk / jnp.maximum(jnp.sqrt(_mm_x3(kk * kk, seg)), 1e-12)
    ksum = jnp.zeros_like(k)
    for d, (k_o, b_o, lw_o) in enumerate(((kf_o, bf_o, lwf_o), (kb_o, bb_o, lwb_o))):
        wd = tail[:, d * RWKV_W_RANK:(d + 1) * RWKV_W_RANK]
        ad = tail[:, 2 * RWKV_W_RANK + d * RWKV_A_RANK:2 * RWKV_W_RANK + (d + 1) * RWKV_A_RANK]
        u = -(w0_ref[d:d + 1, :] + _mm_3pass(jnp.tanh(wd), wup_ref[d]))
        w = -(jnp.maximum(u, 0.0) + jnp.log(1.0 + jnp.exp(-jnp.abs(u)))) - 0.5
        a = _sigmoid(a0_ref[d:d + 1, :] + _mm_3pass(ad, aup_ref[d]))
        k_d = k * (1.0 + (a - 1.0) * ka_ref[...])
        ksum = ksum + k_d
        k_o[...] = k_d
        b_o[...] = a * kk
        lw_o[...] = -jnp.exp(w)
    gd = tail[:, 2 * RWKV_W_RANK + 2 * RWKV_A_RANK:]
    r_o[...] = r
    v_o[...] = v
    kk_o[...] = kk
    g_o[...] = _mm_3pass(_sigmoid(gd), gup_ref[...])
    bonus_o[...] = _mm((r * ksum * rk_ref[...]).astype(BF16), seg) * v


def _rwkv_prep(pb, mu, w0, w_up, a0, a_up, g_up, kk_scale, k_a, r_k, *, t_ctx):
    bsz, t_tot, _ = pb.shape
    tm = TOKEN_TILE
    nb8 = t_tot // 8
    per8 = tm // 8
    bw = B_WIDTH
    full = lambda shape: pl.BlockSpec(shape, lambda b, i: (0,) * len(shape))
    out = jax.ShapeDtypeStruct((bsz, t_tot, bw), F32)
    return pl.pallas_call(
        functools.partial(_rwkv_prep_body, t_ctx=t_ctx, t_tot=t_tot),
        grid=(bsz, t_tot // tm),
        in_specs=[
            pl.BlockSpec((None, tm, B_PROJ), lambda b, i: (b, i, 0)),
            pl.BlockSpec((None, 8, B_PROJ), lambda b, i: (b, jnp.maximum(i * per8 - 1, 0), 0)),
            pl.BlockSpec((None, 8, B_PROJ), lambda b, i: (b, jnp.minimum((i + 1) * per8, nb8 - 1), 0)),
            full((1, B_PROJ)), full((2, bw)), full((2, RWKV_W_RANK, bw)), full((2, bw)),
            full((2, RWKV_A_RANK, bw)), full((RWKV_G_RANK, bw)), full((1, bw)), full((1, bw)), full((1, bw)),
        ],
        out_specs=[pl.BlockSpec((None, tm, bw), lambda b, i: (b, i, 0))] * 11,
        out_shape=[out] * 11,
        compiler_params=_cparams("arbitrary", "arbitrary"),
        name="rwkv_prep",
    )(pb, pb, pb, mu.reshape(1, B_PROJ), w0, w_up, a0, a_up, g_up,
      kk_scale.reshape(1, bw), k_a.reshape(1, bw), r_k.reshape(1, bw))


def _rwkv_scan_body(r_ref, v_ref, kk_ref, kf_ref, kb_ref, bf_ref, bb_ref, lwf_ref, lwb_ref, y_ref,
                    rq_s, y0_s, p_s, z_s, *, t_ctx):
    t_tot = r_ref.shape[0]
    c = RWKV_CHUNK
    rb = RWKV_BLOCK
    n_blk = t_tot // rb
    n_chunk = t_tot // c
    n_chunk_ctx = t_ctx // c
    cpb = rb // c
    n_double = c.bit_length() - 2
    row = _iota((rb, rb), 0)
    col = _iota((rb, rb), 1)
    same = (row // c) == (col // c)
    same_bf = jnp.where(same, 1.0, 0.0).astype(BF16)
    eye = jnp.where(row == col, 1.0, 0.0)
    lane = _iota((1, LANES), 1)
    head_masks = [jnp.where(lane // HEAD_DIM == h, 1.0, 0.0) for h in range(LANES // HEAD_DIM)]
    r128 = _iota((LANES, LANES), 0)
    c128 = _iota((LANES, LANES), 1)
    blockdiag = (r128 // HEAD_DIM) == (c128 // HEAD_DIM)
    diag128 = r128 == c128

    def block(it, carry):
        bis = [it * RWKV_BLOCKS_PER_ITER + j for j in range(RWKV_BLOCKS_PER_ITER)]
        sls = [pl.ds(pl.multiple_of(bi * rb, rb), rb) for bi in bis]
        v_bfs = [v_ref[sl, :].astype(BF16) for sl in sls]
        pre = {}
        for j, sl in enumerate(sls):
            r = r_ref[sl, :]
            kk = kk_ref[sl, :]
            for d in range(2):
                rev = d == 1
                k_ref, b_ref, lw_ref = (kb_ref, bb_ref, lwb_ref) if rev else (kf_ref, bf_ref, lwf_ref)
                incl = same & ((col >= row) if rev else (col <= row))
                strict = same & ((col > row) if rev else (col < row))
                incl_bf = jnp.where(incl, 1.0, 0.0).astype(BF16)
                lw = lw_ref[sl, :]
                k = k_ref[sl, :]
                b = b_ref[sl, :]
                cl = _mm_l3(incl_bf, lw)
                tot = _mm_l3(same_bf, lw)
                w_inv = jnp.exp(-cl)
                w_end = jnp.exp(tot - cl)
                pre[j, d] = dict(incl=incl, strict=strict, tot=tot, kk_d=kk * jnp.exp(cl - lw), r_d=r * jnp.exp(cl),
                                 kb=jnp.concatenate([k * w_inv, b * w_inv], axis=0).astype(BF16),
                                 b_e=(b * w_end).astype(BF16), k_e=(k * w_end).astype(BF16))
        chains = [dict(j=j, d=d, mh=mh, kkm=pre[j, d]['kk_d'] * mh, rm=pre[j, d]['r_d'] * mh)
                  for j in range(len(bis)) for d in range(2) for mh in head_masks]
        for ch in chains:
            p = pre[ch['j'], ch['d']]
            aa = _nt(jnp.concatenate([ch['kkm'], ch['rm']], axis=0).astype(BF16), p['kb'])
            ch['a_kb'] = jnp.where(p['strict'], aa[:rb, :rb], 0.0)
            ch['a_rk'] = jnp.where(p['incl'], aa[rb:, :rb], 0.0)
            ch['a_rb'] = jnp.where(p['incl'], aa[rb:, rb:], 0.0)
            m = -jnp.where(p['strict'], aa[:rb, rb:], 0.0)
            ch['t'] = eye + m
            ch['m'] = m
        for ch in chains:
            m_bf = ch['m'].astype(BF16)
            ch['m'] = _mm(m_bf, m_bf)
        for _ in range(n_double - 1):
            for ch in chains:
                both = _mm(jnp.concatenate([ch['t'], ch['m']], axis=0).astype(BF16), ch['m'].astype(BF16))
                ch['t'] = ch['t'] + both[:rb]
                ch['m'] = both[rb:]
        for ch in chains:
            ch['t'] = ch['t'] + _mm(ch['t'].astype(BF16), ch['m'].astype(BF16))
            ch['av'] = _mm(jnp.concatenate([ch['a_kb'], ch['a_rk']], axis=0).astype(BF16), v_bfs[ch['j']])
        for ch in chains:
            ch['ku'] = _mm(ch['t'].astype(BF16), jnp.concatenate([ch['kkm'], ch['av'][:rb]], axis=1).astype(BF16))
        for ch in chains:
            ch['rb_ku'] = _mm(ch['a_rb'].astype(BF16), ch['ku'].astype(BF16))
        for j, (bi, sl) in enumerate(zip(bis, sls)):
            for d in range(2):
                p = pre[j, d]
                mine = [ch for ch in chains if ch['j'] == j and ch['d'] == d]
                kkt = sum(ch['ku'][:, :LANES] for ch in mine)
                u = sum(ch['ku'][:, LANES:] * ch['mh'] for ch in mine)
                rq_s[d, sl, :] = sum(ch['rm'] - ch['rb_ku'][:, :LANES] for ch in mine)
                y0_s[d, sl, :] = sum((ch['av'][rb:] - ch['rb_ku'][:, LANES:]) * ch['mh'] for ch in mine)
                ktu = jnp.concatenate([kkt, u], axis=1).astype(BF16)
                for n in range(cpb):
                    rows = slice(n * c, (n + 1) * c)
                    wc = jnp.exp(p['tot'][n * c:n * c + 1, :])
                    kub = _tn(ktu[rows], p['b_e'][rows])
                    p_s[d, bi * cpb + n] = jnp.where(diag128, wc, 0.0) - jnp.where(blockdiag, kub[:LANES], 0.0)
                    p_z = jnp.where(blockdiag, _tn(v_bfs[j][rows], p['k_e'][rows]) - kub[LANES:], 0.0)
                    z_s[d, bi * cpb + n] = p_z
        return carry

    lax.fori_loop(0, n_blk // RWKV_BLOCKS_PER_ITER, block, 0)

    y_ref[...] = jnp.zeros_like(y_ref)

    def step(i, carry):
        n_b = jnp.where(i < n_chunk_ctx, n_chunk_ctx - 1 - i, n_chunk - 1 - (i - n_chunk_ctx))
        chunk = (i, n_b)
        s_bf = [s.astype(BF16) for s in carry]
        ys = [_nt(rq_s[d, pl.ds(pl.multiple_of(chunk[d] * c, c), c), :].astype(BF16), s_bf[d]) for d in range(2)]
        states = [_mm(s_bf[d], p_s[d, chunk[d]].astype(BF16)) + z_s[d, chunk[d]] for d in range(2)]
        for d in range(2):
            sl = pl.ds(pl.multiple_of(chunk[d] * c, c), c)
            y_ref[sl, :] += ys[d] + y0_s[d, sl, :]
        return tuple(states)

    zero = jnp.zeros((LANES, LANES), F32)
    lax.fori_loop(0, n_chunk, step, (zero, zero))


def _rwkv_scan(r, v, kk, kf, kb, bf, bb, lwf, lwb, *, t_ctx):
    bsz, t_tot, bw = r.shape
    n_chunk = t_tot // RWKV_CHUNK
    spec = pl.BlockSpec((None, t_tot, LANES), lambda b, h: (b, 0, h))
    return pl.pallas_call(
        functools.partial(_rwkv_scan_body, t_ctx=t_ctx),
        grid=(bsz, bw // LANES),
        in_specs=[spec] * 9,
        out_specs=spec,
        out_shape=jax.ShapeDtypeStruct((bsz, t_tot, bw), F32),
        scratch_shapes=[
            pltpu.VMEM((2, t_tot, LANES), F32),
            pltpu.VMEM((2, t_tot, LANES), F32),
            pltpu.VMEM((2, n_chunk, LANES, LANES), F32),
            pltpu.VMEM((2, n_chunk, LANES, LANES), F32),
        ],
        compiler_params=_cparams("arbitrary", "arbitrary"),
        name="rwkv_scan",
    )(r, v, kk, kf, kb, bf, bb, lwf, lwb)


def _rwkv_readout(y, g, bonus, ln_gain, ln_bias):
    seg = _head_seg(B_WIDTH)
    mean = _mm_x3(y, seg) * (1.0 / HEAD_DIM)
    yc = y - mean
    var = _mm((yc * yc).astype(BF16), seg) * (1.0 / HEAD_DIM)
    return (yc * lax.rsqrt(var + RWKV_LN_EPS) * ln_gain + ln_bias + bonus) * g


def _na_body(q_ref, k_ref, v_ref, qg_ref, kg_ref, cos_ref, sin_ref, bias_ref, o_ref,
             qs, ks, vs, *, t_ctx, need_ctx):
    t_tot = q_ref.shape[0]
    t_lat = t_tot - t_ctx
    rows = t_lat // GRID_W
    win_r = min(NA_WIN_ROWS, rows)
    scale = HEAD_DIM ** -0.5
    seg = _head_seg(LANES)
    lane = _iota((1, LANES), 1)
    head_masks = [jnp.where(lane // HEAD_DIM == h, 1.0, 0.0) for h in range(LANES // HEAD_DIM)]
    half = (lane % HEAD_DIM) < HEAD_DIM // 2
    blk = 256

    def norm_block(i, carry):
        sl = pl.ds(pl.multiple_of(i * blk, blk), blk)
        q = q_ref[sl, :]
        k = k_ref[sl, :]
        q = q * lax.rsqrt(_mm((q * q).astype(BF16), seg) * (1.0 / HEAD_DIM) + EPS) * qg_ref[...]
        k = k * lax.rsqrt(_mm((k * k).astype(BF16), seg) * (1.0 / HEAD_DIM) + EPS) * kg_ref[...]
        cos = cos_ref[sl, :]
        sin = sin_ref[sl, :]

        def rope(t):
            swapped = jnp.where(half, pltpu.roll(t, LANES - HEAD_DIM // 2, 1), pltpu.roll(t, HEAD_DIM // 2, 1))
            return t * cos + swapped * sin

        qs[sl, :] = (rope(q) * scale).astype(BF16)
        ks[sl, :] = rope(k).astype(BF16)
        vs[sl, :] = v_ref[sl, :].astype(BF16)
        return carry

    lax.fori_loop(0, t_tot // blk, norm_block, 0)

    masks_bf = [mh.astype(BF16) for mh in head_masks]

    def by_head(q):
        return jnp.concatenate([q * mb for mb in masks_bf], axis=0)

    def merge_heads(o, n):
        return sum(o[h * n:(h + 1) * n] * mh for h, mh in enumerate(head_masks))

    if need_ctx:
        s = _nt(by_head(qs[0:t_ctx, :]), ks[0:t_ctx, :])
        p = jnp.exp(s - jnp.max(s, axis=-1, keepdims=True))
        o = _mm(p.astype(BF16), vs[0:t_ctx, :]) / jnp.sum(p, axis=-1, keepdims=True)
        o_ref[0:t_ctx, :] = merge_heads(o, t_ctx)
    else:
        o_ref[0:t_ctx, :] = jnp.zeros((t_ctx, LANES), F32)

    def q_rows(it, carry):
        rs = [it * NA_ROW_UNROLL + j for j in range(NA_ROW_UNROLL)]
        starts = [jnp.clip(r - win_r // 2, 0, rows - win_r) for r in rs]
        q_sl = [pl.ds(pl.multiple_of(t_ctx + r * GRID_W, GRID_W), GRID_W) for r in rs]
        k_sl = [pl.ds(pl.multiple_of(t_ctx + r0 * GRID_W, GRID_W), win_r * GRID_W) for r0 in starts]
        qs_ = [by_head(qs[sl, :]) for sl in q_sl]
        s_win = [_nt(q, ks[sl, :]) + bias_ref[r - r0] for q, sl, r, r0 in zip(qs_, k_sl, rs, starts)]
        s_ctx = [_nt(q, ks[0:t_ctx, :]) for q in qs_]
        ms = [jnp.maximum(jnp.max(a, axis=-1, keepdims=True), jnp.max(b, axis=-1, keepdims=True))
              for a, b in zip(s_win, s_ctx)]
        p_win = [jnp.exp(a - m) for a, m in zip(s_win, ms)]
        p_ctx = [jnp.exp(b - m) for b, m in zip(s_ctx, ms)]
        den = [jnp.sum(a, axis=-1, keepdims=True) + jnp.sum(b, axis=-1, keepdims=True) for a, b in zip(p_win, p_ctx)]
        o_win = [_mm(a.astype(BF16), vs[sl, :]) for a, sl in zip(p_win, k_sl)]
        o_ctx = [_mm(b.astype(BF16), vs[0:t_ctx, :]) for b in p_ctx]
        for sl, a, b, d in zip(q_sl, o_win, o_ctx, den):
            o_ref[sl, :] = merge_heads((a + b) / d, GRID_W)
        return carry

    lax.fori_loop(0, rows // NA_ROW_UNROLL, q_rows, 0)


def _na_tables(t_ctx, t_lat):
    quarter = HEAD_DIM // 4
    pos = np.arange(t_lat)
    inv = ROPE_THETA ** (-np.arange(quarter, dtype=np.float32) / quarter)
    pos_r = (pos // GRID_W).astype(np.float32)
    pos_c = (pos % GRID_W).astype(np.float32)
    return pos_r, pos_c, inv


def _na_bias_table(rpb, rows):
    win_r = min(NA_WIN_ROWS, rows)
    c = np.arange(GRID_W)
    w_start = np.clip(c - NA_WIN_COLS // 2, 0, GRID_W - NA_WIN_COLS)
    kc = np.arange(GRID_W)
    in_win = (kc[None, :] >= w_start[:, None]) & (kc[None, :] < w_start[:, None] + NA_WIN_COLS)
    col_idx = np.clip(kc[None, :] - c[:, None] + NA_WIN_COLS - 1, 0, 2 * NA_WIN_COLS - 2)
    n_col = 2 * NA_WIN_COLS - 1
    onehot = jnp.asarray(col_idx[None] == np.arange(n_col)[:, None, None], F32)
    band = jnp.einsum('...hrj,jck->...hrck', rpb.astype(F32), onehot, precision=_HI)
    band = jnp.where(in_win, band, NEG_INF)
    per_off = [band[..., NA_WIN_ROWS - 1 - off:NA_WIN_ROWS - 1 - off + win_r, :, :] for off in range(win_r)]
    g = jnp.stack(per_off, axis=-4)
    g = jnp.swapaxes(g, -3, -2)
    return g.reshape(g.shape[:-2] + (win_r * GRID_W,))


def _na(pc, q_gain, k_gain, bias, layer, *, t_ctx, need_ctx):
    bsz, t_tot, _ = pc.shape
    t_lat = t_tot - t_ctx
    rows = t_lat // GRID_W
    win_r = min(NA_WIN_ROWS, rows)
    n_hp = C_WIDTH // LANES
    hpl = LANES // HEAD_DIM
    pos_r, pos_c, inv = _na_tables(t_ctx, t_lat)
    ang = np.concatenate([pos_r[:, None] * inv, pos_c[:, None] * inv], axis=-1)
    ang = np.concatenate([np.zeros((t_ctx, HEAD_DIM // 2), np.float32), ang], axis=0)
    cos = np.cos(ang)
    sin = np.sin(ang)
    cos_t = jnp.asarray(np.tile(np.concatenate([cos, cos], axis=-1), (1, hpl)), F32)
    sin_t = jnp.asarray(np.tile(np.concatenate([-sin, sin], axis=-1), (1, hpl)), F32)
    bias = bias.reshape(bias.shape[0], n_hp, hpl, win_r, GRID_W, win_r * GRID_W)
    bias = jnp.swapaxes(bias, 2, 3).reshape(bias.shape[0], n_hp, win_r, hpl * GRID_W, win_r * GRID_W)

    def sec(s):
        return pl.BlockSpec((None, t_tot, LANES), lambda b, h, s=s: (b, 0, s * n_hp + h))

    gain = lambda g: jnp.tile(g.reshape(1, HEAD_DIM), (1, hpl))
    return pl.pallas_call(
        functools.partial(_na_body, t_ctx=t_ctx, need_ctx=need_ctx),
        grid=(bsz, n_hp),
        in_specs=[sec(0), sec(1), sec(2),
                  pl.BlockSpec((1, LANES), lambda b, h: (0, 0)),
                  pl.BlockSpec((1, LANES), lambda b, h: (0, 0)),
                  pl.BlockSpec((t_tot, LANES), lambda b, h: (0, 0)),
                  pl.BlockSpec((t_tot, LANES), lambda b, h: (0, 0)),
                  pl.BlockSpec((None, None, win_r, hpl * GRID_W, win_r * GRID_W),
                               lambda b, h: (layer, h, 0, 0, 0))],
        out_specs=pl.BlockSpec((None, t_tot, LANES), lambda b, h: (b, 0, h)),
        out_shape=jax.ShapeDtypeStruct((bsz, t_tot, C_WIDTH), F32),
        scratch_shapes=[pltpu.VMEM((t_tot, LANES), BF16)] * 3,
        compiler_params=_cparams("arbitrary", "arbitrary"),
        name="na",
    )(pc, pc, pc, gain(q_gain), gain(k_gain), cos_t, sin_t, bias)


def _outproj_body(x_ref, a_ref, by_ref, bg_ref, bb_ref, lng_ref, lnb_ref, c_ref, *refs, per):
    m_refs = refs[:per]
    g_ref, w_ref, wr_ref, br_ref, xo_ref, h_ref, route_ref = refs[per:]
    tt = TOKEN_TILE
    b_mix = _rwkv_readout(by_ref[...], bg_ref[...], bb_ref[...], lng_ref[...], lnb_ref[...])
    mix = (_mm(a_ref[...].astype(BF16), w_ref[0:A_WIDTH, :])
           + _mm(b_mix.astype(BF16), w_ref[A_WIDTH:A_WIDTH + B_WIDTH, :])
           + _mm(c_ref[...].astype(BF16), w_ref[A_WIDTH + B_WIDTH:, :]))
    hs = []
    for j, m_ref in enumerate(m_refs):
        rows = slice(j * tt, (j + 1) * tt)
        x = x_ref[rows, :] + m_ref[2:3, :] * mix[rows]
        xo_ref[rows, :] = x
        h = x * lax.rsqrt(jnp.mean(x * x, axis=-1, keepdims=True) + EPS) * g_ref[...]
        h = h * (1.0 + m_ref[4:5, :]) + m_ref[3:4, :]
        h_ref[rows, :] = h
        hs.append(h)
    h = jnp.concatenate(hs, axis=0)
    logits = _mm_3pass(h, wr_ref[...]) + br_ref[...]
    lane = _iota(logits.shape, 1).astype(F32)
    big = float(LANES)
    is_g = (lane >= N_EXPERTS) & (lane < N_EXPERTS + N_GROUPS)
    gl = jnp.where(is_g, logits, -jnp.inf)
    gmax = jnp.max(gl, axis=-1, keepdims=True)
    g_w = 1.0 / jnp.sum(jnp.exp(gl - gmax), axis=-1, keepdims=True)
    g_sel = jnp.min(jnp.where(gl == gmax, lane, big), axis=-1, keepdims=True) - N_EXPERTS
    lo = g_sel * EXPERTS_PER_GROUP
    el = jnp.where((lane >= lo) & (lane < lo + EXPERTS_PER_GROUP), logits, -jnp.inf)
    m1 = jnp.max(el, axis=-1, keepdims=True)
    i1 = jnp.min(jnp.where(el == m1, lane, big), axis=-1, keepdims=True)
    el2 = jnp.where(lane == i1, -jnp.inf, el)
    m2 = jnp.max(el2, axis=-1, keepdims=True)
    i2 = jnp.min(jnp.where(el2 == m2, lane, big), axis=-1, keepdims=True)
    e2 = jnp.exp(m2 - m1)
    w1 = g_w / (1.0 + e2)
    route_ref[...] = (jnp.where(lane == 0.0, i1, 0.0) + jnp.where(lane == 1.0, i2, 0.0)
                      + jnp.where(lane == 2.0, w1, 0.0) + jnp.where(lane == 3.0, w1 * e2, 0.0))


def _outproj(x, a, b_scan, b_gate, b_bonus, ln_gain, ln_bias, c, mod_l, gain2, w_bf, w_router, b_router, *,
             bps, ctx_blocks, ctx_row):
    n = x.shape[0]
    per = PROJ_TILES
    tm = per * TOKEN_TILE
    tok = lambda w: pl.BlockSpec((tm, w), lambda i: (i, 0))
    full = lambda shape: pl.BlockSpec(shape, lambda i: (0,) * len(shape))
    return pl.pallas_call(
        functools.partial(_outproj_body, per=per),
        grid=(n // tm,),
        in_specs=[tok(D_MODEL), tok(A_WIDTH), tok(B_WIDTH), tok(B_WIDTH), tok(B_WIDTH),
                  full((1, B_WIDTH)), full((1, B_WIDTH)), tok(C_WIDTH),
                  *_mod_specs(per, bps, ctx_blocks, ctx_row),
                  full((1, D_MODEL)), full((D_MODEL, D_MODEL)), full((D_MODEL, LANES)), full((1, LANES))],
        out_specs=[tok(D_MODEL), tok(D_MODEL), tok(LANES)],
        out_shape=[jax.ShapeDtypeStruct((n, D_MODEL), F32),
                   jax.ShapeDtypeStruct((n, D_MODEL), F32),
                   jax.ShapeDtypeStruct((n, LANES), F32)],
        compiler_params=_cparams("arbitrary"),
        name="outproj",
    )(x, a, b_scan, b_gate, b_bonus, ln_gain.reshape(1, B_WIDTH), ln_bias.reshape(1, B_WIDTH), c,
      *([mod_l] * per), gain2.reshape(1, D_MODEL), w_bf, w_router, b_router)


def _route_tables(route, bsz, t_tot, skip_rows):
    n_pairs = TOP_K * t_tot
    ids = route[:, 0:TOP_K].astype(jnp.int32).reshape(bsz, t_tot, TOP_K)
    wts = route[:, TOP_K:2 * TOP_K].reshape(bsz, t_tot, TOP_K)
    ids = jnp.where(jnp.arange(t_tot)[None, :, None] < skip_rows, N_EXPERTS, ids)
    ids = jnp.swapaxes(ids, 1, 2).reshape(bsz, n_pairs)
    wts = jnp.swapaxes(wts, 1, 2).reshape(bsz, n_pairs)
    key = ids * n_pairs + jnp.arange(n_pairs, dtype=jnp.int32)
    key, wts = lax.sort((key, wts), dimension=1, num_keys=1)
    tok = (key % n_pairs) % t_tot
    count = jnp.sum((ids[:, :, None] == jnp.arange(N_EXPERTS, dtype=jnp.int32)).astype(jnp.int32), axis=1)
    start = jnp.cumsum(count, axis=1) - count
    pad = ((0, 0), (0, MOE_TILE))
    return (jnp.pad(tok, pad).reshape(-1), jnp.pad(wts, pad).reshape(-1), start.reshape(-1), count.reshape(-1))


def _moe_body(tok_ref, w_ref, start_ref, count_ref, h_ref, wg_ref, wu_ref, wd_ref, y_ref, hbuf, obuf):
    c = pl.program_id(0)
    step = pl.program_id(1)
    t_tot = h_ref.shape[0]
    p_len = TOP_K * t_tot + MOE_TILE

    @pl.when(jnp.logical_and(c == 0, step == 0))
    def _():
        hbuf[...] = jnp.zeros_like(hbuf)

    @pl.when(step == 0)
    def _():
        def zero(i, carry):
            y_ref[pl.ds(pl.multiple_of(i * TOKEN_TILE, TOKEN_TILE), TOKEN_TILE), :] = jnp.zeros(
                (TOKEN_TILE, D_MODEL), F32)
            return carry
        lax.fori_loop(0, t_tot // TOKEN_TILE, zero, 0)

    for k in range(MOE_EXPERTS_PER_STEP):
        e = step * MOE_EXPERTS_PER_STEP + k
        _moe_expert(tok_ref, w_ref, h_ref, wg_ref.at[k], wu_ref.at[k], wd_ref.at[k], y_ref, hbuf, obuf,
                    count=count_ref[c * N_EXPERTS + e], pair_base=c * p_len + start_ref[c * N_EXPERTS + e])


def _moe_expert(tok_ref, w_ref, h_ref, wg_ref, wu_ref, wd_ref, y_ref, hbuf, obuf, *, count, pair_base):
    tm = MOE_TILE

    def tile(t, carry):
        n_valid = jnp.minimum(tm, count - t * tm)
        pair0 = pair_base + t * tm

        def gather(g, carry):
            for j in range(8):
                tok = tok_ref[pair0 + g * 8 + j]
                hbuf[g, pl.ds(j, 1), :] = h_ref[pl.ds(tok, 1), :]
            return carry
        lax.fori_loop(0, (n_valid + 7) // 8, gather, 0)

        h = hbuf[...].reshape(tm, D_MODEL).astype(BF16)
        gate = _mm(h, wg_ref[...])
        up = _mm(h, wu_ref[...])
        hid = gate * _sigmoid(gate) * up
        obuf[...] = _mm(hid.astype(BF16), wd_ref[...]).reshape(tm // 8, 8, D_MODEL)

        def add_row(r):
            tok = tok_ref[pair0 + r]
            y_ref[pl.ds(tok, 1), :] += w_ref[pair0 + r] * obuf[r // 8, pl.ds(r % 8, 1), :]

        def scatter(g, carry):
            toks = [tok_ref[pair0 + g * 8 + j] for j in range(8)]
            new = [y_ref[pl.ds(toks[j], 1), :] + w_ref[pair0 + g * 8 + j] * obuf[g, pl.ds(j, 1), :]
                   for j in range(8)]
            for j in range(8):
                y_ref[pl.ds(toks[j], 1), :] = new[j]
            return carry
        lax.fori_loop(0, n_valid // 8, scatter, 0)

        def scatter_tail(r, carry):
            add_row(r)
            return carry
        lax.fori_loop(n_valid // 8 * 8, n_valid, scatter_tail, 0)
        return carry

    lax.fori_loop(0, (count + tm - 1) // tm, tile, 0)


def _moe(h, route, wg, wu, wd, layer, *, bsz, t_tot, skip_rows=0):
    tok, w, start, count = _route_tables(route, bsz, t_tot, skip_rows)
    grid_spec = pltpu.PrefetchScalarGridSpec(
        num_scalar_prefetch=4,
        grid=(bsz, N_EXPERTS // MOE_EXPERTS_PER_STEP),
        in_specs=[
            pl.BlockSpec((None, t_tot, D_MODEL), lambda c, e, *_: (c, 0, 0)),
            pl.BlockSpec((None, MOE_EXPERTS_PER_STEP, D_MODEL, D_EXPERT), lambda c, e, *_: (layer, e, 0, 0)),
            pl.BlockSpec((None, MOE_EXPERTS_PER_STEP, D_MODEL, D_EXPERT), lambda c, e, *_: (layer, e, 0, 0)),
            pl.BlockSpec((None, MOE_EXPERTS_PER_STEP, D_EXPERT, D_MODEL), lambda c, e, *_: (layer, e, 0, 0)),
        ],
        out_specs=pl.BlockSpec((None, t_tot, D_MODEL), lambda c, e, *_: (c, 0, 0)),
        scratch_shapes=[pltpu.VMEM((MOE_TILE // 8, 8, D_MODEL), F32)] * 2,
    )
    y = pl.pallas_call(
        _moe_body,
        grid_spec=grid_spec,
        out_shape=jax.ShapeDtypeStruct((bsz, t_tot, D_MODEL), F32),
        compiler_params=_cparams("arbitrary", "arbitrary"),
        name="moe",
    )(tok, w, start, count, h.reshape(bsz, t_tot, D_MODEL), wg, wu, wd)
    return y.reshape(bsz * t_tot, D_MODEL)


def _residual_body(x_ref, y_ref, m_ref, o_ref):
    o_ref[...] = (x_ref[...] + m_ref[5:6, :] * y_ref[...]).astype(o_ref.dtype)


def _final_residual(x, y, mod_l, dtype, *, bsz, t_ctx, t_tot):
    tm = TOKEN_TILE
    tok = pl.BlockSpec((None, tm, D_MODEL), lambda b, i: (b, t_ctx // tm + i, 0))
    return pl.pallas_call(
        _residual_body,
        grid=(bsz, (t_tot - t_ctx) // tm),
        in_specs=[tok, tok, pl.BlockSpec((None, 6, D_MODEL), lambda b, i: (b, 0, 0))],
        out_specs=pl.BlockSpec((None, tm, D_MODEL), lambda b, i: (b, i, 0)),
        out_shape=jax.ShapeDtypeStruct((bsz, t_tot - t_ctx, D_MODEL), dtype),
        compiler_params=_cparams("arbitrary", "arbitrary"),
        name="residual",
    )(x.reshape(bsz, t_tot, D_MODEL), y.reshape(bsz, t_tot, D_MODEL), mod_l)


def kernel(x, c, ctx, c_ctx, norm1_gain, norm2_gain, w_ada, b_ada, w_in, w_out, hgrn_lb_logits, hgrn_gn_gain, rwkv_mu, rwkv_w0, rwkv_w_up, rwkv_a0, rwkv_a_up, rwkv_g_up, rwkv_kk_scale, rwkv_k_a, rwkv_r_k, rwkv_ln_gain, rwkv_ln_bias, na_q_gain, na_k_gain, na_rpb, w_router_group, b_router_group, w_router_expert, b_router_expert, w_exp_gate, w_exp_up, w_exp_down):
    bsz, t_lat, _ = x.shape
    t_ctx = ctx.shape[1]
    t_tot = t_ctx + t_lat
    n = bsz * t_tot
    assert t_ctx % TOKEN_TILE == 0 and t_lat % TOKEN_TILE == 0 and bsz < 16
    bps = t_tot // TOKEN_TILE
    tile_kw = dict(bps=bps, ctx_blocks=t_ctx // TOKEN_TILE, ctx_row=bsz)

    lb_p = jax.nn.softmax(hgrn_lb_logits.astype(F32), axis=1)
    lower_bounds = jnp.cumsum(lb_p, axis=1) - lb_p[:, :1]

    cc = jnp.zeros((16, D_MODEL), F32).at[:bsz].set(c.astype(F32)).at[bsz].set(c_ctx.astype(F32))
    mod = _ada_mod(cc, w_ada, b_ada).reshape(DEPTH, 16, 6, D_MODEL)

    w_in_bf = w_in.astype(BF16)
    w_out_bf = w_out.astype(BF16)
    wg_bf = w_exp_gate.astype(BF16).reshape(DEPTH, N_EXPERTS, D_MODEL, D_EXPERT)
    wu_bf = w_exp_up.astype(BF16).reshape(DEPTH, N_EXPERTS, D_MODEL, D_EXPERT)
    wd_bf = w_exp_down.astype(BF16).reshape(DEPTH, N_EXPERTS, D_EXPERT, D_MODEL)
    pad = LANES - N_EXPERTS - N_GROUPS
    w_router = jnp.concatenate([w_router_expert, w_router_group,
                                jnp.zeros((DEPTH, D_MODEL, pad), F32)], axis=-1)
    b_router = jnp.concatenate([b_router_expert, b_router_group, jnp.zeros((DEPTH, pad), F32)], axis=-1)

    na_bias = _na_bias_table(na_rpb, t_lat // GRID_W)
    xs = jnp.concatenate([ctx.astype(F32), x.astype(F32)], axis=1).reshape(n, D_MODEL)
    y_moe = None
    for layer in range(DEPTH):
        last = layer == DEPTH - 1
        if layer == 0:
            pa, pb, pc = _inproj(xs, mod[layer], norm1_gain[layer], w_in_bf[layer], **tile_kw)
        else:
            xs, pa, pb, pc = _inproj(xs, mod[layer], norm1_gain[layer], w_in_bf[layer],
                                     (y_moe, mod[layer - 1]), **tile_kw)
        a_mix = _hgrn(pa.reshape(bsz, t_tot, A_PROJ), lower_bounds[:, layer], hgrn_gn_gain[layer], t_ctx=t_ctx)
        prep = _rwkv_prep(pb.reshape(bsz, t_tot, B_PROJ), rwkv_mu[layer], rwkv_w0[layer], rwkv_w_up[layer],
                          rwkv_a0[layer], rwkv_a_up[layer], rwkv_g_up[layer], rwkv_kk_scale[layer],
                          rwkv_k_a[layer], rwkv_r_k[layer], t_ctx=t_ctx)
        y = _rwkv_scan(*prep[:9], t_ctx=t_ctx)
        c_mix = _na(pc.reshape(bsz, t_tot, C_PROJ), na_q_gain[layer], na_k_gain[layer], na_bias, layer,
                    t_ctx=t_ctx, need_ctx=not last)
        xs, h2, route = _outproj(xs, a_mix.reshape(n, A_WIDTH), y.reshape(n, B_WIDTH), prep[9].reshape(n, B_WIDTH),
                                 prep[10].reshape(n, B_WIDTH), rwkv_ln_gain[layer], rwkv_ln_bias[layer],
                                 c_mix.reshape(n, C_WIDTH), mod[layer], norm2_gain[layer], w_out_bf[layer],
                                 w_router[layer], b_router[layer].reshape(1, LANES), **tile_kw)
        y_moe = _moe(h2, route, wg_bf, wu_bf, wd_bf, layer, bsz=bsz, t_tot=t_tot, skip_rows=t_ctx if last else 0)
    return _final_residual(xs, y_moe, mod[DEPTH - 1], x.dtype, bsz=bsz, t_ctx=t_ctx, t_tot=t_tot)
```

```python
import functools

import jax
import jax.numpy as jnp
import numpy as np
from jax import lax
from jax.experimental import pallas as pl
from jax.experimental.pallas import tpu as pltpu

F32 = jnp.float32
BF16 = jnp.bfloat16

D_MODEL = 1024
DEPTH = 4
GRID_W = 64
HEAD_DIM = 64
A_HEADS = 4
B_HEADS = 4
C_HEADS = 8
A_WIDTH = A_HEADS * HEAD_DIM
B_WIDTH = B_HEADS * HEAD_DIM
C_WIDTH = C_HEADS * HEAD_DIM
HGRN_CHUNK = 16
HGRN_F_FLOOR = 1e-20
RWKV_W_RANK = 64
RWKV_A_RANK = 64
RWKV_G_RANK = 128
RWKV_LN_EPS = 64e-5
NA_WIN_ROWS = 8
NA_WIN_COLS = 16
ROPE_THETA = 10000.0
N_GROUPS = 4
EXPERTS_PER_GROUP = 8
N_EXPERTS = N_GROUPS * EXPERTS_PER_GROUP
TOP_K = 2
D_EXPERT = 512
EPS = 1e-6
LOG2E = 1.4426950408889634
NEG_INF = -1e30
A_PROJ = 5 * A_WIDTH
B_PROJ = 3 * B_WIDTH + 2 * RWKV_W_RANK + 2 * RWKV_A_RANK + RWKV_G_RANK
C_PROJ = 3 * C_WIDTH
P_TOTAL = A_PROJ + B_PROJ + C_PROJ
B_TAIL = B_PROJ - 3 * B_WIDTH

LANES = 128
PROJ_TILES = 2
TOKEN_TILE = 256
RWKV_CHUNK = 64
RWKV_BLOCKS_PER_ITER = 2
RWKV_BLOCK = 128
HGRN_BLOCK = 128
NA_ROW_UNROLL = 8
HGRN_SCAN_UNROLL = 8
MOE_EXPERTS_PER_STEP = 2
MOE_TILE = 192
VMEM_LIMIT = 56 * 1024 * 1024

_HI = lax.Precision.HIGHEST


def _cparams(*sem):
    return pltpu.CompilerParams(dimension_semantics=sem, vmem_limit_bytes=VMEM_LIMIT)


def _mm(a, b):
    return jnp.dot(a, b, preferred_element_type=F32)


def _nt(a, b):
    return lax.dot_general(a, b, (((1,), (1,)), ((), ())), preferred_element_type=F32)


def _tn(a, b):
    return lax.dot_general(a, b, (((0,), (0,)), ((), ())), preferred_element_type=F32)


def _split3(a):
    a1 = a.astype(BF16)
    r1 = a - a1.astype(F32)
    a2 = r1.astype(BF16)
    a3 = (r1 - a2.astype(F32)).astype(BF16)
    return a1, a2, a3


def _mm_x3(a, m):
    a1, a2, a3 = _split3(a)
    return _mm(a1, m) + _mm(a2, m) + _mm(a3, m)


def _mm_l3(m, a):
    a1, a2, a3 = _split3(a)
    return _mm(m, a1) + _mm(m, a2) + _mm(m, a3)


def _mm_3pass(a, b):
    a1 = a.astype(BF16)
    a2 = (a - a1.astype(F32)).astype(BF16)
    b1 = b.astype(BF16)
    b2 = (b - b1.astype(F32)).astype(BF16)
    return _mm(a1, b1) + (_mm(a1, b2) + _mm(a2, b1))


def _sigmoid(x):
    return 1.0 / (1.0 + jnp.exp(-x))


def _iota(shape, dim):
    return lax.broadcasted_iota(jnp.int32, shape, dim)


def _head_seg(n):
    return jnp.where(_iota((n, n), 0) // HEAD_DIM == _iota((n, n), 1) // HEAD_DIM, 1.0, 0.0).astype(BF16)


def _ada_body(c_ref, w_ref, b_ref, o_ref):
    c = c_ref[...]
    o_ref[...] = _mm_3pass(c * _sigmoid(c), w_ref[...]) + b_ref[...]


def _ada_mod(cc, w_ada, b_ada):
    rows = cc.shape[0]
    tn = 1536
    return pl.pallas_call(
        _ada_body,
        grid=(DEPTH, 6 * D_MODEL // tn),
        in_specs=[
            pl.BlockSpec((rows, D_MODEL), lambda l, j: (0, 0)),
            pl.BlockSpec((None, D_MODEL, tn), lambda l, j: (l, 0, j)),
            pl.BlockSpec((None, 1, tn), lambda l, j: (l, 0, j)),
        ],
        out_specs=pl.BlockSpec((None, rows, tn), lambda l, j: (l, 0, j)),
        out_shape=jax.ShapeDtypeStruct((DEPTH, rows, 6 * D_MODEL), F32),
        compiler_params=_cparams("arbitrary", "arbitrary"),
        name="ada_mod",
    )(cc, w_ada, b_ada.reshape(DEPTH, 1, 6 * D_MODEL))


def _mod_row(i, blocks_per_seq, ctx_blocks, ctx_row):
    return jnp.where(i % blocks_per_seq < ctx_blocks, ctx_row, i // blocks_per_seq)


def _inproj_body(*refs, with_moe, per):
    tt = TOKEN_TILE
    if with_moe:
        x_ref, y_ref = refs[:2]
        mp_refs, m_refs = refs[2:2 + per], refs[2 + per:2 + 2 * per]
        g_ref, w_ref, xo_ref, pa_ref, pb_ref, pc_ref = refs[2 + 2 * per:]
    else:
        x_ref = refs[0]
        m_refs = refs[1:1 + per]
        g_ref, w_ref, pa_ref, pb_ref, pc_ref = refs[1 + per:]
    hs = []
    for j, m_ref in enumerate(m_refs):
        rows = slice(j * tt, (j + 1) * tt)
        x = x_ref[rows, :]
        if with_moe:
            x = x + mp_refs[j][5:6, :] * y_ref[rows, :]
            xo_ref[rows, :] = x
        h = x * lax.rsqrt(jnp.mean(x * x, axis=-1, keepdims=True) + EPS) * g_ref[...]
        hs.append((h * (1.0 + m_ref[1:2, :]) + m_ref[0:1, :]).astype(BF16))
    h = jnp.concatenate(hs, axis=0)
    pa_ref[...] = _mm(h, w_ref[:, :A_PROJ])
    pb_ref[...] = _mm(h, w_ref[:, A_PROJ:A_PROJ + B_PROJ])
    pc_ref[...] = _mm(h, w_ref[:, A_PROJ + B_PROJ:])


def _mod_specs(per, bps, ctx_blocks, ctx_row):
    row = functools.partial(_mod_row, blocks_per_seq=bps, ctx_blocks=ctx_blocks, ctx_row=ctx_row)
    return [pl.BlockSpec((None, 6, D_MODEL), lambda i, j=j: (row(i * per + j), 0, 0)) for j in range(per)]


def _inproj(x, mod_l, gain, w_bf, moe=None, *, bps, ctx_blocks, ctx_row):
    n = x.shape[0]
    per = PROJ_TILES
    tm = per * TOKEN_TILE
    tok = lambda w: pl.BlockSpec((tm, w), lambda i: (i, 0))
    mods = _mod_specs(per, bps, ctx_blocks, ctx_row)
    with_moe = moe is not None
    ins = (x, moe[0], *([moe[1]] * per)) if with_moe else (x,)
    return pl.pallas_call(
        functools.partial(_inproj_body, with_moe=with_moe, per=per),
        grid=(n // tm,),
        in_specs=[tok(D_MODEL)] + ([tok(D_MODEL)] + mods if with_moe else []) + mods + [
            pl.BlockSpec((1, D_MODEL), lambda i: (0, 0)),
            pl.BlockSpec((D_MODEL, P_TOTAL), lambda i: (0, 0)),
        ],
        out_specs=([tok(D_MODEL)] if with_moe else []) + [tok(A_PROJ), tok(B_PROJ), tok(C_PROJ)],
        out_shape=([jax.ShapeDtypeStruct((n, D_MODEL), F32)] if with_moe else []) + [
            jax.ShapeDtypeStruct((n, A_PROJ), F32),
            jax.ShapeDtypeStruct((n, B_PROJ), F32),
            jax.ShapeDtypeStruct((n, C_PROJ), F32),
        ],
        compiler_params=_cparams("arbitrary"),
        name="inproj",
    )(*ins, *([mod_l] * per), gain.reshape(1, D_MODEL), w_bf)


def _hgrn_body(q_ref, ff_ref, fb_ref, i_ref, g_ref, lb_ref, gn_ref, o_ref,
               qin_s, kout_s, cum_s, acc_s, kpad_s, cpad_s, vpad_s, *, t_ctx):
    t_tot = q_ref.shape[0]
    c = HGRN_CHUNK
    rb = HGRN_BLOCK
    n_blk = t_tot // rb
    n_chunk = t_tot // c
    n_chunk_ctx = t_ctx // c
    row = _iota((rb, rb), 0)
    col = _iota((rb, rb), 1)
    same = (row // c) == (col // c)
    same_bf = jnp.where(same, 1.0, 0.0).astype(BF16)
    seg = _head_seg(LANES)
    pos = _iota((rb, LANES), 0) % c
    blockdiag = _iota((LANES, LANES), 0) // HEAD_DIM == _iota((LANES, LANES), 1) // HEAD_DIM

    pad = HGRN_CHUNK
    zpad = jnp.zeros((pad, LANES), F32)
    for d in range(2):
        for ref in (kpad_s, cpad_s):
            ref[d, 0:pad, :] = zpad
            ref[d, pad + t_tot:, :] = zpad
    vpad_s[0:pad, :] = zpad
    vpad_s[pad + t_tot:, :] = zpad
    tris = [jnp.where(same & ((col >= row) if d == 1 else (col <= row)), 1.0, 0.0).astype(BF16) for d in range(2)]

    def gates(b, carry):
        sl = pl.ds(pl.multiple_of(b * rb, rb), rb)
        slp = pl.ds(pl.multiple_of(b * rb, rb) + pad, rb)
        q = q_ref[sl, :]
        vpad_s[slp, :] = i_ref[sl, :]
        for d, f_ref in enumerate((ff_ref, fb_ref)):
            lb = lb_ref[d:d + 1, :]
            fpre = f_ref[sl, :]
            f = lb + (1.0 - lb) * _sigmoid(fpre)
            logf = jnp.log(jnp.maximum(f, HGRN_F_FLOOR))
            k = (1.0 - lb) * _sigmoid(-fpre)
            cum = _mm_l3(tris[d], logf)
            tot = _mm_l3(same_bf, logf)
            kpad_s[d, slp, :] = k
            cpad_s[d, slp, :] = cum * LOG2E
            qin_s[d, sl, :] = q * jnp.exp(cum)
            kout_s[d, sl, :] = k * jnp.exp(tot - cum)
            cum_s[d, sl, :] = tot
        return carry

    lax.fori_loop(0, n_blk, gates, 0)

    def intra(b, carry):
        r0 = pl.multiple_of(b * rb, rb)
        sl = pl.ds(r0, rb)
        q = q_ref[sl, :]
        cums = [cpad_s[d, pl.ds(r0 + pad, rb), :] for d in range(2)]
        o = jnp.zeros((rb, LANES), F32)
        for j in range(c):
            for d in range(2):
                off = r0 + pad + (j if d == 1 else -j)
                valid = (pos <= c - 1 - j) if d == 1 else (pos >= j)
                ks = kpad_s[d, pl.ds(off, rb), :]
                cs = cpad_s[d, pl.ds(off, rb), :]
                vs = vpad_s[pl.ds(off, rb), :]
                prod = jnp.where(valid, q * ks * jnp.exp2(cums[d] - cs), 0.0)
                o = o + _mm(prod.astype(BF16), seg) * vs
        acc_s[sl, :] = o
        return carry

    lax.fori_loop(0, n_blk, intra, 0)

    def step(it, carry):
        idx = [it * HGRN_SCAN_UNROLL + j for j in range(HGRN_SCAN_UNROLL)]
        order = [(i, jnp.where(i < n_chunk_ctx, n_chunk_ctx - 1 - i, n_chunk - 1 - (i - n_chunk_ctx))) for i in idx]
        sls = [[pl.ds(pl.multiple_of(n[d] * c, c), c) for d in range(2)] for n in order]
        kvs = [[jnp.where(blockdiag, _tn(i_ref[sl[d], :].astype(BF16), kout_s[d, sl[d], :].astype(BF16)), 0.0)
                for d in range(2)] for sl in sls]
        decs = [[jnp.exp(cum_s[d, pl.ds(pl.multiple_of(n[d] * c, c), 1), :]) for d in range(2)] for n in order]
        states = list(carry)
        for j in range(HGRN_SCAN_UNROLL):
            for d in range(2):
                acc_s[sls[j][d], :] += _nt(qin_s[d, sls[j][d], :].astype(BF16), states[d].astype(BF16))
                states[d] = states[d] * decs[j][d] + kvs[j][d]
        return tuple(states)

    zero = jnp.zeros((LANES, LANES), F32)
    lax.fori_loop(0, n_chunk // HGRN_SCAN_UNROLL, step, (zero, zero))

    def readout(b, carry):
        sl = pl.ds(pl.multiple_of(b * rb, rb), rb)
        o = acc_s[sl, :]
        ms = _mm((o * o).astype(BF16), seg) * (1.0 / HEAD_DIM)
        g = g_ref[sl, :]
        o_ref[sl, :] = o * lax.rsqrt(ms + EPS) * gn_ref[...] * (g * _sigmoid(g))
        return carry

    lax.fori_loop(0, n_blk, readout, 0)


def _hgrn(pa, lb, gn_gain, *, t_ctx):
    bsz, t_tot, _ = pa.shape
    n_hp = A_WIDTH // LANES

    def sec(s):
        return pl.BlockSpec((None, t_tot, LANES), lambda b, h, s=s: (b, 0, s * n_hp + h))

    return pl.pallas_call(
        functools.partial(_hgrn_body, t_ctx=t_ctx),
        grid=(bsz, n_hp),
        in_specs=[sec(0), sec(1), sec(2), sec(3), sec(4),
                  pl.BlockSpec((2, LANES), lambda b, h: (0, h)),
                  pl.BlockSpec((1, LANES), lambda b, h: (0, h))],
        out_specs=pl.BlockSpec((None, t_tot, LANES), lambda b, h: (b, 0, h)),
        out_shape=jax.ShapeDtypeStruct((bsz, t_tot, A_WIDTH), F32),
        scratch_shapes=[
            pltpu.VMEM((2, t_tot, LANES), F32),
            pltpu.VMEM((2, t_tot, LANES), F32),
            pltpu.VMEM((2, t_tot, LANES), F32),
            pltpu.VMEM((t_tot, LANES), F32),
            pltpu.VMEM((2, t_tot + 2 * HGRN_CHUNK, LANES), F32),
            pltpu.VMEM((2, t_tot + 2 * HGRN_CHUNK, LANES), F32),
            pltpu.VMEM((t_tot + 2 * HGRN_CHUNK, LANES), F32),
        ],
        compiler_params=_cparams("arbitrary", "arbitrary"),
        name="hgrn",
    )(pa, pa, pa, pa, pa, lb, gn_gain.reshape(1, A_WIDTH))


def _rwkv_prep_body(x_ref, xp_ref, xn_ref, mu_ref, w0_ref, wup_ref, a0_ref, aup_ref, gup_ref,
                    kks_ref, ka_ref, rk_ref,
                    r_o, v_o, kk_o, kf_o, kb_o, bf_o, bb_o, lwf_o, lwb_o, g_o, bonus_o, *, t_ctx, t_tot):
    tm = x_ref.shape[0]
    r0 = pl.program_id(1) * tm
    prev_ok = jnp.logical_and(r0 != 0, r0 != t_ctx)
    next_ok = jnp.logical_and(r0 + tm != t_ctx, r0 + tm != t_tot)
    first = _iota((tm, 1), 0) == 0
    last = _iota((tm, 1), 0) == tm - 1

    def shifted(lo, hi):
        x = x_ref[:, lo:hi]
        p_row = jnp.where(prev_ok, xp_ref[7:8, lo:hi], 0.0)
        n_row = jnp.where(next_ok, xn_ref[0:1, lo:hi], 0.0)
        prev = jnp.where(first, p_row, pltpu.roll(x, 1, 0))
        nxt = jnp.where(last, n_row, pltpu.roll(x, tm - 1, 0))
        return x + (0.5 * (prev + nxt) - x) * mu_ref[:, lo:hi]

    bw = B_WIDTH
    r = shifted(0, bw)
    k = shifted(bw, 2 * bw)
    v = shifted(2 * bw, 3 * bw)
    tail = shifted(3 * bw, B_PROJ)
    seg = _head_seg(bw)

    kk = k * kks_ref[...]
    kk = kk / jnp.maximum(jnp.sqrt(_mm_x3(kk * kk, seg)), 1e-12)
    ksum = jnp.zeros_like(k)
    for d, (k_o, b_o, lw_o) in enumerate(((kf_o, bf_o, lwf_o), (kb_o, bb_o, lwb_o))):
        wd = tail[:, d * RWKV_W_RANK:(d + 1) * RWKV_W_RANK]
        ad = tail[:, 2 * RWKV_W_RANK + d * RWKV_A_RANK:2 * RWKV_W_RANK + (d + 1) * RWKV_A_RANK]
        u = -(w0_ref[d:d + 1, :] + _mm_3pass(jnp.tanh(wd), wup_ref[d]))
        w = -(jnp.maximum(u, 0.0) + jnp.log(1.0 + jnp.exp(-jnp.abs(u)))) - 0.5
        a = _sigmoid(a0_ref[d:d + 1, :] + _mm_3pass(ad, aup_ref[d]))
        k_d = k * (1.0 + (a - 1.0) * ka_ref[...])
        ksum = ksum + k_d
        k_o[...] = k_d
        b_o[...] = a * kk
        lw_o[...] = -jnp.exp(w)
    gd = tail[:, 2 * RWKV_W_RANK + 2 * RWKV_A_RANK:]
    r_o[...] = r
    v_o[...] = v
    kk_o[...] = kk
    g_o[...] = _mm_3pass(_sigmoid(gd), gup_ref[...])
    bonus_o[...] = _mm((r * ksum * rk_ref[...]).astype(BF16), seg) * v


def _rwkv_prep(pb, mu, w0, w_up, a0, a_up, g_up, kk_scale, k_a, r_k, *, t_ctx):
    bsz, t_tot, _ = pb.shape
    tm = TOKEN_TILE
    nb8 = t_tot // 8
    per8 = tm // 8
    bw = B_WIDTH
    full = lambda shape: pl.BlockSpec(shape, lambda b, i: (0,) * len(shape))
    out = jax.ShapeDtypeStruct((bsz, t_tot, bw), F32)
    return pl.pallas_call(
        functools.partial(_rwkv_prep_body, t_ctx=t_ctx, t_tot=t_tot),
        grid=(bsz, t_tot // tm),
        in_specs=[
            pl.BlockSpec((None, tm, B_PROJ), lambda b, i: (b, i, 0)),
            pl.BlockSpec((None, 8, B_PROJ), lambda b, i: (b, jnp.maximum(i * per8 - 1, 0), 0)),
            pl.BlockSpec((None, 8, B_PROJ), lambda b, i: (b, jnp.minimum((i + 1) * per8, nb8 - 1), 0)),
            full((1, B_PROJ)), full((2, bw)), full((2, RWKV_W_RANK, bw)), full((2, bw)),
            full((2, RWKV_A_RANK, bw)), full((RWKV_G_RANK, bw)), full((1, bw)), full((1, bw)), full((1, bw)),
        ],
        out_specs=[pl.BlockSpec((None, tm, bw), lambda b, i: (b, i, 0))] * 11,
        out_shape=[out] * 11,
        compiler_params=_cparams("arbitrary", "arbitrary"),
        name="rwkv_prep",
    )(pb, pb, pb, mu.reshape(1, B_PROJ), w0, w_up, a0, a_up, g_up,
      kk_scale.reshape(1, bw), k_a.reshape(1, bw), r_k.reshape(1, bw))


def _rwkv_scan_body(r_ref, v_ref, kk_ref, kf_ref, kb_ref, bf_ref, bb_ref, lwf_ref, lwb_ref, y_ref,
                    rq_s, y0_s, p_s, z_s, *, t_ctx):
    t_tot = r_ref.shape[0]
    c = RWKV_CHUNK
    rb = RWKV_BLOCK
    n_blk = t_tot // rb
    n_chunk = t_tot // c
    n_chunk_ctx = t_ctx // c
    cpb = rb // c
    n_double = c.bit_length() - 2
    row = _iota((rb, rb), 0)
    col = _iota((rb, rb), 1)
    same = (row // c) == (col // c)
    same_bf = jnp.where(same, 1.0, 0.0).astype(BF16)
    eye = jnp.where(row == col, 1.0, 0.0)
    lane = _iota((1, LANES), 1)
    head_masks = [jnp.where(lane // HEAD_DIM == h, 1.0, 0.0) for h in range(LANES // HEAD_DIM)]
    r128 = _iota((LANES, LANES), 0)
    c128 = _iota((LANES, LANES), 1)
    blockdiag = (r128 // HEAD_DIM) == (c128 // HEAD_DIM)
    diag128 = r128 == c128

    def block(it, carry):
        bis = [it * RWKV_BLOCKS_PER_ITER + j for j in range(RWKV_BLOCKS_PER_ITER)]
        sls = [pl.ds(pl.multiple_of(bi * rb, rb), rb) for bi in bis]
        v_bfs = [v_ref[sl, :].astype(BF16) for sl in sls]
        pre = {}
        for j, sl in enumerate(sls):
            r = r_ref[sl, :]
            kk = kk_ref[sl, :]
            for d in range(2):
                rev = d == 1
                k_ref, b_ref, lw_ref = (kb_ref, bb_ref, lwb_ref) if rev else (kf_ref, bf_ref, lwf_ref)
                incl = same & ((col >= row) if rev else (col <= row))
                strict = same & ((col > row) if rev else (col < row))
                incl_bf = jnp.where(incl, 1.0, 0.0).astype(BF16)
                lw = lw_ref[sl, :]
                k = k_ref[sl, :]
                b = b_ref[sl, :]
                cl = _mm_l3(incl_bf, lw)
                tot = _mm_l3(same_bf, lw)
                w_inv = jnp.exp(-cl)
                w_end = jnp.exp(tot - cl)
                pre[j, d] = dict(incl=incl, strict=strict, tot=tot, kk_d=kk * jnp.exp(cl - lw), r_d=r * jnp.exp(cl),
                                 kb=jnp.concatenate([k * w_inv, b * w_inv], axis=0).astype(BF16),
                                 b_e=(b * w_end).astype(BF16), k_e=(k * w_end).astype(BF16))
        chains = [dict(j=j, d=d, mh=mh, kkm=pre[j, d]['kk_d'] * mh, rm=pre[j, d]['r_d'] * mh)
                  for j in range(len(bis)) for d in range(2) for mh in head_masks]
        for ch in chains:
            p = pre[ch['j'], ch['d']]
            aa = _nt(jnp.concatenate([ch['kkm'], ch['rm']], axis=0).astype(BF16), p['kb'])
            ch['a_kb'] = jnp.where(p['strict'], aa[:rb, :rb], 0.0)
            ch['a_rk'] = jnp.where(p['incl'], aa[rb:, :rb], 0.0)
            ch['a_rb'] = jnp.where(p['incl'], aa[rb:, rb:], 0.0)
            m = -jnp.where(p['strict'], aa[:rb, rb:], 0.0)
            ch['t'] = eye + m
            ch['m'] = m
        for ch in chains:
            m_bf = ch['m'].astype(BF16)
            ch['m'] = _mm(m_bf, m_bf)
        for _ in range(n_double - 1):
            for ch in chains:
                both = _mm(jnp.concatenate([ch['t'], ch['m']], axis=0).astype(BF16), ch['m'].astype(BF16))
                ch['t'] = ch['t'] + both[:rb]
                ch['m'] = both[rb:]
        for ch in chains:
            ch['t'] = ch['t'] + _mm(ch['t'].astype(BF16), ch['m'].astype(BF16))
            ch['av'] = _mm(jnp.concatenate([ch['a_kb'], ch['a_rk']], axis=0).astype(BF16), v_bfs[ch['j']])
        for ch in chains:
            ch['ku'] = _mm(ch['t'].astype(BF16), jnp.concatenate([ch['kkm'], ch['av'][:rb]], axis=1).astype(BF16))
        for ch in chains:
            ch['rb_ku'] = _mm(ch['a_rb'].astype(BF16), ch['ku'].astype(BF16))
        for j, (bi, sl) in enumerate(zip(bis, sls)):
            for d in range(2):
                p = pre[j, d]
                mine = [ch for ch in chains if ch['j'] == j and ch['d'] == d]
                kkt = sum(ch['ku'][:, :LANES] for ch in mine)
                u = sum(ch['ku'][:, LANES:] * ch['mh'] for ch in mine)
                rq_s[d, sl, :] = sum(ch['rm'] - ch['rb_ku'][:, :LANES] for ch in mine)
                y0_s[d, sl, :] = sum((ch['av'][rb:] - ch['rb_ku'][:, LANES:]) * ch['mh'] for ch in mine)
                ktu = jnp.concatenate([kkt, u], axis=1).astype(BF16)
                for n in range(cpb):
                    rows = slice(n * c, (n + 1) * c)
                    wc = jnp.exp(p['tot'][n * c:n * c + 1, :])
                    kub = _tn(ktu[rows], p['b_e'][rows])
                    p_s[d, bi * cpb + n] = jnp.where(diag128, wc, 0.0) - jnp.where(blockdiag, kub[:LANES], 0.0)
                    p_z = jnp.where(blockdiag, _tn(v_bfs[j][rows], p['k_e'][rows]) - kub[LANES:], 0.0)
                    z_s[d, bi * cpb + n] = p_z
        return carry

    lax.fori_loop(0, n_blk // RWKV_BLOCKS_PER_ITER, block, 0)

    y_ref[...] = jnp.zeros_like(y_ref)

    def step(i, carry):
        n_b = jnp.where(i < n_chunk_ctx, n_chunk_ctx - 1 - i, n_chunk - 1 - (i - n_chunk_ctx))
        chunk = (i, n_b)
        s_bf = [s.astype(BF16) for s in carry]
        ys = [_nt(rq_s[d, pl.ds(pl.multiple_of(chunk[d] * c, c), c), :].astype(BF16), s_bf[d]) for d in range(2)]
        states = [_mm(s_bf[d], p_s[d, chunk[d]].astype(BF16)) + z_s[d, chunk[d]] for d in range(2)]
        for d in range(2):
            sl = pl.ds(pl.multiple_of(chunk[d] * c, c), c)
            y_ref[sl, :] += ys[d] + y0_s[d, sl, :]
        return tuple(states)

    zero = jnp.zeros((LANES, LANES), F32)
    lax.fori_loop(0, n_chunk, step, (zero, zero))


def _rwkv_scan(r, v, kk, kf, kb, bf, bb, lwf, lwb, *, t_ctx):
    bsz, t_tot, bw = r.shape
    n_chunk = t_tot // RWKV_CHUNK
    spec = pl.BlockSpec((None, t_tot, LANES), lambda b, h: (b, 0, h))
    return pl.pallas_call(
        functools.partial(_rwkv_scan_body, t_ctx=t_ctx),
        grid=(bsz, bw // LANES),
        in_specs=[spec] * 9,
        out_specs=spec,
        out_shape=jax.ShapeDtypeStruct((bsz, t_tot, bw), F32),
        scratch_shapes=[
            pltpu.VMEM((2, t_tot, LANES), F32),
            pltpu.VMEM((2, t_tot, LANES), F32),
            pltpu.VMEM((2, n_chunk, LANES, LANES), F32),
            pltpu.VMEM((2, n_chunk, LANES, LANES), F32),
        ],
        compiler_params=_cparams("arbitrary", "arbitrary"),
        name="rwkv_scan",
    )(r, v, kk, kf, kb, bf, bb, lwf, lwb)


def _rwkv_readout(y, g, bonus, ln_gain, ln_bias):
    seg = _head_seg(B_WIDTH)
    mean = _mm_x3(y, seg) * (1.0 / HEAD_DIM)
    yc = y - mean
    var = _mm((yc * yc).astype(BF16), seg) * (1.0 / HEAD_DIM)
    return (yc * lax.rsqrt(var + RWKV_LN_EPS) * ln_gain + ln_bias + bonus) * g


def _na_body(q_ref, k_ref, v_ref, qg_ref, kg_ref, cos_ref, sin_ref, bias_ref, o_ref,
             qs, ks, vs, *, t_ctx, need_ctx):
    t_tot = q_ref.shape[0]
    t_lat = t_tot - t_ctx
    rows = t_lat // GRID_W
    win_r = min(NA_WIN_ROWS, rows)
    scale = HEAD_DIM ** -0.5
    seg = _head_seg(LANES)
    lane = _iota((1, LANES), 1)
    head_masks = [jnp.where(lane // HEAD_DIM == h, 1.0, 0.0) for h in range(LANES // HEAD_DIM)]
    half = (lane % HEAD_DIM) < HEAD_DIM // 2
    blk = 256

    def norm_block(i, carry):
        sl = pl.ds(pl.multiple_of(i * blk, blk), blk)
        q = q_ref[sl, :]
        k = k_ref[sl, :]
        q = q * lax.rsqrt(_mm((q * q).astype(BF16), seg) * (1.0 / HEAD_DIM) + EPS) * qg_ref[...]
        k = k * lax.rsqrt(_mm((k * k).astype(BF16), seg) * (1.0 / HEAD_DIM) + EPS) * kg_ref[...]
        cos = cos_ref[sl, :]
        sin = sin_ref[sl, :]

        def rope(t):
            swapped = jnp.where(half, pltpu.roll(t, LANES - HEAD_DIM // 2, 1), pltpu.roll(t, HEAD_DIM // 2, 1))
            return t * cos + swapped * sin

        qs[sl, :] = (rope(q) * scale).astype(BF16)
        ks[sl, :] = rope(k).astype(BF16)
        vs[sl, :] = v_ref[sl, :].astype(BF16)
        return carry

    lax.fori_loop(0, t_tot // blk, norm_block, 0)

    masks_bf = [mh.astype(BF16) for mh in head_masks]

    def by_head(q):
        return jnp.concatenate([q * mb for mb in masks_bf], axis=0)

    def merge_heads(o, n):
        return sum(o[h * n:(h + 1) * n] * mh for h, mh in enumerate(head_masks))

    if need_ctx:
        s = _nt(by_head(qs[0:t_ctx, :]), ks[0:t_ctx, :])
        p = jnp.exp(s - jnp.max(s, axis=-1, keepdims=True))
        o = _mm(p.astype(BF16), vs[0:t_ctx, :]) / jnp.sum(p, axis=-1, keepdims=True)
        o_ref[0:t_ctx, :] = merge_heads(o, t_ctx)
    else:
        o_ref[0:t_ctx, :] = jnp.zeros((t_ctx, LANES), F32)

    def q_rows(it, carry):
        rs = [it * NA_ROW_UNROLL + j for j in range(NA_ROW_UNROLL)]
        starts = [jnp.clip(r - win_r // 2, 0, rows - win_r) for r in rs]
        q_sl = [pl.ds(pl.multiple_of(t_ctx + r * GRID_W, GRID_W), GRID_W) for r in rs]
        k_sl = [pl.ds(pl.multiple_of(t_ctx + r0 * GRID_W, GRID_W), win_r * GRID_W) for r0 in starts]
        qs_ = [by_head(qs[sl, :]) for sl in q_sl]
        s_win = [_nt(q, ks[sl, :]) + bias_ref[r - r0] for q, sl, r, r0 in zip(qs_, k_sl, rs, starts)]
        s_ctx = [_nt(q, ks[0:t_ctx, :]) for q in qs_]
        ms = [jnp.maximum(jnp.max(a, axis=-1, keepdims=True), jnp.max(b, axis=-1, keepdims=True))
              for a, b in zip(s_win, s_ctx)]
        p_win = [jnp.exp(a - m) for a, m in zip(s_win, ms)]
        p_ctx = [jnp.exp(b - m) for b, m in zip(s_ctx, ms)]
        den = [jnp.sum(a, axis=-1, keepdims=True) + jnp.sum(b, axis=-1, keepdims=True) for a, b in zip(p_win, p_ctx)]
        o_win = [_mm(a.astype(BF16), vs[sl, :]) for a, sl in zip(p_win, k_sl)]
        o_ctx = [_mm(b.astype(BF16), vs[0:t_ctx, :]) for b in p_ctx]
        for sl, a, b, d in zip(q_sl, o_win, o_ctx, den):
            o_ref[sl, :] = merge_heads((a + b) / d, GRID_W)
        return carry

    lax.fori_loop(0, rows // NA_ROW_UNROLL, q_rows, 0)


def _na_tables(t_ctx, t_lat):
    quarter = HEAD_DIM // 4
    pos = np.arange(t_lat)
    inv = ROPE_THETA ** (-np.arange(quarter, dtype=np.float32) / quarter)
    pos_r = (pos // GRID_W).astype(np.float32)
    pos_c = (pos % GRID_W).astype(np.float32)
    return pos_r, pos_c, inv


def _na_bias_table(rpb, rows):
    win_r = min(NA_WIN_ROWS, rows)
    c = np.arange(GRID_W)
    w_start = np.clip(c - NA_WIN_COLS // 2, 0, GRID_W - NA_WIN_COLS)
    kc = np.arange(GRID_W)
    in_win = (kc[None, :] >= w_start[:, None]) & (kc[None, :] < w_start[:, None] + NA_WIN_COLS)
    col_idx = np.clip(kc[None, :] - c[:, None] + NA_WIN_COLS - 1, 0, 2 * NA_WIN_COLS - 2)
    n_col = 2 * NA_WIN_COLS - 1
    onehot = jnp.asarray(col_idx[None] == np.arange(n_col)[:, None, None], F32)
    band = jnp.einsum('...hrj,jck->...hrck', rpb.astype(F32), onehot, precision=_HI)
    band = jnp.where(in_win, band, NEG_INF)
    per_off = [band[..., NA_WIN_ROWS - 1 - off:NA_WIN_ROWS - 1 - off + win_r, :, :] for off in range(win_r)]
    g = jnp.stack(per_off, axis=-4)
    g = jnp.swapaxes(g, -3, -2)
    return g.reshape(g.shape[:-2] + (win_r * GRID_W,))


def _na(pc, q_gain, k_gain, bias, layer, *, t_ctx, need_ctx):
    bsz, t_tot, _ = pc.shape
    t_lat = t_tot - t_ctx
    rows = t_lat // GRID_W
    win_r = min(NA_WIN_ROWS, rows)
    n_hp = C_WIDTH // LANES
    hpl = LANES // HEAD_DIM
    pos_r, pos_c, inv = _na_tables(t_ctx, t_lat)
    ang = np.concatenate([pos_r[:, None] * inv, pos_c[:, None] * inv], axis=-1)
    ang = np.concatenate([np.zeros((t_ctx, HEAD_DIM // 2), np.float32), ang], axis=0)
    cos = np.cos(ang)
    sin = np.sin(ang)
    cos_t = jnp.asarray(np.tile(np.concatenate([cos, cos], axis=-1), (1, hpl)), F32)
    sin_t = jnp.asarray(np.tile(np.concatenate([-sin, sin], axis=-1), (1, hpl)), F32)
    bias = bias.reshape(bias.shape[0], n_hp, hpl, win_r, GRID_W, win_r * GRID_W)
    bias = jnp.swapaxes(bias, 2, 3).reshape(bias.shape[0], n_hp, win_r, hpl * GRID_W, win_r * GRID_W)

    def sec(s):
        return pl.BlockSpec((None, t_tot, LANES), lambda b, h, s=s: (b, 0, s * n_hp + h))

    gain = lambda g: jnp.tile(g.reshape(1, HEAD_DIM), (1, hpl))
    return pl.pallas_call(
        functools.partial(_na_body, t_ctx=t_ctx, need_ctx=need_ctx),
        grid=(bsz, n_hp),
        in_specs=[sec(0), sec(1), sec(2),
                  pl.BlockSpec((1, LANES), lambda b, h: (0, 0)),
                  pl.BlockSpec((1, LANES), lambda b, h: (0, 0)),
                  pl.BlockSpec((t_tot, LANES), lambda b, h: (0, 0)),
                  pl.BlockSpec((t_tot, LANES), lambda b, h: (0, 0)),
                  pl.BlockSpec((None, None, win_r, hpl * GRID_W, win_r * GRID_W),
                               lambda b, h: (layer, h, 0, 0, 0))],
        out_specs=pl.BlockSpec((None, t_tot, LANES), lambda b, h: (b, 0, h)),
        out_shape=jax.ShapeDtypeStruct((bsz, t_tot, C_WIDTH), F32),
        scratch_shapes=[pltpu.VMEM((t_tot, LANES), BF16)] * 3,
        compiler_params=_cparams("arbitrary", "arbitrary"),
        name="na",
    )(pc, pc, pc, gain(q_gain), gain(k_gain), cos_t, sin_t, bias)


def _outproj_body(x_ref, a_ref, by_ref, bg_ref, bb_ref, lng_ref, lnb_ref, c_ref, *refs, per):
    m_refs = refs[:per]
    g_ref, w_ref, wr_ref, br_ref, xo_ref, h_ref, route_ref = refs[per:]
    tt = TOKEN_TILE
    b_mix = _rwkv_readout(by_ref[...], bg_ref[...], bb_ref[...], lng_ref[...], lnb_ref[...])
    mix = (_mm(a_ref[...].astype(BF16), w_ref[0:A_WIDTH, :])
           + _mm(b_mix.astype(BF16), w_ref[A_WIDTH:A_WIDTH + B_WIDTH, :])
           + _mm(c_ref[...].astype(BF16), w_ref[A_WIDTH + B_WIDTH:, :]))
    hs = []
    for j, m_ref in enumerate(m_refs):
        rows = slice(j * tt, (j + 1) * tt)
        x = x_ref[rows, :] + m_ref[2:3, :] * mix[rows]
        xo_ref[rows, :] = x
        h = x * lax.rsqrt(jnp.mean(x * x, axis=-1, keepdims=True) + EPS) * g_ref[...]
        h = h * (1.0 + m_ref[4:5, :]) + m_ref[3:4, :]
        h_ref[rows, :] = h
        hs.append(h)
    h = jnp.concatenate(hs, axis=0)
    logits = _mm_3pass(h, wr_ref[...]) + br_ref[...]
    lane = _iota(logits.shape, 1).astype(F32)
    big = float(LANES)
    is_g = (lane >= N_EXPERTS) & (lane < N_EXPERTS + N_GROUPS)
    gl = jnp.where(is_g, logits, -jnp.inf)
    gmax = jnp.max(gl, axis=-1, keepdims=True)
    g_w = 1.0 / jnp.sum(jnp.exp(gl - gmax), axis=-1, keepdims=True)
    g_sel = jnp.min(jnp.where(gl == gmax, lane, big), axis=-1, keepdims=True) - N_EXPERTS
    lo = g_sel * EXPERTS_PER_GROUP
    el = jnp.where((lane >= lo) & (lane < lo + EXPERTS_PER_GROUP), logits, -jnp.inf)
    m1 = jnp.max(el, axis=-1, keepdims=True)
    i1 = jnp.min(jnp.where(el == m1, lane, big), axis=-1, keepdims=True)
    el2 = jnp.where(lane == i1, -jnp.inf, el)
    m2 = jnp.max(el2, axis=-1, keepdims=True)
    i2 = jnp.min(jnp.where(el2 == m2, lane, big), axis=-1, keepdims=True)
    e2 = jnp.exp(m2 - m1)
    w1 = g_w / (1.0 + e2)
    route_ref[...] = (jnp.where(lane == 0.0, i1, 0.0) + jnp.where(lane == 1.0, i2, 0.0)
                      + jnp.where(lane == 2.0, w1, 0.0) + jnp.where(lane == 3.0, w1 * e2, 0.0))


def _outproj(x, a, b_scan, b_gate, b_bonus, ln_gain, ln_bias, c, mod_l, gain2, w_bf, w_router, b_router, *,
             bps, ctx_blocks, ctx_row):
    n = x.shape[0]
    per = PROJ_TILES
    tm = per * TOKEN_TILE
    tok = lambda w: pl.BlockSpec((tm, w), lambda i: (i, 0))
    full = lambda shape: pl.BlockSpec(shape, lambda i: (0,) * len(shape))
    return pl.pallas_call(
        functools.partial(_outproj_body, per=per),
        grid=(n // tm,),
        in_specs=[tok(D_MODEL), tok(A_WIDTH), tok(B_WIDTH), tok(B_WIDTH), tok(B_WIDTH),
                  full((1, B_WIDTH)), full((1, B_WIDTH)), tok(C_WIDTH),
                  *_mod_specs(per, bps, ctx_blocks, ctx_row),
                  full((1, D_MODEL)), full((D_MODEL, D_MODEL)), full((D_MODEL, LANES)), full((1, LANES))],
        out_specs=[tok(D_MODEL), tok(D_MODEL), tok(LANES)],
        out_shape=[jax.ShapeDtypeStruct((n, D_MODEL), F32),
                   jax.ShapeDtypeStruct((n, D_MODEL), F32),
                   jax.ShapeDtypeStruct((n, LANES), F32)],
        compiler_params=_cparams("arbitrary"),
        name="outproj",
    )(x, a, b_scan, b_gate, b_bonus, ln_gain.reshape(1, B_WIDTH), ln_bias.reshape(1, B_WIDTH), c,
      *([mod_l] * per), gain2.reshape(1, D_MODEL), w_bf, w_router, b_router)


def _route_tables(route, bsz, t_tot, skip_rows):
    n_pairs = TOP_K * t_tot
    ids = route[:, 0:TOP_K].astype(jnp.int32).reshape(bsz, t_tot, TOP_K)
    wts = route[:, TOP_K:2 * TOP_K].reshape(bsz, t_tot, TOP_K)
    ids = jnp.where(jnp.arange(t_tot)[None, :, None] < skip_rows, N_EXPERTS, ids)
    ids = jnp.swapaxes(ids, 1, 2).reshape(bsz, n_pairs)
    wts = jnp.swapaxes(wts, 1, 2).reshape(bsz, n_pairs)
    key = ids * n_pairs + jnp.arange(n_pairs, dtype=jnp.int32)
    key, wts = lax.sort((key, wts), dimension=1, num_keys=1)
    tok = (key % n_pairs) % t_tot
    count = jnp.sum((ids[:, :, None] == jnp.arange(N_EXPERTS, dtype=jnp.int32)).astype(jnp.int32), axis=1)
    start = jnp.cumsum(count, axis=1) - count
    pad = ((0, 0), (0, MOE_TILE))
    return (jnp.pad(tok, pad).reshape(-1), jnp.pad(wts, pad).reshape(-1), start.reshape(-1), count.reshape(-1))


def _moe_body(tok_ref, w_ref, start_ref, count_ref, h_ref, wg_ref, wu_ref, wd_ref, y_ref, hbuf, obuf):
    c = pl.program_id(0)
    step = pl.program_id(1)
    t_tot = h_ref.shape[0]
    p_len = TOP_K * t_tot + MOE_TILE

    @pl.when(jnp.logical_and(c == 0, step == 0))
    def _():
        hbuf[...] = jnp.zeros_like(hbuf)

    @pl.when(step == 0)
    def _():
        def zero(i, carry):
            y_ref[pl.ds(pl.multiple_of(i * TOKEN_TILE, TOKEN_TILE), TOKEN_TILE), :] = jnp.zeros(
                (TOKEN_TILE, D_MODEL), F32)
            return carry
        lax.fori_loop(0, t_tot // TOKEN_TILE, zero, 0)

    for k in range(MOE_EXPERTS_PER_STEP):
        e = step * MOE_EXPERTS_PER_STEP + k
        _moe_expert(tok_ref, w_ref, h_ref, wg_ref.at[k], wu_ref.at[k], wd_ref.at[k], y_ref, hbuf, obuf,
                    count=count_ref[c * N_EXPERTS + e], pair_base=c * p_len + start_ref[c * N_EXPERTS + e])


def _moe_expert(tok_ref, w_ref, h_ref, wg_ref, wu_ref, wd_ref, y_ref, hbuf, obuf, *, count, pair_base):
    tm = MOE_TILE

    def tile(t, carry):
        n_valid = jnp.minimum(tm, count - t * tm)
        pair0 = pair_base + t * tm

        def gather(g, carry):
            for j in range(8):
                tok = tok_ref[pair0 + g * 8 + j]
                hbuf[g, pl.ds(j, 1), :] = h_ref[pl.ds(tok, 1), :]
            return carry
        lax.fori_loop(0, (n_valid + 7) // 8, gather, 0)

        h = hbuf[...].reshape(tm, D_MODEL).astype(BF16)
        gate = _mm(h, wg_ref[...])
        up = _mm(h, wu_ref[...])
        hid = gate * _sigmoid(gate) * up
        obuf[...] = _mm(hid.astype(BF16), wd_ref[...]).reshape(tm // 8, 8, D_MODEL)

        def add_row(r):
            tok = tok_ref[pair0 + r]
            y_ref[pl.ds(tok, 1), :] += w_ref[pair0 + r] * obuf[r // 8, pl.ds(r % 8, 1), :]

        def scatter(g, carry):
            toks = [tok_ref[pair0 + g * 8 + j] for j in range(8)]
            new = [y_ref[pl.ds(toks[j], 1), :] + w_ref[pair0 + g * 8 + j] * obuf[g, pl.ds(j, 1), :]
                   for j in range(8)]
            for j in range(8):
                y_ref[pl.ds(toks[j], 1), :] = new[j]
            return carry
        lax.fori_loop(0, n_valid // 8, scatter, 0)

        def scatter_tail(r, carry):
            add_row(r)
            return carry
        lax.fori_loop(n_valid // 8 * 8, n_valid, scatter_tail, 0)
        return carry

    lax.fori_loop(0, (count + tm - 1) // tm, tile, 0)


def _moe(h, route, wg, wu, wd, layer, *, bsz, t_tot, skip_rows=0):
    tok, w, start, count = _route_tables(route, bsz, t_tot, skip_rows)
    grid_spec = pltpu.PrefetchScalarGridSpec(
        num_scalar_prefetch=4,
        grid=(bsz, N_EXPERTS // MOE_EXPERTS_PER_STEP),
        in_specs=[
            pl.BlockSpec((None, t_tot, D_MODEL), lambda c, e, *_: (c, 0, 0)),
            pl.BlockSpec((None, MOE_EXPERTS_PER_STEP, D_MODEL, D_EXPERT), lambda c, e, *_: (layer, e, 0, 0)),
            pl.BlockSpec((None, MOE_EXPERTS_PER_STEP, D_MODEL, D_EXPERT), lambda c, e, *_: (layer, e, 0, 0)),
            pl.BlockSpec((None, MOE_EXPERTS_PER_STEP, D_EXPERT, D_MODEL), lambda c, e, *_: (layer, e, 0, 0)),
        ],
        out_specs=pl.BlockSpec((None, t_tot, D_MODEL), lambda c, e, *_: (c, 0, 0)),
        scratch_shapes=[pltpu.VMEM((MOE_TILE // 8, 8, D_MODEL), F32)] * 2,
    )
    y = pl.pallas_call(
        _moe_body,
        grid_spec=grid_spec,
        out_shape=jax.ShapeDtypeStruct((bsz, t_tot, D_MODEL), F32),
        compiler_params=_cparams("arbitrary", "arbitrary"),
        name="moe",
    )(tok, w, start, count, h.reshape(bsz, t_tot, D_MODEL), wg, wu, wd)
    return y.reshape(bsz * t_tot, D_MODEL)


def _residual_body(x_ref, y_ref, m_ref, o_ref):
    o_ref[...] = (x_ref[...] + m_ref[5:6, :] * y_ref[...]).astype(o_ref.dtype)


def _final_residual(x, y, mod_l, dtype, *, bsz, t_ctx, t_tot):
    tm = TOKEN_TILE
    tok = pl.BlockSpec((None, tm, D_MODEL), lambda b, i: (b, t_ctx // tm + i, 0))
    return pl.pallas_call(
        _residual_body,
        grid=(bsz, (t_tot - t_ctx) // tm),
        in_specs=[tok, tok, pl.BlockSpec((None, 6, D_MODEL), lambda b, i: (b, 0, 0))],
        out_specs=pl.BlockSpec((None, tm, D_MODEL), lambda b, i: (b, i, 0)),
        out_shape=jax.ShapeDtypeStruct((bsz, t_tot - t_ctx, D_MODEL), dtype),
        compiler_params=_cparams("arbitrary", "arbitrary"),
        name="residual",
    )(x.reshape(bsz, t_tot, D_MODEL), y.reshape(bsz, t_tot, D_MODEL), mod_l)


def kernel(x, c, ctx, c_ctx, norm1_gain, norm2_gain, w_ada, b_ada, w_in, w_out, hgrn_lb_logits, hgrn_gn_gain, rwkv_mu, rwkv_w0, rwkv_w_up, rwkv_a0, rwkv_a_up, rwkv_g_up, rwkv_kk_scale, rwkv_k_a, rwkv_r_k, rwkv_ln_gain, rwkv_ln_bias, na_q_gain, na_k_gain, na_rpb, w_router_group, b_router_group, w_router_expert, b_router_expert, w_exp_gate, w_exp_up, w_exp_down):
    bsz, t_lat, _ = x.shape
    t_ctx = ctx.shape[1]
    t_tot = t_ctx + t_lat
    n = bsz * t_tot
    assert t_ctx % TOKEN_TILE == 0 and t_lat % TOKEN_TILE == 0 and bsz < 16
    bps = t_tot // TOKEN_TILE
    tile_kw = dict(bps=bps, ctx_blocks=t_ctx // TOKEN_TILE, ctx_row=bsz)

    lb_p = jax.nn.softmax(hgrn_lb_logits.astype(F32), axis=1)
    lower_bounds = jnp.cumsum(lb_p, axis=1) - lb_p[:, :1]

    cc = jnp.zeros((16, D_MODEL), F32).at[:bsz].set(c.astype(F32)).at[bsz].set(c_ctx.astype(F32))
    mod = _ada_mod(cc, w_ada, b_ada).reshape(DEPTH, 16, 6, D_MODEL)

    w_in_bf = w_in.astype(BF16)
    w_out_bf = w_out.astype(BF16)
    wg_bf = w_exp_gate.astype(BF16).reshape(DEPTH, N_EXPERTS, D_MODEL, D_EXPERT)
    wu_bf = w_exp_up.astype(BF16).reshape(DEPTH, N_EXPERTS, D_MODEL, D_EXPERT)
    wd_bf = w_exp_down.astype(BF16).reshape(DEPTH, N_EXPERTS, D_EXPERT, D_MODEL)
    pad = LANES - N_EXPERTS - N_GROUPS
    w_router = jnp.concatenate([w_router_expert, w_router_group,
                                jnp.zeros((DEPTH, D_MODEL, pad), F32)], axis=-1)
    b_router = jnp.concatenate([b_router_expert, b_router_group, jnp.zeros((DEPTH, pad), F32)], axis=-1)

    na_bias = _na_bias_table(na_rpb, t_lat // GRID_W)
    xs = jnp.concatenate([ctx.astype(F32), x.astype(F32)], axis=1).reshape(n, D_MODEL)
    y_moe = None
    for layer in range(DEPTH):
        last = layer == DEPTH - 1
        if layer == 0:
            pa, pb, pc = _inproj(xs, mod[layer], norm1_gain[layer], w_in_bf[layer], **tile_kw)
        else:
            xs, pa, pb, pc = _inproj(xs, mod[layer], norm1_gain[layer], w_in_bf[layer],
                                     (y_moe, mod[layer - 1]), **tile_kw)
        a_mix = _hgrn(pa.reshape(bsz, t_tot, A_PROJ), lower_bounds[:, layer], hgrn_gn_gain[layer], t_ctx=t_ctx)
        prep = _rwkv_prep(pb.reshape(bsz, t_tot, B_PROJ), rwkv_mu[layer], rwkv_w0[layer], rwkv_w_up[layer],
                          rwkv_a0[layer], rwkv_a_up[layer], rwkv_g_up[layer], rwkv_kk_scale[layer],
                          rwkv_k_a[layer], rwkv_r_k[layer], t_ctx=t_ctx)
        y = _rwkv_scan(*prep[:9], t_ctx=t_ctx)
        c_mix = _na(pc.reshape(bsz, t_tot, C_PROJ), na_q_gain[layer], na_k_gain[layer], na_bias, layer,
                    t_ctx=t_ctx, need_ctx=not last)
        xs, h2, route = _outproj(xs, a_mix.reshape(n, A_WIDTH), y.reshape(n, B_WIDTH), prep[9].reshape(n, B_WIDTH),
                                 prep[10].reshape(n, B_WIDTH), rwkv_ln_gain[layer], rwkv_ln_bias[layer],
                                 c_mix.reshape(n, C_WIDTH), mod[layer], norm2_gain[layer], w_out_bf[layer],
                                 w_router[layer], b_router[layer].reshape(1, LANES), **tile_kw)
        y_moe = _moe(h2, route, wg_bf, wu_bf, wd_bf, layer, bsz=bsz, t_tot=t_tot, skip_rows=t_ctx if last else 0)
    return _final_residual(xs, y_moe, mod[DEPTH - 1], x.dtype, bsz=bsz, t_ctx=t_ctx, t_tot=t_tot)
```

```python
import functools

import jax
import jax.numpy as jnp
import numpy as np
from jax import lax
from jax.experimental import pallas as pl
from jax.experimental.pallas import tpu as pltpu

F32 = jnp.float32
BF16 = jnp.bfloat16

D_MODEL = 1024
DEPTH = 4
GRID_W = 64
HEAD_DIM = 64
A_HEADS = 4
B_HEADS = 4
C_HEADS = 8
A_WIDTH = A_HEADS * HEAD_DIM
B_WIDTH = B_HEADS * HEAD_DIM
C_WIDTH = C_HEADS * HEAD_DIM
HGRN_CHUNK = 16
HGRN_F_FLOOR = 1e-20
RWKV_W_RANK = 64
RWKV_A_RANK = 64
RWKV_G_RANK = 128
RWKV_LN_EPS = 64e-5
NA_WIN_ROWS = 8
NA_WIN_COLS = 16
ROPE_THETA = 10000.0
N_GROUPS = 4
EXPERTS_PER_GROUP = 8
N_EXPERTS = N_GROUPS * EXPERTS_PER_GROUP
TOP_K = 2
D_EXPERT = 512
EPS = 1e-6
LOG2E = 1.4426950408889634
NEG_INF = -1e30
A_PROJ = 5 * A_WIDTH
B_PROJ = 3 * B_WIDTH + 2 * RWKV_W_RANK + 2 * RWKV_A_RANK + RWKV_G_RANK
C_PROJ = 3 * C_WIDTH
P_TOTAL = A_PROJ + B_PROJ + C_PROJ
B_TAIL = B_PROJ - 3 * B_WIDTH

LANES = 128
PROJ_TILES = 2
TOKEN_TILE = 256
RWKV_CHUNK = 64
RWKV_BLOCKS_PER_ITER = 2
RWKV_BLOCK = 128
HGRN_BLOCK = 128
NA_ROW_UNROLL = 8
HGRN_SCAN_UNROLL = 16
MOE_EXPERTS_PER_STEP = 2
MOE_TILE = 192
VMEM_LIMIT = 56 * 1024 * 1024

_HI = lax.Precision.HIGHEST


def _cparams(*sem):
    return pltpu.CompilerParams(dimension_semantics=sem, vmem_limit_bytes=VMEM_LIMIT)


def _mm(a, b):
    return jnp.dot(a, b, preferred_element_type=F32)


def _nt(a, b):
    return lax.dot_general(a, b, (((1,), (1,)), ((), ())), preferred_element_type=F32)


def _tn(a, b):
    return lax.dot_general(a, b, (((0,), (0,)), ((), ())), preferred_element_type=F32)


def _split3(a):
    a1 = a.astype(BF16)
    r1 = a - a1.astype(F32)
    a2 = r1.astype(BF16)
    a3 = (r1 - a2.astype(F32)).astype(BF16)
    return a1, a2, a3


def _mm_x3(a, m):
    a1, a2, a3 = _split3(a)
    return _mm(a1, m) + _mm(a2, m) + _mm(a3, m)


def _mm_l3(m, a):
    a1, a2, a3 = _split3(a)
    return _mm(m, a1) + _mm(m, a2) + _mm(m, a3)


def _mm_3pass(a, b):
    a1 = a.astype(BF16)
    a2 = (a - a1.astype(F32)).astype(BF16)
    b1 = b.astype(BF16)
    b2 = (b - b1.astype(F32)).astype(BF16)
    return _mm(a1, b1) + (_mm(a1, b2) + _mm(a2, b1))


def _sigmoid(x):
    return 1.0 / (1.0 + jnp.exp(-x))


def _iota(shape, dim):
    return lax.broadcasted_iota(jnp.int32, shape, dim)


def _head_seg(n):
    return jnp.where(_iota((n, n), 0) // HEAD_DIM == _iota((n, n), 1) // HEAD_DIM, 1.0, 0.0).astype(BF16)


def _ada_body(c_ref, w_ref, b_ref, o_ref):
    c = c_ref[...]
    o_ref[...] = _mm_3pass(c * _sigmoid(c), w_ref[...]) + b_ref[...]


def _ada_mod(cc, w_ada, b_ada):
    rows = cc.shape[0]
    tn = 1536
    return pl.pallas_call(
        _ada_body,
        grid=(DEPTH, 6 * D_MODEL // tn),
        in_specs=[
            pl.BlockSpec((rows, D_MODEL), lambda l, j: (0, 0)),
            pl.BlockSpec((None, D_MODEL, tn), lambda l, j: (l, 0, j)),
            pl.BlockSpec((None, 1, tn), lambda l, j: (l, 0, j)),
        ],
        out_specs=pl.BlockSpec((None, rows, tn), lambda l, j: (l, 0, j)),
        out_shape=jax.ShapeDtypeStruct((DEPTH, rows, 6 * D_MODEL), F32),
        compiler_params=_cparams("arbitrary", "arbitrary"),
        name="ada_mod",
    )(cc, w_ada, b_ada.reshape(DEPTH, 1, 6 * D_MODEL))


def _mod_row(i, blocks_per_seq, ctx_blocks, ctx_row):
    return jnp.where(i % blocks_per_seq < ctx_blocks, ctx_row, i // blocks_per_seq)


def _inproj_body(*refs, with_moe, per):
    tt = TOKEN_TILE
    if with_moe:
        x_ref, y_ref = refs[:2]
        mp_refs, m_refs = refs[2:2 + per], refs[2 + per:2 + 2 * per]
        g_ref, w_ref, xo_ref, pa_ref, pb_ref, pc_ref = refs[2 + 2 * per:]
    else:
        x_ref = refs[0]
        m_refs = refs[1:1 + per]
        g_ref, w_ref, pa_ref, pb_ref, pc_ref = refs[1 + per:]
    hs = []
    for j, m_ref in enumerate(m_refs):
        rows = slice(j * tt, (j + 1) * tt)
        x = x_ref[rows, :]
        if with_moe:
            x = x + mp_refs[j][5:6, :] * y_ref[rows, :]
            xo_ref[rows, :] = x
        h = x * lax.rsqrt(jnp.mean(x * x, axis=-1, keepdims=True) + EPS) * g_ref[...]
        hs.append((h * (1.0 + m_ref[1:2, :]) + m_ref[0:1, :]).astype(BF16))
    h = jnp.concatenate(hs, axis=0)
    pa_ref[...] = _mm(h, w_ref[:, :A_PROJ])
    pb_ref[...] = _mm(h, w_ref[:, A_PROJ:A_PROJ + B_PROJ])
    pc_ref[...] = _mm(h, w_ref[:, A_PROJ + B_PROJ:])


def _mod_specs(per, bps, ctx_blocks, ctx_row):
    row = functools.partial(_mod_row, blocks_per_seq=bps, ctx_blocks=ctx_blocks, ctx_row=ctx_row)
    return [pl.BlockSpec((None, 6, D_MODEL), lambda i, j=j: (row(i * per + j), 0, 0)) for j in range(per)]


def _inproj(x, mod_l, gain, w_bf, moe=None, *, bps, ctx_blocks, ctx_row):
    n = x.shape[0]
    per = PROJ_TILES
    tm = per * TOKEN_TILE
    tok = lambda w: pl.BlockSpec((tm, w), lambda i: (i, 0))
    mods = _mod_specs(per, bps, ctx_blocks, ctx_row)
    with_moe = moe is not None
    ins = (x, moe[0], *([moe[1]] * per)) if with_moe else (x,)
    return pl.pallas_call(
        functools.partial(_inproj_body, with_moe=with_moe, per=per),
        grid=(n // tm,),
        in_specs=[tok(D_MODEL)] + ([tok(D_MODEL)] + mods if with_moe else []) + mods + [
            pl.BlockSpec((1, D_MODEL), lambda i: (0, 0)),
            pl.BlockSpec((D_MODEL, P_TOTAL), lambda i: (0, 0)),
        ],
        out_specs=([tok(D_MODEL)] if with_moe else []) + [tok(A_PROJ), tok(B_PROJ), tok(C_PROJ)],
        out_shape=([jax.ShapeDtypeStruct((n, D_MODEL), F32)] if with_moe else []) + [
            jax.ShapeDtypeStruct((n, A_PROJ), F32),
            jax.ShapeDtypeStruct((n, B_PROJ), F32),
            jax.ShapeDtypeStruct((n, C_PROJ), F32),
        ],
        compiler_params=_cparams("arbitrary"),
        name="inproj",
    )(*ins, *([mod_l] * per), gain.reshape(1, D_MODEL), w_bf)


def _hgrn_body(q_ref, ff_ref, fb_ref, i_ref, g_ref, lb_ref, gn_ref, o_ref,
               qin_s, kout_s, cum_s, acc_s, kpad_s, cpad_s, vpad_s, *, t_ctx):
    t_tot = q_ref.shape[0]
    c = HGRN_CHUNK
    rb = HGRN_BLOCK
    n_blk = t_tot // rb
    n_chunk = t_tot // c
    n_chunk_ctx = t_ctx // c
    row = _iota((rb, rb), 0)
    col = _iota((rb, rb), 1)
    same = (row // c) == (col // c)
    same_bf = jnp.where(same, 1.0, 0.0).astype(BF16)
    seg = _head_seg(LANES)
    pos = _iota((rb, LANES), 0) % c
    blockdiag = _iota((LANES, LANES), 0) // HEAD_DIM == _iota((LANES, LANES), 1) // HEAD_DIM

    pad = HGRN_CHUNK
    zpad = jnp.zeros((pad, LANES), F32)
    for d in range(2):
        for ref in (kpad_s, cpad_s):
            ref[d, 0:pad, :] = zpad
            ref[d, pad + t_tot:, :] = zpad
    vpad_s[0:pad, :] = zpad
    vpad_s[pad + t_tot:, :] = zpad
    tris = [jnp.where(same & ((col >= row) if d == 1 else (col <= row)), 1.0, 0.0).astype(BF16) for d in range(2)]

    def gates(b, carry):
        sl = pl.ds(pl.multiple_of(b * rb, rb), rb)
        slp = pl.ds(pl.multiple_of(b * rb, rb) + pad, rb)
        q = q_ref[sl, :]
        vpad_s[slp, :] = i_ref[sl, :]
        for d, f_ref in enumerate((ff_ref, fb_ref)):
            lb = lb_ref[d:d + 1, :]
            fpre = f_ref[sl, :]
            f = lb + (1.0 - lb) * _sigmoid(fpre)
            logf = jnp.log(jnp.maximum(f, HGRN_F_FLOOR))
            k = (1.0 - lb) * _sigmoid(-fpre)
            cum = _mm_l3(tris[d], logf)
            tot = _mm_l3(same_bf, logf)
            kpad_s[d, slp, :] = k
            cpad_s[d, slp, :] = cum * LOG2E
            qin_s[d, sl, :] = q * jnp.exp(cum)
            kout_s[d, sl, :] = k * jnp.exp(tot - cum)
            cum_s[d, sl, :] = tot
        return carry

    lax.fori_loop(0, n_blk, gates, 0)

    def intra(b, carry):
        r0 = pl.multiple_of(b * rb, rb)
        sl = pl.ds(r0, rb)
        q = q_ref[sl, :]
        cums = [cpad_s[d, pl.ds(r0 + pad, rb), :] for d in range(2)]
        o = jnp.zeros((rb, LANES), F32)
        for j in range(c):
            for d in range(2):
                off = r0 + pad + (j if d == 1 else -j)
                valid = (pos <= c - 1 - j) if d == 1 else (pos >= j)
                ks = kpad_s[d, pl.ds(off, rb), :]
                cs = cpad_s[d, pl.ds(off, rb), :]
                vs = vpad_s[pl.ds(off, rb), :]
                prod = jnp.where(valid, q * ks * jnp.exp2(cums[d] - cs), 0.0)
                o = o + _mm(prod.astype(BF16), seg) * vs
        acc_s[sl, :] = o
        return carry

    lax.fori_loop(0, n_blk, intra, 0)

    def step(it, carry):
        idx = [it * HGRN_SCAN_UNROLL + j for j in range(HGRN_SCAN_UNROLL)]
        order = [(i, jnp.where(i < n_chunk_ctx, n_chunk_ctx - 1 - i, n_chunk - 1 - (i - n_chunk_ctx))) for i in idx]
        sls = [[pl.ds(pl.multiple_of(n[d] * c, c), c) for d in range(2)] for n in order]
        kvs = [[jnp.where(blockdiag, _tn(i_ref[sl[d], :].astype(BF16), kout_s[d, sl[d], :].astype(BF16)), 0.0)
                for d in range(2)] for sl in sls]
        decs = [[jnp.exp(cum_s[d, pl.ds(pl.multiple_of(n[d] * c, c), 1), :]) for d in range(2)] for n in order]
        states = list(carry)
        for j in range(HGRN_SCAN_UNROLL):
            for d in range(2):
                acc_s[sls[j][d], :] += _nt(qin_s[d, sls[j][d], :].astype(BF16), states[d].astype(BF16))
                states[d] = states[d] * decs[j][d] + kvs[j][d]
        return tuple(states)

    zero = jnp.zeros((LANES, LANES), F32)
    lax.fori_loop(0, n_chunk // HGRN_SCAN_UNROLL, step, (zero, zero))

    def readout(b, carry):
        sl = pl.ds(pl.multiple_of(b * rb, rb), rb)
        o = acc_s[sl, :]
        ms = _mm((o * o).astype(BF16), seg) * (1.0 / HEAD_DIM)
        g = g_ref[sl, :]
        o_ref[sl, :] = o * lax.rsqrt(ms + EPS) * gn_ref[...] * (g * _sigmoid(g))
        return carry

    lax.fori_loop(0, n_blk, readout, 0)


def _hgrn(pa, lb, gn_gain, *, t_ctx):
    bsz, t_tot, _ = pa.shape
    n_hp = A_WIDTH // LANES

    def sec(s):
        return pl.BlockSpec((None, t_tot, LANES), lambda b, h, s=s: (b, 0, s * n_hp + h))

    return pl.pallas_call(
        functools.partial(_hgrn_body, t_ctx=t_ctx),
        grid=(bsz, n_hp),
        in_specs=[sec(0), sec(1), sec(2), sec(3), sec(4),
                  pl.BlockSpec((2, LANES), lambda b, h: (0, h)),
                  pl.BlockSpec((1, LANES), lambda b, h: (0, h))],
        out_specs=pl.BlockSpec((None, t_tot, LANES), lambda b, h: (b, 0, h)),
        out_shape=jax.ShapeDtypeStruct((bsz, t_tot, A_WIDTH), F32),
        scratch_shapes=[
            pltpu.VMEM((2, t_tot, LANES), F32),
            pltpu.VMEM((2, t_tot, LANES), F32),
            pltpu.VMEM((2, t_tot, LANES), F32),
            pltpu.VMEM((t_tot, LANES), F32),
            pltpu.VMEM((2, t_tot + 2 * HGRN_CHUNK, LANES), F32),
            pltpu.VMEM((2, t_tot + 2 * HGRN_CHUNK, LANES), F32),
            pltpu.VMEM((t_tot + 2 * HGRN_CHUNK, LANES), F32),
        ],
        compiler_params=_cparams("arbitrary", "arbitrary"),
        name="hgrn",
    )(pa, pa, pa, pa, pa, lb, gn_gain.reshape(1, A_WIDTH))


def _rwkv_prep_body(x_ref, xp_ref, xn_ref, mu_ref, w0_ref, wup_ref, a0_ref, aup_ref, gup_ref,
                    kks_ref, ka_ref, rk_ref,
                    r_o, v_o, kk_o, kf_o, kb_o, bf_o, bb_o, lwf_o, lwb_o, g_o, bonus_o, *, t_ctx, t_tot):
    tm = x_ref.shape[0]
    r0 = pl.program_id(1) * tm
    prev_ok = jnp.logical_and(r0 != 0, r0 != t_ctx)
    next_ok = jnp.logical_and(r0 + tm != t_ctx, r0 + tm != t_tot)
    first = _iota((tm, 1), 0) == 0
    last = _iota((tm, 1), 0) == tm - 1

    def shifted(lo, hi):
        x = x_ref[:, lo:hi]
        p_row = jnp.where(prev_ok, xp_ref[7:8, lo:hi], 0.0)
        n_row = jnp.where(next_ok, xn_ref[0:1, lo:hi], 0.0)
        prev = jnp.where(first, p_row, pltpu.roll(x, 1, 0))
        nxt = jnp.where(last, n_row, pltpu.roll(x, tm - 1, 0))
        return x + (0.5 * (prev + nxt) - x) * mu_ref[:, lo:hi]

    bw = B_WIDTH
    r = shifted(0, bw)
    k = shifted(bw, 2 * bw)
    v = shifted(2 * bw, 3 * bw)
    tail = shifted(3 * bw, B_PROJ)
    seg = _head_seg(bw)

    kk = k * kks_ref[...]
    kk = kk / jnp.maximum(jnp.sqrt(_mm_x3(kk * kk, seg)), 1e-12)
    ksum = jnp.zeros_like(k)
    for d, (k_o, b_o, lw_o) in enumerate(((kf_o, bf_o, lwf_o), (kb_o, bb_o, lwb_o))):
        wd = tail[:, d * RWKV_W_RANK:(d + 1) * RWKV_W_RANK]
        ad = tail[:, 2 * RWKV_W_RANK + d * RWKV_A_RANK:2 * RWKV_W_RANK + (d + 1) * RWKV_A_RANK]
        u = -(w0_ref[d:d + 1, :] + _mm_3pass(jnp.tanh(wd), wup_ref[d]))
        w = -(jnp.maximum(u, 0.0) + jnp.log(1.0 + jnp.exp(-jnp.abs(u)))) - 0.5
        a = _sigmoid(a0_ref[d:d + 1, :] + _mm_3pass(ad, aup_ref[d]))
        k_d = k * (1.0 + (a - 1.0) * ka_ref[...])
        ksum = ksum + k_d
        k_o[...] = k_d
        b_o[...] = a * kk
        lw_o[...] = -jnp.exp(w)
    gd = tail[:, 2 * RWKV_W_RANK + 2 * RWKV_A_RANK:]
    r_o[...] = r
    v_o[...] = v
    kk_o[...] = kk
    g_o[...] = _mm_3pass(_sigmoid(gd), gup_ref[...])
    bonus_o[...] = _mm((r * ksum * rk_ref[...]).astype(BF16), seg) * v


def _rwkv_prep(pb, mu, w0, w_up, a0, a_up, g_up, kk_scale, k_a, r_k, *, t_ctx):
    bsz, t_tot, _ = pb.shape
    tm = TOKEN_TILE
    nb8 = t_tot // 8
    per8 = tm // 8
    bw = B_WIDTH
    full = lambda shape: pl.BlockSpec(shape, lambda b, i: (0,) * len(shape))
    out = jax.ShapeDtypeStruct((bsz, t_tot, bw), F32)
    return pl.pallas_call(
        functools.partial(_rwkv_prep_body, t_ctx=t_ctx, t_tot=t_tot),
        grid=(bsz, t_tot // tm),
        in_specs=[
            pl.BlockSpec((None, tm, B_PROJ), lambda b, i: (b, i, 0)),
            pl.BlockSpec((None, 8, B_PROJ), lambda b, i: (b, jnp.maximum(i * per8 - 1, 0), 0)),
            pl.BlockSpec((None, 8, B_PROJ), lambda b, i: (b, jnp.minimum((i + 1) * per8, nb8 - 1), 0)),
            full((1, B_PROJ)), full((2, bw)), full((2, RWKV_W_RANK, bw)), full((2, bw)),
            full((2, RWKV_A_RANK, bw)), full((RWKV_G_RANK, bw)), full((1, bw)), full((1, bw)), full((1, bw)),
        ],
        out_specs=[pl.BlockSpec((None, tm, bw), lambda b, i: (b, i, 0))] * 11,
        out_shape=[out] * 11,
        compiler_params=_cparams("arbitrary", "arbitrary"),
        name="rwkv_prep",
    )(pb, pb, pb, mu.reshape(1, B_PROJ), w0, w_up, a0, a_up, g_up,
      kk_scale.reshape(1, bw), k_a.reshape(1, bw), r_k.reshape(1, bw))


def _rwkv_scan_body(r_ref, v_ref, kk_ref, kf_ref, kb_ref, bf_ref, bb_ref, lwf_ref, lwb_ref, y_ref,
                    rq_s, y0_s, p_s, z_s, *, t_ctx):
    t_tot = r_ref.shape[0]
    c = RWKV_CHUNK
    rb = RWKV_BLOCK
    n_blk = t_tot // rb
    n_chunk = t_tot // c
    n_chunk_ctx = t_ctx // c
    cpb = rb // c
    n_double = c.bit_length() - 2
    row = _iota((rb, rb), 0)
    col = _iota((rb, rb), 1)
    same = (row // c) == (col // c)
    same_bf = jnp.where(same, 1.0, 0.0).astype(BF16)
    eye = jnp.where(row == col, 1.0, 0.0)
    lane = _iota((1, LANES), 1)
    head_masks = [jnp.where(lane // HEAD_DIM == h, 1.0, 0.0) for h in range(LANES // HEAD_DIM)]
    r128 = _iota((LANES, LANES), 0)
    c128 = _iota((LANES, LANES), 1)
    blockdiag = (r128 // HEAD_DIM) == (c128 // HEAD_DIM)
    diag128 = r128 == c128

    def block(it, carry):
        bis = [it * RWKV_BLOCKS_PER_ITER + j for j in range(RWKV_BLOCKS_PER_ITER)]
        sls = [pl.ds(pl.multiple_of(bi * rb, rb), rb) for bi in bis]
        v_bfs = [v_ref[sl, :].astype(BF16) for sl in sls]
        pre = {}
        for j, sl in enumerate(sls):
            r = r_ref[sl, :]
            kk = kk_ref[sl, :]
            for d in range(2):
                rev = d == 1
                k_ref, b_ref, lw_ref = (kb_ref, bb_ref, lwb_ref) if rev else (kf_ref, bf_ref, lwf_ref)
                incl = same & ((col >= row) if rev else (col <= row))
                strict = same & ((col > row) if rev else (col < row))
                incl_bf = jnp.where(incl, 1.0, 0.0).astype(BF16)
                lw = lw_ref[sl, :]
                k = k_ref[sl, :]
                b = b_ref[sl, :]
                cl = _mm_l3(incl_bf, lw)
                tot = _mm_l3(same_bf, lw)
                w_inv = jnp.exp(-cl)
                w_end = jnp.exp(tot - cl)
                pre[j, d] = dict(incl=incl, strict=strict, tot=tot, kk_d=kk * jnp.exp(cl - lw), r_d=r * jnp.exp(cl),
                                 kb=jnp.concatenate([k * w_inv, b * w_inv], axis=0).astype(BF16),
                                 b_e=(b * w_end).astype(BF16), k_e=(k * w_end).astype(BF16))
        chains = [dict(j=j, d=d, mh=mh, kkm=pre[j, d]['kk_d'] * mh, rm=pre[j, d]['r_d'] * mh)
                  for j in range(len(bis)) for d in range(2) for mh in head_masks]
        for ch in chains:
            p = pre[ch['j'], ch['d']]
            aa = _nt(jnp.concatenate([ch['kkm'], ch['rm']], axis=0).astype(BF16), p['kb'])
            ch['a_kb'] = jnp.where(p['strict'], aa[:rb, :rb], 0.0)
            ch['a_rk'] = jnp.where(p['incl'], aa[rb:, :rb], 0.0)
            ch['a_rb'] = jnp.where(p['incl'], aa[rb:, rb:], 0.0)
            m = -jnp.where(p['strict'], aa[:rb, rb:], 0.0)
            ch['t'] = eye + m
            ch['m'] = m
        for ch in chains:
            m_bf = ch['m'].astype(BF16)
            ch['m'] = _mm(m_bf, m_bf)
        for _ in range(n_double - 1):
            for ch in chains:
                both = _mm(jnp.concatenate([ch['t'], ch['m']], axis=0).astype(BF16), ch['m'].astype(BF16))
                ch['t'] = ch['t'] + both[:rb]
                ch['m'] = both[rb:]
        for ch in chains:
            ch['t'] = ch['t'] + _mm(ch['t'].astype(BF16), ch['m'].astype(BF16))
            ch['av'] = _mm(jnp.concatenate([ch['a_kb'], ch['a_rk']], axis=0).astype(BF16), v_bfs[ch['j']])
        for ch in chains:
            ch['ku'] = _mm(ch['t'].astype(BF16), jnp.concatenate([ch['kkm'], ch['av'][:rb]], axis=1).astype(BF16))
        for ch in chains:
            ch['rb_ku'] = _mm(ch['a_rb'].astype(BF16), ch['ku'].astype(BF16))
        for j, (bi, sl) in enumerate(zip(bis, sls)):
            for d in range(2):
                p = pre[j, d]
                mine = [ch for ch in chains if ch['j'] == j and ch['d'] == d]
                kkt = sum(ch['ku'][:, :LANES] for ch in mine)
                u = sum(ch['ku'][:, LANES:] * ch['mh'] for ch in mine)
                rq_s[d, sl, :] = sum(ch['rm'] - ch['rb_ku'][:, :LANES] for ch in mine)
                y0_s[d, sl, :] = sum((ch['av'][rb:] - ch['rb_ku'][:, LANES:]) * ch['mh'] for ch in mine)
                ktu = jnp.concatenate([kkt, u], axis=1).astype(BF16)
                for n in range(cpb):
                    rows = slice(n * c, (n + 1) * c)
                    wc = jnp.exp(p['tot'][n * c:n * c + 1, :])
                    kub = _tn(ktu[rows], p['b_e'][rows])
                    p_s[d, bi * cpb + n] = jnp.where(diag128, wc, 0.0) - jnp.where(blockdiag, kub[:LANES], 0.0)
                    p_z = jnp.where(blockdiag, _tn(v_bfs[j][rows], p['k_e'][rows]) - kub[LANES:], 0.0)
                    z_s[d, bi * cpb + n] = p_z
        return carry

    lax.fori_loop(0, n_blk // RWKV_BLOCKS_PER_ITER, block, 0)

    y_ref[...] = jnp.zeros_like(y_ref)

    def step(i, carry):
        n_b = jnp.where(i < n_chunk_ctx, n_chunk_ctx - 1 - i, n_chunk - 1 - (i - n_chunk_ctx))
        chunk = (i, n_b)
        s_bf = [s.astype(BF16) for s in carry]
        ys = [_nt(rq_s[d, pl.ds(pl.multiple_of(chunk[d] * c, c), c), :].astype(BF16), s_bf[d]) for d in range(2)]
        states = [_mm(s_bf[d], p_s[d, chunk[d]].astype(BF16)) + z_s[d, chunk[d]] for d in range(2)]
        for d in range(2):
            sl = pl.ds(pl.multiple_of(chunk[d] * c, c), c)
            y_ref[sl, :] += ys[d] + y0_s[d, sl, :]
        return tuple(states)

    zero = jnp.zeros((LANES, LANES), F32)
    lax.fori_loop(0, n_chunk, step, (zero, zero))


def _rwkv_scan(r, v, kk, kf, kb, bf, bb, lwf, lwb, *, t_ctx):
    bsz, t_tot, bw = r.shape
    n_chunk = t_tot // RWKV_CHUNK
    spec = pl.BlockSpec((None, t_tot, LANES), lambda b, h: (b, 0, h))
    return pl.pallas_call(
        functools.partial(_rwkv_scan_body, t_ctx=t_ctx),
        grid=(bsz, bw // LANES),
        in_specs=[spec] * 9,
        out_specs=spec,
        out_shape=jax.ShapeDtypeStruct((bsz, t_tot, bw), F32),
        scratch_shapes=[
            pltpu.VMEM((2, t_tot, LANES), F32),
            pltpu.VMEM((2, t_tot, LANES), F32),
            pltpu.VMEM((2, n_chunk, LANES, LANES), F32),
            pltpu.VMEM((2, n_chunk, LANES, LANES), F32),
        ],
        compiler_params=_cparams("arbitrary", "arbitrary"),
        name="rwkv_scan",
    )(r, v, kk, kf, kb, bf, bb, lwf, lwb)


def _rwkv_readout(y, g, bonus, ln_gain, ln_bias):
    seg = _head_seg(B_WIDTH)
    mean = _mm_x3(y, seg) * (1.0 / HEAD_DIM)
    yc = y - mean
    var = _mm((yc * yc).astype(BF16), seg) * (1.0 / HEAD_DIM)
    return (yc * lax.rsqrt(var + RWKV_LN_EPS) * ln_gain + ln_bias + bonus) * g


def _na_body(q_ref, k_ref, v_ref, qg_ref, kg_ref, cos_ref, sin_ref, bias_ref, o_ref,
             qs, ks, vs, *, t_ctx, need_ctx):
    t_tot = q_ref.shape[0]
    t_lat = t_tot - t_ctx
    rows = t_lat // GRID_W
    win_r = min(NA_WIN_ROWS, rows)
    scale = HEAD_DIM ** -0.5
    seg = _head_seg(LANES)
    lane = _iota((1, LANES), 1)
    head_masks = [jnp.where(lane // HEAD_DIM == h, 1.0, 0.0) for h in range(LANES // HEAD_DIM)]
    half = (lane % HEAD_DIM) < HEAD_DIM // 2
    blk = 256

    def norm_block(i, carry):
        sl = pl.ds(pl.multiple_of(i * blk, blk), blk)
        q = q_ref[sl, :]
        k = k_ref[sl, :]
        q = q * lax.rsqrt(_mm((q * q).astype(BF16), seg) * (1.0 / HEAD_DIM) + EPS) * qg_ref[...]
        k = k * lax.rsqrt(_mm((k * k).astype(BF16), seg) * (1.0 / HEAD_DIM) + EPS) * kg_ref[...]
        cos = cos_ref[sl, :]
        sin = sin_ref[sl, :]

        def rope(t):
            swapped = jnp.where(half, pltpu.roll(t, LANES - HEAD_DIM // 2, 1), pltpu.roll(t, HEAD_DIM // 2, 1))
            return t * cos + swapped * sin

        qs[sl, :] = (rope(q) * scale).astype(BF16)
        ks[sl, :] = rope(k).astype(BF16)
        vs[sl, :] = v_ref[sl, :].astype(BF16)
        return carry

    lax.fori_loop(0, t_tot // blk, norm_block, 0)

    masks_bf = [mh.astype(BF16) for mh in head_masks]

    def by_head(q):
        return jnp.concatenate([q * mb for mb in masks_bf], axis=0)

    def merge_heads(o, n):
        return sum(o[h * n:(h + 1) * n] * mh for h, mh in enumerate(head_masks))

    if need_ctx:
        s = _nt(by_head(qs[0:t_ctx, :]), ks[0:t_ctx, :])
        p = jnp.exp(s - jnp.max(s, axis=-1, keepdims=True))
        o = _mm(p.astype(BF16), vs[0:t_ctx, :]) / jnp.sum(p, axis=-1, keepdims=True)
        o_ref[0:t_ctx, :] = merge_heads(o, t_ctx)
    else:
        o_ref[0:t_ctx, :] = jnp.zeros((t_ctx, LANES), F32)

    def q_rows(it, carry):
        rs = [it * NA_ROW_UNROLL + j for j in range(NA_ROW_UNROLL)]
        starts = [jnp.clip(r - win_r // 2, 0, rows - win_r) for r in rs]
        q_sl = [pl.ds(pl.multiple_of(t_ctx + r * GRID_W, GRID_W), GRID_W) for r in rs]
        k_sl = [pl.ds(pl.multiple_of(t_ctx + r0 * GRID_W, GRID_W), win_r * GRID_W) for r0 in starts]
        qs_ = [by_head(qs[sl, :]) for sl in q_sl]
        s_win = [_nt(q, ks[sl, :]) + bias_ref[r - r0] for q, sl, r, r0 in zip(qs_, k_sl, rs, starts)]
        s_ctx = [_nt(q, ks[0:t_ctx, :]) for q in qs_]
        ms = [jnp.maximum(jnp.max(a, axis=-1, keepdims=True), jnp.max(b, axis=-1, keepdims=True))
              for a, b in zip(s_win, s_ctx)]
        p_win = [jnp.exp(a - m) for a, m in zip(s_win, ms)]
        p_ctx = [jnp.exp(b - m) for b, m in zip(s_ctx, ms)]
        den = [jnp.sum(a, axis=-1, keepdims=True) + jnp.sum(b, axis=-1, keepdims=True) for a, b in zip(p_win, p_ctx)]
        o_win = [_mm(a.astype(BF16), vs[sl, :]) for a, sl in zip(p_win, k_sl)]
        o_ctx = [_mm(b.astype(BF16), vs[0:t_ctx, :]) for b in p_ctx]
        for sl, a, b, d in zip(q_sl, o_win, o_ctx, den):
            o_ref[sl, :] = merge_heads((a + b) / d, GRID_W)
        return carry

    lax.fori_loop(0, rows // NA_ROW_UNROLL, q_rows, 0)


def _na_tables(t_ctx, t_lat):
    quarter = HEAD_DIM // 4
    pos = np.arange(t_lat)
    inv = ROPE_THETA ** (-np.arange(quarter, dtype=np.float32) / quarter)
    pos_r = (pos // GRID_W).astype(np.float32)
    pos_c = (pos % GRID_W).astype(np.float32)
    return pos_r, pos_c, inv


def _na_bias_table(rpb, rows):
    win_r = min(NA_WIN_ROWS, rows)
    c = np.arange(GRID_W)
    w_start = np.clip(c - NA_WIN_COLS // 2, 0, GRID_W - NA_WIN_COLS)
    kc = np.arange(GRID_W)
    in_win = (kc[None, :] >= w_start[:, None]) & (kc[None, :] < w_start[:, None] + NA_WIN_COLS)
    col_idx = np.clip(kc[None, :] - c[:, None] + NA_WIN_COLS - 1, 0, 2 * NA_WIN_COLS - 2)
    n_col = 2 * NA_WIN_COLS - 1
    onehot = jnp.asarray(col_idx[None] == np.arange(n_col)[:, None, None], F32)
    band = jnp.einsum('...hrj,jck->...hrck', rpb.astype(F32), onehot, precision=_HI)
    band = jnp.where(in_win, band, NEG_INF)
    per_off = [band[..., NA_WIN_ROWS - 1 - off:NA_WIN_ROWS - 1 - off + win_r, :, :] for off in range(win_r)]
    g = jnp.stack(per_off, axis=-4)
    g = jnp.swapaxes(g, -3, -2)
    return g.reshape(g.shape[:-2] + (win_r * GRID_W,))


def _na(pc, q_gain, k_gain, bias, layer, *, t_ctx, need_ctx):
    bsz, t_tot, _ = pc.shape
    t_lat = t_tot - t_ctx
    rows = t_lat // GRID_W
    win_r = min(NA_WIN_ROWS, rows)
    n_hp = C_WIDTH // LANES
    hpl = LANES // HEAD_DIM
    pos_r, pos_c, inv = _na_tables(t_ctx, t_lat)
    ang = np.concatenate([pos_r[:, None] * inv, pos_c[:, None] * inv], axis=-1)
    ang = np.concatenate([np.zeros((t_ctx, HEAD_DIM // 2), np.float32), ang], axis=0)
    cos = np.cos(ang)
    sin = np.sin(ang)
    cos_t = jnp.asarray(np.tile(np.concatenate([cos, cos], axis=-1), (1, hpl)), F32)
    sin_t = jnp.asarray(np.tile(np.concatenate([-sin, sin], axis=-1), (1, hpl)), F32)
    bias = bias.reshape(bias.shape[0], n_hp, hpl, win_r, GRID_W, win_r * GRID_W)
    bias = jnp.swapaxes(bias, 2, 3).reshape(bias.shape[0], n_hp, win_r, hpl * GRID_W, win_r * GRID_W)

    def sec(s):
        return pl.BlockSpec((None, t_tot, LANES), lambda b, h, s=s: (b, 0, s * n_hp + h))

    gain = lambda g: jnp.tile(g.reshape(1, HEAD_DIM), (1, hpl))
    return pl.pallas_call(
        functools.partial(_na_body, t_ctx=t_ctx, need_ctx=need_ctx),
        grid=(bsz, n_hp),
        in_specs=[sec(0), sec(1), sec(2),
                  pl.BlockSpec((1, LANES), lambda b, h: (0, 0)),
                  pl.BlockSpec((1, LANES), lambda b, h: (0, 0)),
                  pl.BlockSpec((t_tot, LANES), lambda b, h: (0, 0)),
                  pl.BlockSpec((t_tot, LANES), lambda b, h: (0, 0)),
                  pl.BlockSpec((None, None, win_r, hpl * GRID_W, win_r * GRID_W),
                               lambda b, h: (layer, h, 0, 0, 0))],
        out_specs=pl.BlockSpec((None, t_tot, LANES), lambda b, h: (b, 0, h)),
        out_shape=jax.ShapeDtypeStruct((bsz, t_tot, C_WIDTH), F32),
        scratch_shapes=[pltpu.VMEM((t_tot, LANES), BF16)] * 3,
        compiler_params=_cparams("arbitrary", "arbitrary"),
        name="na",
    )(pc, pc, pc, gain(q_gain), gain(k_gain), cos_t, sin_t, bias)


def _outproj_body(x_ref, a_ref, by_ref, bg_ref, bb_ref, lng_ref, lnb_ref, c_ref, *refs, per):
    m_refs = refs[:per]
    g_ref, w_ref, wr_ref, br_ref, xo_ref, h_ref, route_ref = refs[per:]
    tt = TOKEN_TILE
    b_mix = _rwkv_readout(by_ref[...], bg_ref[...], bb_ref[...], lng_ref[...], lnb_ref[...])
    mix = (_mm(a_ref[...].astype(BF16), w_ref[0:A_WIDTH, :])
           + _mm(b_mix.astype(BF16), w_ref[A_WIDTH:A_WIDTH + B_WIDTH, :])
           + _mm(c_ref[...].astype(BF16), w_ref[A_WIDTH + B_WIDTH:, :]))
    hs = []
    for j, m_ref in enumerate(m_refs):
        rows = slice(j * tt, (j + 1) * tt)
        x = x_ref[rows, :] + m_ref[2:3, :] * mix[rows]
        xo_ref[rows, :] = x
        h = x * lax.rsqrt(jnp.mean(x * x, axis=-1, keepdims=True) + EPS) * g_ref[...]
        h = h * (1.0 + m_ref[4:5, :]) + m_ref[3:4, :]
        h_ref[rows, :] = h
        hs.append(h)
    h = jnp.concatenate(hs, axis=0)
    logits = _mm_3pass(h, wr_ref[...]) + br_ref[...]
    lane = _iota(logits.shape, 1).astype(F32)
    big = float(LANES)
    is_g = (lane >= N_EXPERTS) & (lane < N_EXPERTS + N_GROUPS)
    gl = jnp.where(is_g, logits, -jnp.inf)
    gmax = jnp.max(gl, axis=-1, keepdims=True)
    g_w = 1.0 / jnp.sum(jnp.exp(gl - gmax), axis=-1, keepdims=True)
    g_sel = jnp.min(jnp.where(gl == gmax, lane, big), axis=-1, keepdims=True) - N_EXPERTS
    lo = g_sel * EXPERTS_PER_GROUP
    el = jnp.where((lane >= lo) & (lane < lo + EXPERTS_PER_GROUP), logits, -jnp.inf)
    m1 = jnp.max(el, axis=-1, keepdims=True)
    i1 = jnp.min(jnp.where(el == m1, lane, big), axis=-1, keepdims=True)
    el2 = jnp.where(lane == i1, -jnp.inf, el)
    m2 = jnp.max(el2, axis=-1, keepdims=True)
    i2 = jnp.min(jnp.where(el2 == m2, lane, big), axis=-1, keepdims=True)
    e2 = jnp.exp(m2 - m1)
    w1 = g_w / (1.0 + e2)
    route_ref[...] = (jnp.where(lane == 0.0, i1, 0.0) + jnp.where(lane == 1.0, i2, 0.0)
                      + jnp.where(lane == 2.0, w1, 0.0) + jnp.where(lane == 3.0, w1 * e2, 0.0))


def _outproj(x, a, b_scan, b_gate, b_bonus, ln_gain, ln_bias, c, mod_l, gain2, w_bf, w_router, b_router, *,
             bps, ctx_blocks, ctx_row):
    n = x.shape[0]
    per = PROJ_TILES
    tm = per * TOKEN_TILE
    tok = lambda w: pl.BlockSpec((tm, w), lambda i: (i, 0))
    full = lambda shape: pl.BlockSpec(shape, lambda i: (0,) * len(shape))
    return pl.pallas_call(
        functools.partial(_outproj_body, per=per),
        grid=(n // tm,),
        in_specs=[tok(D_MODEL), tok(A_WIDTH), tok(B_WIDTH), tok(B_WIDTH), tok(B_WIDTH),
                  full((1, B_WIDTH)), full((1, B_WIDTH)), tok(C_WIDTH),
                  *_mod_specs(per, bps, ctx_blocks, ctx_row),
                  full((1, D_MODEL)), full((D_MODEL, D_MODEL)), full((D_MODEL, LANES)), full((1, LANES))],
        out_specs=[tok(D_MODEL), tok(D_MODEL), tok(LANES)],
        out_shape=[jax.ShapeDtypeStruct((n, D_MODEL), F32),
                   jax.ShapeDtypeStruct((n, D_MODEL), F32),
                   jax.ShapeDtypeStruct((n, LANES), F32)],
        compiler_params=_cparams("arbitrary"),
        name="outproj",
    )(x, a, b_scan, b_gate, b_bonus, ln_gain.reshape(1, B_WIDTH), ln_bias.reshape(1, B_WIDTH), c,
      *([mod_l] * per), gain2.reshape(1, D_MODEL), w_bf, w_router, b_router)


def _route_tables(route, bsz, t_tot, skip_rows):
    n_pairs = TOP_K * t_tot
    ids = route[:, 0:TOP_K].astype(jnp.int32).reshape(bsz, t_tot, TOP_K)
    wts = route[:, TOP_K:2 * TOP_K].reshape(bsz, t_tot, TOP_K)
    ids = jnp.where(jnp.arange(t_tot)[None, :, None] < skip_rows, N_EXPERTS, ids)
    ids = jnp.swapaxes(ids, 1, 2).reshape(bsz, n_pairs)
    wts = jnp.swapaxes(wts, 1, 2).reshape(bsz, n_pairs)
    key = ids * n_pairs + jnp.arange(n_pairs, dtype=jnp.int32)
    key, wts = lax.sort((key, wts), dimension=1, num_keys=1)
    tok = (key % n_pairs) % t_tot
    count = jnp.sum((ids[:, :, None] == jnp.arange(N_EXPERTS, dtype=jnp.int32)).astype(jnp.int32), axis=1)
    start = jnp.cumsum(count, axis=1) - count
    pad = ((0, 0), (0, MOE_TILE))
    return (jnp.pad(tok, pad).reshape(-1), jnp.pad(wts, pad).reshape(-1), start.reshape(-1), count.reshape(-1))


def _moe_body(tok_ref, w_ref, start_ref, count_ref, h_ref, wg_ref, wu_ref, wd_ref, y_ref, hbuf, obuf):
    c = pl.program_id(0)
    step = pl.program_id(1)
    t_tot = h_ref.shape[0]
    p_len = TOP_K * t_tot + MOE_TILE

    @pl.when(jnp.logical_and(c == 0, step == 0))
    def _():
        hbuf[...] = jnp.zeros_like(hbuf)

    @pl.when(step == 0)
    def _():
        def zero(i, carry):
            y_ref[pl.ds(pl.multiple_of(i * TOKEN_TILE, TOKEN_TILE), TOKEN_TILE), :] = jnp.zeros(
                (TOKEN_TILE, D_MODEL), F32)
            return carry
        lax.fori_loop(0, t_tot // TOKEN_TILE, zero, 0)

    for k in range(MOE_EXPERTS_PER_STEP):
        e = step * MOE_EXPERTS_PER_STEP + k
        _moe_expert(tok_ref, w_ref, h_ref, wg_ref.at[k], wu_ref.at[k], wd_ref.at[k], y_ref, hbuf, obuf,
                    count=count_ref[c * N_EXPERTS + e], pair_base=c * p_len + start_ref[c * N_EXPERTS + e])


def _moe_expert(tok_ref, w_ref, h_ref, wg_ref, wu_ref, wd_ref, y_ref, hbuf, obuf, *, count, pair_base):
    tm = MOE_TILE

    def tile(t, carry):
        n_valid = jnp.minimum(tm, count - t * tm)
        pair0 = pair_base + t * tm

        def gather(g, carry):
            for j in range(8):
                tok = tok_ref[pair0 + g * 8 + j]
                hbuf[g, pl.ds(j, 1), :] = h_ref[pl.ds(tok, 1), :]
            return carry
        lax.fori_loop(0, (n_valid + 7) // 8, gather, 0)

        h = hbuf[...].reshape(tm, D_MODEL).astype(BF16)
        gate = _mm(h, wg_ref[...])
        up = _mm(h, wu_ref[...])
        hid = gate * _sigmoid(gate) * up
        obuf[...] = _mm(hid.astype(BF16), wd_ref[...]).reshape(tm // 8, 8, D_MODEL)

        def add_row(r):
            tok = tok_ref[pair0 + r]
            y_ref[pl.ds(tok, 1), :] += w_ref[pair0 + r] * obuf[r // 8, pl.ds(r % 8, 1), :]

        def scatter(g, carry):
            toks = [tok_ref[pair0 + g * 8 + j] for j in range(8)]
            new = [y_ref[pl.ds(toks[j], 1), :] + w_ref[pair0 + g * 8 + j] * obuf[g, pl.ds(j, 1), :]
                   for j in range(8)]
            for j in range(8):
                y_ref[pl.ds(toks[j], 1), :] = new[j]
            return carry
        lax.fori_loop(0, n_valid // 8, scatter, 0)

        def scatter_tail(r, carry):
            add_row(r)
            return carry
        lax.fori_loop(n_valid // 8 * 8, n_valid, scatter_tail, 0)
        return carry

    lax.fori_loop(0, (count + tm - 1) // tm, tile, 0)


def _moe(h, route, wg, wu, wd, layer, *, bsz, t_tot, skip_rows=0):
    tok, w, start, count = _route_tables(route, bsz, t_tot, skip_rows)
    grid_spec = pltpu.PrefetchScalarGridSpec(
        num_scalar_prefetch=4,
        grid=(bsz, N_EXPERTS // MOE_EXPERTS_PER_STEP),
        in_specs=[
            pl.BlockSpec((None, t_tot, D_MODEL), lambda c, e, *_: (c, 0, 0)),
            pl.BlockSpec((None, MOE_EXPERTS_PER_STEP, D_MODEL, D_EXPERT), lambda c, e, *_: (layer, e, 0, 0)),
            pl.BlockSpec((None, MOE_EXPERTS_PER_STEP, D_MODEL, D_EXPERT), lambda c, e, *_: (layer, e, 0, 0)),
            pl.BlockSpec((None, MOE_EXPERTS_PER_STEP, D_EXPERT, D_MODEL), lambda c, e, *_: (layer, e, 0, 0)),
        ],
        out_specs=pl.BlockSpec((None, t_tot, D_MODEL), lambda c, e, *_: (c, 0, 0)),
        scratch_shapes=[pltpu.VMEM((MOE_TILE // 8, 8, D_MODEL), F32)] * 2,
    )
    y = pl.pallas_call(
        _moe_body,
        grid_spec=grid_spec,
        out_shape=jax.ShapeDtypeStruct((bsz, t_tot, D_MODEL), F32),
        compiler_params=_cparams("arbitrary", "arbitrary"),
        name="moe",
    )(tok, w, start, count, h.reshape(bsz, t_tot, D_MODEL), wg, wu, wd)
    return y.reshape(bsz * t_tot, D_MODEL)


def _residual_body(x_ref, y_ref, m_ref, o_ref):
    o_ref[...] = (x_ref[...] + m_ref[5:6, :] * y_ref[...]).astype(o_ref.dtype)


def _final_residual(x, y, mod_l, dtype, *, bsz, t_ctx, t_tot):
    tm = TOKEN_TILE
    tok = pl.BlockSpec((None, tm, D_MODEL), lambda b, i: (b, t_ctx // tm + i, 0))
    return pl.pallas_call(
        _residual_body,
        grid=(bsz, (t_tot - t_ctx) // tm),
        in_specs=[tok, tok, pl.BlockSpec((None, 6, D_MODEL), lambda b, i: (b, 0, 0))],
        out_specs=pl.BlockSpec((None, tm, D_MODEL), lambda b, i: (b, i, 0)),
        out_shape=jax.ShapeDtypeStruct((bsz, t_tot - t_ctx, D_MODEL), dtype),
        compiler_params=_cparams("arbitrary", "arbitrary"),
        name="residual",
    )(x.reshape(bsz, t_tot, D_MODEL), y.reshape(bsz, t_tot, D_MODEL), mod_l)


def kernel(x, c, ctx, c_ctx, norm1_gain, norm2_gain, w_ada, b_ada, w_in, w_out, hgrn_lb_logits, hgrn_gn_gain, rwkv_mu, rwkv_w0, rwkv_w_up, rwkv_a0, rwkv_a_up, rwkv_g_up, rwkv_kk_scale, rwkv_k_a, rwkv_r_k, rwkv_ln_gain, rwkv_ln_bias, na_q_gain, na_k_gain, na_rpb, w_router_group, b_router_group, w_router_expert, b_router_expert, w_exp_gate, w_exp_up, w_exp_down):
    bsz, t_lat, _ = x.shape
    t_ctx = ctx.shape[1]
    t_tot = t_ctx + t_lat
    n = bsz * t_tot
    assert t_ctx % TOKEN_TILE == 0 and t_lat % TOKEN_TILE == 0 and bsz < 16
    bps = t_tot // TOKEN_TILE
    tile_kw = dict(bps=bps, ctx_blocks=t_ctx // TOKEN_TILE, ctx_row=bsz)

    lb_p = jax.nn.softmax(hgrn_lb_logits.astype(F32), axis=1)
    lower_bounds = jnp.cumsum(lb_p, axis=1) - lb_p[:, :1]

    cc = jnp.zeros((16, D_MODEL), F32).at[:bsz].set(c.astype(F32)).at[bsz].set(c_ctx.astype(F32))
    mod = _ada_mod(cc, w_ada, b_ada).reshape(DEPTH, 16, 6, D_MODEL)

    w_in_bf = w_in.astype(BF16)
    w_out_bf = w_out.astype(BF16)
    wg_bf = w_exp_gate.astype(BF16).reshape(DEPTH, N_EXPERTS, D_MODEL, D_EXPERT)
    wu_bf = w_exp_up.astype(BF16).reshape(DEPTH, N_EXPERTS, D_MODEL, D_EXPERT)
    wd_bf = w_exp_down.astype(BF16).reshape(DEPTH, N_EXPERTS, D_EXPERT, D_MODEL)
    pad = LANES - N_EXPERTS - N_GROUPS
    w_router = jnp.concatenate([w_router_expert, w_router_group,
                                jnp.zeros((DEPTH, D_MODEL, pad), F32)], axis=-1)
    b_router = jnp.concatenate([b_router_expert, b_router_group, jnp.zeros((DEPTH, pad), F32)], axis=-1)

    na_bias = _na_bias_table(na_rpb, t_lat // GRID_W)
    xs = jnp.concatenate([ctx.astype(F32), x.astype(F32)], axis=1).reshape(n, D_MODEL)
    y_moe = None
    for layer in range(DEPTH):
        last = layer == DEPTH - 1
        if layer == 0:
            pa, pb, pc = _inproj(xs, mod[layer], norm1_gain[layer], w_in_bf[layer], **tile_kw)
        else:
            xs, pa, pb, pc = _inproj(xs, mod[layer], norm1_gain[layer], w_in_bf[layer],
                                     (y_moe, mod[layer - 1]), **tile_kw)
        a_mix = _hgrn(pa.reshape(bsz, t_tot, A_PROJ), lower_bounds[:, layer], hgrn_gn_gain[layer], t_ctx=t_ctx)
        prep = _rwkv_prep(pb.reshape(bsz, t_tot, B_PROJ), rwkv_mu[layer], rwkv_w0[layer], rwkv_w_up[layer],
                          rwkv_a0[layer], rwkv_a_up[layer], rwkv_g_up[layer], rwkv_kk_scale[layer],
                          rwkv_k_a[layer], rwkv_r_k[layer], t_ctx=t_ctx)
        y = _rwkv_scan(*prep[:9], t_ctx=t_ctx)
        c_mix = _na(pc.reshape(bsz, t_tot, C_PROJ), na_q_gain[layer], na_k_gain[layer], na_bias, layer,
                    t_ctx=t_ctx, need_ctx=not last)
        xs, h2, route = _outproj(xs, a_mix.reshape(n, A_WIDTH), y.reshape(n, B_WIDTH), prep[9].reshape(n, B_WIDTH),
                                 prep[10].reshape(n, B_WIDTH), rwkv_ln_gain[layer], rwkv_ln_bias[layer],
                                 c_mix.reshape(n, C_WIDTH), mod[layer], norm2_gain[layer], w_out_bf[layer],
                                 w_router[layer], b_router[layer].reshape(1, LANES), **tile_kw)
        y_moe = _moe(h2, route, wg_bf, wu_bf, wd_bf, layer, bsz=bsz, t_tot=t_tot, skip_rows=t_ctx if last else 0)
    return _final_residual(xs, y_moe, mod[DEPTH - 1], x.dtype, bsz=bsz, t_ctx=t_ctx, t_tot=t_tot)
```

```python
import functools

import jax
import jax.numpy as jnp
import numpy as np
from jax import lax
from jax.experimental import pallas as pl
from jax.experimental.pallas import tpu as pltpu

F32 = jnp.float32
BF16 = jnp.bfloat16

D_MODEL = 1024
DEPTH = 4
GRID_W = 64
HEAD_DIM = 64
A_HEADS = 4
B_HEADS = 4
C_HEADS = 8
A_WIDTH = A_HEADS * HEAD_DIM
B_WIDTH = B_HEADS * HEAD_DIM
C_WIDTH = C_HEADS * HEAD_DIM
HGRN_CHUNK = 16
HGRN_F_FLOOR = 1e-20
RWKV_W_RANK = 64
RWKV_A_RANK = 64
RWKV_G_RANK = 128
RWKV_LN_EPS = 64e-5
NA_WIN_ROWS = 8
NA_WIN_COLS = 16
ROPE_THETA = 10000.0
N_GROUPS = 4
EXPERTS_PER_GROUP = 8
N_EXPERTS = N_GROUPS * EXPERTS_PER_GROUP
TOP_K = 2
D_EXPERT = 512
EPS = 1e-6
LOG2E = 1.4426950408889634
NEG_INF = -1e30
A_PROJ = 5 * A_WIDTH
B_PROJ = 3 * B_WIDTH + 2 * RWKV_W_RANK + 2 * RWKV_A_RANK + RWKV_G_RANK
C_PROJ = 3 * C_WIDTH
P_TOTAL = A_PROJ + B_PROJ + C_PROJ
B_TAIL = B_PROJ - 3 * B_WIDTH

LANES = 128
SUBLANES = 8
PROJ_TILES = 2
TOKEN_TILE = 256
RWKV_CHUNK = 64
RWKV_BLOCKS_PER_ITER = 2
RWKV_BLOCK = 128
HGRN_BLOCK = 128
NA_ROW_UNROLL = 8
HGRN_SCAN_UNROLL = 16
MOE_EXPERTS_PER_STEP = 2
MOE_TILE = 192
VMEM_LIMIT = 56 * 1024 * 1024

_HI = lax.Precision.HIGHEST


def _cparams(*sem):
    return pltpu.CompilerParams(dimension_semantics=sem, vmem_limit_bytes=VMEM_LIMIT)


def _mm(a, b):
    return jnp.dot(a, b, preferred_element_type=F32)


def _nt(a, b):
    return lax.dot_general(a, b, (((1,), (1,)), ((), ())), preferred_element_type=F32)


def _tn(a, b):
    return lax.dot_general(a, b, (((0,), (0,)), ((), ())), preferred_element_type=F32)


def _split3(a):
    a1 = a.astype(BF16)
    r1 = a - a1.astype(F32)
    a2 = r1.astype(BF16)
    a3 = (r1 - a2.astype(F32)).astype(BF16)
    return a1, a2, a3


def _mm_x3(a, m):
    a1, a2, a3 = _split3(a)
    return _mm(a1, m) + _mm(a2, m) + _mm(a3, m)


def _mm_l3(m, a):
    a1, a2, a3 = _split3(a)
    return _mm(m, a1) + _mm(m, a2) + _mm(m, a3)


def _mm_3pass(a, b):
    a1 = a.astype(BF16)
    a2 = (a - a1.astype(F32)).astype(BF16)
    b1 = b.astype(BF16)
    b2 = (b - b1.astype(F32)).astype(BF16)
    return _mm(a1, b1) + (_mm(a1, b2) + _mm(a2, b1))


def _sigmoid(x):
    return 1.0 / (1.0 + jnp.exp(-x))


def _iota(shape, dim):
    return lax.broadcasted_iota(jnp.int32, shape, dim)


def _head_seg(n):
    return jnp.where(_iota((n, n), 0) // HEAD_DIM == _iota((n, n), 1) // HEAD_DIM, 1.0, 0.0).astype(BF16)


def _ada_body(c_ref, w_ref, b_ref, o_ref):
    c = c_ref[...]
    o_ref[...] = _mm_3pass(c * _sigmoid(c), w_ref[...]) + b_ref[...]


def _ada_mod(cc, w_ada, b_ada):
    rows = cc.shape[0]
    tn = 1536
    return pl.pallas_call(
        _ada_body,
        grid=(DEPTH, 6 * D_MODEL // tn),
        in_specs=[
            pl.BlockSpec((rows, D_MODEL), lambda l, j: (0, 0)),
            pl.BlockSpec((None, D_MODEL, tn), lambda l, j: (l, 0, j)),
            pl.BlockSpec((None, 1, tn), lambda l, j: (l, 0, j)),
        ],
        out_specs=pl.BlockSpec((None, rows, tn), lambda l, j: (l, 0, j)),
        out_shape=jax.ShapeDtypeStruct((DEPTH, rows, 6 * D_MODEL), F32),
        compiler_params=_cparams("arbitrary", "arbitrary"),
        name="ada_mod",
    )(cc, w_ada, b_ada.reshape(DEPTH, 1, 6 * D_MODEL))


def _mod_row(i, blocks_per_seq, ctx_blocks, ctx_row):
    return jnp.where(i % blocks_per_seq < ctx_blocks, ctx_row, i // blocks_per_seq)


def _inproj_body(*refs, with_moe, per):
    tt = TOKEN_TILE
    if with_moe:
        x_ref, y_ref = refs[:2]
        mp_refs, m_refs = refs[2:2 + per], refs[2 + per:2 + 2 * per]
        g_ref, w_ref, xo_ref, pa_ref, pb_ref, pc_ref = refs[2 + 2 * per:]
    else:
        x_ref = refs[0]
        m_refs = refs[1:1 + per]
        g_ref, w_ref, pa_ref, pb_ref, pc_ref = refs[1 + per:]
    hs = []
    for j, m_ref in enumerate(m_refs):
        rows = slice(j * tt, (j + 1) * tt)
        x = x_ref[rows, :]
        if with_moe:
            x = x + mp_refs[j][5:6, :] * y_ref[rows, :]
            xo_ref[rows, :] = x
        h = x * lax.rsqrt(jnp.mean(x * x, axis=-1, keepdims=True) + EPS) * g_ref[...]
        hs.append((h * (1.0 + m_ref[1:2, :]) + m_ref[0:1, :]).astype(BF16))
    h = jnp.concatenate(hs, axis=0)
    pa_ref[...] = _mm(h, w_ref[:, :A_PROJ])
    pb_ref[...] = _mm(h, w_ref[:, A_PROJ:A_PROJ + B_PROJ])
    pc_ref[...] = _mm(h, w_ref[:, A_PROJ + B_PROJ:])


def _mod_specs(per, bps, ctx_blocks, ctx_row):
    row = functools.partial(_mod_row, blocks_per_seq=bps, ctx_blocks=ctx_blocks, ctx_row=ctx_row)
    return [pl.BlockSpec((None, 6, D_MODEL), lambda i, j=j: (row(i * per + j), 0, 0)) for j in range(per)]


def _inproj(x, mod_l, gain, w_bf, moe=None, *, bps, ctx_blocks, ctx_row):
    n = x.shape[0]
    per = PROJ_TILES
    tm = per * TOKEN_TILE
    tok = lambda w: pl.BlockSpec((tm, w), lambda i: (i, 0))
    mods = _mod_specs(per, bps, ctx_blocks, ctx_row)
    with_moe = moe is not None
    ins = (x, moe[0], *([moe[1]] * per)) if with_moe else (x,)
    return pl.pallas_call(
        functools.partial(_inproj_body, with_moe=with_moe, per=per),
        grid=(n // tm,),
        in_specs=[tok(D_MODEL)] + ([tok(D_MODEL)] + mods if with_moe else []) + mods + [
            pl.BlockSpec((1, D_MODEL), lambda i: (0, 0)),
            pl.BlockSpec((D_MODEL, P_TOTAL), lambda i: (0, 0)),
        ],
        out_specs=([tok(D_MODEL)] if with_moe else []) + [tok(A_PROJ), tok(B_PROJ), tok(C_PROJ)],
        out_shape=([jax.ShapeDtypeStruct((n, D_MODEL), F32)] if with_moe else []) + [
            jax.ShapeDtypeStruct((n, A_PROJ), F32),
            jax.ShapeDtypeStruct((n, B_PROJ), F32),
            jax.ShapeDtypeStruct((n, C_PROJ), F32),
        ],
        compiler_params=_cparams("arbitrary"),
        name="inproj",
    )(*ins, *([mod_l] * per), gain.reshape(1, D_MODEL), w_bf)


def _hgrn_body(q_ref, ff_ref, fb_ref, i_ref, g_ref, lb_ref, gn_ref, o_ref,
               qin_s, kout_s, cum_s, acc_s, kpad_s, cpad_s, vpad_s, *, t_ctx):
    t_tot = q_ref.shape[0]
    c = HGRN_CHUNK
    rb = HGRN_BLOCK
    n_blk = t_tot // rb
    n_chunk = t_tot // c
    n_chunk_ctx = t_ctx // c
    row = _iota((rb, rb), 0)
    col = _iota((rb, rb), 1)
    same = (row // c) == (col // c)
    same_bf = jnp.where(same, 1.0, 0.0).astype(BF16)
    seg = _head_seg(LANES)
    pos = _iota((rb, LANES), 0) % c
    blockdiag = _iota((LANES, LANES), 0) // HEAD_DIM == _iota((LANES, LANES), 1) // HEAD_DIM

    pad = HGRN_CHUNK
    zpad = jnp.zeros((pad, LANES), F32)
    for d in range(2):
        for ref in (kpad_s, cpad_s):
            ref[d, 0:pad, :] = zpad
            ref[d, pad + t_tot:, :] = zpad
    vpad_s[0:pad, :] = zpad
    vpad_s[pad + t_tot:, :] = zpad
    tris = [jnp.where(same & ((col >= row) if d == 1 else (col <= row)), 1.0, 0.0).astype(BF16) for d in range(2)]

    def gates(b, carry):
        sl = pl.ds(pl.multiple_of(b * rb, rb), rb)
        slp = pl.ds(pl.multiple_of(b * rb, rb) + pad, rb)
        q = q_ref[sl, :]
        vpad_s[slp, :] = i_ref[sl, :]
        for d, f_ref in enumerate((ff_ref, fb_ref)):
            lb = lb_ref[d:d + 1, :]
            fpre = f_ref[sl, :]
            f = lb + (1.0 - lb) * _sigmoid(fpre)
            logf = jnp.log(jnp.maximum(f, HGRN_F_FLOOR))
            k = (1.0 - lb) * _sigmoid(-fpre)
            cum = _mm_l3(tris[d], logf)
            tot = _mm_l3(same_bf, logf)
            kpad_s[d, slp, :] = k
            cpad_s[d, slp, :] = cum * LOG2E
            qin_s[d, sl, :] = q * jnp.exp(cum)
            kout_s[d, sl, :] = k * jnp.exp(tot - cum)
            cum_s[d, sl, :] = tot
        return carry

    lax.fori_loop(0, n_blk, gates, 0)

    def intra(b, carry):
        r0 = pl.multiple_of(b * rb, rb)
        sl = pl.ds(r0, rb)
        q = q_ref[sl, :]
        cums = [cpad_s[d, pl.ds(r0 + pad, rb), :] for d in range(2)]
        o = jnp.zeros((rb, LANES), F32)
        for j in range(c):
            for d in range(2):
                off = r0 + pad + (j if d == 1 else -j)
                valid = (pos <= c - 1 - j) if d == 1 else (pos >= j)
                ks = kpad_s[d, pl.ds(off, rb), :]
                cs = cpad_s[d, pl.ds(off, rb), :]
                vs = vpad_s[pl.ds(off, rb), :]
                prod = jnp.where(valid, q * ks * jnp.exp2(cums[d] - cs), 0.0)
                o = o + _mm(prod.astype(BF16), seg) * vs
        acc_s[sl, :] = o
        return carry

    lax.fori_loop(0, n_blk, intra, 0)

    def step(it, carry):
        idx = [it * HGRN_SCAN_UNROLL + j for j in range(HGRN_SCAN_UNROLL)]
        order = [(i, jnp.where(i < n_chunk_ctx, n_chunk_ctx - 1 - i, n_chunk - 1 - (i - n_chunk_ctx))) for i in idx]
        sls = [[pl.ds(pl.multiple_of(n[d] * c, c), c) for d in range(2)] for n in order]
        kvs = [[jnp.where(blockdiag, _tn(i_ref[sl[d], :].astype(BF16), kout_s[d, sl[d], :].astype(BF16)), 0.0)
                for d in range(2)] for sl in sls]
        decs = [[jnp.exp(cum_s[d, pl.ds(pl.multiple_of(n[d] * c, c), 1), :]) for d in range(2)] for n in order]
        states = list(carry)
        for j in range(HGRN_SCAN_UNROLL):
            for d in range(2):
                acc_s[sls[j][d], :] += _nt(qin_s[d, sls[j][d], :].astype(BF16), states[d].astype(BF16))
                states[d] = states[d] * decs[j][d] + kvs[j][d]
        return tuple(states)

    zero = jnp.zeros((LANES, LANES), F32)
    lax.fori_loop(0, n_chunk // HGRN_SCAN_UNROLL, step, (zero, zero))

    def readout(b, carry):
        sl = pl.ds(pl.multiple_of(b * rb, rb), rb)
        o = acc_s[sl, :]
        ms = _mm((o * o).astype(BF16), seg) * (1.0 / HEAD_DIM)
        g = g_ref[sl, :]
        o_ref[sl, :] = o * lax.rsqrt(ms + EPS) * gn_ref[...] * (g * _sigmoid(g))
        return carry

    lax.fori_loop(0, n_blk, readout, 0)


def _hgrn(pa, lb, gn_gain, *, t_ctx):
    bsz, t_tot, _ = pa.shape
    n_hp = A_WIDTH // LANES

    def sec(s):
        return pl.BlockSpec((None, t_tot, LANES), lambda b, h, s=s: (b, 0, s * n_hp + h))

    return pl.pallas_call(
        functools.partial(_hgrn_body, t_ctx=t_ctx),
        grid=(bsz, n_hp),
        in_specs=[sec(0), sec(1), sec(2), sec(3), sec(4),
                  pl.BlockSpec((2, LANES), lambda b, h: (0, h)),
                  pl.BlockSpec((1, LANES), lambda b, h: (0, h))],
        out_specs=pl.BlockSpec((None, t_tot, LANES), lambda b, h: (b, 0, h)),
        out_shape=jax.ShapeDtypeStruct((bsz, t_tot, A_WIDTH), F32),
        scratch_shapes=[
            pltpu.VMEM((2, t_tot, LANES), F32),
            pltpu.VMEM((2, t_tot, LANES), F32),
            pltpu.VMEM((2, t_tot, LANES), F32),
            pltpu.VMEM((t_tot, LANES), F32),
            pltpu.VMEM((2, t_tot + 2 * HGRN_CHUNK, LANES), F32),
            pltpu.VMEM((2, t_tot + 2 * HGRN_CHUNK, LANES), F32),
            pltpu.VMEM((t_tot + 2 * HGRN_CHUNK, LANES), F32),
        ],
        compiler_params=_cparams("arbitrary", "arbitrary"),
        name="hgrn",
    )(pa, pa, pa, pa, pa, lb, gn_gain.reshape(1, A_WIDTH))


def _rwkv_prep_body(x_ref, xp_ref, xn_ref, mu_ref, w0_ref, wup_ref, a0_ref, aup_ref, gup_ref,
                    kks_ref, ka_ref, rk_ref,
                    r_o, v_o, kk_o, kf_o, kb_o, bf_o, bb_o, lwf_o, lwb_o, g_o, bonus_o, *, t_ctx, t_tot):
    tm = x_ref.shape[0]
    r0 = pl.program_id(1) * tm
    prev_ok = jnp.logical_and(r0 != 0, r0 != t_ctx)
    next_ok = jnp.logical_and(r0 + tm != t_ctx, r0 + tm != t_tot)
    first = _iota((tm, 1), 0) == 0
    last = _iota((tm, 1), 0) == tm - 1

    def shifted(lo, hi):
        x = x_ref[:, lo:hi]
        p_row = jnp.where(prev_ok, xp_ref[SUBLANES - 1:SUBLANES, lo:hi], 0.0)
        n_row = jnp.where(next_ok, xn_ref[0:1, lo:hi], 0.0)
        prev = jnp.where(first, p_row, pltpu.roll(x, 1, 0))
        nxt = jnp.where(last, n_row, pltpu.roll(x, tm - 1, 0))
        return x + (0.5 * (prev + nxt) - x) * mu_ref[:, lo:hi]

    bw = B_WIDTH
    r = shifted(0, bw)
    k = shifted(bw, 2 * bw)
    v = shifted(2 * bw, 3 * bw)
    tail = shifted(3 * bw, B_PROJ)
    seg = _head_seg(bw)

    kk = k * kks_ref[...]
    kk = kk / jnp.maximum(jnp.sqrt(_mm_x3(kk * kk, seg)), 1e-12)
    ksum = jnp.zeros_like(k)
    for d, (k_o, b_o, lw_o) in enumerate(((kf_o, bf_o, lwf_o), (kb_o, bb_o, lwb_o))):
        wd = tail[:, d * RWKV_W_RANK:(d + 1) * RWKV_W_RANK]
        ad = tail[:, 2 * RWKV_W_RANK + d * RWKV_A_RANK:2 * RWKV_W_RANK + (d + 1) * RWKV_A_RANK]
        u = -(w0_ref[d:d + 1, :] + _mm_3pass(jnp.tanh(wd), wup_ref[d]))
        w = -(jnp.maximum(u, 0.0) + jnp.log(1.0 + jnp.exp(-jnp.abs(u)))) - 0.5
        a = _sigmoid(a0_ref[d:d + 1, :] + _mm_3pass(ad, aup_ref[d]))
        k_d = k * (1.0 + (a - 1.0) * ka_ref[...])
        ksum = ksum + k_d
        k_o[...] = k_d
        b_o[...] = a * kk
        lw_o[...] = -jnp.exp(w)
    gd = tail[:, 2 * RWKV_W_RANK + 2 * RWKV_A_RANK:]
    r_o[...] = r
    v_o[...] = v
    kk_o[...] = kk
    g_o[...] = _mm_3pass(_sigmoid(gd), gup_ref[...])
    bonus_o[...] = _mm((r * ksum * rk_ref[...]).astype(BF16), seg) * v


def _rwkv_prep(pb, mu, w0, w_up, a0, a_up, g_up, kk_scale, k_a, r_k, *, t_ctx):
    bsz, t_tot, _ = pb.shape
    tm = TOKEN_TILE
    nb8 = t_tot // SUBLANES
    per8 = tm // SUBLANES
    bw = B_WIDTH
    full = lambda shape: pl.BlockSpec(shape, lambda b, i: (0,) * len(shape))
    out = jax.ShapeDtypeStruct((bsz, t_tot, bw), F32)
    return pl.pallas_call(
        functools.partial(_rwkv_prep_body, t_ctx=t_ctx, t_tot=t_tot),
        grid=(bsz, t_tot // tm),
        in_specs=[
            pl.BlockSpec((None, tm, B_PROJ), lambda b, i: (b, i, 0)),
            pl.BlockSpec((None, SUBLANES, B_PROJ), lambda b, i: (b, jnp.maximum(i * per8 - 1, 0), 0)),
            pl.BlockSpec((None, SUBLANES, B_PROJ), lambda b, i: (b, jnp.minimum((i + 1) * per8, nb8 - 1), 0)),
            full((1, B_PROJ)), full((2, bw)), full((2, RWKV_W_RANK, bw)), full((2, bw)),
            full((2, RWKV_A_RANK, bw)), full((RWKV_G_RANK, bw)), full((1, bw)), full((1, bw)), full((1, bw)),
        ],
        out_specs=[pl.BlockSpec((None, tm, bw), lambda b, i: (b, i, 0))] * 11,
        out_shape=[out] * 11,
        compiler_params=_cparams("arbitrary", "arbitrary"),
        name="rwkv_prep",
    )(pb, pb, pb, mu.reshape(1, B_PROJ), w0, w_up, a0, a_up, g_up,
      kk_scale.reshape(1, bw), k_a.reshape(1, bw), r_k.reshape(1, bw))


def _rwkv_scan_body(r_ref, v_ref, kk_ref, kf_ref, kb_ref, bf_ref, bb_ref, lwf_ref, lwb_ref, y_ref,
                    rq_s, y0_s, p_s, z_s, *, t_ctx):
    t_tot = r_ref.shape[0]
    c = RWKV_CHUNK
    rb = RWKV_BLOCK
    n_blk = t_tot // rb
    n_chunk = t_tot // c
    n_chunk_ctx = t_ctx // c
    cpb = rb // c
    n_double = c.bit_length() - 2
    row = _iota((rb, rb), 0)
    col = _iota((rb, rb), 1)
    same = (row // c) == (col // c)
    same_bf = jnp.where(same, 1.0, 0.0).astype(BF16)
    eye = jnp.where(row == col, 1.0, 0.0)
    lane = _iota((1, LANES), 1)
    head_masks = [jnp.where(lane // HEAD_DIM == h, 1.0, 0.0) for h in range(LANES // HEAD_DIM)]
    r128 = _iota((LANES, LANES), 0)
    c128 = _iota((LANES, LANES), 1)
    blockdiag = (r128 // HEAD_DIM) == (c128 // HEAD_DIM)
    diag128 = r128 == c128

    def block(it, carry):
        bis = [it * RWKV_BLOCKS_PER_ITER + j for j in range(RWKV_BLOCKS_PER_ITER)]
        sls = [pl.ds(pl.multiple_of(bi * rb, rb), rb) for bi in bis]
        v_bfs = [v_ref[sl, :].astype(BF16) for sl in sls]
        pre = {}
        for j, sl in enumerate(sls):
            r = r_ref[sl, :]
            kk = kk_ref[sl, :]
            for d in range(2):
                rev = d == 1
                k_ref, b_ref, lw_ref = (kb_ref, bb_ref, lwb_ref) if rev else (kf_ref, bf_ref, lwf_ref)
                incl = same & ((col >= row) if rev else (col <= row))
                strict = same & ((col > row) if rev else (col < row))
                incl_bf = jnp.where(incl, 1.0, 0.0).astype(BF16)
                lw = lw_ref[sl, :]
                k = k_ref[sl, :]
                b = b_ref[sl, :]
                cl = _mm_l3(incl_bf, lw)
                tot = _mm_l3(same_bf, lw)
                w_inv = jnp.exp(-cl)
                w_end = jnp.exp(tot - cl)
                pre[j, d] = dict(incl=incl, strict=strict, tot=tot, kk_d=kk * jnp.exp(cl - lw), r_d=r * jnp.exp(cl),
                                 kb=jnp.concatenate([k * w_inv, b * w_inv], axis=0).astype(BF16),
                                 b_e=(b * w_end).astype(BF16), k_e=(k * w_end).astype(BF16))
        chains = [dict(j=j, d=d, mh=mh, kkm=pre[j, d]['kk_d'] * mh, rm=pre[j, d]['r_d'] * mh)
                  for j in range(len(bis)) for d in range(2) for mh in head_masks]
        for ch in chains:
            p = pre[ch['j'], ch['d']]
            aa = _nt(jnp.concatenate([ch['kkm'], ch['rm']], axis=0).astype(BF16), p['kb'])
            ch['a_kb'] = jnp.where(p['strict'], aa[:rb, :rb], 0.0)
            ch['a_rk'] = jnp.where(p['incl'], aa[rb:, :rb], 0.0)
            ch['a_rb'] = jnp.where(p['incl'], aa[rb:, rb:], 0.0)
            m = -jnp.where(p['strict'], aa[:rb, rb:], 0.0)
            ch['t'] = eye + m
            ch['m'] = m
        for ch in chains:
            m_bf = ch['m'].astype(BF16)
            ch['m'] = _mm(m_bf, m_bf)
        for _ in range(n_double - 1):
            for ch in chains:
                both = _mm(jnp.concatenate([ch['t'], ch['m']], axis=0).astype(BF16), ch['m'].astype(BF16))
                ch['t'] = ch['t'] + both[:rb]
                ch['m'] = both[rb:]
        for ch in chains:
            ch['t'] = ch['t'] + _mm(ch['t'].astype(BF16), ch['m'].astype(BF16))
            ch['av'] = _mm(jnp.concatenate([ch['a_kb'], ch['a_rk']], axis=0).astype(BF16), v_bfs[ch['j']])
        for ch in chains:
            ch['ku'] = _mm(ch['t'].astype(BF16), jnp.concatenate([ch['kkm'], ch['av'][:rb]], axis=1).astype(BF16))
        for ch in chains:
            ch['rb_ku'] = _mm(ch['a_rb'].astype(BF16), ch['ku'].astype(BF16))
        for j, (bi, sl) in enumerate(zip(bis, sls)):
            for d in range(2):
                p = pre[j, d]
                mine = [ch for ch in chains if ch['j'] == j and ch['d'] == d]
                kkt = sum(ch['ku'][:, :LANES] for ch in mine)
                u = sum(ch['ku'][:, LANES:] * ch['mh'] for ch in mine)
                rq_s[d, sl, :] = sum(ch['rm'] - ch['rb_ku'][:, :LANES] for ch in mine)
                y0_s[d, sl, :] = sum((ch['av'][rb:] - ch['rb_ku'][:, LANES:]) * ch['mh'] for ch in mine)
                ktu = jnp.concatenate([kkt, u], axis=1).astype(BF16)
                for n in range(cpb):
                    rows = slice(n * c, (n + 1) * c)
                    wc = jnp.exp(p['tot'][n * c:n * c + 1, :])
                    kub = _tn(ktu[rows], p['b_e'][rows])
                    p_s[d, bi * cpb + n] = jnp.where(diag128, wc, 0.0) - jnp.where(blockdiag, kub[:LANES], 0.0)
                    p_z = jnp.where(blockdiag, _tn(v_bfs[j][rows], p['k_e'][rows]) - kub[LANES:], 0.0)
                    z_s[d, bi * cpb + n] = p_z
        return carry

    lax.fori_loop(0, n_blk // RWKV_BLOCKS_PER_ITER, block, 0)

    y_ref[...] = jnp.zeros_like(y_ref)

    def step(i, carry):
        n_b = jnp.where(i < n_chunk_ctx, n_chunk_ctx - 1 - i, n_chunk - 1 - (i - n_chunk_ctx))
        chunk = (i, n_b)
        s_bf = [s.astype(BF16) for s in carry]
        ys = [_nt(rq_s[d, pl.ds(pl.multiple_of(chunk[d] * c, c), c), :].astype(BF16), s_bf[d]) for d in range(2)]
        states = [_mm(s_bf[d], p_s[d, chunk[d]].astype(BF16)) + z_s[d, chunk[d]] for d in range(2)]
        for d in range(2):
            sl = pl.ds(pl.multiple_of(chunk[d] * c, c), c)
            y_ref[sl, :] += ys[d] + y0_s[d, sl, :]
        return tuple(states)

    zero = jnp.zeros((LANES, LANES), F32)
    lax.fori_loop(0, n_chunk, step, (zero, zero))


def _rwkv_scan(r, v, kk, kf, kb, bf, bb, lwf, lwb, *, t_ctx):
    bsz, t_tot, bw = r.shape
    n_chunk = t_tot // RWKV_CHUNK
    spec = pl.BlockSpec((None, t_tot, LANES), lambda b, h: (b, 0, h))
    return pl.pallas_call(
        functools.partial(_rwkv_scan_body, t_ctx=t_ctx),
        grid=(bsz, bw // LANES),
        in_specs=[spec] * 9,
        out_specs=spec,
        out_shape=jax.ShapeDtypeStruct((bsz, t_tot, bw), F32),
        scratch_shapes=[
            pltpu.VMEM((2, t_tot, LANES), F32),
            pltpu.VMEM((2, t_tot, LANES), F32),
            pltpu.VMEM((2, n_chunk, LANES, LANES), F32),
            pltpu.VMEM((2, n_chunk, LANES, LANES), F32),
        ],
        compiler_params=_cparams("arbitrary", "arbitrary"),
        name="rwkv_scan",
    )(r, v, kk, kf, kb, bf, bb, lwf, lwb)


def _rwkv_readout(y, g, bonus, ln_gain, ln_bias):
    seg = _head_seg(B_WIDTH)
    mean = _mm_x3(y, seg) * (1.0 / HEAD_DIM)
    yc = y - mean
    var = _mm((yc * yc).astype(BF16), seg) * (1.0 / HEAD_DIM)
    return (yc * lax.rsqrt(var + RWKV_LN_EPS) * ln_gain + ln_bias + bonus) * g


def _na_body(q_ref, k_ref, v_ref, qg_ref, kg_ref, cos_ref, sin_ref, bias_ref, o_ref,
             qs, ks, vs, *, t_ctx, need_ctx):
    t_tot = q_ref.shape[0]
    t_lat = t_tot - t_ctx
    rows = t_lat // GRID_W
    win_r = min(NA_WIN_ROWS, rows)
    scale = HEAD_DIM ** -0.5
    seg = _head_seg(LANES)
    lane = _iota((1, LANES), 1)
    head_masks = [jnp.where(lane // HEAD_DIM == h, 1.0, 0.0) for h in range(LANES // HEAD_DIM)]
    half = (lane % HEAD_DIM) < HEAD_DIM // 2
    blk = 256

    def norm_block(i, carry):
        sl = pl.ds(pl.multiple_of(i * blk, blk), blk)
        q = q_ref[sl, :]
        k = k_ref[sl, :]
        q = q * lax.rsqrt(_mm((q * q).astype(BF16), seg) * (1.0 / HEAD_DIM) + EPS) * qg_ref[...]
        k = k * lax.rsqrt(_mm((k * k).astype(BF16), seg) * (1.0 / HEAD_DIM) + EPS) * kg_ref[...]
        cos = cos_ref[sl, :]
        sin = sin_ref[sl, :]

        def rope(t):
            swapped = jnp.where(half, pltpu.roll(t, LANES - HEAD_DIM // 2, 1), pltpu.roll(t, HEAD_DIM // 2, 1))
            return t * cos + swapped * sin

        qs[sl, :] = (rope(q) * scale).astype(BF16)
        ks[sl, :] = rope(k).astype(BF16)
        vs[sl, :] = v_ref[sl, :].astype(BF16)
        return carry

    lax.fori_loop(0, t_tot // blk, norm_block, 0)

    masks_bf = [mh.astype(BF16) for mh in head_masks]

    def by_head(q):
        return jnp.concatenate([q * mb for mb in masks_bf], axis=0)

    def merge_heads(o, n):
        return sum(o[h * n:(h + 1) * n] * mh for h, mh in enumerate(head_masks))

    if need_ctx:
        s = _nt(by_head(qs[0:t_ctx, :]), ks[0:t_ctx, :])
        p = jnp.exp(s - jnp.max(s, axis=-1, keepdims=True))
        o = _mm(p.astype(BF16), vs[0:t_ctx, :]) / jnp.sum(p, axis=-1, keepdims=True)
        o_ref[0:t_ctx, :] = merge_heads(o, t_ctx)
    else:
        o_ref[0:t_ctx, :] = jnp.zeros((t_ctx, LANES), F32)

    def q_rows(it, carry):
        rs = [it * NA_ROW_UNROLL + j for j in range(NA_ROW_UNROLL)]
        starts = [jnp.clip(r - win_r // 2, 0, rows - win_r) for r in rs]
        q_sl = [pl.ds(pl.multiple_of(t_ctx + r * GRID_W, GRID_W), GRID_W) for r in rs]
        k_sl = [pl.ds(pl.multiple_of(t_ctx + r0 * GRID_W, GRID_W), win_r * GRID_W) for r0 in starts]
        qs_ = [by_head(qs[sl, :]) for sl in q_sl]
        s_win = [_nt(q, ks[sl, :]) + bias_ref[r - r0] for q, sl, r, r0 in zip(qs_, k_sl, rs, starts)]
        s_ctx = [_nt(q, ks[0:t_ctx, :]) for q in qs_]
        ms = [jnp.maximum(jnp.max(a, axis=-1, keepdims=True), jnp.max(b, axis=-1, keepdims=True))
              for a, b in zip(s_win, s_ctx)]
        p_win = [jnp.exp(a - m) for a, m in zip(s_win, ms)]
        p_ctx = [jnp.exp(b - m) for b, m in zip(s_ctx, ms)]
        den = [jnp.sum(a, axis=-1, keepdims=True) + jnp.sum(b, axis=-1, keepdims=True) for a, b in zip(p_win, p_ctx)]
        o_win = [_mm(a.astype(BF16), vs[sl, :]) for a, sl in zip(p_win, k_sl)]
        o_ctx = [_mm(b.astype(BF16), vs[0:t_ctx, :]) for b in p_ctx]
        for sl, a, b, d in zip(q_sl, o_win, o_ctx, den):
            o_ref[sl, :] = merge_heads((a + b) / d, GRID_W)
        return carry

    lax.fori_loop(0, rows // NA_ROW_UNROLL, q_rows, 0)


def _na_tables(t_ctx, t_lat):
    quarter = HEAD_DIM // 4
    pos = np.arange(t_lat)
    inv = ROPE_THETA ** (-np.arange(quarter, dtype=np.float32) / quarter)
    pos_r = (pos // GRID_W).astype(np.float32)
    pos_c = (pos % GRID_W).astype(np.float32)
    return pos_r, pos_c, inv


def _na_bias_table(rpb, rows):
    win_r = min(NA_WIN_ROWS, rows)
    c = np.arange(GRID_W)
    w_start = np.clip(c - NA_WIN_COLS // 2, 0, GRID_W - NA_WIN_COLS)
    kc = np.arange(GRID_W)
    in_win = (kc[None, :] >= w_start[:, None]) & (kc[None, :] < w_start[:, None] + NA_WIN_COLS)
    col_idx = np.clip(kc[None, :] - c[:, None] + NA_WIN_COLS - 1, 0, 2 * NA_WIN_COLS - 2)
    n_col = 2 * NA_WIN_COLS - 1
    onehot = jnp.asarray(col_idx[None] == np.arange(n_col)[:, None, None], F32)
    band = jnp.einsum('...hrj,jck->...hrck', rpb.astype(F32), onehot, precision=_HI)
    band = jnp.where(in_win, band, NEG_INF)
    per_off = [band[..., NA_WIN_ROWS - 1 - off:NA_WIN_ROWS - 1 - off + win_r, :, :] for off in range(win_r)]
    g = jnp.stack(per_off, axis=-4)
    g = jnp.swapaxes(g, -3, -2)
    return g.reshape(g.shape[:-2] + (win_r * GRID_W,))


def _na(pc, q_gain, k_gain, bias, layer, *, t_ctx, need_ctx):
    bsz, t_tot, _ = pc.shape
    t_lat = t_tot - t_ctx
    rows = t_lat // GRID_W
    win_r = min(NA_WIN_ROWS, rows)
    n_hp = C_WIDTH // LANES
    hpl = LANES // HEAD_DIM
    pos_r, pos_c, inv = _na_tables(t_ctx, t_lat)
    ang = np.concatenate([pos_r[:, None] * inv, pos_c[:, None] * inv], axis=-1)
    ang = np.concatenate([np.zeros((t_ctx, HEAD_DIM // 2), np.float32), ang], axis=0)
    cos = np.cos(ang)
    sin = np.sin(ang)
    cos_t = jnp.asarray(np.tile(np.concatenate([cos, cos], axis=-1), (1, hpl)), F32)
    sin_t = jnp.asarray(np.tile(np.concatenate([-sin, sin], axis=-1), (1, hpl)), F32)
    bias = bias.reshape(bias.shape[0], n_hp, hpl, win_r, GRID_W, win_r * GRID_W)
    bias = jnp.swapaxes(bias, 2, 3).reshape(bias.shape[0], n_hp, win_r, hpl * GRID_W, win_r * GRID_W)

    def sec(s):
        return pl.BlockSpec((None, t_tot, LANES), lambda b, h, s=s: (b, 0, s * n_hp + h))

    gain = lambda g: jnp.tile(g.reshape(1, HEAD_DIM), (1, hpl))
    return pl.pallas_call(
        functools.partial(_na_body, t_ctx=t_ctx, need_ctx=need_ctx),
        grid=(bsz, n_hp),
        in_specs=[sec(0), sec(1), sec(2),
                  pl.BlockSpec((1, LANES), lambda b, h: (0, 0)),
                  pl.BlockSpec((1, LANES), lambda b, h: (0, 0)),
                  pl.BlockSpec((t_tot, LANES), lambda b, h: (0, 0)),
                  pl.BlockSpec((t_tot, LANES), lambda b, h: (0, 0)),
                  pl.BlockSpec((None, None, win_r, hpl * GRID_W, win_r * GRID_W),
                               lambda b, h: (layer, h, 0, 0, 0))],
        out_specs=pl.BlockSpec((None, t_tot, LANES), lambda b, h: (b, 0, h)),
        out_shape=jax.ShapeDtypeStruct((bsz, t_tot, C_WIDTH), F32),
        scratch_shapes=[pltpu.VMEM((t_tot, LANES), BF16)] * 3,
        compiler_params=_cparams("arbitrary", "arbitrary"),
        name="na",
    )(pc, pc, pc, gain(q_gain), gain(k_gain), cos_t, sin_t, bias)


def _outproj_body(x_ref, a_ref, by_ref, bg_ref, bb_ref, lng_ref, lnb_ref, c_ref, *refs, per):
    m_refs = refs[:per]
    g_ref, w_ref, wr_ref, br_ref, xo_ref, h_ref, route_ref = refs[per:]
    tt = TOKEN_TILE
    b_mix = _rwkv_readout(by_ref[...], bg_ref[...], bb_ref[...], lng_ref[...], lnb_ref[...])
    mix = (_mm(a_ref[...].astype(BF16), w_ref[0:A_WIDTH, :])
           + _mm(b_mix.astype(BF16), w_ref[A_WIDTH:A_WIDTH + B_WIDTH, :])
           + _mm(c_ref[...].astype(BF16), w_ref[A_WIDTH + B_WIDTH:, :]))
    hs = []
    for j, m_ref in enumerate(m_refs):
        rows = slice(j * tt, (j + 1) * tt)
        x = x_ref[rows, :] + m_ref[2:3, :] * mix[rows]
        xo_ref[rows, :] = x
        h = x * lax.rsqrt(jnp.mean(x * x, axis=-1, keepdims=True) + EPS) * g_ref[...]
        h = h * (1.0 + m_ref[4:5, :]) + m_ref[3:4, :]
        h_ref[rows, :] = h
        hs.append(h)
    h = jnp.concatenate(hs, axis=0)
    logits = _mm_3pass(h, wr_ref[...]) + br_ref[...]
    lane = _iota(logits.shape, 1).astype(F32)
    big = float(LANES)
    is_g = (lane >= N_EXPERTS) & (lane < N_EXPERTS + N_GROUPS)
    gl = jnp.where(is_g, logits, -jnp.inf)
    gmax = jnp.max(gl, axis=-1, keepdims=True)
    g_w = 1.0 / jnp.sum(jnp.exp(gl - gmax), axis=-1, keepdims=True)
    g_sel = jnp.min(jnp.where(gl == gmax, lane, big), axis=-1, keepdims=True) - N_EXPERTS
    lo = g_sel * EXPERTS_PER_GROUP
    el = jnp.where((lane >= lo) & (lane < lo + EXPERTS_PER_GROUP), logits, -jnp.inf)
    m1 = jnp.max(el, axis=-1, keepdims=True)
    i1 = jnp.min(jnp.where(el == m1, lane, big), axis=-1, keepdims=True)
    el2 = jnp.where(lane == i1, -jnp.inf, el)
    m2 = jnp.max(el2, axis=-1, keepdims=True)
    i2 = jnp.min(jnp.where(el2 == m2, lane, big), axis=-1, keepdims=True)
    e2 = jnp.exp(m2 - m1)
    w1 = g_w / (1.0 + e2)
    route_ref[...] = (jnp.where(lane == 0.0, i1, 0.0) + jnp.where(lane == 1.0, i2, 0.0)
                      + jnp.where(lane == 2.0, w1, 0.0) + jnp.where(lane == 3.0, w1 * e2, 0.0))


def _outproj(x, a, b_scan, b_gate, b_bonus, ln_gain, ln_bias, c, mod_l, gain2, w_bf, w_router, b_router, *,
             bps, ctx_blocks, ctx_row):
    n = x.shape[0]
    per = PROJ_TILES
    tm = per * TOKEN_TILE
    tok = lambda w: pl.BlockSpec((tm, w), lambda i: (i, 0))
    full = lambda shape: pl.BlockSpec(shape, lambda i: (0,) * len(shape))
    return pl.pallas_call(
        functools.partial(_outproj_body, per=per),
        grid=(n // tm,),
        in_specs=[tok(D_MODEL), tok(A_WIDTH), tok(B_WIDTH), tok(B_WIDTH), tok(B_WIDTH),
                  full((1, B_WIDTH)), full((1, B_WIDTH)), tok(C_WIDTH),
                  *_mod_specs(per, bps, ctx_blocks, ctx_row),
                  full((1, D_MODEL)), full((D_MODEL, D_MODEL)), full((D_MODEL, LANES)), full((1, LANES))],
        out_specs=[tok(D_MODEL), tok(D_MODEL), tok(LANES)],
        out_shape=[jax.ShapeDtypeStruct((n, D_MODEL), F32),
                   jax.ShapeDtypeStruct((n, D_MODEL), F32),
                   jax.ShapeDtypeStruct((n, LANES), F32)],
        compiler_params=_cparams("arbitrary"),
        name="outproj",
    )(x, a, b_scan, b_gate, b_bonus, ln_gain.reshape(1, B_WIDTH), ln_bias.reshape(1, B_WIDTH), c,
      *([mod_l] * per), gain2.reshape(1, D_MODEL), w_bf, w_router, b_router)


def _route_tables(route, bsz, t_tot, skip_rows):
    n_pairs = TOP_K * t_tot
    ids = route[:, 0:TOP_K].astype(jnp.int32).reshape(bsz, t_tot, TOP_K)
    wts = route[:, TOP_K:2 * TOP_K].reshape(bsz, t_tot, TOP_K)
    ids = jnp.where(jnp.arange(t_tot)[None, :, None] < skip_rows, N_EXPERTS, ids)
    ids = jnp.swapaxes(ids, 1, 2).reshape(bsz, n_pairs)
    wts = jnp.swapaxes(wts, 1, 2).reshape(bsz, n_pairs)
    key = ids * n_pairs + jnp.arange(n_pairs, dtype=jnp.int32)
    key, wts = lax.sort((key, wts), dimension=1, num_keys=1)
    tok = (key % n_pairs) % t_tot
    count = jnp.sum((ids[:, :, None] == jnp.arange(N_EXPERTS, dtype=jnp.int32)).astype(jnp.int32), axis=1)
    start = jnp.cumsum(count, axis=1) - count
    pad = ((0, 0), (0, MOE_TILE))
    return (jnp.pad(tok, pad).reshape(-1), jnp.pad(wts, pad).reshape(-1), start.reshape(-1), count.reshape(-1))


def _moe_body(tok_ref, w_ref, start_ref, count_ref, h_ref, wg_ref, wu_ref, wd_ref, y_ref, hbuf, obuf):
    c = pl.program_id(0)
    step = pl.program_id(1)
    t_tot = h_ref.shape[0]
    p_len = TOP_K * t_tot + MOE_TILE

    @pl.when(jnp.logical_and(c == 0, step == 0))
    def _():
        hbuf[...] = jnp.zeros_like(hbuf)

    @pl.when(step == 0)
    def _():
        def zero(i, carry):
            y_ref[pl.ds(pl.multiple_of(i * TOKEN_TILE, TOKEN_TILE), TOKEN_TILE), :] = jnp.zeros(
                (TOKEN_TILE, D_MODEL), F32)
            return carry
        lax.fori_loop(0, t_tot // TOKEN_TILE, zero, 0)

    for k in range(MOE_EXPERTS_PER_STEP):
        e = step * MOE_EXPERTS_PER_STEP + k
        _moe_expert(tok_ref, w_ref, h_ref, wg_ref.at[k], wu_ref.at[k], wd_ref.at[k], y_ref, hbuf, obuf,
                    count=count_ref[c * N_EXPERTS + e], pair_base=c * p_len + start_ref[c * N_EXPERTS + e])


def _moe_expert(tok_ref, w_ref, h_ref, wg_ref, wu_ref, wd_ref, y_ref, hbuf, obuf, *, count, pair_base):
    tm = MOE_TILE
    grp = SUBLANES

    def tile(t, carry):
        n_valid = jnp.minimum(tm, count - t * tm)
        pair0 = pair_base + t * tm

        def gather(g, carry):
            for j in range(grp):
                tok = tok_ref[pair0 + g * grp + j]
                hbuf[g, pl.ds(j, 1), :] = h_ref[pl.ds(tok, 1), :]
            return carry
        lax.fori_loop(0, (n_valid + grp - 1) // grp, gather, 0)

        h = hbuf[...].reshape(tm, D_MODEL).astype(BF16)
        gate = _mm(h, wg_ref[...])
        up = _mm(h, wu_ref[...])
        hid = gate * _sigmoid(gate) * up
        obuf[...] = _mm(hid.astype(BF16), wd_ref[...]).reshape(tm // grp, grp, D_MODEL)

        def add_row(r):
            tok = tok_ref[pair0 + r]
            y_ref[pl.ds(tok, 1), :] += w_ref[pair0 + r] * obuf[r // grp, pl.ds(r % grp, 1), :]

        def scatter(g, carry):
            toks = [tok_ref[pair0 + g * grp + j] for j in range(grp)]
            new = [y_ref[pl.ds(toks[j], 1), :] + w_ref[pair0 + g * grp + j] * obuf[g, pl.ds(j, 1), :]
                   for j in range(grp)]
            for j in range(grp):
                y_ref[pl.ds(toks[j], 1), :] = new[j]
            return carry
        lax.fori_loop(0, n_valid // grp, scatter, 0)

        def scatter_tail(r, carry):
            add_row(r)
            return carry
        lax.fori_loop(n_valid // grp * grp, n_valid, scatter_tail, 0)
        return carry

    lax.fori_loop(0, (count + tm - 1) // tm, tile, 0)


def _moe(h, route, wg, wu, wd, layer, *, bsz, t_tot, skip_rows=0):
    tok, w, start, count = _route_tables(route, bsz, t_tot, skip_rows)
    grid_spec = pltpu.PrefetchScalarGridSpec(
        num_scalar_prefetch=4,
        grid=(bsz, N_EXPERTS // MOE_EXPERTS_PER_STEP),
        in_specs=[
            pl.BlockSpec((None, t_tot, D_MODEL), lambda c, e, *_: (c, 0, 0)),
            pl.BlockSpec((None, MOE_EXPERTS_PER_STEP, D_MODEL, D_EXPERT), lambda c, e, *_: (layer, e, 0, 0)),
            pl.BlockSpec((None, MOE_EXPERTS_PER_STEP, D_MODEL, D_EXPERT), lambda c, e, *_: (layer, e, 0, 0)),
            pl.BlockSpec((None, MOE_EXPERTS_PER_STEP, D_EXPERT, D_MODEL), lambda c, e, *_: (layer, e, 0, 0)),
        ],
        out_specs=pl.BlockSpec((None, t_tot, D_MODEL), lambda c, e, *_: (c, 0, 0)),
        scratch_shapes=[pltpu.VMEM((MOE_TILE // SUBLANES, SUBLANES, D_MODEL), F32)] * 2,
    )
    y = pl.pallas_call(
        _moe_body,
        grid_spec=grid_spec,
        out_shape=jax.ShapeDtypeStruct((bsz, t_tot, D_MODEL), F32),
        compiler_params=_cparams("arbitrary", "arbitrary"),
        name="moe",
    )(tok, w, start, count, h.reshape(bsz, t_tot, D_MODEL), wg, wu, wd)
    return y.reshape(bsz * t_tot, D_MODEL)


def _residual_body(x_ref, y_ref, m_ref, o_ref):
    o_ref[...] = (x_ref[...] + m_ref[5:6, :] * y_ref[...]).astype(o_ref.dtype)


def _final_residual(x, y, mod_l, dtype, *, bsz, t_ctx, t_tot):
    tm = TOKEN_TILE
    tok = pl.BlockSpec((None, tm, D_MODEL), lambda b, i: (b, t_ctx // tm + i, 0))
    return pl.pallas_call(
        _residual_body,
        grid=(bsz, (t_tot - t_ctx) // tm),
        in_specs=[tok, tok, pl.BlockSpec((None, 6, D_MODEL), lambda b, i: (b, 0, 0))],
        out_specs=pl.BlockSpec((None, tm, D_MODEL), lambda b, i: (b, i, 0)),
        out_shape=jax.ShapeDtypeStruct((bsz, t_tot - t_ctx, D_MODEL), dtype),
        compiler_params=_cparams("arbitrary", "arbitrary"),
        name="residual",
    )(x.reshape(bsz, t_tot, D_MODEL), y.reshape(bsz, t_tot, D_MODEL), mod_l)


def kernel(x, c, ctx, c_ctx, norm1_gain, norm2_gain, w_ada, b_ada, w_in, w_out, hgrn_lb_logits, hgrn_gn_gain, rwkv_mu, rwkv_w0, rwkv_w_up, rwkv_a0, rwkv_a_up, rwkv_g_up, rwkv_kk_scale, rwkv_k_a, rwkv_r_k, rwkv_ln_gain, rwkv_ln_bias, na_q_gain, na_k_gain, na_rpb, w_router_group, b_router_group, w_router_expert, b_router_expert, w_exp_gate, w_exp_up, w_exp_down):
    bsz, t_lat, _ = x.shape
    t_ctx = ctx.shape[1]
    t_tot = t_ctx + t_lat
    n = bsz * t_tot
    assert t_ctx % TOKEN_TILE == 0 and t_lat % TOKEN_TILE == 0 and bsz < 16
    bps = t_tot // TOKEN_TILE
    tile_kw = dict(bps=bps, ctx_blocks=t_ctx // TOKEN_TILE, ctx_row=bsz)

    lb_p = jax.nn.softmax(hgrn_lb_logits.astype(F32), axis=1)
    lower_bounds = jnp.cumsum(lb_p, axis=1) - lb_p[:, :1]

    cc = jnp.zeros((16, D_MODEL), F32).at[:bsz].set(c.astype(F32)).at[bsz].set(c_ctx.astype(F32))
    mod = _ada_mod(cc, w_ada, b_ada).reshape(DEPTH, 16, 6, D_MODEL)

    w_in_bf = w_in.astype(BF16)
    w_out_bf = w_out.astype(BF16)
    wg_bf = w_exp_gate.astype(BF16).reshape(DEPTH, N_EXPERTS, D_MODEL, D_EXPERT)
    wu_bf = w_exp_up.astype(BF16).reshape(DEPTH, N_EXPERTS, D_MODEL, D_EXPERT)
    wd_bf = w_exp_down.astype(BF16).reshape(DEPTH, N_EXPERTS, D_EXPERT, D_MODEL)
    pad = LANES - N_EXPERTS - N_GROUPS
    w_router = jnp.concatenate([w_router_expert, w_router_group,
                                jnp.zeros((DEPTH, D_MODEL, pad), F32)], axis=-1)
    b_router = jnp.concatenate([b_router_expert, b_router_group, jnp.zeros((DEPTH, pad), F32)], axis=-1)

    na_bias = _na_bias_table(na_rpb, t_lat // GRID_W)
    xs = jnp.concatenate([ctx.astype(F32), x.astype(F32)], axis=1).reshape(n, D_MODEL)
    y_moe = None
    for layer in range(DEPTH):
        last = layer == DEPTH - 1
        if layer == 0:
            pa, pb, pc = _inproj(xs, mod[layer], norm1_gain[layer], w_in_bf[layer], **tile_kw)
        else:
            xs, pa, pb, pc = _inproj(xs, mod[layer], norm1_gain[layer], w_in_bf[layer],
                                     (y_moe, mod[layer - 1]), **tile_kw)
        a_mix = _hgrn(pa.reshape(bsz, t_tot, A_PROJ), lower_bounds[:, layer], hgrn_gn_gain[layer], t_ctx=t_ctx)
        prep = _rwkv_prep(pb.reshape(bsz, t_tot, B_PROJ), rwkv_mu[layer], rwkv_w0[layer], rwkv_w_up[layer],
                          rwkv_a0[layer], rwkv_a_up[layer], rwkv_g_up[layer], rwkv_kk_scale[layer],
                          rwkv_k_a[layer], rwkv_r_k[layer], t_ctx=t_ctx)
        y = _rwkv_scan(*prep[:9], t_ctx=t_ctx)
        c_mix = _na(pc.reshape(bsz, t_tot, C_PROJ), na_q_gain[layer], na_k_gain[layer], na_bias, layer,
                    t_ctx=t_ctx, need_ctx=not last)
        xs, h2, route = _outproj(xs, a_mix.reshape(n, A_WIDTH), y.reshape(n, B_WIDTH), prep[9].reshape(n, B_WIDTH),
                                 prep[10].reshape(n, B_WIDTH), rwkv_ln_gain[layer], rwkv_ln_bias[layer],
                                 c_mix.reshape(n, C_WIDTH), mod[layer], norm2_gain[layer], w_out_bf[layer],
                                 w_router[layer], b_router[layer].reshape(1, LANES), **tile_kw)
        y_moe = _moe(h2, route, wg_bf, wu_bf, wd_bf, layer, bsz=bsz, t_tot=t_tot, skip_rows=t_ctx if last else 0)
    return _final_residual(xs, y_moe, mod[DEPTH - 1], x.dtype, bsz=bsz, t_ctx=t_ctx, t_tot=t_tot)
```

```python
import functools

import jax
import jax.numpy as jnp
import numpy as np
from jax import lax
from jax.experimental import pallas as pl
from jax.experimental.pallas import tpu as pltpu

F32 = jnp.float32
BF16 = jnp.bfloat16

D_MODEL = 1024
DEPTH = 4
GRID_W = 64
HEAD_DIM = 64
A_HEADS = 4
B_HEADS = 4
C_HEADS = 8
A_WIDTH = A_HEADS * HEAD_DIM
B_WIDTH = B_HEADS * HEAD_DIM
C_WIDTH = C_HEADS * HEAD_DIM
HGRN_CHUNK = 16
HGRN_F_FLOOR = 1e-20
RWKV_W_RANK = 64
RWKV_A_RANK = 64
RWKV_G_RANK = 128
RWKV_LN_EPS = 64e-5
NA_WIN_ROWS = 8
NA_WIN_COLS = 16
ROPE_THETA = 10000.0
N_GROUPS = 4
EXPERTS_PER_GROUP = 8
N_EXPERTS = N_GROUPS * EXPERTS_PER_GROUP
TOP_K = 2
D_EXPERT = 512
EPS = 1e-6
LOG2E = 1.4426950408889634
NEG_INF = -1e30
A_PROJ = 5 * A_WIDTH
B_PROJ = 3 * B_WIDTH + 2 * RWKV_W_RANK + 2 * RWKV_A_RANK + RWKV_G_RANK
C_PROJ = 3 * C_WIDTH
P_TOTAL = A_PROJ + B_PROJ + C_PROJ
B_TAIL = B_PROJ - 3 * B_WIDTH

LANES = 128
SUBLANES = 8
PROJ_TILES = 2
TOKEN_TILE = 256
RWKV_CHUNK = 64
RWKV_BLOCKS_PER_ITER = 3
RWKV_BLOCK = 128
HGRN_BLOCK = 128
NA_ROW_UNROLL = 8
HGRN_SCAN_UNROLL = 24
MOE_EXPERTS_PER_STEP = 2
MOE_TILE = 192
VMEM_LIMIT = 56 * 1024 * 1024

_HI = lax.Precision.HIGHEST


def _cparams(*sem):
    return pltpu.CompilerParams(dimension_semantics=sem, vmem_limit_bytes=VMEM_LIMIT)


def _mm(a, b):
    return jnp.dot(a, b, preferred_element_type=F32)


def _nt(a, b):
    return lax.dot_general(a, b, (((1,), (1,)), ((), ())), preferred_element_type=F32)


def _tn(a, b):
    return lax.dot_general(a, b, (((0,), (0,)), ((), ())), preferred_element_type=F32)


def _split3(a):
    a1 = a.astype(BF16)
    r1 = a - a1.astype(F32)
    a2 = r1.astype(BF16)
    a3 = (r1 - a2.astype(F32)).astype(BF16)
    return a1, a2, a3


def _mm_x3(a, m):
    a1, a2, a3 = _split3(a)
    return _mm(a1, m) + _mm(a2, m) + _mm(a3, m)


def _mm_l3(m, a):
    a1, a2, a3 = _split3(a)
    return _mm(m, a1) + _mm(m, a2) + _mm(m, a3)


def _mm_3pass(a, b):
    a1 = a.astype(BF16)
    a2 = (a - a1.astype(F32)).astype(BF16)
    b1 = b.astype(BF16)
    b2 = (b - b1.astype(F32)).astype(BF16)
    return _mm(a1, b1) + (_mm(a1, b2) + _mm(a2, b1))


def _sigmoid(x):
    return 1.0 / (1.0 + jnp.exp(-x))


def _iota(shape, dim):
    return lax.broadcasted_iota(jnp.int32, shape, dim)


def _head_seg(n):
    return jnp.where(_iota((n, n), 0) // HEAD_DIM == _iota((n, n), 1) // HEAD_DIM, 1.0, 0.0).astype(BF16)


def _ada_body(c_ref, w_ref, b_ref, o_ref):
    c = c_ref[...]
    o_ref[...] = _mm_3pass(c * _sigmoid(c), w_ref[...]) + b_ref[...]


def _ada_mod(cc, w_ada, b_ada):
    rows = cc.shape[0]
    tn = 1536
    return pl.pallas_call(
        _ada_body,
        grid=(DEPTH, 6 * D_MODEL // tn),
        in_specs=[
            pl.BlockSpec((rows, D_MODEL), lambda l, j: (0, 0)),
            pl.BlockSpec((None, D_MODEL, tn), lambda l, j: (l, 0, j)),
            pl.BlockSpec((None, 1, tn), lambda l, j: (l, 0, j)),
        ],
        out_specs=pl.BlockSpec((None, rows, tn), lambda l, j: (l, 0, j)),
        out_shape=jax.ShapeDtypeStruct((DEPTH, rows, 6 * D_MODEL), F32),
        compiler_params=_cparams("arbitrary", "arbitrary"),
        name="ada_mod",
    )(cc, w_ada, b_ada.reshape(DEPTH, 1, 6 * D_MODEL))


def _mod_row(i, blocks_per_seq, ctx_blocks, ctx_row):
    return jnp.where(i % blocks_per_seq < ctx_blocks, ctx_row, i // blocks_per_seq)


def _inproj_body(*refs, with_moe, per):
    tt = TOKEN_TILE
    if with_moe:
        x_ref, y_ref = refs[:2]
        mp_refs, m_refs = refs[2:2 + per], refs[2 + per:2 + 2 * per]
        g_ref, w_ref, xo_ref, pa_ref, pb_ref, pc_ref = refs[2 + 2 * per:]
    else:
        x_ref = refs[0]
        m_refs = refs[1:1 + per]
        g_ref, w_ref, pa_ref, pb_ref, pc_ref = refs[1 + per:]
    hs = []
    for j, m_ref in enumerate(m_refs):
        rows = slice(j * tt, (j + 1) * tt)
        x = x_ref[rows, :]
        if with_moe:
            x = x + mp_refs[j][5:6, :] * y_ref[rows, :]
            xo_ref[rows, :] = x
        h = x * lax.rsqrt(jnp.mean(x * x, axis=-1, keepdims=True) + EPS) * g_ref[...]
        hs.append((h * (1.0 + m_ref[1:2, :]) + m_ref[0:1, :]).astype(BF16))
    h = jnp.concatenate(hs, axis=0)
    pa_ref[...] = _mm(h, w_ref[:, :A_PROJ])
    pb_ref[...] = _mm(h, w_ref[:, A_PROJ:A_PROJ + B_PROJ])
    pc_ref[...] = _mm(h, w_ref[:, A_PROJ + B_PROJ:])


def _mod_specs(per, bps, ctx_blocks, ctx_row):
    row = functools.partial(_mod_row, blocks_per_seq=bps, ctx_blocks=ctx_blocks, ctx_row=ctx_row)
    return [pl.BlockSpec((None, 6, D_MODEL), lambda i, j=j: (row(i * per + j), 0, 0)) for j in range(per)]


def _inproj(x, mod_l, gain, w_bf, moe=None, *, bps, ctx_blocks, ctx_row):
    n = x.shape[0]
    per = PROJ_TILES
    tm = per * TOKEN_TILE
    tok = lambda w: pl.BlockSpec((tm, w), lambda i: (i, 0))
    mods = _mod_specs(per, bps, ctx_blocks, ctx_row)
    with_moe = moe is not None
    ins = (x, moe[0], *([moe[1]] * per)) if with_moe else (x,)
    return pl.pallas_call(
        functools.partial(_inproj_body, with_moe=with_moe, per=per),
        grid=(n // tm,),
        in_specs=[tok(D_MODEL)] + ([tok(D_MODEL)] + mods if with_moe else []) + mods + [
            pl.BlockSpec((1, D_MODEL), lambda i: (0, 0)),
            pl.BlockSpec((D_MODEL, P_TOTAL), lambda i: (0, 0)),
        ],
        out_specs=([tok(D_MODEL)] if with_moe else []) + [tok(A_PROJ), tok(B_PROJ), tok(C_PROJ)],
        out_shape=([jax.ShapeDtypeStruct((n, D_MODEL), F32)] if with_moe else []) + [
            jax.ShapeDtypeStruct((n, A_PROJ), F32),
            jax.ShapeDtypeStruct((n, B_PROJ), F32),
            jax.ShapeDtypeStruct((n, C_PROJ), F32),
        ],
        compiler_params=_cparams("arbitrary"),
        name="inproj",
    )(*ins, *([mod_l] * per), gain.reshape(1, D_MODEL), w_bf)


def _hgrn_body(q_ref, ff_ref, fb_ref, i_ref, g_ref, lb_ref, gn_ref, o_ref,
               qin_s, kout_s, cum_s, acc_s, kpad_s, cpad_s, vpad_s, *, t_ctx):
    t_tot = q_ref.shape[0]
    c = HGRN_CHUNK
    rb = HGRN_BLOCK
    n_blk = t_tot // rb
    n_chunk = t_tot // c
    n_chunk_ctx = t_ctx // c
    row = _iota((rb, rb), 0)
    col = _iota((rb, rb), 1)
    same = (row // c) == (col // c)
    same_bf = jnp.where(same, 1.0, 0.0).astype(BF16)
    seg = _head_seg(LANES)
    pos = _iota((rb, LANES), 0) % c
    blockdiag = _iota((LANES, LANES), 0) // HEAD_DIM == _iota((LANES, LANES), 1) // HEAD_DIM

    pad = HGRN_CHUNK
    zpad = jnp.zeros((pad, LANES), F32)
    for d in range(2):
        for ref in (kpad_s, cpad_s):
            ref[d, 0:pad, :] = zpad
            ref[d, pad + t_tot:, :] = zpad
    vpad_s[0:pad, :] = zpad
    vpad_s[pad + t_tot:, :] = zpad
    tris = [jnp.where(same & ((col >= row) if d == 1 else (col <= row)), 1.0, 0.0).astype(BF16) for d in range(2)]

    def gates(b, carry):
        sl = pl.ds(pl.multiple_of(b * rb, rb), rb)
        slp = pl.ds(pl.multiple_of(b * rb, rb) + pad, rb)
        q = q_ref[sl, :]
        vpad_s[slp, :] = i_ref[sl, :]
        for d, f_ref in enumerate((ff_ref, fb_ref)):
            lb = lb_ref[d:d + 1, :]
            fpre = f_ref[sl, :]
            f = lb + (1.0 - lb) * _sigmoid(fpre)
            logf = jnp.log(jnp.maximum(f, HGRN_F_FLOOR))
            k = (1.0 - lb) * _sigmoid(-fpre)
            cum = _mm_l3(tris[d], logf)
            tot = _mm_l3(same_bf, logf)
            kpad_s[d, slp, :] = k
            cpad_s[d, slp, :] = cum * LOG2E
            qin_s[d, sl, :] = q * jnp.exp(cum)
            kout_s[d, sl, :] = k * jnp.exp(tot - cum)
            cum_s[d, sl, :] = tot
        return carry

    lax.fori_loop(0, n_blk, gates, 0)

    def intra(b, carry):
        r0 = pl.multiple_of(b * rb, rb)
        sl = pl.ds(r0, rb)
        q = q_ref[sl, :]
        cums = [cpad_s[d, pl.ds(r0 + pad, rb), :] for d in range(2)]
        o = jnp.zeros((rb, LANES), F32)
        for j in range(c):
            for d in range(2):
                off = r0 + pad + (j if d == 1 else -j)
                valid = (pos <= c - 1 - j) if d == 1 else (pos >= j)
                ks = kpad_s[d, pl.ds(off, rb), :]
                cs = cpad_s[d, pl.ds(off, rb), :]
                vs = vpad_s[pl.ds(off, rb), :]
                prod = jnp.where(valid, q * ks * jnp.exp2(cums[d] - cs), 0.0)
                o = o + _mm(prod.astype(BF16), seg) * vs
        acc_s[sl, :] = o
        return carry

    lax.fori_loop(0, n_blk, intra, 0)

    def step(it, carry):
        idx = [it * HGRN_SCAN_UNROLL + j for j in range(HGRN_SCAN_UNROLL)]
        order = [(i, jnp.where(i < n_chunk_ctx, n_chunk_ctx - 1 - i, n_chunk - 1 - (i - n_chunk_ctx))) for i in idx]
        sls = [[pl.ds(pl.multiple_of(n[d] * c, c), c) for d in range(2)] for n in order]
        kvs = [[jnp.where(blockdiag, _tn(i_ref[sl[d], :].astype(BF16), kout_s[d, sl[d], :].astype(BF16)), 0.0)
                for d in range(2)] for sl in sls]
        decs = [[jnp.exp(cum_s[d, pl.ds(pl.multiple_of(n[d] * c, c), 1), :]) for d in range(2)] for n in order]
        states = list(carry)
        for j in range(HGRN_SCAN_UNROLL):
            for d in range(2):
                acc_s[sls[j][d], :] += _nt(qin_s[d, sls[j][d], :].astype(BF16), states[d].astype(BF16))
                states[d] = states[d] * decs[j][d] + kvs[j][d]
        return tuple(states)

    zero = jnp.zeros((LANES, LANES), F32)
    lax.fori_loop(0, n_chunk // HGRN_SCAN_UNROLL, step, (zero, zero))

    def readout(b, carry):
        sl = pl.ds(pl.multiple_of(b * rb, rb), rb)
        o = acc_s[sl, :]
        ms = _mm((o * o).astype(BF16), seg) * (1.0 / HEAD_DIM)
        g = g_ref[sl, :]
        o_ref[sl, :] = o * lax.rsqrt(ms + EPS) * gn_ref[...] * (g * _sigmoid(g))
        return carry

    lax.fori_loop(0, n_blk, readout, 0)


def _hgrn(pa, lb, gn_gain, *, t_ctx):
    bsz, t_tot, _ = pa.shape
    n_hp = A_WIDTH // LANES

    def sec(s):
        return pl.BlockSpec((None, t_tot, LANES), lambda b, h, s=s: (b, 0, s * n_hp + h))

    return pl.pallas_call(
        functools.partial(_hgrn_body, t_ctx=t_ctx),
        grid=(bsz, n_hp),
        in_specs=[sec(0), sec(1), sec(2), sec(3), sec(4),
                  pl.BlockSpec((2, LANES), lambda b, h: (0, h)),
                  pl.BlockSpec((1, LANES), lambda b, h: (0, h))],
        out_specs=pl.BlockSpec((None, t_tot, LANES), lambda b, h: (b, 0, h)),
        out_shape=jax.ShapeDtypeStruct((bsz, t_tot, A_WIDTH), F32),
        scratch_shapes=[
            pltpu.VMEM((2, t_tot, LANES), F32),
            pltpu.VMEM((2, t_tot, LANES), F32),
            pltpu.VMEM((2, t_tot, LANES), F32),
            pltpu.VMEM((t_tot, LANES), F32),
            pltpu.VMEM((2, t_tot + 2 * HGRN_CHUNK, LANES), F32),
            pltpu.VMEM((2, t_tot + 2 * HGRN_CHUNK, LANES), F32),
            pltpu.VMEM((t_tot + 2 * HGRN_CHUNK, LANES), F32),
        ],
        compiler_params=_cparams("arbitrary", "arbitrary"),
        name="hgrn",
    )(pa, pa, pa, pa, pa, lb, gn_gain.reshape(1, A_WIDTH))


def _rwkv_prep_body(x_ref, xp_ref, xn_ref, mu_ref, w0_ref, wup_ref, a0_ref, aup_ref, gup_ref,
                    kks_ref, ka_ref, rk_ref,
                    r_o, v_o, kk_o, kf_o, kb_o, bf_o, bb_o, lwf_o, lwb_o, g_o, bonus_o, *, t_ctx, t_tot):
    tm = x_ref.shape[0]
    r0 = pl.program_id(1) * tm
    prev_ok = jnp.logical_and(r0 != 0, r0 != t_ctx)
    next_ok = jnp.logical_and(r0 + tm != t_ctx, r0 + tm != t_tot)
    first = _iota((tm, 1), 0) == 0
    last = _iota((tm, 1), 0) == tm - 1

    def shifted(lo, hi):
        x = x_ref[:, lo:hi]
        p_row = jnp.where(prev_ok, xp_ref[SUBLANES - 1:SUBLANES, lo:hi], 0.0)
        n_row = jnp.where(next_ok, xn_ref[0:1, lo:hi], 0.0)
        prev = jnp.where(first, p_row, pltpu.roll(x, 1, 0))
        nxt = jnp.where(last, n_row, pltpu.roll(x, tm - 1, 0))
        return x + (0.5 * (prev + nxt) - x) * mu_ref[:, lo:hi]

    bw = B_WIDTH
    r = shifted(0, bw)
    k = shifted(bw, 2 * bw)
    v = shifted(2 * bw, 3 * bw)
    tail = shifted(3 * bw, B_PROJ)
    seg = _head_seg(bw)

    kk = k * kks_ref[...]
    kk = kk / jnp.maximum(jnp.sqrt(_mm_x3(kk * kk, seg)), 1e-12)
    ksum = jnp.zeros_like(k)
    for d, (k_o, b_o, lw_o) in enumerate(((kf_o, bf_o, lwf_o), (kb_o, bb_o, lwb_o))):
        wd = tail[:, d * RWKV_W_RANK:(d + 1) * RWKV_W_RANK]
        ad = tail[:, 2 * RWKV_W_RANK + d * RWKV_A_RANK:2 * RWKV_W_RANK + (d + 1) * RWKV_A_RANK]
        u = -(w0_ref[d:d + 1, :] + _mm_3pass(jnp.tanh(wd), wup_ref[d]))
        w = -(jnp.maximum(u, 0.0) + jnp.log(1.0 + jnp.exp(-jnp.abs(u)))) - 0.5
        a = _sigmoid(a0_ref[d:d + 1, :] + _mm_3pass(ad, aup_ref[d]))
        k_d = k * (1.0 + (a - 1.0) * ka_ref[...])
        ksum = ksum + k_d
        k_o[...] = k_d
        b_o[...] = a * kk
        lw_o[...] = -jnp.exp(w)
    gd = tail[:, 2 * RWKV_W_RANK + 2 * RWKV_A_RANK:]
    r_o[...] = r
    v_o[...] = v
    kk_o[...] = kk
    g_o[...] = _mm_3pass(_sigmoid(gd), gup_ref[...])
    bonus_o[...] = _mm((r * ksum * rk_ref[...]).astype(BF16), seg) * v


def _rwkv_prep(pb, mu, w0, w_up, a0, a_up, g_up, kk_scale, k_a, r_k, *, t_ctx):
    bsz, t_tot, _ = pb.shape
    tm = TOKEN_TILE
    nb8 = t_tot // SUBLANES
    per8 = tm // SUBLANES
    bw = B_WIDTH
    full = lambda shape: pl.BlockSpec(shape, lambda b, i: (0,) * len(shape))
    out = jax.ShapeDtypeStruct((bsz, t_tot, bw), F32)
    return pl.pallas_call(
        functools.partial(_rwkv_prep_body, t_ctx=t_ctx, t_tot=t_tot),
        grid=(bsz, t_tot // tm),
        in_specs=[
            pl.BlockSpec((None, tm, B_PROJ), lambda b, i: (b, i, 0)),
            pl.BlockSpec((None, SUBLANES, B_PROJ), lambda b, i: (b, jnp.maximum(i * per8 - 1, 0), 0)),
            pl.BlockSpec((None, SUBLANES, B_PROJ), lambda b, i: (b, jnp.minimum((i + 1) * per8, nb8 - 1), 0)),
            full((1, B_PROJ)), full((2, bw)), full((2, RWKV_W_RANK, bw)), full((2, bw)),
            full((2, RWKV_A_RANK, bw)), full((RWKV_G_RANK, bw)), full((1, bw)), full((1, bw)), full((1, bw)),
        ],
        out_specs=[pl.BlockSpec((None, tm, bw), lambda b, i: (b, i, 0))] * 11,
        out_shape=[out] * 11,
        compiler_params=_cparams("arbitrary", "arbitrary"),
        name="rwkv_prep",
    )(pb, pb, pb, mu.reshape(1, B_PROJ), w0, w_up, a0, a_up, g_up,
      kk_scale.reshape(1, bw), k_a.reshape(1, bw), r_k.reshape(1, bw))


def _rwkv_scan_body(r_ref, v_ref, kk_ref, kf_ref, kb_ref, bf_ref, bb_ref, lwf_ref, lwb_ref, y_ref,
                    rq_s, y0_s, p_s, z_s, *, t_ctx):
    t_tot = r_ref.shape[0]
    c = RWKV_CHUNK
    rb = RWKV_BLOCK
    n_blk = t_tot // rb
    n_chunk = t_tot // c
    n_chunk_ctx = t_ctx // c
    cpb = rb // c
    n_double = c.bit_length() - 2
    row = _iota((rb, rb), 0)
    col = _iota((rb, rb), 1)
    same = (row // c) == (col // c)
    same_bf = jnp.where(same, 1.0, 0.0).astype(BF16)
    eye = jnp.where(row == col, 1.0, 0.0)
    lane = _iota((1, LANES), 1)
    head_masks = [jnp.where(lane // HEAD_DIM == h, 1.0, 0.0) for h in range(LANES // HEAD_DIM)]
    r128 = _iota((LANES, LANES), 0)
    c128 = _iota((LANES, LANES), 1)
    blockdiag = (r128 // HEAD_DIM) == (c128 // HEAD_DIM)
    diag128 = r128 == c128

    def block(it, carry):
        bis = [it * RWKV_BLOCKS_PER_ITER + j for j in range(RWKV_BLOCKS_PER_ITER)]
        sls = [pl.ds(pl.multiple_of(bi * rb, rb), rb) for bi in bis]
        v_bfs = [v_ref[sl, :].astype(BF16) for sl in sls]
        pre = {}
        for j, sl in enumerate(sls):
            r = r_ref[sl, :]
            kk = kk_ref[sl, :]
            for d in range(2):
                rev = d == 1
                k_ref, b_ref, lw_ref = (kb_ref, bb_ref, lwb_ref) if rev else (kf_ref, bf_ref, lwf_ref)
                incl = same & ((col >= row) if rev else (col <= row))
                strict = same & ((col > row) if rev else (col < row))
                incl_bf = jnp.where(incl, 1.0, 0.0).astype(BF16)
                lw = lw_ref[sl, :]
                k = k_ref[sl, :]
                b = b_ref[sl, :]
                cl = _mm_l3(incl_bf, lw)
                tot = _mm_l3(same_bf, lw)
                w_inv = jnp.exp(-cl)
                w_end = jnp.exp(tot - cl)
                pre[j, d] = dict(incl=incl, strict=strict, tot=tot, kk_d=kk * jnp.exp(cl - lw), r_d=r * jnp.exp(cl),
                                 kb=jnp.concatenate([k * w_inv, b * w_inv], axis=0).astype(BF16),
                                 b_e=(b * w_end).astype(BF16), k_e=(k * w_end).astype(BF16))
        chains = [dict(j=j, d=d, mh=mh, kkm=pre[j, d]['kk_d'] * mh, rm=pre[j, d]['r_d'] * mh)
                  for j in range(len(bis)) for d in range(2) for mh in head_masks]
        for ch in chains:
            p = pre[ch['j'], ch['d']]
            aa = _nt(jnp.concatenate([ch['kkm'], ch['rm']], axis=0).astype(BF16), p['kb'])
            ch['a_kb'] = jnp.where(p['strict'], aa[:rb, :rb], 0.0)
            ch['a_rk'] = jnp.where(p['incl'], aa[rb:, :rb], 0.0)
            ch['a_rb'] = jnp.where(p['incl'], aa[rb:, rb:], 0.0)
            m = -jnp.where(p['strict'], aa[:rb, rb:], 0.0)
            ch['t'] = eye + m
            ch['m'] = m
        for ch in chains:
            m_bf = ch['m'].astype(BF16)
            ch['m'] = _mm(m_bf, m_bf)
        for _ in range(n_double - 1):
            for ch in chains:
                both = _mm(jnp.concatenate([ch['t'], ch['m']], axis=0).astype(BF16), ch['m'].astype(BF16))
                ch['t'] = ch['t'] + both[:rb]
                ch['m'] = both[rb:]
        for ch in chains:
            ch['t'] = ch['t'] + _mm(ch['t'].astype(BF16), ch['m'].astype(BF16))
            ch['av'] = _mm(jnp.concatenate([ch['a_kb'], ch['a_rk']], axis=0).astype(BF16), v_bfs[ch['j']])
        for ch in chains:
            ch['ku'] = _mm(ch['t'].astype(BF16), jnp.concatenate([ch['kkm'], ch['av'][:rb]], axis=1).astype(BF16))
        for ch in chains:
            ch['rb_ku'] = _mm(ch['a_rb'].astype(BF16), ch['ku'].astype(BF16))
        for j, (bi, sl) in enumerate(zip(bis, sls)):
            for d in range(2):
                p = pre[j, d]
                mine = [ch for ch in chains if ch['j'] == j and ch['d'] == d]
                kkt = sum(ch['ku'][:, :LANES] for ch in mine)
                u = sum(ch['ku'][:, LANES:] * ch['mh'] for ch in mine)
                rq_s[d, sl, :] = sum(ch['rm'] - ch['rb_ku'][:, :LANES] for ch in mine)
                y0_s[d, sl, :] = sum((ch['av'][rb:] - ch['rb_ku'][:, LANES:]) * ch['mh'] for ch in mine)
                ktu = jnp.concatenate([kkt, u], axis=1).astype(BF16)
                for n in range(cpb):
                    rows = slice(n * c, (n + 1) * c)
                    wc = jnp.exp(p['tot'][n * c:n * c + 1, :])
                    kub = _tn(ktu[rows], p['b_e'][rows])
                    p_s[d, bi * cpb + n] = jnp.where(diag128, wc, 0.0) - jnp.where(blockdiag, kub[:LANES], 0.0)
                    p_z = jnp.where(blockdiag, _tn(v_bfs[j][rows], p['k_e'][rows]) - kub[LANES:], 0.0)
                    z_s[d, bi * cpb + n] = p_z
        return carry

    lax.fori_loop(0, n_blk // RWKV_BLOCKS_PER_ITER, block, 0)

    y_ref[...] = jnp.zeros_like(y_ref)

    def step(i, carry):
        n_b = jnp.where(i < n_chunk_ctx, n_chunk_ctx - 1 - i, n_chunk - 1 - (i - n_chunk_ctx))
        chunk = (i, n_b)
        s_bf = [s.astype(BF16) for s in carry]
        ys = [_nt(rq_s[d, pl.ds(pl.multiple_of(chunk[d] * c, c), c), :].astype(BF16), s_bf[d]) for d in range(2)]
        states = [_mm(s_bf[d], p_s[d, chunk[d]].astype(BF16)) + z_s[d, chunk[d]] for d in range(2)]
        for d in range(2):
            sl = pl.ds(pl.multiple_of(chunk[d] * c, c), c)
            y_ref[sl, :] += ys[d] + y0_s[d, sl, :]
        return tuple(states)

    zero = jnp.zeros((LANES, LANES), F32)
    lax.fori_loop(0, n_chunk, step, (zero, zero))


def _rwkv_scan(r, v, kk, kf, kb, bf, bb, lwf, lwb, *, t_ctx):
    bsz, t_tot, bw = r.shape
    n_chunk = t_tot // RWKV_CHUNK
    spec = pl.BlockSpec((None, t_tot, LANES), lambda b, h: (b, 0, h))
    return pl.pallas_call(
        functools.partial(_rwkv_scan_body, t_ctx=t_ctx),
        grid=(bsz, bw // LANES),
        in_specs=[spec] * 9,
        out_specs=spec,
        out_shape=jax.ShapeDtypeStruct((bsz, t_tot, bw), F32),
        scratch_shapes=[
            pltpu.VMEM((2, t_tot, LANES), F32),
            pltpu.VMEM((2, t_tot, LANES), F32),
            pltpu.VMEM((2, n_chunk, LANES, LANES), F32),
            pltpu.VMEM((2, n_chunk, LANES, LANES), F32),
        ],
        compiler_params=_cparams("arbitrary", "arbitrary"),
        name="rwkv_scan",
    )(r, v, kk, kf, kb, bf, bb, lwf, lwb)


def _rwkv_readout(y, g, bonus, ln_gain, ln_bias):
    seg = _head_seg(B_WIDTH)
    mean = _mm_x3(y, seg) * (1.0 / HEAD_DIM)
    yc = y - mean
    var = _mm((yc * yc).astype(BF16), seg) * (1.0 / HEAD_DIM)
    return (yc * lax.rsqrt(var + RWKV_LN_EPS) * ln_gain + ln_bias + bonus) * g


def _na_body(q_ref, k_ref, v_ref, qg_ref, kg_ref, cos_ref, sin_ref, bias_ref, o_ref,
             qs, ks, vs, *, t_ctx, need_ctx):
    t_tot = q_ref.shape[0]
    t_lat = t_tot - t_ctx
    rows = t_lat // GRID_W
    win_r = min(NA_WIN_ROWS, rows)
    scale = HEAD_DIM ** -0.5
    seg = _head_seg(LANES)
    lane = _iota((1, LANES), 1)
    head_masks = [jnp.where(lane // HEAD_DIM == h, 1.0, 0.0) for h in range(LANES // HEAD_DIM)]
    half = (lane % HEAD_DIM) < HEAD_DIM // 2
    blk = 256

    def norm_block(i, carry):
        sl = pl.ds(pl.multiple_of(i * blk, blk), blk)
        q = q_ref[sl, :]
        k = k_ref[sl, :]
        q = q * lax.rsqrt(_mm((q * q).astype(BF16), seg) * (1.0 / HEAD_DIM) + EPS) * qg_ref[...]
        k = k * lax.rsqrt(_mm((k * k).astype(BF16), seg) * (1.0 / HEAD_DIM) + EPS) * kg_ref[...]
        cos = cos_ref[sl, :]
        sin = sin_ref[sl, :]

        def rope(t):
            swapped = jnp.where(half, pltpu.roll(t, LANES - HEAD_DIM // 2, 1), pltpu.roll(t, HEAD_DIM // 2, 1))
            return t * cos + swapped * sin

        qs[sl, :] = (rope(q) * scale).astype(BF16)
        ks[sl, :] = rope(k).astype(BF16)
        vs[sl, :] = v_ref[sl, :].astype(BF16)
        return carry

    lax.fori_loop(0, t_tot // blk, norm_block, 0)

    masks_bf = [mh.astype(BF16) for mh in head_masks]

    def by_head(q):
        return jnp.concatenate([q * mb for mb in masks_bf], axis=0)

    def merge_heads(o, n):
        return sum(o[h * n:(h + 1) * n] * mh for h, mh in enumerate(head_masks))

    if need_ctx:
        s = _nt(by_head(qs[0:t_ctx, :]), ks[0:t_ctx, :])
        p = jnp.exp(s - jnp.max(s, axis=-1, keepdims=True))
        o = _mm(p.astype(BF16), vs[0:t_ctx, :]) / jnp.sum(p, axis=-1, keepdims=True)
        o_ref[0:t_ctx, :] = merge_heads(o, t_ctx)
    else:
        o_ref[0:t_ctx, :] = jnp.zeros((t_ctx, LANES), F32)

    def q_rows(it, carry):
        rs = [it * NA_ROW_UNROLL + j for j in range(NA_ROW_UNROLL)]
        starts = [jnp.clip(r - win_r // 2, 0, rows - win_r) for r in rs]
        q_sl = [pl.ds(pl.multiple_of(t_ctx + r * GRID_W, GRID_W), GRID_W) for r in rs]
        k_sl = [pl.ds(pl.multiple_of(t_ctx + r0 * GRID_W, GRID_W), win_r * GRID_W) for r0 in starts]
        qs_ = [by_head(qs[sl, :]) for sl in q_sl]
        s_win = [_nt(q, ks[sl, :]) + bias_ref[r - r0] for q, sl, r, r0 in zip(qs_, k_sl, rs, starts)]
        s_ctx = [_nt(q, ks[0:t_ctx, :]) for q in qs_]
        ms = [jnp.maximum(jnp.max(a, axis=-1, keepdims=True), jnp.max(b, axis=-1, keepdims=True))
              for a, b in zip(s_win, s_ctx)]
        p_win = [jnp.exp(a - m) for a, m in zip(s_win, ms)]
        p_ctx = [jnp.exp(b - m) for b, m in zip(s_ctx, ms)]
        den = [jnp.sum(a, axis=-1, keepdims=True) + jnp.sum(b, axis=-1, keepdims=True) for a, b in zip(p_win, p_ctx)]
        o_win = [_mm(a.astype(BF16), vs[sl, :]) for a, sl in zip(p_win, k_sl)]
        o_ctx = [_mm(b.astype(BF16), vs[0:t_ctx, :]) for b in p_ctx]
        for sl, a, b, d in zip(q_sl, o_win, o_ctx, den):
            o_ref[sl, :] = merge_heads((a + b) / d, GRID_W)
        return carry

    lax.fori_loop(0, rows // NA_ROW_UNROLL, q_rows, 0)


def _na_tables(t_ctx, t_lat):
    quarter = HEAD_DIM // 4
    pos = np.arange(t_lat)
    inv = ROPE_THETA ** (-np.arange(quarter, dtype=np.float32) / quarter)
    pos_r = (pos // GRID_W).astype(np.float32)
    pos_c = (pos % GRID_W).astype(np.float32)
    return pos_r, pos_c, inv


def _na_bias_table(rpb, rows):
    win_r = min(NA_WIN_ROWS, rows)
    c = np.arange(GRID_W)
    w_start = np.clip(c - NA_WIN_COLS // 2, 0, GRID_W - NA_WIN_COLS)
    kc = np.arange(GRID_W)
    in_win = (kc[None, :] >= w_start[:, None]) & (kc[None, :] < w_start[:, None] + NA_WIN_COLS)
    col_idx = np.clip(kc[None, :] - c[:, None] + NA_WIN_COLS - 1, 0, 2 * NA_WIN_COLS - 2)
    n_col = 2 * NA_WIN_COLS - 1
    onehot = jnp.asarray(col_idx[None] == np.arange(n_col)[:, None, None], F32)
    band = jnp.einsum('...hrj,jck->...hrck', rpb.astype(F32), onehot, precision=_HI)
    band = jnp.where(in_win, band, NEG_INF)
    per_off = [band[..., NA_WIN_ROWS - 1 - off:NA_WIN_ROWS - 1 - off + win_r, :, :] for off in range(win_r)]
    g = jnp.stack(per_off, axis=-4)
    g = jnp.swapaxes(g, -3, -2)
    return g.reshape(g.shape[:-2] + (win_r * GRID_W,))


def _na(pc, q_gain, k_gain, bias, layer, *, t_ctx, need_ctx):
    bsz, t_tot, _ = pc.shape
    t_lat = t_tot - t_ctx
    rows = t_lat // GRID_W
    win_r = min(NA_WIN_ROWS, rows)
    n_hp = C_WIDTH // LANES
    hpl = LANES // HEAD_DIM
    pos_r, pos_c, inv = _na_tables(t_ctx, t_lat)
    ang = np.concatenate([pos_r[:, None] * inv, pos_c[:, None] * inv], axis=-1)
    ang = np.concatenate([np.zeros((t_ctx, HEAD_DIM // 2), np.float32), ang], axis=0)
    cos = np.cos(ang)
    sin = np.sin(ang)
    cos_t = jnp.asarray(np.tile(np.concatenate([cos, cos], axis=-1), (1, hpl)), F32)
    sin_t = jnp.asarray(np.tile(np.concatenate([-sin, sin], axis=-1), (1, hpl)), F32)
    bias = bias.reshape(bias.shape[0], n_hp, hpl, win_r, GRID_W, win_r * GRID_W)
    bias = jnp.swapaxes(bias, 2, 3).reshape(bias.shape[0], n_hp, win_r, hpl * GRID_W, win_r * GRID_W)

    def sec(s):
        return pl.BlockSpec((None, t_tot, LANES), lambda b, h, s=s: (b, 0, s * n_hp + h))

    gain = lambda g: jnp.tile(g.reshape(1, HEAD_DIM), (1, hpl))
    return pl.pallas_call(
        functools.partial(_na_body, t_ctx=t_ctx, need_ctx=need_ctx),
        grid=(bsz, n_hp),
        in_specs=[sec(0), sec(1), sec(2),
                  pl.BlockSpec((1, LANES), lambda b, h: (0, 0)),
                  pl.BlockSpec((1, LANES), lambda b, h: (0, 0)),
                  pl.BlockSpec((t_tot, LANES), lambda b, h: (0, 0)),
                  pl.BlockSpec((t_tot, LANES), lambda b, h: (0, 0)),
                  pl.BlockSpec((None, None, win_r, hpl * GRID_W, win_r * GRID_W),
                               lambda b, h: (layer, h, 0, 0, 0))],
        out_specs=pl.BlockSpec((None, t_tot, LANES), lambda b, h: (b, 0, h)),
        out_shape=jax.ShapeDtypeStruct((bsz, t_tot, C_WIDTH), F32),
        scratch_shapes=[pltpu.VMEM((t_tot, LANES), BF16)] * 3,
        compiler_params=_cparams("arbitrary", "arbitrary"),
        name="na",
    )(pc, pc, pc, gain(q_gain), gain(k_gain), cos_t, sin_t, bias)


def _outproj_body(x_ref, a_ref, by_ref, bg_ref, bb_ref, lng_ref, lnb_ref, c_ref, *refs, per):
    m_refs = refs[:per]
    g_ref, w_ref, wr_ref, br_ref, xo_ref, h_ref, route_ref = refs[per:]
    tt = TOKEN_TILE
    b_mix = _rwkv_readout(by_ref[...], bg_ref[...], bb_ref[...], lng_ref[...], lnb_ref[...])
    mix = (_mm(a_ref[...].astype(BF16), w_ref[0:A_WIDTH, :])
           + _mm(b_mix.astype(BF16), w_ref[A_WIDTH:A_WIDTH + B_WIDTH, :])
           + _mm(c_ref[...].astype(BF16), w_ref[A_WIDTH + B_WIDTH:, :]))
    hs = []
    for j, m_ref in enumerate(m_refs):
        rows = slice(j * tt, (j + 1) * tt)
        x = x_ref[rows, :] + m_ref[2:3, :] * mix[rows]
        xo_ref[rows, :] = x
        h = x * lax.rsqrt(jnp.mean(x * x, axis=-1, keepdims=True) + EPS) * g_ref[...]
        h = h * (1.0 + m_ref[4:5, :]) + m_ref[3:4, :]
        h_ref[rows, :] = h
        hs.append(h)
    h = jnp.concatenate(hs, axis=0)
    logits = _mm_3pass(h, wr_ref[...]) + br_ref[...]
    lane = _iota(logits.shape, 1).astype(F32)
    big = float(LANES)
    is_g = (lane >= N_EXPERTS) & (lane < N_EXPERTS + N_GROUPS)
    gl = jnp.where(is_g, logits, -jnp.inf)
    gmax = jnp.max(gl, axis=-1, keepdims=True)
    g_w = 1.0 / jnp.sum(jnp.exp(gl - gmax), axis=-1, keepdims=True)
    g_sel = jnp.min(jnp.where(gl == gmax, lane, big), axis=-1, keepdims=True) - N_EXPERTS
    lo = g_sel * EXPERTS_PER_GROUP
    el = jnp.where((lane >= lo) & (lane < lo + EXPERTS_PER_GROUP), logits, -jnp.inf)
    m1 = jnp.max(el, axis=-1, keepdims=True)
    i1 = jnp.min(jnp.where(el == m1, lane, big), axis=-1, keepdims=True)
    el2 = jnp.where(lane == i1, -jnp.inf, el)
    m2 = jnp.max(el2, axis=-1, keepdims=True)
    i2 = jnp.min(jnp.where(el2 == m2, lane, big), axis=-1, keepdims=True)
    e2 = jnp.exp(m2 - m1)
    w1 = g_w / (1.0 + e2)
    route_ref[...] = (jnp.where(lane == 0.0, i1, 0.0) + jnp.where(lane == 1.0, i2, 0.0)
                      + jnp.where(lane == 2.0, w1, 0.0) + jnp.where(lane == 3.0, w1 * e2, 0.0))


def _outproj(x, a, b_scan, b_gate, b_bonus, ln_gain, ln_bias, c, mod_l, gain2, w_bf, w_router, b_router, *,
             bps, ctx_blocks, ctx_row):
    n = x.shape[0]
    per = PROJ_TILES
    tm = per * TOKEN_TILE
    tok = lambda w: pl.BlockSpec((tm, w), lambda i: (i, 0))
    full = lambda shape: pl.BlockSpec(shape, lambda i: (0,) * len(shape))
    return pl.pallas_call(
        functools.partial(_outproj_body, per=per),
        grid=(n // tm,),
        in_specs=[tok(D_MODEL), tok(A_WIDTH), tok(B_WIDTH), tok(B_WIDTH), tok(B_WIDTH),
                  full((1, B_WIDTH)), full((1, B_WIDTH)), tok(C_WIDTH),
                  *_mod_specs(per, bps, ctx_blocks, ctx_row),
                  full((1, D_MODEL)), full((D_MODEL, D_MODEL)), full((D_MODEL, LANES)), full((1, LANES))],
        out_specs=[tok(D_MODEL), tok(D_MODEL), tok(LANES)],
        out_shape=[jax.ShapeDtypeStruct((n, D_MODEL), F32),
                   jax.ShapeDtypeStruct((n, D_MODEL), F32),
                   jax.ShapeDtypeStruct((n, LANES), F32)],
        compiler_params=_cparams("arbitrary"),
        name="outproj",
    )(x, a, b_scan, b_gate, b_bonus, ln_gain.reshape(1, B_WIDTH), ln_bias.reshape(1, B_WIDTH), c,
      *([mod_l] * per), gain2.reshape(1, D_MODEL), w_bf, w_router, b_router)


def _route_tables(route, bsz, t_tot, skip_rows):
    n_pairs = TOP_K * t_tot
    ids = route[:, 0:TOP_K].astype(jnp.int32).reshape(bsz, t_tot, TOP_K)
    wts = route[:, TOP_K:2 * TOP_K].reshape(bsz, t_tot, TOP_K)
    ids = jnp.where(jnp.arange(t_tot)[None, :, None] < skip_rows, N_EXPERTS, ids)
    ids = jnp.swapaxes(ids, 1, 2).reshape(bsz, n_pairs)
    wts = jnp.swapaxes(wts, 1, 2).reshape(bsz, n_pairs)
    key = ids * n_pairs + jnp.arange(n_pairs, dtype=jnp.int32)
    key, wts = lax.sort((key, wts), dimension=1, num_keys=1)
    tok = (key % n_pairs) % t_tot
    count = jnp.sum((ids[:, :, None] == jnp.arange(N_EXPERTS, dtype=jnp.int32)).astype(jnp.int32), axis=1)
    start = jnp.cumsum(count, axis=1) - count
    pad = ((0, 0), (0, MOE_TILE))
    return (jnp.pad(tok, pad).reshape(-1), jnp.pad(wts, pad).reshape(-1), start.reshape(-1), count.reshape(-1))


def _moe_body(tok_ref, w_ref, start_ref, count_ref, h_ref, wg_ref, wu_ref, wd_ref, y_ref, hbuf, obuf):
    c = pl.program_id(0)
    step = pl.program_id(1)
    t_tot = h_ref.shape[0]
    p_len = TOP_K * t_tot + MOE_TILE

    @pl.when(jnp.logical_and(c == 0, step == 0))
    def _():
        hbuf[...] = jnp.zeros_like(hbuf)

    @pl.when(step == 0)
    def _():
        def zero(i, carry):
            y_ref[pl.ds(pl.multiple_of(i * TOKEN_TILE, TOKEN_TILE), TOKEN_TILE), :] = jnp.zeros(
                (TOKEN_TILE, D_MODEL), F32)
            return carry
        lax.fori_loop(0, t_tot // TOKEN_TILE, zero, 0)

    for k in range(MOE_EXPERTS_PER_STEP):
        e = step * MOE_EXPERTS_PER_STEP + k
        _moe_expert(tok_ref, w_ref, h_ref, wg_ref.at[k], wu_ref.at[k], wd_ref.at[k], y_ref, hbuf, obuf,
                    count=count_ref[c * N_EXPERTS + e], pair_base=c * p_len + start_ref[c * N_EXPERTS + e])


def _moe_expert(tok_ref, w_ref, h_ref, wg_ref, wu_ref, wd_ref, y_ref, hbuf, obuf, *, count, pair_base):
    tm = MOE_TILE
    grp = SUBLANES

    def tile(t, carry):
        n_valid = jnp.minimum(tm, count - t * tm)
        pair0 = pair_base + t * tm

        def gather(g, carry):
            for j in range(grp):
                tok = tok_ref[pair0 + g * grp + j]
                hbuf[g, pl.ds(j, 1), :] = h_ref[pl.ds(tok, 1), :]
            return carry
        lax.fori_loop(0, (n_valid + grp - 1) // grp, gather, 0)

        h = hbuf[...].reshape(tm, D_MODEL).astype(BF16)
        gate = _mm(h, wg_ref[...])
        up = _mm(h, wu_ref[...])
        hid = gate * _sigmoid(gate) * up
        obuf[...] = _mm(hid.astype(BF16), wd_ref[...]).reshape(tm // grp, grp, D_MODEL)

        def add_row(r):
            tok = tok_ref[pair0 + r]
            y_ref[pl.ds(tok, 1), :] += w_ref[pair0 + r] * obuf[r // grp, pl.ds(r % grp, 1), :]

        def scatter(g, carry):
            toks = [tok_ref[pair0 + g * grp + j] for j in range(grp)]
            new = [y_ref[pl.ds(toks[j], 1), :] + w_ref[pair0 + g * grp + j] * obuf[g, pl.ds(j, 1), :]
                   for j in range(grp)]
            for j in range(grp):
                y_ref[pl.ds(toks[j], 1), :] = new[j]
            return carry
        lax.fori_loop(0, n_valid // grp, scatter, 0)

        def scatter_tail(r, carry):
            add_row(r)
            return carry
        lax.fori_loop(n_valid // grp * grp, n_valid, scatter_tail, 0)
        return carry

    lax.fori_loop(0, (count + tm - 1) // tm, tile, 0)


def _moe(h, route, wg, wu, wd, layer, *, bsz, t_tot, skip_rows=0):
    tok, w, start, count = _route_tables(route, bsz, t_tot, skip_rows)
    grid_spec = pltpu.PrefetchScalarGridSpec(
        num_scalar_prefetch=4,
        grid=(bsz, N_EXPERTS // MOE_EXPERTS_PER_STEP),
        in_specs=[
            pl.BlockSpec((None, t_tot, D_MODEL), lambda c, e, *_: (c, 0, 0)),
            pl.BlockSpec((None, MOE_EXPERTS_PER_STEP, D_MODEL, D_EXPERT), lambda c, e, *_: (layer, e, 0, 0)),
            pl.BlockSpec((None, MOE_EXPERTS_PER_STEP, D_MODEL, D_EXPERT), lambda c, e, *_: (layer, e, 0, 0)),
            pl.BlockSpec((None, MOE_EXPERTS_PER_STEP, D_EXPERT, D_MODEL), lambda c, e, *_: (layer, e, 0, 0)),
        ],
        out_specs=pl.BlockSpec((None, t_tot, D_MODEL), lambda c, e, *_: (c, 0, 0)),
        scratch_shapes=[pltpu.VMEM((MOE_TILE // SUBLANES, SUBLANES, D_MODEL), F32)] * 2,
    )
    y = pl.pallas_call(
        _moe_body,
        grid_spec=grid_spec,
        out_shape=jax.ShapeDtypeStruct((bsz, t_tot, D_MODEL), F32),
        compiler_params=_cparams("arbitrary", "arbitrary"),
        name="moe",
    )(tok, w, start, count, h.reshape(bsz, t_tot, D_MODEL), wg, wu, wd)
    return y.reshape(bsz * t_tot, D_MODEL)


def _residual_body(x_ref, y_ref, m_ref, o_ref):
    o_ref[...] = (x_ref[...] + m_ref[5:6, :] * y_ref[...]).astype(o_ref.dtype)


def _final_residual(x, y, mod_l, dtype, *, bsz, t_ctx, t_tot):
    tm = TOKEN_TILE
    tok = pl.BlockSpec((None, tm, D_MODEL), lambda b, i: (b, t_ctx // tm + i, 0))
    return pl.pallas_call(
        _residual_body,
        grid=(bsz, (t_tot - t_ctx) // tm),
        in_specs=[tok, tok, pl.BlockSpec((None, 6, D_MODEL), lambda b, i: (b, 0, 0))],
        out_specs=pl.BlockSpec((None, tm, D_MODEL), lambda b, i: (b, i, 0)),
        out_shape=jax.ShapeDtypeStruct((bsz, t_tot - t_ctx, D_MODEL), dtype),
        compiler_params=_cparams("arbitrary", "arbitrary"),
        name="residual",
    )(x.reshape(bsz, t_tot, D_MODEL), y.reshape(bsz, t_tot, D_MODEL), mod_l)


def kernel(x, c, ctx, c_ctx, norm1_gain, norm2_gain, w_ada, b_ada, w_in, w_out, hgrn_lb_logits, hgrn_gn_gain, rwkv_mu, rwkv_w0, rwkv_w_up, rwkv_a0, rwkv_a_up, rwkv_g_up, rwkv_kk_scale, rwkv_k_a, rwkv_r_k, rwkv_ln_gain, rwkv_ln_bias, na_q_gain, na_k_gain, na_rpb, w_router_group, b_router_group, w_router_expert, b_router_expert, w_exp_gate, w_exp_up, w_exp_down):
    bsz, t_lat, _ = x.shape
    t_ctx = ctx.shape[1]
    t_tot = t_ctx + t_lat
    n = bsz * t_tot
    assert t_ctx % TOKEN_TILE == 0 and t_lat % TOKEN_TILE == 0 and bsz < 16
    bps = t_tot // TOKEN_TILE
    tile_kw = dict(bps=bps, ctx_blocks=t_ctx // TOKEN_TILE, ctx_row=bsz)

    lb_p = jax.nn.softmax(hgrn_lb_logits.astype(F32), axis=1)
    lower_bounds = jnp.cumsum(lb_p, axis=1) - lb_p[:, :1]

    cc = jnp.zeros((16, D_MODEL), F32).at[:bsz].set(c.astype(F32)).at[bsz].set(c_ctx.astype(F32))
    mod = _ada_mod(cc, w_ada, b_ada).reshape(DEPTH, 16, 6, D_MODEL)

    w_in_bf = w_in.astype(BF16)
    w_out_bf = w_out.astype(BF16)
    wg_bf = w_exp_gate.astype(BF16).reshape(DEPTH, N_EXPERTS, D_MODEL, D_EXPERT)
    wu_bf = w_exp_up.astype(BF16).reshape(DEPTH, N_EXPERTS, D_MODEL, D_EXPERT)
    wd_bf = w_exp_down.astype(BF16).reshape(DEPTH, N_EXPERTS, D_EXPERT, D_MODEL)
    pad = LANES - N_EXPERTS - N_GROUPS
    w_router = jnp.concatenate([w_router_expert, w_router_group,
                                jnp.zeros((DEPTH, D_MODEL, pad), F32)], axis=-1)
    b_router = jnp.concatenate([b_router_expert, b_router_group, jnp.zeros((DEPTH, pad), F32)], axis=-1)

    na_bias = _na_bias_table(na_rpb, t_lat // GRID_W)
    xs = jnp.concatenate([ctx.astype(F32), x.astype(F32)], axis=1).reshape(n, D_MODEL)
    y_moe = None
    for layer in range(DEPTH):
        last = layer == DEPTH - 1
        if layer == 0:
            pa, pb, pc = _inproj(xs, mod[layer], norm1_gain[layer], w_in_bf[layer], **tile_kw)
        else:
            xs, pa, pb, pc = _inproj(xs, mod[layer], norm1_gain[layer], w_in_bf[layer],
                                     (y_moe, mod[layer - 1]), **tile_kw)
        a_mix = _hgrn(pa.reshape(bsz, t_tot, A_PROJ), lower_bounds[:, layer], hgrn_gn_gain[layer], t_ctx=t_ctx)
        prep = _rwkv_prep(pb.reshape(bsz, t_tot, B_PROJ), rwkv_mu[layer], rwkv_w0[layer], rwkv_w_up[layer],
                          rwkv_a0[layer], rwkv_a_up[layer], rwkv_g_up[layer], rwkv_kk_scale[layer],
                          rwkv_k_a[layer], rwkv_r_k[layer], t_ctx=t_ctx)
        y = _rwkv_scan(*prep[:9], t_ctx=t_ctx)
        c_mix = _na(pc.reshape(bsz, t_tot, C_PROJ), na_q_gain[layer], na_k_gain[layer], na_bias, layer,
                    t_ctx=t_ctx, need_ctx=not last)
        xs, h2, route = _outproj(xs, a_mix.reshape(n, A_WIDTH), y.reshape(n, B_WIDTH), prep[9].reshape(n, B_WIDTH),
                                 prep[10].reshape(n, B_WIDTH), rwkv_ln_gain[layer], rwkv_ln_bias[layer],
                                 c_mix.reshape(n, C_WIDTH), mod[layer], norm2_gain[layer], w_out_bf[layer],
                                 w_router[layer], b_router[layer].reshape(1, LANES), **tile_kw)
        y_moe = _moe(h2, route, wg_bf, wu_bf, wd_bf, layer, bsz=bsz, t_tot=t_tot, skip_rows=t_ctx if last else 0)
    return _final_residual(xs, y_moe, mod[DEPTH - 1], x.dtype, bsz=bsz, t_ctx=t_ctx, t_tot=t_tot)
```

```python
import functools

import jax
import jax.numpy as jnp
import numpy as np
from jax import lax
from jax.experimental import pallas as pl
from jax.experimental.pallas import tpu as pltpu

F32 = jnp.float32
BF16 = jnp.bfloat16

D_MODEL = 1024
DEPTH = 4
GRID_W = 64
HEAD_DIM = 64
A_HEADS = 4
B_HEADS = 4
C_HEADS = 8
A_WIDTH = A_HEADS * HEAD_DIM
B_WIDTH = B_HEADS * HEAD_DIM
C_WIDTH = C_HEADS * HEAD_DIM
HGRN_CHUNK = 16
HGRN_F_FLOOR = 1e-20
RWKV_W_RANK = 64
RWKV_A_RANK = 64
RWKV_G_RANK = 128
RWKV_LN_EPS = 64e-5
NA_WIN_ROWS = 8
NA_WIN_COLS = 16
ROPE_THETA = 10000.0
N_GROUPS = 4
EXPERTS_PER_GROUP = 8
N_EXPERTS = N_GROUPS * EXPERTS_PER_GROUP
TOP_K = 2
D_EXPERT = 512
EPS = 1e-6
LOG2E = 1.4426950408889634
NEG_INF = -1e30
A_PROJ = 5 * A_WIDTH
B_PROJ = 3 * B_WIDTH + 2 * RWKV_W_RANK + 2 * RWKV_A_RANK + RWKV_G_RANK
C_PROJ = 3 * C_WIDTH
P_TOTAL = A_PROJ + B_PROJ + C_PROJ
B_TAIL = B_PROJ - 3 * B_WIDTH

LANES = 128
SUBLANES = 8
PROJ_TILES = 2
TOKEN_TILE = 256
RWKV_CHUNK = 64
RWKV_BLOCKS_PER_ITER = 6
RWKV_BLOCK = 128
HGRN_BLOCK = 128
NA_ROW_UNROLL = 16
HGRN_SCAN_UNROLL = 48
MOE_EXPERTS_PER_STEP = 2
MOE_TILE = 192
VMEM_LIMIT = 56 * 1024 * 1024

_HI = lax.Precision.HIGHEST


def _cparams(*sem):
    return pltpu.CompilerParams(dimension_semantics=sem, vmem_limit_bytes=VMEM_LIMIT)


def _mm(a, b):
    return jnp.dot(a, b, preferred_element_type=F32)


def _nt(a, b):
    return lax.dot_general(a, b, (((1,), (1,)), ((), ())), preferred_element_type=F32)


def _tn(a, b):
    return lax.dot_general(a, b, (((0,), (0,)), ((), ())), preferred_element_type=F32)


def _split3(a):
    a1 = a.astype(BF16)
    r1 = a - a1.astype(F32)
    a2 = r1.astype(BF16)
    a3 = (r1 - a2.astype(F32)).astype(BF16)
    return a1, a2, a3


def _mm_x3(a, m):
    a1, a2, a3 = _split3(a)
    return _mm(a1, m) + _mm(a2, m) + _mm(a3, m)


def _mm_l3(m, a):
    a1, a2, a3 = _split3(a)
    return _mm(m, a1) + _mm(m, a2) + _mm(m, a3)


def _mm_3pass(a, b):
    a1 = a.astype(BF16)
    a2 = (a - a1.astype(F32)).astype(BF16)
    b1 = b.astype(BF16)
    b2 = (b - b1.astype(F32)).astype(BF16)
    return _mm(a1, b1) + (_mm(a1, b2) + _mm(a2, b1))


def _sigmoid(x):
    return 1.0 / (1.0 + jnp.exp(-x))


def _iota(shape, dim):
    return lax.broadcasted_iota(jnp.int32, shape, dim)


def _head_seg(n):
    return jnp.where(_iota((n, n), 0) // HEAD_DIM == _iota((n, n), 1) // HEAD_DIM, 1.0, 0.0).astype(BF16)


def _ada_body(c_ref, w_ref, b_ref, o_ref):
    c = c_ref[...]
    o_ref[...] = _mm_3pass(c * _sigmoid(c), w_ref[...]) + b_ref[...]


def _ada_mod(cc, w_ada, b_ada):
    rows = cc.shape[0]
    tn = 1536
    return pl.pallas_call(
        _ada_body,
        grid=(DEPTH, 6 * D_MODEL // tn),
        in_specs=[
            pl.BlockSpec((rows, D_MODEL), lambda l, j: (0, 0)),
            pl.BlockSpec((None, D_MODEL, tn), lambda l, j: (l, 0, j)),
            pl.BlockSpec((None, 1, tn), lambda l, j: (l, 0, j)),
        ],
        out_specs=pl.BlockSpec((None, rows, tn), lambda l, j: (l, 0, j)),
        out_shape=jax.ShapeDtypeStruct((DEPTH, rows, 6 * D_MODEL), F32),
        compiler_params=_cparams("arbitrary", "arbitrary"),
        name="ada_mod",
    )(cc, w_ada, b_ada.reshape(DEPTH, 1, 6 * D_MODEL))


def _mod_row(i, blocks_per_seq, ctx_blocks, ctx_row):
    return jnp.where(i % blocks_per_seq < ctx_blocks, ctx_row, i // blocks_per_seq)


def _inproj_body(*refs, with_moe, per):
    tt = TOKEN_TILE
    if with_moe:
        x_ref, y_ref = refs[:2]
        mp_refs, m_refs = refs[2:2 + per], refs[2 + per:2 + 2 * per]
        g_ref, w_ref, xo_ref, pa_ref, pb_ref, pc_ref = refs[2 + 2 * per:]
    else:
        x_ref = refs[0]
        m_refs = refs[1:1 + per]
        g_ref, w_ref, pa_ref, pb_ref, pc_ref = refs[1 + per:]
    hs = []
    for j, m_ref in enumerate(m_refs):
        rows = slice(j * tt, (j + 1) * tt)
        x = x_ref[rows, :]
        if with_moe:
            x = x + mp_refs[j][5:6, :] * y_ref[rows, :]
            xo_ref[rows, :] = x
        h = x * lax.rsqrt(jnp.mean(x * x, axis=-1, keepdims=True) + EPS) * g_ref[...]
        hs.append((h * (1.0 + m_ref[1:2, :]) + m_ref[0:1, :]).astype(BF16))
    h = jnp.concatenate(hs, axis=0)
    pa_ref[...] = _mm(h, w_ref[:, :A_PROJ])
    pb_ref[...] = _mm(h, w_ref[:, A_PROJ:A_PROJ + B_PROJ])
    pc_ref[...] = _mm(h, w_ref[:, A_PROJ + B_PROJ:])


def _mod_specs(per, bps, ctx_blocks, ctx_row):
    row = functools.partial(_mod_row, blocks_per_seq=bps, ctx_blocks=ctx_blocks, ctx_row=ctx_row)
    return [pl.BlockSpec((None, 6, D_MODEL), lambda i, j=j: (row(i * per + j), 0, 0)) for j in range(per)]


def _inproj(x, mod_l, gain, w_bf, moe=None, *, bps, ctx_blocks, ctx_row):
    n = x.shape[0]
    per = PROJ_TILES
    tm = per * TOKEN_TILE
    tok = lambda w: pl.BlockSpec((tm, w), lambda i: (i, 0))
    mods = _mod_specs(per, bps, ctx_blocks, ctx_row)
    with_moe = moe is not None
    ins = (x, moe[0], *([moe[1]] * per)) if with_moe else (x,)
    return pl.pallas_call(
        functools.partial(_inproj_body, with_moe=with_moe, per=per),
        grid=(n // tm,),
        in_specs=[tok(D_MODEL)] + ([tok(D_MODEL)] + mods if with_moe else []) + mods + [
            pl.BlockSpec((1, D_MODEL), lambda i: (0, 0)),
            pl.BlockSpec((D_MODEL, P_TOTAL), lambda i: (0, 0)),
        ],
        out_specs=([tok(D_MODEL)] if with_moe else []) + [tok(A_PROJ), tok(B_PROJ), tok(C_PROJ)],
        out_shape=([jax.ShapeDtypeStruct((n, D_MODEL), F32)] if with_moe else []) + [
            jax.ShapeDtypeStruct((n, A_PROJ), F32),
            jax.ShapeDtypeStruct((n, B_PROJ), F32),
            jax.ShapeDtypeStruct((n, C_PROJ), F32),
        ],
        compiler_params=_cparams("arbitrary"),
        name="inproj",
    )(*ins, *([mod_l] * per), gain.reshape(1, D_MODEL), w_bf)


def _hgrn_body(q_ref, ff_ref, fb_ref, i_ref, g_ref, lb_ref, gn_ref, o_ref,
               qin_s, kout_s, cum_s, acc_s, kpad_s, cpad_s, vpad_s, *, t_ctx):
    t_tot = q_ref.shape[0]
    c = HGRN_CHUNK
    rb = HGRN_BLOCK
    n_blk = t_tot // rb
    n_chunk = t_tot // c
    n_chunk_ctx = t_ctx // c
    row = _iota((rb, rb), 0)
    col = _iota((rb, rb), 1)
    same = (row // c) == (col // c)
    same_bf = jnp.where(same, 1.0, 0.0).astype(BF16)
    seg = _head_seg(LANES)
    pos = _iota((rb, LANES), 0) % c
    blockdiag = _iota((LANES, LANES), 0) // HEAD_DIM == _iota((LANES, LANES), 1) // HEAD_DIM

    pad = HGRN_CHUNK
    zpad = jnp.zeros((pad, LANES), F32)
    for d in range(2):
        for ref in (kpad_s, cpad_s):
            ref[d, 0:pad, :] = zpad
            ref[d, pad + t_tot:, :] = zpad
    vpad_s[0:pad, :] = zpad
    vpad_s[pad + t_tot:, :] = zpad
    tris = [jnp.where(same & ((col >= row) if d == 1 else (col <= row)), 1.0, 0.0).astype(BF16) for d in range(2)]

    def gates(b, carry):
        sl = pl.ds(pl.multiple_of(b * rb, rb), rb)
        slp = pl.ds(pl.multiple_of(b * rb, rb) + pad, rb)
        q = q_ref[sl, :]
        vpad_s[slp, :] = i_ref[sl, :]
        for d, f_ref in enumerate((ff_ref, fb_ref)):
            lb = lb_ref[d:d + 1, :]
            fpre = f_ref[sl, :]
            f = lb + (1.0 - lb) * _sigmoid(fpre)
            logf = jnp.log(jnp.maximum(f, HGRN_F_FLOOR))
            k = (1.0 - lb) * _sigmoid(-fpre)
            cum = _mm_l3(tris[d], logf)
            tot = _mm_l3(same_bf, logf)
            kpad_s[d, slp, :] = k
            cpad_s[d, slp, :] = cum * LOG2E
            qin_s[d, sl, :] = q * jnp.exp(cum)
            kout_s[d, sl, :] = k * jnp.exp(tot - cum)
            cum_s[d, sl, :] = tot
        return carry

    lax.fori_loop(0, n_blk, gates, 0)

    def intra(b, carry):
        r0 = pl.multiple_of(b * rb, rb)
        sl = pl.ds(r0, rb)
        q = q_ref[sl, :]
        cums = [cpad_s[d, pl.ds(r0 + pad, rb), :] for d in range(2)]
        o = jnp.zeros((rb, LANES), F32)
        for j in range(c):
            for d in range(2):
                off = r0 + pad + (j if d == 1 else -j)
                valid = (pos <= c - 1 - j) if d == 1 else (pos >= j)
                ks = kpad_s[d, pl.ds(off, rb), :]
                cs = cpad_s[d, pl.ds(off, rb), :]
                vs = vpad_s[pl.ds(off, rb), :]
                prod = jnp.where(valid, q * ks * jnp.exp2(cums[d] - cs), 0.0)
                o = o + _mm(prod.astype(BF16), seg) * vs
        acc_s[sl, :] = o
        return carry

    lax.fori_loop(0, n_blk, intra, 0)

    def step(it, carry):
        idx = [it * HGRN_SCAN_UNROLL + j for j in range(HGRN_SCAN_UNROLL)]
        order = [(i, jnp.where(i < n_chunk_ctx, n_chunk_ctx - 1 - i, n_chunk - 1 - (i - n_chunk_ctx))) for i in idx]
        sls = [[pl.ds(pl.multiple_of(n[d] * c, c), c) for d in range(2)] for n in order]
        kvs = [[jnp.where(blockdiag, _tn(i_ref[sl[d], :].astype(BF16), kout_s[d, sl[d], :].astype(BF16)), 0.0)
                for d in range(2)] for sl in sls]
        decs = [[jnp.exp(cum_s[d, pl.ds(pl.multiple_of(n[d] * c, c), 1), :]) for d in range(2)] for n in order]
        states = list(carry)
        for j in range(HGRN_SCAN_UNROLL):
            for d in range(2):
                acc_s[sls[j][d], :] += _nt(qin_s[d, sls[j][d], :].astype(BF16), states[d].astype(BF16))
                states[d] = states[d] * decs[j][d] + kvs[j][d]
        return tuple(states)

    zero = jnp.zeros((LANES, LANES), F32)
    lax.fori_loop(0, n_chunk // HGRN_SCAN_UNROLL, step, (zero, zero))

    def readout(b, carry):
        sl = pl.ds(pl.multiple_of(b * rb, rb), rb)
        o = acc_s[sl, :]
        ms = _mm((o * o).astype(BF16), seg) * (1.0 / HEAD_DIM)
        g = g_ref[sl, :]
        o_ref[sl, :] = o * lax.rsqrt(ms + EPS) * gn_ref[...] * (g * _sigmoid(g))
        return carry

    lax.fori_loop(0, n_blk, readout, 0)


def _hgrn(pa, lb, gn_gain, *, t_ctx):
    bsz, t_tot, _ = pa.shape
    n_hp = A_WIDTH // LANES

    def sec(s):
        return pl.BlockSpec((None, t_tot, LANES), lambda b, h, s=s: (b, 0, s * n_hp + h))

    return pl.pallas_call(
        functools.partial(_hgrn_body, t_ctx=t_ctx),
        grid=(bsz, n_hp),
        in_specs=[sec(0), sec(1), sec(2), sec(3), sec(4),
                  pl.BlockSpec((2, LANES), lambda b, h: (0, h)),
                  pl.BlockSpec((1, LANES), lambda b, h: (0, h))],
        out_specs=pl.BlockSpec((None, t_tot, LANES), lambda b, h: (b, 0, h)),
        out_shape=jax.ShapeDtypeStruct((bsz, t_tot, A_WIDTH), F32),
        scratch_shapes=[
            pltpu.VMEM((2, t_tot, LANES), F32),
            pltpu.VMEM((2, t_tot, LANES), F32),
            pltpu.VMEM((2, t_tot, LANES), F32),
            pltpu.VMEM((t_tot, LANES), F32),
            pltpu.VMEM((2, t_tot + 2 * HGRN_CHUNK, LANES), F32),
            pltpu.VMEM((2, t_tot + 2 * HGRN_CHUNK, LANES), F32),
            pltpu.VMEM((t_tot + 2 * HGRN_CHUNK, LANES), F32),
        ],
        compiler_params=_cparams("arbitrary", "arbitrary"),
        name="hgrn",
    )(pa, pa, pa, pa, pa, lb, gn_gain.reshape(1, A_WIDTH))


def _rwkv_prep_body(x_ref, xp_ref, xn_ref, mu_ref, w0_ref, wup_ref, a0_ref, aup_ref, gup_ref,
                    kks_ref, ka_ref, rk_ref,
                    r_o, v_o, kk_o, kf_o, kb_o, bf_o, bb_o, lwf_o, lwb_o, g_o, bonus_o, *, t_ctx, t_tot):
    tm = x_ref.shape[0]
    r0 = pl.program_id(1) * tm
    prev_ok = jnp.logical_and(r0 != 0, r0 != t_ctx)
    next_ok = jnp.logical_and(r0 + tm != t_ctx, r0 + tm != t_tot)
    first = _iota((tm, 1), 0) == 0
    last = _iota((tm, 1), 0) == tm - 1

    def shifted(lo, hi):
        x = x_ref[:, lo:hi]
        p_row = jnp.where(prev_ok, xp_ref[SUBLANES - 1:SUBLANES, lo:hi], 0.0)
        n_row = jnp.where(next_ok, xn_ref[0:1, lo:hi], 0.0)
        prev = jnp.where(first, p_row, pltpu.roll(x, 1, 0))
        nxt = jnp.where(last, n_row, pltpu.roll(x, tm - 1, 0))
        return x + (0.5 * (prev + nxt) - x) * mu_ref[:, lo:hi]

    bw = B_WIDTH
    r = shifted(0, bw)
    k = shifted(bw, 2 * bw)
    v = shifted(2 * bw, 3 * bw)
    tail = shifted(3 * bw, B_PROJ)
    seg = _head_seg(bw)

    kk = k * kks_ref[...]
    kk = kk / jnp.maximum(jnp.sqrt(_mm_x3(kk * kk, seg)), 1e-12)
    ksum = jnp.zeros_like(k)
    for d, (k_o, b_o, lw_o) in enumerate(((kf_o, bf_o, lwf_o), (kb_o, bb_o, lwb_o))):
        wd = tail[:, d * RWKV_W_RANK:(d + 1) * RWKV_W_RANK]
        ad = tail[:, 2 * RWKV_W_RANK + d * RWKV_A_RANK:2 * RWKV_W_RANK + (d + 1) * RWKV_A_RANK]
        u = -(w0_ref[d:d + 1, :] + _mm_3pass(jnp.tanh(wd), wup_ref[d]))
        w = -(jnp.maximum(u, 0.0) + jnp.log(1.0 + jnp.exp(-jnp.abs(u)))) - 0.5
        a = _sigmoid(a0_ref[d:d + 1, :] + _mm_3pass(ad, aup_ref[d]))
        k_d = k * (1.0 + (a - 1.0) * ka_ref[...])
        ksum = ksum + k_d
        k_o[...] = k_d
        b_o[...] = a * kk
        lw_o[...] = -jnp.exp(w)
    gd = tail[:, 2 * RWKV_W_RANK + 2 * RWKV_A_RANK:]
    r_o[...] = r
    v_o[...] = v
    kk_o[...] = kk
    g_o[...] = _mm_3pass(_sigmoid(gd), gup_ref[...])
    bonus_o[...] = _mm((r * ksum * rk_ref[...]).astype(BF16), seg) * v


def _rwkv_prep(pb, mu, w0, w_up, a0, a_up, g_up, kk_scale, k_a, r_k, *, t_ctx):
    bsz, t_tot, _ = pb.shape
    tm = TOKEN_TILE
    nb8 = t_tot // SUBLANES
    per8 = tm // SUBLANES
    bw = B_WIDTH
    full = lambda shape: pl.BlockSpec(shape, lambda b, i: (0,) * len(shape))
    out = jax.ShapeDtypeStruct((bsz, t_tot, bw), F32)
    return pl.pallas_call(
        functools.partial(_rwkv_prep_body, t_ctx=t_ctx, t_tot=t_tot),
        grid=(bsz, t_tot // tm),
        in_specs=[
            pl.BlockSpec((None, tm, B_PROJ), lambda b, i: (b, i, 0)),
            pl.BlockSpec((None, SUBLANES, B_PROJ), lambda b, i: (b, jnp.maximum(i * per8 - 1, 0), 0)),
            pl.BlockSpec((None, SUBLANES, B_PROJ), lambda b, i: (b, jnp.minimum((i + 1) * per8, nb8 - 1), 0)),
            full((1, B_PROJ)), full((2, bw)), full((2, RWKV_W_RANK, bw)), full((2, bw)),
            full((2, RWKV_A_RANK, bw)), full((RWKV_G_RANK, bw)), full((1, bw)), full((1, bw)), full((1, bw)),
        ],
        out_specs=[pl.BlockSpec((None, tm, bw), lambda b, i: (b, i, 0))] * 11,
        out_shape=[out] * 11,
        compiler_params=_cparams("arbitrary", "arbitrary"),
        name="rwkv_prep",
    )(pb, pb, pb, mu.reshape(1, B_PROJ), w0, w_up, a0, a_up, g_up,
      kk_scale.reshape(1, bw), k_a.reshape(1, bw), r_k.reshape(1, bw))


def _rwkv_scan_body(r_ref, v_ref, kk_ref, kf_ref, kb_ref, bf_ref, bb_ref, lwf_ref, lwb_ref, y_ref,
                    rq_s, y0_s, p_s, z_s, *, t_ctx):
    t_tot = r_ref.shape[0]
    c = RWKV_CHUNK
    rb = RWKV_BLOCK
    n_blk = t_tot // rb
    n_chunk = t_tot // c
    n_chunk_ctx = t_ctx // c
    cpb = rb // c
    n_double = c.bit_length() - 2
    row = _iota((rb, rb), 0)
    col = _iota((rb, rb), 1)
    same = (row // c) == (col // c)
    same_bf = jnp.where(same, 1.0, 0.0).astype(BF16)
    eye = jnp.where(row == col, 1.0, 0.0)
    lane = _iota((1, LANES), 1)
    head_masks = [jnp.where(lane // HEAD_DIM == h, 1.0, 0.0) for h in range(LANES // HEAD_DIM)]
    r128 = _iota((LANES, LANES), 0)
    c128 = _iota((LANES, LANES), 1)
    blockdiag = (r128 // HEAD_DIM) == (c128 // HEAD_DIM)
    diag128 = r128 == c128

    def block(it, carry):
        bis = [it * RWKV_BLOCKS_PER_ITER + j for j in range(RWKV_BLOCKS_PER_ITER)]
        sls = [pl.ds(pl.multiple_of(bi * rb, rb), rb) for bi in bis]
        v_bfs = [v_ref[sl, :].astype(BF16) for sl in sls]
        pre = {}
        for j, sl in enumerate(sls):
            r = r_ref[sl, :]
            kk = kk_ref[sl, :]
            for d in range(2):
                rev = d == 1
                k_ref, b_ref, lw_ref = (kb_ref, bb_ref, lwb_ref) if rev else (kf_ref, bf_ref, lwf_ref)
                incl = same & ((col >= row) if rev else (col <= row))
                strict = same & ((col > row) if rev else (col < row))
                incl_bf = jnp.where(incl, 1.0, 0.0).astype(BF16)
                lw = lw_ref[sl, :]
                k = k_ref[sl, :]
                b = b_ref[sl, :]
                cl = _mm_l3(incl_bf, lw)
                tot = _mm_l3(same_bf, lw)
                w_inv = jnp.exp(-cl)
                w_end = jnp.exp(tot - cl)
                pre[j, d] = dict(incl=incl, strict=strict, tot=tot, kk_d=kk * jnp.exp(cl - lw), r_d=r * jnp.exp(cl),
                                 kb=jnp.concatenate([k * w_inv, b * w_inv], axis=0).astype(BF16),
                                 b_e=(b * w_end).astype(BF16), k_e=(k * w_end).astype(BF16))
        chains = [dict(j=j, d=d, mh=mh, kkm=pre[j, d]['kk_d'] * mh, rm=pre[j, d]['r_d'] * mh)
                  for j in range(len(bis)) for d in range(2) for mh in head_masks]
        for ch in chains:
            p = pre[ch['j'], ch['d']]
            aa = _nt(jnp.concatenate([ch['kkm'], ch['rm']], axis=0).astype(BF16), p['kb'])
            ch['a_kb'] = jnp.where(p['strict'], aa[:rb, :rb], 0.0)
            ch['a_rk'] = jnp.where(p['incl'], aa[rb:, :rb], 0.0)
            ch['a_rb'] = jnp.where(p['incl'], aa[rb:, rb:], 0.0)
            m = -jnp.where(p['strict'], aa[:rb, rb:], 0.0)
            ch['t'] = eye + m
            ch['m'] = m
        for ch in chains:
            m_bf = ch['m'].astype(BF16)
            ch['m'] = _mm(m_bf, m_bf)
        for _ in range(n_double - 1):
            for ch in chains:
                both = _mm(jnp.concatenate([ch['t'], ch['m']], axis=0).astype(BF16), ch['m'].astype(BF16))
                ch['t'] = ch['t'] + both[:rb]
                ch['m'] = both[rb:]
        for ch in chains:
            ch['t'] = ch['t'] + _mm(ch['t'].astype(BF16), ch['m'].astype(BF16))
            ch['av'] = _mm(jnp.concatenate([ch['a_kb'], ch['a_rk']], axis=0).astype(BF16), v_bfs[ch['j']])
        for ch in chains:
            ch['ku'] = _mm(ch['t'].astype(BF16), jnp.concatenate([ch['kkm'], ch['av'][:rb]], axis=1).astype(BF16))
        for ch in chains:
            ch['rb_ku'] = _mm(ch['a_rb'].astype(BF16), ch['ku'].astype(BF16))
        for j, (bi, sl) in enumerate(zip(bis, sls)):
            for d in range(2):
                p = pre[j, d]
                mine = [ch for ch in chains if ch['j'] == j and ch['d'] == d]
                kkt = sum(ch['ku'][:, :LANES] for ch in mine)
                u = sum(ch['ku'][:, LANES:] * ch['mh'] for ch in mine)
                rq_s[d, sl, :] = sum(ch['rm'] - ch['rb_ku'][:, :LANES] for ch in mine)
                y0_s[d, sl, :] = sum((ch['av'][rb:] - ch['rb_ku'][:, LANES:]) * ch['mh'] for ch in mine)
                ktu = jnp.concatenate([kkt, u], axis=1).astype(BF16)
                for n in range(cpb):
                    rows = slice(n * c, (n + 1) * c)
                    wc = jnp.exp(p['tot'][n * c:n * c + 1, :])
                    kub = _tn(ktu[rows], p['b_e'][rows])
                    p_s[d, bi * cpb + n] = jnp.where(diag128, wc, 0.0) - jnp.where(blockdiag, kub[:LANES], 0.0)
                    p_z = jnp.where(blockdiag, _tn(v_bfs[j][rows], p['k_e'][rows]) - kub[LANES:], 0.0)
                    z_s[d, bi * cpb + n] = p_z
        return carry

    lax.fori_loop(0, n_blk // RWKV_BLOCKS_PER_ITER, block, 0)

    y_ref[...] = jnp.zeros_like(y_ref)

    def step(i, carry):
        n_b = jnp.where(i < n_chunk_ctx, n_chunk_ctx - 1 - i, n_chunk - 1 - (i - n_chunk_ctx))
        chunk = (i, n_b)
        s_bf = [s.astype(BF16) for s in carry]
        ys = [_nt(rq_s[d, pl.ds(pl.multiple_of(chunk[d] * c, c), c), :].astype(BF16), s_bf[d]) for d in range(2)]
        states = [_mm(s_bf[d], p_s[d, chunk[d]].astype(BF16)) + z_s[d, chunk[d]] for d in range(2)]
        for d in range(2):
            sl = pl.ds(pl.multiple_of(chunk[d] * c, c), c)
            y_ref[sl, :] += ys[d] + y0_s[d, sl, :]
        return tuple(states)

    zero = jnp.zeros((LANES, LANES), F32)
    lax.fori_loop(0, n_chunk, step, (zero, zero))


def _rwkv_scan(r, v, kk, kf, kb, bf, bb, lwf, lwb, *, t_ctx):
    bsz, t_tot, bw = r.shape
    n_chunk = t_tot // RWKV_CHUNK
    spec = pl.BlockSpec((None, t_tot, LANES), lambda b, h: (b, 0, h))
    return pl.pallas_call(
        functools.partial(_rwkv_scan_body, t_ctx=t_ctx),
        grid=(bsz, bw // LANES),
        in_specs=[spec] * 9,
        out_specs=spec,
        out_shape=jax.ShapeDtypeStruct((bsz, t_tot, bw), F32),
        scratch_shapes=[
            pltpu.VMEM((2, t_tot, LANES), F32),
            pltpu.VMEM((2, t_tot, LANES), F32),
            pltpu.VMEM((2, n_chunk, LANES, LANES), F32),
            pltpu.VMEM((2, n_chunk, LANES, LANES), F32),
        ],
        compiler_params=_cparams("arbitrary", "arbitrary"),
        name="rwkv_scan",
    )(r, v, kk, kf, kb, bf, bb, lwf, lwb)


def _rwkv_readout(y, g, bonus, ln_gain, ln_bias):
    seg = _head_seg(B_WIDTH)
    mean = _mm_x3(y, seg) * (1.0 / HEAD_DIM)
    yc = y - mean
    var = _mm((yc * yc).astype(BF16), seg) * (1.0 / HEAD_DIM)
    return (yc * lax.rsqrt(var + RWKV_LN_EPS) * ln_gain + ln_bias + bonus) * g


def _na_body(q_ref, k_ref, v_ref, qg_ref, kg_ref, cos_ref, sin_ref, bias_ref, o_ref,
             qs, ks, vs, *, t_ctx, need_ctx):
    t_tot = q_ref.shape[0]
    t_lat = t_tot - t_ctx
    rows = t_lat // GRID_W
    win_r = min(NA_WIN_ROWS, rows)
    scale = HEAD_DIM ** -0.5
    seg = _head_seg(LANES)
    lane = _iota((1, LANES), 1)
    head_masks = [jnp.where(lane // HEAD_DIM == h, 1.0, 0.0) for h in range(LANES // HEAD_DIM)]
    half = (lane % HEAD_DIM) < HEAD_DIM // 2
    blk = 256

    def norm_block(i, carry):
        sl = pl.ds(pl.multiple_of(i * blk, blk), blk)
        q = q_ref[sl, :]
        k = k_ref[sl, :]
        q = q * lax.rsqrt(_mm((q * q).astype(BF16), seg) * (1.0 / HEAD_DIM) + EPS) * qg_ref[...]
        k = k * lax.rsqrt(_mm((k * k).astype(BF16), seg) * (1.0 / HEAD_DIM) + EPS) * kg_ref[...]
        cos = cos_ref[sl, :]
        sin = sin_ref[sl, :]

        def rope(t):
            swapped = jnp.where(half, pltpu.roll(t, LANES - HEAD_DIM // 2, 1), pltpu.roll(t, HEAD_DIM // 2, 1))
            return t * cos + swapped * sin

        qs[sl, :] = (rope(q) * scale).astype(BF16)
        ks[sl, :] = rope(k).astype(BF16)
        vs[sl, :] = v_ref[sl, :].astype(BF16)
        return carry

    lax.fori_loop(0, t_tot // blk, norm_block, 0)

    masks_bf = [mh.astype(BF16) for mh in head_masks]

    def by_head(q):
        return jnp.concatenate([q * mb for mb in masks_bf], axis=0)

    def merge_heads(o, n):
        return sum(o[h * n:(h + 1) * n] * mh for h, mh in enumerate(head_masks))

    if need_ctx:
        s = _nt(by_head(qs[0:t_ctx, :]), ks[0:t_ctx, :])
        p = jnp.exp(s - jnp.max(s, axis=-1, keepdims=True))
        o = _mm(p.astype(BF16), vs[0:t_ctx, :]) / jnp.sum(p, axis=-1, keepdims=True)
        o_ref[0:t_ctx, :] = merge_heads(o, t_ctx)
    else:
        o_ref[0:t_ctx, :] = jnp.zeros((t_ctx, LANES), F32)

    def q_rows(it, carry):
        rs = [it * NA_ROW_UNROLL + j for j in range(NA_ROW_UNROLL)]
        starts = [jnp.clip(r - win_r // 2, 0, rows - win_r) for r in rs]
        q_sl = [pl.ds(pl.multiple_of(t_ctx + r * GRID_W, GRID_W), GRID_W) for r in rs]
        k_sl = [pl.ds(pl.multiple_of(t_ctx + r0 * GRID_W, GRID_W), win_r * GRID_W) for r0 in starts]
        qs_ = [by_head(qs[sl, :]) for sl in q_sl]
        s_win = [_nt(q, ks[sl, :]) + bias_ref[r - r0] for q, sl, r, r0 in zip(qs_, k_sl, rs, starts)]
        s_ctx = [_nt(q, ks[0:t_ctx, :]) for q in qs_]
        ms = [jnp.maximum(jnp.max(a, axis=-1, keepdims=True), jnp.max(b, axis=-1, keepdims=True))
              for a, b in zip(s_win, s_ctx)]
        p_win = [jnp.exp(a - m) for a, m in zip(s_win, ms)]
        p_ctx = [jnp.exp(b - m) for b, m in zip(s_ctx, ms)]
        den = [jnp.sum(a, axis=-1, keepdims=True) + jnp.sum(b, axis=-1, keepdims=True) for a, b in zip(p_win, p_ctx)]
        o_win = [_mm(a.astype(BF16), vs[sl, :]) for a, sl in zip(p_win, k_sl)]
        o_ctx = [_mm(b.astype(BF16), vs[0:t_ctx, :]) for b in p_ctx]
        for sl, a, b, d in zip(q_sl, o_win, o_ctx, den):
            o_ref[sl, :] = merge_heads((a + b) / d, GRID_W)
        return carry

    lax.fori_loop(0, rows // NA_ROW_UNROLL, q_rows, 0)


def _na_tables(t_ctx, t_lat):
    quarter = HEAD_DIM // 4
    pos = np.arange(t_lat)
    inv = ROPE_THETA ** (-np.arange(quarter, dtype=np.float32) / quarter)
    pos_r = (pos // GRID_W).astype(np.float32)
    pos_c = (pos % GRID_W).astype(np.float32)
    return pos_r, pos_c, inv


def _na_bias_table(rpb, rows):
    win_r = min(NA_WIN_ROWS, rows)
    c = np.arange(GRID_W)
    w_start = np.clip(c - NA_WIN_COLS // 2, 0, GRID_W - NA_WIN_COLS)
    kc = np.arange(GRID_W)
    in_win = (kc[None, :] >= w_start[:, None]) & (kc[None, :] < w_start[:, None] + NA_WIN_COLS)
    col_idx = np.clip(kc[None, :] - c[:, None] + NA_WIN_COLS - 1, 0, 2 * NA_WIN_COLS - 2)
    n_col = 2 * NA_WIN_COLS - 1
    onehot = jnp.asarray(col_idx[None] == np.arange(n_col)[:, None, None], F32)
    band = jnp.einsum('...hrj,jck->...hrck', rpb.astype(F32), onehot, precision=_HI)
    band = jnp.where(in_win, band, NEG_INF)
    per_off = [band[..., NA_WIN_ROWS - 1 - off:NA_WIN_ROWS - 1 - off + win_r, :, :] for off in range(win_r)]
    g = jnp.stack(per_off, axis=-4)
    g = jnp.swapaxes(g, -3, -2)
    return g.reshape(g.shape[:-2] + (win_r * GRID_W,))


def _na(pc, q_gain, k_gain, bias, layer, *, t_ctx, need_ctx):
    bsz, t_tot, _ = pc.shape
    t_lat = t_tot - t_ctx
    rows = t_lat // GRID_W
    win_r = min(NA_WIN_ROWS, rows)
    n_hp = C_WIDTH // LANES
    hpl = LANES // HEAD_DIM
    pos_r, pos_c, inv = _na_tables(t_ctx, t_lat)
    ang = np.concatenate([pos_r[:, None] * inv, pos_c[:, None] * inv], axis=-1)
    ang = np.concatenate([np.zeros((t_ctx, HEAD_DIM // 2), np.float32), ang], axis=0)
    cos = np.cos(ang)
    sin = np.sin(ang)
    cos_t = jnp.asarray(np.tile(np.concatenate([cos, cos], axis=-1), (1, hpl)), F32)
    sin_t = jnp.asarray(np.tile(np.concatenate([-sin, sin], axis=-1), (1, hpl)), F32)
    bias = bias.reshape(bias.shape[0], n_hp, hpl, win_r, GRID_W, win_r * GRID_W)
    bias = jnp.swapaxes(bias, 2, 3).reshape(bias.shape[0], n_hp, win_r, hpl * GRID_W, win_r * GRID_W)

    def sec(s):
        return pl.BlockSpec((None, t_tot, LANES), lambda b, h, s=s: (b, 0, s * n_hp + h))

    gain = lambda g: jnp.tile(g.reshape(1, HEAD_DIM), (1, hpl))
    return pl.pallas_call(
        functools.partial(_na_body, t_ctx=t_ctx, need_ctx=need_ctx),
        grid=(bsz, n_hp),
        in_specs=[sec(0), sec(1), sec(2),
                  pl.BlockSpec((1, LANES), lambda b, h: (0, 0)),
                  pl.BlockSpec((1, LANES), lambda b, h: (0, 0)),
                  pl.BlockSpec((t_tot, LANES), lambda b, h: (0, 0)),
                  pl.BlockSpec((t_tot, LANES), lambda b, h: (0, 0)),
                  pl.BlockSpec((None, None, win_r, hpl * GRID_W, win_r * GRID_W),
                               lambda b, h: (layer, h, 0, 0, 0))],
        out_specs=pl.BlockSpec((None, t_tot, LANES), lambda b, h: (b, 0, h)),
        out_shape=jax.ShapeDtypeStruct((bsz, t_tot, C_WIDTH), F32),
        scratch_shapes=[pltpu.VMEM((t_tot, LANES), BF16)] * 3,
        compiler_params=_cparams("arbitrary", "arbitrary"),
        name="na",
    )(pc, pc, pc, gain(q_gain), gain(k_gain), cos_t, sin_t, bias)


def _outproj_body(x_ref, a_ref, by_ref, bg_ref, bb_ref, lng_ref, lnb_ref, c_ref, *refs, per):
    m_refs = refs[:per]
    g_ref, w_ref, wr_ref, br_ref, xo_ref, h_ref, route_ref = refs[per:]
    tt = TOKEN_TILE
    b_mix = _rwkv_readout(by_ref[...], bg_ref[...], bb_ref[...], lng_ref[...], lnb_ref[...])
    mix = (_mm(a_ref[...].astype(BF16), w_ref[0:A_WIDTH, :])
           + _mm(b_mix.astype(BF16), w_ref[A_WIDTH:A_WIDTH + B_WIDTH, :])
           + _mm(c_ref[...].astype(BF16), w_ref[A_WIDTH + B_WIDTH:, :]))
    hs = []
    for j, m_ref in enumerate(m_refs):
        rows = slice(j * tt, (j + 1) * tt)
        x = x_ref[rows, :] + m_ref[2:3, :] * mix[rows]
        xo_ref[rows, :] = x
        h = x * lax.rsqrt(jnp.mean(x * x, axis=-1, keepdims=True) + EPS) * g_ref[...]
        h = h * (1.0 + m_ref[4:5, :]) + m_ref[3:4, :]
        h_ref[rows, :] = h
        hs.append(h)
    h = jnp.concatenate(hs, axis=0)
    logits = _mm_3pass(h, wr_ref[...]) + br_ref[...]
    lane = _iota(logits.shape, 1).astype(F32)
    big = float(LANES)
    is_g = (lane >= N_EXPERTS) & (lane < N_EXPERTS + N_GROUPS)
    gl = jnp.where(is_g, logits, -jnp.inf)
    gmax = jnp.max(gl, axis=-1, keepdims=True)
    g_w = 1.0 / jnp.sum(jnp.exp(gl - gmax), axis=-1, keepdims=True)
    g_sel = jnp.min(jnp.where(gl == gmax, lane, big), axis=-1, keepdims=True) - N_EXPERTS
    lo = g_sel * EXPERTS_PER_GROUP
    el = jnp.where((lane >= lo) & (lane < lo + EXPERTS_PER_GROUP), logits, -jnp.inf)
    m1 = jnp.max(el, axis=-1, keepdims=True)
    i1 = jnp.min(jnp.where(el == m1, lane, big), axis=-1, keepdims=True)
    el2 = jnp.where(lane == i1, -jnp.inf, el)
    m2 = jnp.max(el2, axis=-1, keepdims=True)
    i2 = jnp.min(jnp.where(el2 == m2, lane, big), axis=-1, keepdims=True)
    e2 = jnp.exp(m2 - m1)
    w1 = g_w / (1.0 + e2)
    route_ref[...] = (jnp.where(lane == 0.0, i1, 0.0) + jnp.where(lane == 1.0, i2, 0.0)
                      + jnp.where(lane == 2.0, w1, 0.0) + jnp.where(lane == 3.0, w1 * e2, 0.0))


def _outproj(x, a, b_scan, b_gate, b_bonus, ln_gain, ln_bias, c, mod_l, gain2, w_bf, w_router, b_router, *,
             bps, ctx_blocks, ctx_row):
    n = x.shape[0]
    per = PROJ_TILES
    tm = per * TOKEN_TILE
    tok = lambda w: pl.BlockSpec((tm, w), lambda i: (i, 0))
    full = lambda shape: pl.BlockSpec(shape, lambda i: (0,) * len(shape))
    return pl.pallas_call(
        functools.partial(_outproj_body, per=per),
        grid=(n // tm,),
        in_specs=[tok(D_MODEL), tok(A_WIDTH), tok(B_WIDTH), tok(B_WIDTH), tok(B_WIDTH),
                  full((1, B_WIDTH)), full((1, B_WIDTH)), tok(C_WIDTH),
                  *_mod_specs(per, bps, ctx_blocks, ctx_row),
                  full((1, D_MODEL)), full((D_MODEL, D_MODEL)), full((D_MODEL, LANES)), full((1, LANES))],
        out_specs=[tok(D_MODEL), tok(D_MODEL), tok(LANES)],
        out_shape=[jax.ShapeDtypeStruct((n, D_MODEL), F32),
                   jax.ShapeDtypeStruct((n, D_MODEL), F32),
                   jax.ShapeDtypeStruct((n, LANES), F32)],
        compiler_params=_cparams("arbitrary"),
        name="outproj",
    )(x, a, b_scan, b_gate, b_bonus, ln_gain.reshape(1, B_WIDTH), ln_bias.reshape(1, B_WIDTH), c,
      *([mod_l] * per), gain2.reshape(1, D_MODEL), w_bf, w_router, b_router)


def _route_tables(route, bsz, t_tot, skip_rows):
    n_pairs = TOP_K * t_tot
    ids = route[:, 0:TOP_K].astype(jnp.int32).reshape(bsz, t_tot, TOP_K)
    wts = route[:, TOP_K:2 * TOP_K].reshape(bsz, t_tot, TOP_K)
    ids = jnp.where(jnp.arange(t_tot)[None, :, None] < skip_rows, N_EXPERTS, ids)
    ids = jnp.swapaxes(ids, 1, 2).reshape(bsz, n_pairs)
    wts = jnp.swapaxes(wts, 1, 2).reshape(bsz, n_pairs)
    key = ids * n_pairs + jnp.arange(n_pairs, dtype=jnp.int32)
    key, wts = lax.sort((key, wts), dimension=1, num_keys=1)
    tok = (key % n_pairs) % t_tot
    count = jnp.sum((ids[:, :, None] == jnp.arange(N_EXPERTS, dtype=jnp.int32)).astype(jnp.int32), axis=1)
    start = jnp.cumsum(count, axis=1) - count
    pad = ((0, 0), (0, MOE_TILE))
    return (jnp.pad(tok, pad).reshape(-1), jnp.pad(wts, pad).reshape(-1), start.reshape(-1), count.reshape(-1))


def _moe_body(tok_ref, w_ref, start_ref, count_ref, h_ref, wg_ref, wu_ref, wd_ref, y_ref, hbuf, obuf):
    c = pl.program_id(0)
    step = pl.program_id(1)
    t_tot = h_ref.shape[0]
    p_len = TOP_K * t_tot + MOE_TILE

    @pl.when(jnp.logical_and(c == 0, step == 0))
    def _():
        hbuf[...] = jnp.zeros_like(hbuf)

    @pl.when(step == 0)
    def _():
        def zero(i, carry):
            y_ref[pl.ds(pl.multiple_of(i * TOKEN_TILE, TOKEN_TILE), TOKEN_TILE), :] = jnp.zeros(
                (TOKEN_TILE, D_MODEL), F32)
            return carry
        lax.fori_loop(0, t_tot // TOKEN_TILE, zero, 0)

    for k in range(MOE_EXPERTS_PER_STEP):
        e = step * MOE_EXPERTS_PER_STEP + k
        _moe_expert(tok_ref, w_ref, h_ref, wg_ref.at[k], wu_ref.at[k], wd_ref.at[k], y_ref, hbuf, obuf,
                    count=count_ref[c * N_EXPERTS + e], pair_base=c * p_len + start_ref[c * N_EXPERTS + e])


def _moe_expert(tok_ref, w_ref, h_ref, wg_ref, wu_ref, wd_ref, y_ref, hbuf, obuf, *, count, pair_base):
    tm = MOE_TILE
    grp = SUBLANES

    def tile(t, carry):
        n_valid = jnp.minimum(tm, count - t * tm)
        pair0 = pair_base + t * tm

        def gather(g, carry):
            for j in range(grp):
                tok = tok_ref[pair0 + g * grp + j]
                hbuf[g, pl.ds(j, 1), :] = h_ref[pl.ds(tok, 1), :]
            return carry
        lax.fori_loop(0, (n_valid + grp - 1) // grp, gather, 0)

        h = hbuf[...].reshape(tm, D_MODEL).astype(BF16)
        gate = _mm(h, wg_ref[...])
        up = _mm(h, wu_ref[...])
        hid = gate * _sigmoid(gate) * up
        obuf[...] = _mm(hid.astype(BF16), wd_ref[...]).reshape(tm // grp, grp, D_MODEL)

        def add_row(r):
            tok = tok_ref[pair0 + r]
            y_ref[pl.ds(tok, 1), :] += w_ref[pair0 + r] * obuf[r // grp, pl.ds(r % grp, 1), :]

        def scatter(g, carry):
            toks = [tok_ref[pair0 + g * grp + j] for j in range(grp)]
            new = [y_ref[pl.ds(toks[j], 1), :] + w_ref[pair0 + g * grp + j] * obuf[g, pl.ds(j, 1), :]
                   for j in range(grp)]
            for j in range(grp):
                y_ref[pl.ds(toks[j], 1), :] = new[j]
            return carry
        lax.fori_loop(0, n_valid // grp, scatter, 0)

        def scatter_tail(r, carry):
            add_row(r)
            return carry
        lax.fori_loop(n_valid // grp * grp, n_valid, scatter_tail, 0)
        return carry

    lax.fori_loop(0, (count + tm - 1) // tm, tile, 0)


def _moe(h, route, wg, wu, wd, layer, *, bsz, t_tot, skip_rows=0):
    tok, w, start, count = _route_tables(route, bsz, t_tot, skip_rows)
    grid_spec = pltpu.PrefetchScalarGridSpec(
        num_scalar_prefetch=4,
        grid=(bsz, N_EXPERTS // MOE_EXPERTS_PER_STEP),
        in_specs=[
            pl.BlockSpec((None, t_tot, D_MODEL), lambda c, e, *_: (c, 0, 0)),
            pl.BlockSpec((None, MOE_EXPERTS_PER_STEP, D_MODEL, D_EXPERT), lambda c, e, *_: (layer, e, 0, 0)),
            pl.BlockSpec((None, MOE_EXPERTS_PER_STEP, D_MODEL, D_EXPERT), lambda c, e, *_: (layer, e, 0, 0)),
            pl.BlockSpec((None, MOE_EXPERTS_PER_STEP, D_EXPERT, D_MODEL), lambda c, e, *_: (layer, e, 0, 0)),
        ],
        out_specs=pl.BlockSpec((None, t_tot, D_MODEL), lambda c, e, *_: (c, 0, 0)),
        scratch_shapes=[pltpu.VMEM((MOE_TILE // SUBLANES, SUBLANES, D_MODEL), F32)] * 2,
    )
    y = pl.pallas_call(
        _moe_body,
        grid_spec=grid_spec,
        out_shape=jax.ShapeDtypeStruct((bsz, t_tot, D_MODEL), F32),
        compiler_params=_cparams("arbitrary", "arbitrary"),
        name="moe",
    )(tok, w, start, count, h.reshape(bsz, t_tot, D_MODEL), wg, wu, wd)
    return y.reshape(bsz * t_tot, D_MODEL)


def _residual_body(x_ref, y_ref, m_ref, o_ref):
    o_ref[...] = (x_ref[...] + m_ref[5:6, :] * y_ref[...]).astype(o_ref.dtype)


def _final_residual(x, y, mod_l, dtype, *, bsz, t_ctx, t_tot):
    tm = TOKEN_TILE
    tok = pl.BlockSpec((None, tm, D_MODEL), lambda b, i: (b, t_ctx // tm + i, 0))
    return pl.pallas_call(
        _residual_body,
        grid=(bsz, (t_tot - t_ctx) // tm),
        in_specs=[tok, tok, pl.BlockSpec((None, 6, D_MODEL), lambda b, i: (b, 0, 0))],
        out_specs=pl.BlockSpec((None, tm, D_MODEL), lambda b, i: (b, i, 0)),
        out_shape=jax.ShapeDtypeStruct((bsz, t_tot - t_ctx, D_MODEL), dtype),
        compiler_params=_cparams("arbitrary", "arbitrary"),
        name="residual",
    )(x.reshape(bsz, t_tot, D_MODEL), y.reshape(bsz, t_tot, D_MODEL), mod_l)


def kernel(x, c, ctx, c_ctx, norm1_gain, norm2_gain, w_ada, b_ada, w_in, w_out, hgrn_lb_logits, hgrn_gn_gain, rwkv_mu, rwkv_w0, rwkv_w_up, rwkv_a0, rwkv_a_up, rwkv_g_up, rwkv_kk_scale, rwkv_k_a, rwkv_r_k, rwkv_ln_gain, rwkv_ln_bias, na_q_gain, na_k_gain, na_rpb, w_router_group, b_router_group, w_router_expert, b_router_expert, w_exp_gate, w_exp_up, w_exp_down):
    bsz, t_lat, _ = x.shape
    t_ctx = ctx.shape[1]
    t_tot = t_ctx + t_lat
    n = bsz * t_tot
    assert t_ctx % TOKEN_TILE == 0 and t_lat % TOKEN_TILE == 0 and bsz < 16
    bps = t_tot // TOKEN_TILE
    tile_kw = dict(bps=bps, ctx_blocks=t_ctx // TOKEN_TILE, ctx_row=bsz)

    lb_p = jax.nn.softmax(hgrn_lb_logits.astype(F32), axis=1)
    lower_bounds = jnp.cumsum(lb_p, axis=1) - lb_p[:, :1]

    cc = jnp.zeros((16, D_MODEL), F32).at[:bsz].set(c.astype(F32)).at[bsz].set(c_ctx.astype(F32))
    mod = _ada_mod(cc, w_ada, b_ada).reshape(DEPTH, 16, 6, D_MODEL)

    w_in_bf = w_in.astype(BF16)
    w_out_bf = w_out.astype(BF16)
    wg_bf = w_exp_gate.astype(BF16).reshape(DEPTH, N_EXPERTS, D_MODEL, D_EXPERT)
    wu_bf = w_exp_up.astype(BF16).reshape(DEPTH, N_EXPERTS, D_MODEL, D_EXPERT)
    wd_bf = w_exp_down.astype(BF16).reshape(DEPTH, N_EXPERTS, D_EXPERT, D_MODEL)
    pad = LANES - N_EXPERTS - N_GROUPS
    w_router = jnp.concatenate([w_router_expert, w_router_group,
                                jnp.zeros((DEPTH, D_MODEL, pad), F32)], axis=-1)
    b_router = jnp.concatenate([b_router_expert, b_router_group, jnp.zeros((DEPTH, pad), F32)], axis=-1)

    na_bias = _na_bias_table(na_rpb, t_lat // GRID_W)
    xs = jnp.concatenate([ctx.astype(F32), x.astype(F32)], axis=1).reshape(n, D_MODEL)
    y_moe = None
    for layer in range(DEPTH):
        last = layer == DEPTH - 1
        if layer == 0:
            pa, pb, pc = _inproj(xs, mod[layer], norm1_gain[layer], w_in_bf[layer], **tile_kw)
        else:
            xs, pa, pb, pc = _inproj(xs, mod[layer], norm1_gain[layer], w_in_bf[layer],
                                     (y_moe, mod[layer - 1]), **tile_kw)
        a_mix = _hgrn(pa.reshape(bsz, t_tot, A_PROJ), lower_bounds[:, layer], hgrn_gn_gain[layer], t_ctx=t_ctx)
        prep = _rwkv_prep(pb.reshape(bsz, t_tot, B_PROJ), rwkv_mu[layer], rwkv_w0[layer], rwkv_w_up[layer],
                          rwkv_a0[layer], rwkv_a_up[layer], rwkv_g_up[layer], rwkv_kk_scale[layer],
                          rwkv_k_a[layer], rwkv_r_k[layer], t_ctx=t_ctx)
        y = _rwkv_scan(*prep[:9], t_ctx=t_ctx)
        c_mix = _na(pc.reshape(bsz, t_tot, C_PROJ), na_q_gain[layer], na_k_gain[layer], na_bias, layer,
                    t_ctx=t_ctx, need_ctx=not last)
        xs, h2, route = _outproj(xs, a_mix.reshape(n, A_WIDTH), y.reshape(n, B_WIDTH), prep[9].reshape(n, B_WIDTH),
                                 prep[10].reshape(n, B_WIDTH), rwkv_ln_gain[layer], rwkv_ln_bias[layer],
                                 c_mix.reshape(n, C_WIDTH), mod[layer], norm2_gain[layer], w_out_bf[layer],
                                 w_router[layer], b_router[layer].reshape(1, LANES), **tile_kw)
        y_moe = _moe(h2, route, wg_bf, wu_bf, wd_bf, layer, bsz=bsz, t_tot=t_tot, skip_rows=t_ctx if last else 0)
    return _final_residual(xs, y_moe, mod[DEPTH - 1], x.dtype, bsz=bsz, t_ctx=t_ctx, t_tot=t_tot)
```
